```python
import math
import jax, jax.numpy as jnp
from jax import lax
import numpy as np

D_MODEL = 2048
BATCH = 4
SEQ = 2048
DEPTH = 1
DEC_BATCH = 128
DEC_SEQ = 8
PAST_LEN = 16384
PAGE_SIZE = 128

MIX_WIDTH = D_MODEL
GLA_WIDTH = MIX_WIDTH // 2
POOL_WIDTH = MIX_WIDTH - GLA_WIDTH
GLA_HEADS = 4
GLA_DV = GLA_WIDTH // GLA_HEADS
GLA_DK = GLA_DV // 2
GLA_KEY_WIDTH = GLA_HEADS * GLA_DK
GATE_RANK = 16
GATE_TEMP = 16.0
GLA_CHUNK = 64
POOL_WINDOWS = (2, 4, 8, 16)
POOL_GROUPS = len(POOL_WINDOWS)
POOL_GW = POOL_WIDTH // POOL_GROUPS
POOL_BUF = max(POOL_WINDOWS) - 1
N_GROUPS = 4
EXPERTS_PER_GROUP = 8
N_EXPERTS = N_GROUPS * EXPERTS_PER_GROUP
TOP_K = 2
EXPERT_FF = 512
EPS = 1e-6
IN_SIZES = (GLA_KEY_WIDTH, GLA_KEY_WIDTH, GLA_WIDTH, GLA_WIDTH, GATE_RANK, POOL_WIDTH)
IN_WIDTH = sum(IN_SIZES)
IN_OFFSETS = tuple(int(o) for o in np.cumsum(IN_SIZES)[:-1])

kernel_name = "hymba_gla_pool_hiermoe_step"


def rmsnorm(x, g):
    x32 = x.astype(jnp.float32)
    y = x32 * lax.rsqrt(jnp.mean(x32 * x32, axis=-1, keepdims=True) + EPS)
    return (y * g.astype(jnp.float32)).astype(x.dtype)


def gla_chunk_step(S, inp):
    q, k, v, g = inp
    C = q.shape[2]
    b = jnp.cumsum(g, axis=2)
    causal = jnp.tril(jnp.ones((C, C), dtype=bool))
    diff = b[:, :, :, None, :] - b[:, :, None, :, :]
    decay = jnp.exp(jnp.where(causal[None, None, :, :, None], diff, -jnp.inf))
    scores = jnp.einsum('bhid,bhjd,bhijd->bhij', q, k, decay)
    o = jnp.einsum('bhij,bhjv->bhiv', scores, v) + jnp.einsum('bhid,bhdv->bhiv', q * jnp.exp(b), S)
    b_last = b[:, :, -1:, :]
    S_new = jnp.exp(b_last[:, :, 0, :, None]) * S + jnp.einsum('bhjd,bhjv->bhdv', k * jnp.exp(b_last - b), v)
    return S_new, o


def gla_recurrence(q, k, v, log_a, S0):
    B, T = q.shape[0], q.shape[1]
    C = math.gcd(T, GLA_CHUNK)
    n = T // C

    def to_blocks(a):
        return a.astype(jnp.float32).reshape(B, n, C, GLA_HEADS, a.shape[-1]).transpose(1, 0, 3, 2, 4)

    S_last, o = lax.scan(gla_chunk_step, S0.astype(jnp.float32),
                         (to_blocks(q), to_blocks(k), to_blocks(v), to_blocks(log_a)))
    o = o.transpose(1, 0, 3, 2, 4).reshape(B, T, GLA_HEADS, GLA_DV)
    return o, S_last


def multiscale_pool(u, buf, pos0):
    T = u.shape[1]
    u_ext = jnp.concatenate([buf.astype(u.dtype), u], axis=1)
    cs = jnp.pad(jnp.cumsum(u_ext.astype(jnp.float32), axis=1), ((0, 0), (1, 0), (0, 0)))
    end = cs[:, POOL_BUF + 1:]
    pos = pos0 + jnp.arange(T)
    means = []
    for gi, w in enumerate(POOL_WINDOWS):
        sl = slice(gi * POOL_GW, (gi + 1) * POOL_GW)
        start = cs[:, POOL_BUF + 1 - w: POOL_BUF + 1 - w + T, sl]
        cnt = jnp.minimum(pos + 1, w).astype(jnp.float32)[None, :, None]
        means.append((end[..., sl] - start) / cnt)
    pooled = (jnp.concatenate(means, axis=-1) - u.astype(jnp.float32)).astype(u.dtype)
    return pooled, u_ext[:, -POOL_BUF:]


def mixing_sublayer(h, S0, buf0, pos0, w_in, gate_up, gate_bias, gla_norm, pool_w, pool_scale, w_out):
    B, T, _ = h.shape
    proj = h @ w_in
    q, k, v, r, g_lr, u = jnp.split(proj, IN_OFFSETS, axis=-1)
    q = q.reshape(B, T, GLA_HEADS, GLA_DK) * (GLA_DK ** -0.5)
    k = k.reshape(B, T, GLA_HEADS, GLA_DK)
    v = v.reshape(B, T, GLA_HEADS, GLA_DV)
    log_a = jax.nn.log_sigmoid((g_lr @ gate_up + gate_bias).astype(jnp.float32)) / GATE_TEMP
    log_a = log_a.reshape(B, T, GLA_HEADS, GLA_DK)
    o, S_new = gla_recurrence(q, k, v, log_a, S0)
    o = rmsnorm(o.astype(h.dtype), gla_norm.reshape(GLA_HEADS, GLA_DV)).reshape(B, T, GLA_WIDTH)
    y_gla = o * jax.nn.silu(r)
    pooled, buf_new = multiscale_pool(u, buf0, pos0)
    y_pool = jnp.einsum('btgc,gcd->btgd', pooled.reshape(B, T, POOL_GROUPS, POOL_GW), pool_w)
    y_pool = y_pool.reshape(B, T, POOL_WIDTH) * pool_scale
    y = jnp.concatenate([y_gla, y_pool], axis=-1) @ w_out
    return y, S_new.astype(S0.dtype), buf_new


def hier_moe(h, w_group, b_group, w_expert, b_expert, w1, w3, w2):
    logits_g = (h @ w_group).astype(jnp.float32) + b_group
    p_g = jax.nn.softmax(logits_g, axis=-1)
    g_idx = jnp.argmax(logits_g, axis=-1)
    p_sel = jnp.take_along_axis(p_g, g_idx[..., None], axis=-1)
    logits_e = jnp.einsum('btd,gde->btge', h, w_expert).astype(jnp.float32) + b_expert
    logits_sel = jnp.take_along_axis(logits_e, g_idx[..., None, None], axis=2)[:, :, 0, :]
    top_v, top_i = lax.top_k(logits_sel, TOP_K)
    w_top = jax.nn.softmax(top_v, axis=-1) * p_sel
    expert_id = g_idx[..., None] * EXPERTS_PER_GROUP + top_i
    combine = jnp.sum(jax.nn.one_hot(expert_id, N_EXPERTS, dtype=jnp.float32) * w_top[..., None], axis=-2)
    hid = jax.nn.silu(jnp.einsum('btd,edf->btef', h, w1)) * jnp.einsum('btd,edf->btef', h, w3)
    hid = hid * combine.astype(h.dtype)[..., None]
    return jnp.einsum('btef,efd->btd', hid, w2)


def layer(x, c, S0, buf0, pos0, w_ada, b_ada, norm1, norm2, w_in, gate_up, gate_bias, gla_norm,
          pool_w, pool_scale, w_out, w_group, b_group, w_expert, b_expert, w1, w3, w2):
    B = x.shape[0]
    mod = (jax.nn.silu(c) @ w_ada + b_ada).reshape(B, 6, D_MODEL)[:, :, None, :]
    shift1, scale1, gate1, shift2, scale2, gate2 = [mod[:, i] for i in range(6)]
    h = rmsnorm(x, norm1) * (1 + scale1) + shift1
    y, S_new, buf_new = mixing_sublayer(h, S0, buf0, pos0, w_in, gate_up, gate_bias, gla_norm,
                                        pool_w, pool_scale, w_out)
    x = x + gate1 * y
    h = rmsnorm(x, norm2) * (1 + scale2) + shift2
    x = x + gate2 * hier_moe(h, w_group, b_group, w_expert, b_expert, w1, w3, w2)
    return x, S_new, buf_new


def setup_inputs(seed: int = 0) -> dict:
    key = jax.random.key(seed)
    ks = jax.random.split(key, 32)
    f32 = jnp.float32

    def nrm(k, shape, scale):
        return jax.random.normal(k, shape, f32) * scale

    L, D = DEPTH, D_MODEL
    return {
        "x_prompt": nrm(ks[0], (BATCH, SEQ, D), 1.0),
        "x_sample": nrm(ks[1], (DEC_BATCH, DEC_SEQ, D), 1.0),
        "c_prompt": nrm(ks[2], (BATCH, D), 1.0),
        "c_sample": nrm(ks[3], (DEC_BATCH, D), 1.0),
        "state_gla": nrm(ks[4], (L, DEC_BATCH, GLA_HEADS, GLA_DK, GLA_DV), 1.0),
        "state_pool": nrm(ks[5], (L, DEC_BATCH, POOL_BUF, POOL_WIDTH), 1.0),
        "w_ada": nrm(ks[6], (L, D, 6 * D), 0.5 * D ** -0.5),
        "b_ada": nrm(ks[7], (L, 6 * D), 0.02),
        "norm1": 1.0 + nrm(ks[8], (L, D), 0.02),
        "norm2": 1.0 + nrm(ks[9], (L, D), 0.02),
        "w_in": nrm(ks[10], (L, D, IN_WIDTH), D ** -0.5),
        "gate_up": nrm(ks[11], (L, GATE_RANK, GLA_KEY_WIDTH), GATE_RANK ** -0.5),
        "gate_bias": nrm(ks[12], (L, GLA_KEY_WIDTH), 0.02),
        "gla_norm": 1.0 + nrm(ks[13], (L, GLA_HEADS * GLA_DV), 0.02),
        "pool_w": nrm(ks[14], (L, POOL_GROUPS, POOL_GW, POOL_GW), POOL_GW ** -0.5),
        "pool_scale": 1.0 + nrm(ks[15], (L, POOL_WIDTH), 0.02),
        "w_out": nrm(ks[16], (L, MIX_WIDTH, D), MIX_WIDTH ** -0.5),
        "w_group": nrm(ks[17], (L, D, N_GROUPS), D ** -0.5),
        "b_group": nrm(ks[18], (L, N_GROUPS), 0.01),
        "w_expert": nrm(ks[19], (L, N_GROUPS, D, EXPERTS_PER_GROUP), D ** -0.5),
        "b_expert": nrm(ks[20], (L, N_GROUPS, EXPERTS_PER_GROUP), 0.01),
        "w1": nrm(ks[21], (L, N_EXPERTS, D, EXPERT_FF), D ** -0.5),
        "w3": nrm(ks[22], (L, N_EXPERTS, D, EXPERT_FF), D ** -0.5),
        "w2": nrm(ks[23], (L, N_EXPERTS, EXPERT_FF, D), EXPERT_FF ** -0.5),
        "norm_f": 1.0 + nrm(ks[24], (D,), 0.02),
    }


def reference(x_prompt, x_sample, c_prompt, c_sample, state_gla, state_pool, w_ada, b_ada, norm1, norm2,
              w_in, gate_up, gate_bias, gla_norm, pool_w, pool_scale, w_out, w_group, b_group, w_expert,
              b_expert, w1, w3, w2, norm_f):
    xp, xs = x_prompt, x_sample
    bp = x_prompt.shape[0]
    gla_p, pool_p, gla_s, pool_s = [], [], [], []
    for l in range(DEPTH):
        lw = (w_ada[l], b_ada[l], norm1[l], norm2[l], w_in[l], gate_up[l], gate_bias[l], gla_norm[l],
              pool_w[l], pool_scale[l], w_out[l], w_group[l], b_group[l], w_expert[l], b_expert[l],
              w1[l], w3[l], w2[l])
        S0p = jnp.zeros((bp, GLA_HEADS, GLA_DK, GLA_DV), state_gla.dtype)
        buf0p = jnp.zeros((bp, POOL_BUF, POOL_WIDTH), state_pool.dtype)
        xp, Sp, bufp = layer(xp, c_prompt, S0p, buf0p, 0, *lw)
        xs, Ss, bufs = layer(xs, c_sample, state_gla[l], state_pool[l], PAST_LEN, *lw)
        gla_p.append(Sp)
        pool_p.append(bufp)
        gla_s.append(Ss)
        pool_s.append(bufs)
    y_prompt = rmsnorm(xp, norm_f)
    y_sample = rmsnorm(xs, norm_f)
    new_gla_prompt = jnp.stack(gla_p)
    new_pool_prompt = jnp.stack(pool_p)
    new_gla_sample = jnp.stack(gla_s)
    new_pool_sample = jnp.stack(pool_s)
    return (y_prompt, y_sample, new_gla_prompt, new_pool_prompt, new_gla_sample, new_pool_sample)
```

```python
import functools

import jax
import jax.numpy as jnp
from jax import lax
from jax.experimental import pallas as pl
from jax.experimental.pallas import tpu as pltpu

D_MODEL = 2048
GLA_HEADS = 4
GLA_DK = 128
GLA_DV = 256
GLA_KEY_WIDTH = GLA_HEADS * GLA_DK
GLA_WIDTH = GLA_HEADS * GLA_DV
POOL_WIDTH = 1024
POOL_WINDOWS = (2, 4, 8, 16)
POOL_GW = 256
POOL_BUF = 15
HALO = 16
GATE_RANK = 16
GATE_TEMP = 16.0
N_GROUPS = 4
EXPERTS_PER_GROUP = 8
N_EXPERTS = 32
EXPERT_FF = 512
EPS = 1e-6
PAST_LEN = 16384
QKVR_WIDTH = 2 * GLA_KEY_WIDTH + 2 * GLA_WIDTH

LANES = 128
TM = 256
TME = 256
GLA_CHUNK = 64
GLA_SUB = 16
GLA_STEP = 512
VMEM_LIMIT = 56 * 1024 * 1024

BF16 = jnp.bfloat16
F32 = jnp.float32
NEG = -1e30


def _cparams(n_axes):
    return pltpu.CompilerParams(dimension_semantics=("arbitrary",) * n_axes,
                                vmem_limit_bytes=VMEM_LIMIT)


def _silu(x):
    return x / (1.0 + jnp.exp(-x))


def _bdot(a, b):
    return jnp.dot(a.astype(BF16), b.astype(BF16), preferred_element_type=F32)


def _split3(a):
    a1 = a.astype(BF16)
    r1 = a - a1.astype(F32)
    a2 = r1.astype(BF16)
    a3 = (r1 - a2.astype(F32)).astype(BF16)
    return a1, a2, a3


def _dot_f32(a, b):
    a1, a2, a3 = _split3(a)
    b1, b2, b3 = _split3(b)
    d = lambda x, y: jnp.dot(x, y, preferred_element_type=F32)
    small = d(a2, b2) + d(a1, b3) + d(a3, b1)
    mid = d(a1, b2) + d(a2, b1)
    return d(a1, b1) + (mid + small)


def _dot_exact_lhs(tri_bf16, g):
    g1, g2, g3 = _split3(g)
    d = lambda y: jnp.dot(tri_bf16, y, preferred_element_type=F32)
    return d(g1) + (d(g2) + d(g3))


def _adaln_kernel(c_ref, w_ref, b_ref, o_ref):
    c = c_ref[...]
    o_ref[...] = _bdot(_silu(c), w_ref[...]) + b_ref[...]


def _adaln(c_all, w_ada, b_ada):
    n, d = c_all.shape
    width = w_ada.shape[1]
    tn = 1024
    return pl.pallas_call(
        _adaln_kernel,
        grid=(width // tn,),
        in_specs=[pl.BlockSpec((n, d), lambda j: (0, 0)),
                  pl.BlockSpec((d, tn), lambda j: (0, j)),
                  pl.BlockSpec((1, tn), lambda j: (0, j))],
        out_specs=pl.BlockSpec((n, tn), lambda j: (0, j)),
        out_shape=jax.ShapeDtypeStruct((n, width), F32),
        compiler_params=_cparams(1),
    )(c_all, w_ada, b_ada.reshape(1, width))


class _Group:
    def __init__(self, batch, seq, row_off):
        self.batch, self.seq, self.row_off = batch, seq, row_off
        if seq >= TM:
            assert seq % TM == 0
            self.nb, self.tt = 1, TM
            self.tiles_per_batch = seq // TM
            self.n_tiles = batch * self.tiles_per_batch
        else:
            assert TM % seq == 0 and batch % (TM // seq) == 0
            self.nb, self.tt = TM // seq, seq
            self.tiles_per_batch = 1
            self.n_tiles = batch // self.nb
        self.rows = batch * seq
        self.tile_off = row_off // TM

    def x_map(self):
        if self.nb == 1:
            tpb = self.tiles_per_batch
            return lambda i, *_: (i // tpb, i % tpb, 0)
        return lambda i, *_: (i, 0, 0)

    def mod_map(self):
        if self.nb == 1:
            tpb = self.tiles_per_batch
            return lambda i, *_: (i // tpb, 0, 0)
        return lambda i, *_: (i, 0, 0)


def _mod_rows(mod_ref, idx):
    return mod_ref[:, idx:idx + 1, :]


def _rmsnorm_mod(x, gain, scale, shift):
    ms = jnp.mean(x * x, axis=-1, keepdims=True)
    y = x * lax.rsqrt(ms + EPS) * gain
    return y * (1.0 + scale) + shift


def _in_proj_kernel(x_ref, mod_ref, n1_ref, wq_ref, wg_ref, wu_ref, gup_ref, gb_ref,
                    qkvr_ref, la_ref, u_ref):
    x = x_ref[...]
    h = _rmsnorm_mod(x, n1_ref[...], _mod_rows(mod_ref, 1), _mod_rows(mod_ref, 0))
    hb = h.reshape(TM, D_MODEL).astype(BF16)
    qkvr_ref[...] = jnp.dot(hb, wq_ref[...], preferred_element_type=F32)
    u_ref[...] = jnp.dot(hb, wu_ref[...], preferred_element_type=F32)
    g_lr = jnp.dot(hb, wg_ref[...], preferred_element_type=F32)
    pre = jnp.dot(g_lr.astype(BF16), gup_ref[...], preferred_element_type=F32) + gb_ref[...]
    log_sig = jnp.minimum(pre, 0.0) - jnp.log1p(jnp.exp(-jnp.abs(pre)))
    la_ref[...] = log_sig / GATE_TEMP


def _in_proj(grp, x, mod, norm1, wq, wg, wu, gup, gb):
    const = lambda i: (0, 0)
    row = lambda i: (i, 0)
    return pl.pallas_call(
        _in_proj_kernel,
        grid=(grp.n_tiles,),
        in_specs=[pl.BlockSpec((grp.nb, grp.tt, D_MODEL), grp.x_map()),
                  pl.BlockSpec((grp.nb, 6, D_MODEL), grp.mod_map()),
                  pl.BlockSpec((1, 1, D_MODEL), lambda i: (0, 0, 0)),
                  pl.BlockSpec(wq.shape, const, pipeline_mode=pl.Buffered(1)),
                  pl.BlockSpec(wg.shape, const, pipeline_mode=pl.Buffered(1)),
                  pl.BlockSpec(wu.shape, const, pipeline_mode=pl.Buffered(1)),
                  pl.BlockSpec(gup.shape, const, pipeline_mode=pl.Buffered(1)),
                  pl.BlockSpec(gb.shape, const, pipeline_mode=pl.Buffered(1))],
        out_specs=[pl.BlockSpec((TM, QKVR_WIDTH), row),
                   pl.BlockSpec((TM, GLA_KEY_WIDTH), row),
                   pl.BlockSpec((TM, POOL_WIDTH), row)],
        out_shape=[jax.ShapeDtypeStruct((grp.rows, QKVR_WIDTH), F32),
                   jax.ShapeDtypeStruct((grp.rows, GLA_KEY_WIDTH), F32),
                   jax.ShapeDtypeStruct((grp.rows, POOL_WIDTH), F32)],
        compiler_params=_cparams(1),
    )(x, mod, norm1.reshape(1, 1, D_MODEL), wq, wg, wu, gup, gb)


def _gla_head_chunk(q, k, v, g, state_t, chunk, sub):
    rows = lax.broadcasted_iota(jnp.int32, (chunk, chunk), 0)
    cols = lax.broadcasted_iota(jnp.int32, (chunk, chunk), 1)
    tri = (rows >= cols).astype(BF16)
    b = _dot_exact_lhs(tri, g)
    q = q * (GLA_DK ** -0.5)
    nt = (((1,), (1,)), ((), ()))
    tn = (((0,), (0,)), ((), ()))
    o = lax.dot_general((q * jnp.exp(b)).astype(BF16), state_t.astype(BF16), nt,
                        preferred_element_type=F32)

    b_last = b[chunk - 1:chunk, :]
    k_dec = k * jnp.exp(b_last - b)
    new_state_t = jnp.exp(b_last) * state_t + lax.dot_general(
        v.astype(BF16), k_dec.astype(BF16), tn, preferred_element_type=F32)

    lane = lax.broadcasted_iota(jnp.int32, (sub, chunk), 1)
    sub_row = lax.broadcasted_iota(jnp.int32, (sub, LANES), 0)
    key_row = lax.broadcasted_iota(jnp.int32, (chunk, LANES), 0)
    p_blocks = []
    for s in range(chunk // sub):
        lo = s * sub
        q_s = q[lo:lo + sub, :]
        b_s = b[lo:lo + sub, :]
        p = jnp.zeros((sub, chunk), F32)
        if s > 0:
            ref_row = b[lo - 1:lo, :]
            q_rel = q_s * jnp.exp(b_s - ref_row)
            k_rel = k * jnp.exp(jnp.where(key_row < lo, ref_row - b, NEG))
            p = lax.dot_general(q_rel.astype(BF16), k_rel.astype(BF16), nt, preferred_element_type=F32)
        for jl in range(sub):
            j = lo + jl
            expo = jnp.where(sub_row >= jl, b_s - b[j:j + 1, :], NEG)
            col = jnp.sum(q_s * k[j:j + 1, :] * jnp.exp(expo), axis=-1, keepdims=True)
            p = jnp.where(lane == j, col, p)
        p_blocks.append(p)
    p_full = p_blocks[0] if len(p_blocks) == 1 else jnp.concatenate(p_blocks, axis=0)
    o = o + _bdot(p_full, v)
    return o, new_state_t


def _gla_prompt_kernel(q_ref, k_ref, v_ref, la_ref, o_ref, s_ref, st_ref):
    step = pl.program_id(2)

    @pl.when(step == 0)
    def _():
        st_ref[...] = jnp.zeros_like(st_ref)

    def body(c, carry):
        r0 = pl.multiple_of(c * GLA_CHUNK, GLA_CHUNK)
        sl = pl.ds(r0, GLA_CHUNK)
        o, new_state_t = _gla_head_chunk(q_ref[sl, :], k_ref[sl, :], v_ref[sl, :], la_ref[sl, :],
                                         st_ref[...], GLA_CHUNK, GLA_SUB)
        o_ref[sl, :] = o
        st_ref[...] = new_state_t
        return carry

    lax.fori_loop(0, GLA_STEP // GLA_CHUNK, body, 0)

    @pl.when(step == pl.num_programs(2) - 1)
    def _():
        s_ref[0, 0] = st_ref[...].T


def _gla_prompt(batch, seq, qkvr, log_a):
    steps = seq // GLA_STEP
    row = lambda b, h, s: b * steps + s
    return pl.pallas_call(
        _gla_prompt_kernel,
        grid=(batch, GLA_HEADS, steps),
        in_specs=[pl.BlockSpec((GLA_STEP, GLA_DK), lambda b, h, s: (row(b, h, s), h)),
                  pl.BlockSpec((GLA_STEP, GLA_DK), lambda b, h, s: (row(b, h, s), GLA_HEADS + h)),
                  pl.BlockSpec((GLA_STEP, GLA_DV), lambda b, h, s: (row(b, h, s), GLA_HEADS + h)),
                  pl.BlockSpec((GLA_STEP, GLA_DK), lambda b, h, s: (row(b, h, s), h))],
        out_specs=[pl.BlockSpec((GLA_STEP, GLA_DV), lambda b, h, s: (row(b, h, s), h)),
                   pl.BlockSpec((1, 1, GLA_DK, GLA_DV), lambda b, h, s: (b, h, 0, 0))],
        out_shape=[jax.ShapeDtypeStruct((batch * seq, GLA_WIDTH), F32),
                   jax.ShapeDtypeStruct((batch, GLA_HEADS, GLA_DK, GLA_DV), F32)],
        scratch_shapes=[pltpu.VMEM((GLA_DV, GLA_DK), F32)],
        compiler_params=_cparams(3),
    )(qkvr, qkvr, qkvr, log_a)


GLA_DEC_BB = 16


def _gla_decode_kernel(seq, q_ref, k_ref, v_ref, la_ref, s0_ref, o_ref, s_ref):
    def body(i, carry):
        r0 = pl.multiple_of(i * seq, seq)
        sl = pl.ds(r0, seq)
        o, new_state_t = _gla_head_chunk(q_ref[sl, :], k_ref[sl, :], v_ref[sl, :], la_ref[sl, :],
                                         s0_ref[i, 0].T, seq, seq)
        o_ref[sl, :] = o
        s_ref[i, 0] = new_state_t.T
        return carry

    lax.fori_loop(0, GLA_DEC_BB, body, 0)


def _gla_decode(batch, seq, qkvr, log_a, state):
    rows = GLA_DEC_BB * seq
    return pl.pallas_call(
        functools.partial(_gla_decode_kernel, seq),
        grid=(batch // GLA_DEC_BB, GLA_HEADS),
        in_specs=[pl.BlockSpec((rows, GLA_DK), lambda i, h: (i, h)),
                  pl.BlockSpec((rows, GLA_DK), lambda i, h: (i, GLA_HEADS + h)),
                  pl.BlockSpec((rows, GLA_DV), lambda i, h: (i, GLA_HEADS + h)),
                  pl.BlockSpec((rows, GLA_DK), lambda i, h: (i, h)),
                  pl.BlockSpec((GLA_DEC_BB, 1, GLA_DK, GLA_DV), lambda i, h: (i, h, 0, 0))],
        out_specs=[pl.BlockSpec((rows, GLA_DV), lambda i, h: (i, h)),
                   pl.BlockSpec((GLA_DEC_BB, 1, GLA_DK, GLA_DV), lambda i, h: (i, h, 0, 0))],
        out_shape=[jax.ShapeDtypeStruct((batch * seq, GLA_WIDTH), F32),
                   jax.ShapeDtypeStruct((batch, GLA_HEADS, GLA_DK, GLA_DV), F32)],
        compiler_params=_cparams(2),
    )(qkvr, qkvr, qkvr, log_a, state)


def _route(logits):
    lane = lax.broadcasted_iota(jnp.int32, logits.shape, 1)
    big = jnp.int32(10 ** 6)
    is_group = lane < N_GROUPS
    lg = jnp.where(is_group, logits, NEG)
    mg = jnp.max(lg, axis=-1, keepdims=True)
    g_idx = jnp.min(jnp.where(is_group & (lg == mg), lane, big), axis=-1, keepdims=True)
    denom = jnp.sum(jnp.where(is_group, jnp.exp(lg - mg), 0.0), axis=-1, keepdims=True)
    p_sel = 1.0 / denom
    first = N_GROUPS + EXPERTS_PER_GROUP * g_idx
    in_grp = (lane >= first) & (lane < first + EXPERTS_PER_GROUP)
    le = jnp.where(in_grp, logits, NEG)
    m1 = jnp.max(le, axis=-1, keepdims=True)
    i1 = jnp.min(jnp.where(in_grp & (le == m1), lane, big), axis=-1, keepdims=True)
    rest = in_grp & (lane != i1)
    le2 = jnp.where(rest, logits, NEG)
    m2 = jnp.max(le2, axis=-1, keepdims=True)
    i2 = jnp.min(jnp.where(rest & (le2 == m2), lane, big), axis=-1, keepdims=True)
    e2 = jnp.exp(m2 - m1)
    w1 = p_sel / (1.0 + e2)
    w2 = p_sel * e2 / (1.0 + e2)
    ex1 = (i1 - N_GROUPS).astype(F32)
    ex2 = (i2 - N_GROUPS).astype(F32)
    return jnp.where(lane == 0, ex1, jnp.where(lane == 1, ex2,
                     jnp.where(lane == 2, w1, jnp.where(lane == 3, w2, 0.0))))


def _mix_out_kernel(grp, pos0, zero_first_halo,
                    o_ref, r_ref, u_ref, halo_ref, x_ref, mod_ref, n2_ref, gn_ref, pw_ref, ps_ref,
                    wo_ref, wr_ref, br_ref, h2_in_ref, rt_in_ref,
                    x1_ref, h2_ref, rt_ref, ext_ref, ymix_ref):
    del h2_in_ref, rt_in_ref
    nb, tt = grp.nb, grp.tt
    i = pl.program_id(0)

    for h in range(GLA_HEADS):
        cs = slice(h * GLA_DV, (h + 1) * GLA_DV)
        oh = o_ref[:, cs]
        ms = jnp.mean(oh * oh, axis=-1, keepdims=True)
        yh = oh * lax.rsqrt(ms + EPS) * gn_ref[:, cs] * _silu(r_ref[:, cs])
        ymix_ref[:, cs] = yh.astype(BF16)

    halo = halo_ref[...]
    if zero_first_halo:
        halo = jnp.where(i % grp.tiles_per_batch == 0, 0.0, halo)
    ext_ref[:, 0:HALO, :] = halo.reshape(nb, HALO, POOL_WIDTH)
    u = u_ref[...].reshape(nb, tt, POOL_WIDTH)
    ext_ref[:, HALO:HALO + tt, :] = u
    t_idx = lax.broadcasted_iota(jnp.int32, (nb, tt, POOL_GW), 1)
    if grp.nb == 1:
        pos = (i % grp.tiles_per_batch) * TM + t_idx + pos0
    else:
        pos = t_idx + pos0
    for gi, w in enumerate(POOL_WINDOWS):
        cs = slice(gi * POOL_GW, (gi + 1) * POOL_GW)
        acc = ext_ref[:, pl.ds(HALO, tt), cs]
        for kk in range(1, w):
            acc = acc + ext_ref[:, pl.ds(HALO - kk, tt), cs]
        cnt = jnp.minimum(pos + 1, w).astype(F32)
        pooled = acc / cnt - u[:, :, cs]
        yp = _bdot(pooled.reshape(TM, POOL_GW), pw_ref[gi]) * ps_ref[:, cs]
        ymix_ref[:, GLA_WIDTH + gi * POOL_GW:GLA_WIDTH + (gi + 1) * POOL_GW] = yp.astype(BF16)

    y = jnp.dot(ymix_ref[...], wo_ref[...], preferred_element_type=F32)
    x1 = x_ref[...] + _mod_rows(mod_ref, 2) * y.reshape(nb, tt, D_MODEL)
    x1_ref[...] = x1
    h2 = _rmsnorm_mod(x1, n2_ref[...], _mod_rows(mod_ref, 4), _mod_rows(mod_ref, 3)).reshape(TM, D_MODEL)
    h2_ref[...] = h2
    logits = _dot_f32(h2, wr_ref[...]) + br_ref[...]
    rt_ref[...] = _route(logits)


def _mix_out(grp, pos0, zero_first_halo, o, qkvr, u, halo_src, halo_spec, x, mod, norm2, gla_norm,
             pool_w, pool_scale, w_out, w_router, b_router, h2_all, route_all):
    const2 = lambda i: (0, 0)
    row = lambda i: (i, 0)
    off = grp.tile_off
    kern = functools.partial(_mix_out_kernel, grp, pos0, zero_first_halo)
    return pl.pallas_call(
        kern,
        grid=(grp.n_tiles,),
        in_specs=[pl.BlockSpec((TM, GLA_WIDTH), row),
                  pl.BlockSpec((TM, GLA_WIDTH), lambda i: (i, 2)),
                  pl.BlockSpec((TM, POOL_WIDTH), row),
                  halo_spec,
                  pl.BlockSpec((grp.nb, grp.tt, D_MODEL), grp.x_map()),
                  pl.BlockSpec((grp.nb, 6, D_MODEL), grp.mod_map()),
                  pl.BlockSpec((1, 1, D_MODEL), lambda i: (0, 0, 0)),
                  pl.BlockSpec((1, GLA_WIDTH), const2),
                  pl.BlockSpec(pool_w.shape, lambda i: (0, 0, 0), pipeline_mode=pl.Buffered(1)),
                  pl.BlockSpec((1, POOL_WIDTH), const2),
                  pl.BlockSpec(w_out.shape, const2, pipeline_mode=pl.Buffered(1)),
                  pl.BlockSpec(w_router.shape, const2, pipeline_mode=pl.Buffered(1)),
                  pl.BlockSpec((1, LANES), const2),
                  pl.BlockSpec(memory_space=pl.ANY),
                  pl.BlockSpec(memory_space=pl.ANY)],
        out_specs=[pl.BlockSpec((grp.nb, grp.tt, D_MODEL), grp.x_map()),
                   pl.BlockSpec((TM, D_MODEL), lambda i: (i + off, 0)),
                   pl.BlockSpec((TM, LANES), lambda i: (i + off, 0))],
        out_shape=[jax.ShapeDtypeStruct(x.shape, F32),
                   jax.ShapeDtypeStruct(h2_all.shape, F32),
                   jax.ShapeDtypeStruct(route_all.shape, F32)],
        scratch_shapes=[pltpu.VMEM((grp.nb, HALO + grp.tt, POOL_WIDTH), F32),
                        pltpu.VMEM((TM, D_MODEL), BF16)],
        input_output_aliases={13: 1, 14: 2},
        compiler_params=_cparams(1),
    )(o, qkvr, u, halo_src, x, mod, norm2.reshape(1, 1, D_MODEL), gla_norm.reshape(1, GLA_WIDTH),
      pool_w, pool_scale.reshape(1, POOL_WIDTH), w_out, w_router, b_router, h2_all, route_all)


def _row_copy(src_hbm, src_row, dst, dst_row, sem):
    return pltpu.make_async_copy(src_hbm.at[pl.ds(src_row, 1), :], dst.at[pl.ds(dst_row, 1), :], sem)


def _moe_kernel(te_ref, src_ref, nu_ref, h_hbm, wrow_ref, w1_ref, w3_ref, w2_ref, y_ref, buf, sem):
    i = pl.program_id(0)

    @pl.when(i < nu_ref[0])
    def _():
        base = i * TME

        def issue(r, carry):
            _row_copy(h_hbm, src_ref[base + r], buf, r, sem).start()
            return carry

        lax.fori_loop(0, TME, issue, 0)

        def drain(r, carry):
            _row_copy(h_hbm, 0, buf, r, sem).wait()
            return carry

        lax.fori_loop(0, TME, drain, 0)

        hb = buf[...].astype(BF16)
        a = jnp.dot(hb, w1_ref[...].astype(BF16), preferred_element_type=F32)
        b = jnp.dot(hb, w3_ref[...].astype(BF16), preferred_element_type=F32)
        hid = _silu(a) * b * wrow_ref[...]
        y_ref[...] = jnp.dot(hid.astype(BF16), w2_ref[...].astype(BF16), preferred_element_type=F32)

    @pl.when(i >= nu_ref[0])
    def _():
        y_ref[...] = jnp.zeros_like(y_ref)


def _moe(tile_expert, src_row, n_used, h2_all, w_sorted, w1, w3, w2):
    n_sorted = src_row.shape[0]
    n_tiles = n_sorted // TME
    grid_spec = pltpu.PrefetchScalarGridSpec(
        num_scalar_prefetch=3,
        grid=(n_tiles,),
        in_specs=[pl.BlockSpec(memory_space=pl.ANY),
                  pl.BlockSpec((TME, 1), lambda i, te, sr, nu: (i, 0)),
                  pl.BlockSpec((None, D_MODEL, EXPERT_FF), lambda i, te, sr, nu: (te[i], 0, 0)),
                  pl.BlockSpec((None, D_MODEL, EXPERT_FF), lambda i, te, sr, nu: (te[i], 0, 0)),
                  pl.BlockSpec((None, EXPERT_FF, D_MODEL), lambda i, te, sr, nu: (te[i], 0, 0))],
        out_specs=pl.BlockSpec((TME, D_MODEL), lambda i, te, sr, nu: (i, 0)),
        scratch_shapes=[pltpu.VMEM((TME, D_MODEL), F32), pltpu.SemaphoreType.DMA(())],
    )
    return pl.pallas_call(
        _moe_kernel,
        grid_spec=grid_spec,
        out_shape=jax.ShapeDtypeStruct((n_sorted, D_MODEL), F32),
        compiler_params=_cparams(1),
    )(tile_expert, src_row, n_used, h2_all, w_sorted, w1, w3, w2)


def _finish_kernel(grp, pos_ref, x1_ref, mod_ref, nf_ref, y_hbm, out_ref, buf_a, buf_b, sem):
    i = pl.program_id(0)
    base = (i * TM + grp.row_off) * 2

    def issue(r, carry):
        _row_copy(y_hbm, pos_ref[base + 2 * r], buf_a, r, sem).start()
        _row_copy(y_hbm, pos_ref[base + 2 * r + 1], buf_b, r, sem).start()
        return carry

    lax.fori_loop(0, TM, issue, 0)

    def drain(r, carry):
        _row_copy(y_hbm, 0, buf_a, r, sem).wait()
        _row_copy(y_hbm, 0, buf_b, r, sem).wait()
        return carry

    lax.fori_loop(0, TM, drain, 0)

    moe = (buf_a[...] + buf_b[...]).reshape(grp.nb, grp.tt, D_MODEL)
    x2 = x1_ref[...] + _mod_rows(mod_ref, 5) * moe
    ms = jnp.mean(x2 * x2, axis=-1, keepdims=True)
    out_ref[...] = x2 * lax.rsqrt(ms + EPS) * nf_ref[...]


def _finish(grp, pos, x1, mod, norm_f, y_sorted):
    grid_spec = pltpu.PrefetchScalarGridSpec(
        num_scalar_prefetch=1,
        grid=(grp.n_tiles,),
        in_specs=[pl.BlockSpec((grp.nb, grp.tt, D_MODEL), grp.x_map()),
                  pl.BlockSpec((grp.nb, 6, D_MODEL), grp.mod_map()),
                  pl.BlockSpec((1, 1, D_MODEL), lambda i, p: (0, 0, 0)),
                  pl.BlockSpec(memory_space=pl.ANY)],
        out_specs=pl.BlockSpec((grp.nb, grp.tt, D_MODEL), grp.x_map()),
        scratch_shapes=[pltpu.VMEM((TM, D_MODEL), F32), pltpu.VMEM((TM, D_MODEL), F32),
                        pltpu.SemaphoreType.DMA(())],
    )
    return pl.pallas_call(
        functools.partial(_finish_kernel, grp),
        grid_spec=grid_spec,
        out_shape=jax.ShapeDtypeStruct(x1.shape, F32),
        compiler_params=_cparams(1),
    )(pos, x1, mod, norm_f.reshape(1, 1, D_MODEL), y_sorted)


def _sort_plan(route_all):
    n_tok = route_all.shape[0]
    n_pairs = 2 * n_tok
    n_sorted = n_pairs + N_EXPERTS * TME
    flat_e = route_all[:, 0:2].astype(jnp.int32).reshape(n_pairs)
    flat_w = route_all[:, 2:4].reshape(n_pairs)
    onehot = (flat_e[:, None] == jnp.arange(N_EXPERTS, dtype=jnp.int32)[None, :]).astype(jnp.int32)
    csum = jnp.cumsum(onehot, axis=0)
    rank = jnp.sum(onehot * csum, axis=1) - 1
    counts = csum[-1]
    padded = ((counts + TME - 1) // TME) * TME
    ends = jnp.cumsum(padded)
    starts = ends - padded
    pos = starts[flat_e] + rank
    token = jnp.arange(n_pairs, dtype=jnp.int32) // 2
    src_row = jnp.zeros((n_sorted,), jnp.int32).at[pos].set(token)
    w_sorted = jnp.zeros((n_sorted,), F32).at[pos].set(flat_w)
    tile_start = jnp.arange(n_sorted // TME, dtype=jnp.int32) * TME
    tile_expert = jnp.minimum(jnp.searchsorted(ends, tile_start, side="right"), N_EXPERTS - 1).astype(jnp.int32)
    n_used = (ends[-1] // TME).astype(jnp.int32).reshape(1)
    return pos.astype(jnp.int32), src_row, w_sorted.reshape(n_sorted, 1), tile_expert, n_used


def kernel(x_prompt, x_sample, c_prompt, c_sample, state_gla, state_pool, w_ada, b_ada, norm1, norm2, w_in,
           gate_up, gate_bias, gla_norm, pool_w, pool_scale, w_out, w_group, b_group, w_expert, b_expert,
           w1, w3, w2, norm_f):
    assert w_ada.shape[0] == 1, "single-layer step"
    bp, tp, _ = x_prompt.shape
    bs, ts, _ = x_sample.shape
    grp_p = _Group(bp, tp, 0)
    grp_s = _Group(bs, ts, bp * tp)
    n_tok = bp * tp + bs * ts

    n_c = bp + bs
    n_c_pad = -(-n_c // 8) * 8
    c_all = jnp.concatenate([c_prompt, c_sample, jnp.zeros((n_c_pad - n_c, D_MODEL), F32)], axis=0)
    mod = _adaln(c_all, w_ada[0], b_ada[0]).reshape(n_c_pad, 6, D_MODEL)
    mod_p, mod_s = mod[:bp], mod[bp:bp + bs]

    wi = w_in[0]
    wq = wi[:, :QKVR_WIDTH].astype(BF16)
    wg = jnp.pad(wi[:, QKVR_WIDTH:QKVR_WIDTH + GATE_RANK], ((0, 0), (0, LANES - GATE_RANK))).astype(BF16)
    wu = wi[:, QKVR_WIDTH + GATE_RANK:].astype(BF16)
    gup = jnp.pad(gate_up[0], ((0, LANES - GATE_RANK), (0, 0))).astype(BF16)
    gb = gate_bias[0].reshape(1, GLA_KEY_WIDTH)
    pw = pool_w[0].astype(BF16)
    wo = w_out[0].astype(BF16)
    w_router = jnp.concatenate(
        [w_group[0], jnp.transpose(w_expert[0], (1, 0, 2)).reshape(D_MODEL, N_EXPERTS),
         jnp.zeros((D_MODEL, LANES - N_GROUPS - N_EXPERTS), F32)], axis=1)
    b_router = jnp.concatenate([b_group[0], b_expert[0].reshape(N_EXPERTS),
                                jnp.zeros((LANES - N_GROUPS - N_EXPERTS,), F32)]).reshape(1, LANES)

    qkvr_p, la_p, u_p = _in_proj(grp_p, x_prompt, mod_p, norm1[0], wq, wg, wu, gup, gb)
    qkvr_s, la_s, u_s = _in_proj(grp_s, x_sample, mod_s, norm1[0], wq, wg, wu, gup, gb)

    o_p, gla_p = _gla_prompt(bp, tp, qkvr_p, la_p)
    o_s, gla_s = _gla_decode(bs, ts, qkvr_s, la_s, state_gla[0])

    h2_all = jnp.zeros((n_tok, D_MODEL), F32)
    route_all = jnp.zeros((n_tok, LANES), F32)
    halo_per_tile = TM // HALO
    halo_spec_p = pl.BlockSpec((HALO, POOL_WIDTH), lambda i: (jnp.maximum(i * halo_per_tile - 1, 0), 0))
    x1_p, h2_all, route_all = _mix_out(grp_p, 0, True, o_p, qkvr_p, u_p, u_p, halo_spec_p, x_prompt, mod_p,
                                       norm2[0], gla_norm[0], pw, pool_scale[0], wo, w_router, b_router,
                                       h2_all, route_all)
    hist_s = jnp.pad(state_pool[0], ((0, 0), (HALO - POOL_BUF, 0), (0, 0)))
    halo_spec_s = pl.BlockSpec((grp_s.nb, HALO, POOL_WIDTH), lambda i: (i, 0, 0))
    x1_s, h2_all, route_all = _mix_out(grp_s, PAST_LEN, False, o_s, qkvr_s, u_s, hist_s, halo_spec_s, x_sample,
                                       mod_s, norm2[0], gla_norm[0], pw, pool_scale[0], wo, w_router, b_router,
                                       h2_all, route_all)

    pos, src_row, w_sorted, tile_expert, n_used = _sort_plan(route_all)
    y_sorted = _moe(tile_expert, src_row, n_used, h2_all, w_sorted, w1[0], w3[0], w2[0])

    y_p = _finish(grp_p, pos, x1_p, mod_p, norm_f, y_sorted)
    y_s = _finish(grp_s, pos, x1_s, mod_s, norm_f, y_sorted)

    u_p3 = u_p.reshape(bp, tp, POOL_WIDTH)
    u_s3 = u_s.reshape(bs, ts, POOL_WIDTH)
    pool_p = jnp.concatenate([jnp.zeros((bp, POOL_BUF, POOL_WIDTH), F32), u_p3], axis=1)[:, -POOL_BUF:]
    pool_s = jnp.concatenate([state_pool[0], u_s3], axis=1)[:, -POOL_BUF:]
    return (y_p, y_s, gla_p[None], pool_p[None], gla_s[None], pool_s[None])
```

```python
import functools

import jax
import jax.numpy as jnp
from jax import lax
from jax.experimental import pallas as pl
from jax.experimental.pallas import tpu as pltpu

D_MODEL = 2048
GLA_HEADS = 4
GLA_DK = 128
GLA_DV = 256
GLA_KEY_WIDTH = GLA_HEADS * GLA_DK
GLA_WIDTH = GLA_HEADS * GLA_DV
POOL_WIDTH = 1024
POOL_WINDOWS = (2, 4, 8, 16)
POOL_GW = 256
POOL_BUF = 15
HALO = 16
GATE_RANK = 16
GATE_TEMP = 16.0
N_GROUPS = 4
EXPERTS_PER_GROUP = 8
N_EXPERTS = 32
EXPERT_FF = 512
EPS = 1e-6
PAST_LEN = 16384
QKVR_WIDTH = 2 * GLA_KEY_WIDTH + 2 * GLA_WIDTH

LANES = 128
TM = 256
TME = 256
GLA_CHUNK = 64
GLA_SUB = 16
GLA_STEP = 512
VMEM_LIMIT = 56 * 1024 * 1024

BF16 = jnp.bfloat16
F32 = jnp.float32
NEG = -1e30


def _cparams(n_axes):
    return pltpu.CompilerParams(dimension_semantics=("arbitrary",) * n_axes,
                                vmem_limit_bytes=VMEM_LIMIT)


def _silu(x):
    return x / (1.0 + jnp.exp(-x))


def _bdot(a, b):
    return jnp.dot(a.astype(BF16), b.astype(BF16), preferred_element_type=F32)


def _split3(a):
    a1 = a.astype(BF16)
    r1 = a - a1.astype(F32)
    a2 = r1.astype(BF16)
    a3 = (r1 - a2.astype(F32)).astype(BF16)
    return a1, a2, a3


def _dot_f32(a, b):
    a1, a2, a3 = _split3(a)
    b1, b2, b3 = _split3(b)
    d = lambda x, y: jnp.dot(x, y, preferred_element_type=F32)
    small = d(a2, b2) + d(a1, b3) + d(a3, b1)
    mid = d(a1, b2) + d(a2, b1)
    return d(a1, b1) + (mid + small)


def _dot_exact_lhs(tri_bf16, g):
    g1, g2, g3 = _split3(g)
    d = lambda y: jnp.dot(tri_bf16, y, preferred_element_type=F32)
    return d(g1) + (d(g2) + d(g3))


def _adaln_kernel(c_ref, w_ref, b_ref, o_ref):
    c = c_ref[...]
    o_ref[...] = _bdot(_silu(c), w_ref[...]) + b_ref[...]


def _adaln(c_all, w_ada, b_ada):
    n, d = c_all.shape
    width = w_ada.shape[1]
    tn = 1024
    return pl.pallas_call(
        _adaln_kernel,
        name="adaln",
        grid=(width // tn,),
        in_specs=[pl.BlockSpec((n, d), lambda j: (0, 0)),
                  pl.BlockSpec((d, tn), lambda j: (0, j)),
                  pl.BlockSpec((1, tn), lambda j: (0, j))],
        out_specs=pl.BlockSpec((n, tn), lambda j: (0, j)),
        out_shape=jax.ShapeDtypeStruct((n, width), F32),
        compiler_params=_cparams(1),
    )(c_all, w_ada, b_ada.reshape(1, width))


class _Group:
    def __init__(self, batch, seq, row_off, mod_off):
        self.batch, self.seq, self.row_off, self.mod_off = batch, seq, row_off, mod_off
        if seq >= TM:
            assert seq % TM == 0
            self.nb, self.tt = 1, TM
            self.tiles_per_batch = seq // TM
            self.n_tiles = batch * self.tiles_per_batch
        else:
            assert TM % seq == 0 and batch % (TM // seq) == 0
            self.nb, self.tt = TM // seq, seq
            self.tiles_per_batch = 1
            self.n_tiles = batch // self.nb
        self.rows = batch * seq
        self.tile_off = row_off // TM

    def x_map(self):
        if self.nb == 1:
            tpb = self.tiles_per_batch
            return lambda i, *_: (i // tpb, i % tpb, 0)
        return lambda i, *_: (i, 0, 0)

    def mod_map(self):
        assert self.mod_off % self.nb == 0
        off = self.mod_off // self.nb
        if self.nb == 1:
            tpb = self.tiles_per_batch
            return lambda i, *_: (i // tpb + off, 0, 0)
        return lambda i, *_: (i + off, 0, 0)


def _mod_rows(mod_ref, idx):
    return mod_ref[:, idx:idx + 1, :]


def _rmsnorm_mod(x, gain, scale, shift):
    ms = jnp.mean(x * x, axis=-1, keepdims=True)
    y = x * lax.rsqrt(ms + EPS) * gain
    return y * (1.0 + scale) + shift


def _in_proj_kernel(x_ref, mod_ref, n1_ref, wq_ref, wg_ref, wu_ref, gup_ref, gb_ref,
                    qkvr_ref, la_ref, u_ref):
    x = x_ref[...]
    h = _rmsnorm_mod(x, n1_ref[...], _mod_rows(mod_ref, 1), _mod_rows(mod_ref, 0))
    hb = h.reshape(TM, D_MODEL).astype(BF16)
    qkvr_ref[...] = jnp.dot(hb, wq_ref[...], preferred_element_type=F32)
    u_ref[...] = jnp.dot(hb, wu_ref[...], preferred_element_type=F32)
    g_lr = jnp.dot(hb, wg_ref[...], preferred_element_type=F32)
    pre = jnp.dot(g_lr.astype(BF16), gup_ref[...], preferred_element_type=F32) + gb_ref[...]
    log_sig = jnp.minimum(pre, 0.0) - jnp.log1p(jnp.exp(-jnp.abs(pre)))
    la_ref[...] = log_sig / GATE_TEMP


def _in_proj(grp, x, mod, norm1, wq, wg, wu, gup, gb):
    const = lambda i: (0, 0)
    row = lambda i: (i, 0)
    return pl.pallas_call(
        _in_proj_kernel,
        name="in_proj",
        grid=(grp.n_tiles,),
        in_specs=[pl.BlockSpec((grp.nb, grp.tt, D_MODEL), grp.x_map()),
                  pl.BlockSpec((grp.nb, 6, D_MODEL), grp.mod_map()),
                  pl.BlockSpec((1, 1, D_MODEL), lambda i: (0, 0, 0)),
                  pl.BlockSpec(wq.shape, const, pipeline_mode=pl.Buffered(1)),
                  pl.BlockSpec(wg.shape, const, pipeline_mode=pl.Buffered(1)),
                  pl.BlockSpec(wu.shape, const, pipeline_mode=pl.Buffered(1)),
                  pl.BlockSpec(gup.shape, const, pipeline_mode=pl.Buffered(1)),
                  pl.BlockSpec(gb.shape, const, pipeline_mode=pl.Buffered(1))],
        out_specs=[pl.BlockSpec((TM, QKVR_WIDTH), row),
                   pl.BlockSpec((TM, GLA_KEY_WIDTH), row),
                   pl.BlockSpec((TM, POOL_WIDTH), row)],
        out_shape=[jax.ShapeDtypeStruct((grp.rows, QKVR_WIDTH), F32),
                   jax.ShapeDtypeStruct((grp.rows, GLA_KEY_WIDTH), F32),
                   jax.ShapeDtypeStruct((grp.rows, POOL_WIDTH), F32)],
        compiler_params=_cparams(1),
    )(x, mod, norm1.reshape(1, 1, D_MODEL), wq, wg, wu, gup, gb)


def _gla_head_chunk(q, k, v, g, state_t, chunk, sub):
    rows = lax.broadcasted_iota(jnp.int32, (chunk, chunk), 0)
    cols = lax.broadcasted_iota(jnp.int32, (chunk, chunk), 1)
    tri = (rows >= cols).astype(BF16)
    b = _dot_exact_lhs(tri, g)
    q = q * (GLA_DK ** -0.5)
    nt = (((1,), (1,)), ((), ()))
    tn = (((0,), (0,)), ((), ()))
    o = lax.dot_general((q * jnp.exp(b)).astype(BF16), state_t.astype(BF16), nt,
                        preferred_element_type=F32)

    b_last = b[chunk - 1:chunk, :]
    k_dec = k * jnp.exp(b_last - b)
    new_state_t = jnp.exp(b_last) * state_t + lax.dot_general(
        v.astype(BF16), k_dec.astype(BF16), tn, preferred_element_type=F32)

    lane = lax.broadcasted_iota(jnp.int32, (sub, chunk), 1)
    sub_row = lax.broadcasted_iota(jnp.int32, (sub, LANES), 0)
    key_row = lax.broadcasted_iota(jnp.int32, (chunk, LANES), 0)
    p_blocks = []
    for s in range(chunk // sub):
        lo = s * sub
        q_s = q[lo:lo + sub, :]
        b_s = b[lo:lo + sub, :]
        p = jnp.zeros((sub, chunk), F32)
        if s > 0:
            ref_row = b[lo - 1:lo, :]
            q_rel = q_s * jnp.exp(b_s - ref_row)
            k_rel = k * jnp.exp(jnp.where(key_row < lo, ref_row - b, NEG))
            p = lax.dot_general(q_rel.astype(BF16), k_rel.astype(BF16), nt, preferred_element_type=F32)
        for jl in range(sub):
            j = lo + jl
            expo = jnp.where(sub_row >= jl, b_s - b[j:j + 1, :], NEG)
            col = jnp.sum(q_s * k[j:j + 1, :] * jnp.exp(expo), axis=-1, keepdims=True)
            p = jnp.where(lane == j, col, p)
        p_blocks.append(p)
    p_full = p_blocks[0] if len(p_blocks) == 1 else jnp.concatenate(p_blocks, axis=0)
    o = o + _bdot(p_full, v)
    return o, new_state_t


def _gla_prompt_kernel(q_ref, k_ref, v_ref, la_ref, o_ref, s_ref, st_ref):
    step = pl.program_id(2)

    @pl.when(step == 0)
    def _():
        st_ref[...] = jnp.zeros_like(st_ref)

    def body(c, carry):
        r0 = pl.multiple_of(c * GLA_CHUNK, GLA_CHUNK)
        sl = pl.ds(r0, GLA_CHUNK)
        o, new_state_t = _gla_head_chunk(q_ref[sl, :], k_ref[sl, :], v_ref[sl, :], la_ref[sl, :],
                                         st_ref[...], GLA_CHUNK, GLA_SUB)
        o_ref[sl, :] = o
        st_ref[...] = new_state_t
        return carry

    lax.fori_loop(0, GLA_STEP // GLA_CHUNK, body, 0)

    @pl.when(step == pl.num_programs(2) - 1)
    def _():
        s_ref[0, 0] = st_ref[...].T


def _gla_prompt(batch, seq, qkvr, log_a):
    steps = seq // GLA_STEP
    row = lambda b, h, s: b * steps + s
    return pl.pallas_call(
        _gla_prompt_kernel,
        name="gla_prompt",
        grid=(batch, GLA_HEADS, steps),
        in_specs=[pl.BlockSpec((GLA_STEP, GLA_DK), lambda b, h, s: (row(b, h, s), h)),
                  pl.BlockSpec((GLA_STEP, GLA_DK), lambda b, h, s: (row(b, h, s), GLA_HEADS + h)),
                  pl.BlockSpec((GLA_STEP, GLA_DV), lambda b, h, s: (row(b, h, s), GLA_HEADS + h)),
                  pl.BlockSpec((GLA_STEP, GLA_DK), lambda b, h, s: (row(b, h, s), h))],
        out_specs=[pl.BlockSpec((GLA_STEP, GLA_DV), lambda b, h, s: (row(b, h, s), h)),
                   pl.BlockSpec((1, 1, GLA_DK, GLA_DV), lambda b, h, s: (b, h, 0, 0))],
        out_shape=[jax.ShapeDtypeStruct((batch * seq, GLA_WIDTH), F32),
                   jax.ShapeDtypeStruct((batch, GLA_HEADS, GLA_DK, GLA_DV), F32)],
        scratch_shapes=[pltpu.VMEM((GLA_DV, GLA_DK), F32)],
        compiler_params=_cparams(3),
    )(qkvr, qkvr, qkvr, log_a)


GLA_DEC_BB = 16


def _gla_decode_kernel(seq, q_ref, k_ref, v_ref, la_ref, s0_ref, o_ref, s_ref):
    def body(i, carry):
        r0 = pl.multiple_of(i * seq, seq)
        sl = pl.ds(r0, seq)
        o, new_state_t = _gla_head_chunk(q_ref[sl, :], k_ref[sl, :], v_ref[sl, :], la_ref[sl, :],
                                         s0_ref[i, 0].T, seq, seq)
        o_ref[sl, :] = o
        s_ref[i, 0] = new_state_t.T
        return carry

    lax.fori_loop(0, GLA_DEC_BB, body, 0)


def _gla_decode(batch, seq, qkvr, log_a, state):
    rows = GLA_DEC_BB * seq
    return pl.pallas_call(
        functools.partial(_gla_decode_kernel, seq),
        name="gla_decode",
        grid=(batch // GLA_DEC_BB, GLA_HEADS),
        in_specs=[pl.BlockSpec((rows, GLA_DK), lambda i, h: (i, h)),
                  pl.BlockSpec((rows, GLA_DK), lambda i, h: (i, GLA_HEADS + h)),
                  pl.BlockSpec((rows, GLA_DV), lambda i, h: (i, GLA_HEADS + h)),
                  pl.BlockSpec((rows, GLA_DK), lambda i, h: (i, h)),
                  pl.BlockSpec((GLA_DEC_BB, 1, GLA_DK, GLA_DV), lambda i, h: (i, h, 0, 0))],
        out_specs=[pl.BlockSpec((rows, GLA_DV), lambda i, h: (i, h)),
                   pl.BlockSpec((GLA_DEC_BB, 1, GLA_DK, GLA_DV), lambda i, h: (i, h, 0, 0))],
        out_shape=[jax.ShapeDtypeStruct((batch * seq, GLA_WIDTH), F32),
                   jax.ShapeDtypeStruct((batch, GLA_HEADS, GLA_DK, GLA_DV), F32)],
        compiler_params=_cparams(2),
    )(qkvr, qkvr, qkvr, log_a, state)


def _route(logits):
    lane = lax.broadcasted_iota(jnp.int32, logits.shape, 1)
    big = jnp.int32(10 ** 6)
    is_group = lane < N_GROUPS
    lg = jnp.where(is_group, logits, NEG)
    mg = jnp.max(lg, axis=-1, keepdims=True)
    g_idx = jnp.min(jnp.where(is_group & (lg == mg), lane, big), axis=-1, keepdims=True)
    denom = jnp.sum(jnp.where(is_group, jnp.exp(lg - mg), 0.0), axis=-1, keepdims=True)
    p_sel = 1.0 / denom
    first = N_GROUPS + EXPERTS_PER_GROUP * g_idx
    in_grp = (lane >= first) & (lane < first + EXPERTS_PER_GROUP)
    le = jnp.where(in_grp, logits, NEG)
    m1 = jnp.max(le, axis=-1, keepdims=True)
    i1 = jnp.min(jnp.where(in_grp & (le == m1), lane, big), axis=-1, keepdims=True)
    rest = in_grp & (lane != i1)
    le2 = jnp.where(rest, logits, NEG)
    m2 = jnp.max(le2, axis=-1, keepdims=True)
    i2 = jnp.min(jnp.where(rest & (le2 == m2), lane, big), axis=-1, keepdims=True)
    e2 = jnp.exp(m2 - m1)
    w1 = p_sel / (1.0 + e2)
    w2 = p_sel * e2 / (1.0 + e2)
    ex1 = (i1 - N_GROUPS).astype(F32)
    ex2 = (i2 - N_GROUPS).astype(F32)
    return jnp.where(lane == 0, ex1, jnp.where(lane == 1, ex2,
                     jnp.where(lane == 2, w1, jnp.where(lane == 3, w2, 0.0))))


def _mix_out_kernel(grp, pos0, zero_first_halo, n_alias,
                    o_ref, r_ref, u_ref, halo_ref, x_ref, mod_ref, n2_ref, gn_ref, pw_ref, ps_ref,
                    wo_ref, wr_ref, br_ref, *rest):
    x1_ref, h2_ref, rt_ref, ext_ref, ymix_ref = rest[n_alias:]
    i = pl.program_id(0)

    @pl.when(i < grp.n_tiles)
    def _():
        _mix_out_tile(grp, pos0, zero_first_halo, i, o_ref, r_ref, u_ref, halo_ref, x_ref, mod_ref, n2_ref,
                      gn_ref, pw_ref, ps_ref, wo_ref, wr_ref, br_ref, x1_ref, h2_ref, rt_ref, ext_ref, ymix_ref)

    @pl.when(i >= grp.n_tiles)
    def _():
        h2_ref[...] = jnp.zeros_like(h2_ref)
        rt_ref[...] = jnp.zeros_like(rt_ref)


def _mix_out_tile(grp, pos0, zero_first_halo, i, o_ref, r_ref, u_ref, halo_ref, x_ref, mod_ref, n2_ref,
                  gn_ref, pw_ref, ps_ref, wo_ref, wr_ref, br_ref, x1_ref, h2_ref, rt_ref, ext_ref, ymix_ref):
    nb, tt = grp.nb, grp.tt
    hist = halo_ref.shape[-2]

    for h in range(GLA_HEADS):
        cs = slice(h * GLA_DV, (h + 1) * GLA_DV)
        oh = o_ref[:, cs]
        ms = jnp.mean(oh * oh, axis=-1, keepdims=True)
        yh = oh * lax.rsqrt(ms + EPS) * gn_ref[:, cs] * _silu(r_ref[:, cs])
        ymix_ref[:, cs] = yh.astype(BF16)

    halo = halo_ref[...]
    if zero_first_halo:
        halo = jnp.where(i % grp.tiles_per_batch == 0, 0.0, halo)
    ext_ref[:, HALO - hist:HALO, :] = halo.reshape(nb, hist, POOL_WIDTH)
    u = u_ref[...].reshape(nb, tt, POOL_WIDTH)
    ext_ref[:, HALO:HALO + tt, :] = u
    t_idx = lax.broadcasted_iota(jnp.int32, (nb, tt, POOL_GW), 1)
    if grp.nb == 1:
        pos = (i % grp.tiles_per_batch) * TM + t_idx + pos0
    else:
        pos = t_idx + pos0
    for gi, w in enumerate(POOL_WINDOWS):
        cs = slice(gi * POOL_GW, (gi + 1) * POOL_GW)
        acc = ext_ref[:, pl.ds(HALO, tt), cs]
        for kk in range(1, w):
            acc = acc + ext_ref[:, pl.ds(HALO - kk, tt), cs]
        cnt = jnp.minimum(pos + 1, w).astype(F32)
        pooled = acc / cnt - u[:, :, cs]
        yp = _bdot(pooled.reshape(TM, POOL_GW), pw_ref[gi]) * ps_ref[:, cs]
        ymix_ref[:, GLA_WIDTH + gi * POOL_GW:GLA_WIDTH + (gi + 1) * POOL_GW] = yp.astype(BF16)

    y = jnp.dot(ymix_ref[...], wo_ref[...], preferred_element_type=F32)
    x1 = x_ref[...] + _mod_rows(mod_ref, 2) * y.reshape(nb, tt, D_MODEL)
    x1_ref[...] = x1
    h2 = _rmsnorm_mod(x1, n2_ref[...], _mod_rows(mod_ref, 4), _mod_rows(mod_ref, 3)).reshape(TM, D_MODEL)
    h2_ref[...] = h2
    logits = _dot_f32(h2, wr_ref[...]) + br_ref[...]
    rt_ref[...] = _route(logits)


def _mix_out(grp, n_tok, pos0, zero_first_halo, o, qkvr, u, halo_src, halo_block, halo_map, x, mod, norm2,
             gla_norm, pool_w, pool_scale, w_out, w_router, b_router, shared=()):
    n_alias = len(shared)
    n = grp.n_tiles
    n_fill = 0 if shared else n_tok // TM - n
    assert n_fill == 0 or grp.tile_off == 0
    clamp = lambda f: (lambda i: f(jnp.minimum(i, n - 1)))
    const2 = lambda i: (0, 0)
    row = clamp(lambda i: (i, 0))
    off = grp.tile_off
    kern = functools.partial(_mix_out_kernel, grp, pos0, zero_first_halo, n_alias)
    return pl.pallas_call(
        kern,
        name="mix_out",
        grid=(n + n_fill,),
        in_specs=[pl.BlockSpec((TM, GLA_WIDTH), row),
                  pl.BlockSpec((TM, GLA_WIDTH), clamp(lambda i: (i, 2))),
                  pl.BlockSpec((TM, POOL_WIDTH), row),
                  pl.BlockSpec(halo_block, clamp(halo_map)),
                  pl.BlockSpec((grp.nb, grp.tt, D_MODEL), clamp(grp.x_map())),
                  pl.BlockSpec((grp.nb, 6, D_MODEL), clamp(grp.mod_map())),
                  pl.BlockSpec((1, 1, D_MODEL), lambda i: (0, 0, 0)),
                  pl.BlockSpec((1, GLA_WIDTH), const2),
                  pl.BlockSpec(pool_w.shape, lambda i: (0, 0, 0), pipeline_mode=pl.Buffered(1)),
                  pl.BlockSpec((1, POOL_WIDTH), const2),
                  pl.BlockSpec(w_out.shape, const2, pipeline_mode=pl.Buffered(1)),
                  pl.BlockSpec(w_router.shape, const2, pipeline_mode=pl.Buffered(1)),
                  pl.BlockSpec((1, LANES), const2)]
                 + [pl.BlockSpec(memory_space=pl.ANY)] * n_alias,
        out_specs=[pl.BlockSpec((grp.nb, grp.tt, D_MODEL), clamp(grp.x_map())),
                   pl.BlockSpec((TM, D_MODEL), lambda i: (i + off, 0)),
                   pl.BlockSpec((TM, LANES), lambda i: (i + off, 0))],
        out_shape=[jax.ShapeDtypeStruct(x.shape, F32),
                   jax.ShapeDtypeStruct((n_tok, D_MODEL), F32),
                   jax.ShapeDtypeStruct((n_tok, LANES), F32)],
        scratch_shapes=[pltpu.VMEM((grp.nb, HALO + grp.tt, POOL_WIDTH), F32),
                        pltpu.VMEM((TM, D_MODEL), BF16)],
        input_output_aliases={13 + k: 1 + k for k in range(n_alias)},
        compiler_params=_cparams(1),
    )(o, qkvr, u, halo_src, x, mod, norm2.reshape(1, 1, D_MODEL), gla_norm.reshape(1, GLA_WIDTH),
      pool_w, pool_scale.reshape(1, POOL_WIDTH), w_out, w_router, b_router, *shared)


def _row_copy(src_hbm, src_row, dst, dst_row, sem):
    return pltpu.make_async_copy(src_hbm.at[pl.ds(src_row, 1), :], dst.at[pl.ds(dst_row, 1), :], sem)


def _tile_wait(src_hbm, dst, sem):
    pltpu.make_async_copy(src_hbm.at[pl.ds(0, dst.shape[0]), :], dst, sem).wait()


def _moe_kernel(te_ref, src_ref, nu_ref, h_hbm, w1_ref, w3_ref, w2_ref, y_ref,
                buf0, buf1, sem, w1b, w3b, w2b):
    i = pl.program_id(0)
    n_used = nu_ref[0]
    bufs = (buf0, buf1)

    def issue(tile, slot):
        base = tile * TME
        for r in range(TME):
            _row_copy(h_hbm, src_ref[base + r], bufs[slot], r, sem.at[slot]).start()

    def compute(slot):
        _tile_wait(h_hbm, bufs[slot], sem.at[slot])
        hb = bufs[slot][...].astype(BF16)
        a = jnp.dot(hb, w1b[...], preferred_element_type=F32)
        b = jnp.dot(hb, w3b[...], preferred_element_type=F32)
        hid = _silu(a) * b
        y_ref[...] = jnp.dot(hid.astype(BF16), w2b[...], preferred_element_type=F32)

    @pl.when(i == 0)
    def _():
        issue(0, 0)

    prev = jnp.maximum(i - 1, 0)

    @pl.when((i < n_used) & ((i == 0) | (te_ref[i] != te_ref[prev])))
    def _():
        w1b[...] = w1_ref[...].astype(BF16)
        w3b[...] = w3_ref[...].astype(BF16)
        w2b[...] = w2_ref[...].astype(BF16)

    for slot in range(2):
        @pl.when((i < n_used - 1) & (i % 2 == slot))
        def _():
            issue(i + 1, 1 - slot)
            compute(slot)

        @pl.when((i == n_used - 1) & (i % 2 == slot))
        def _():
            compute(slot)

    @pl.when(i >= n_used)
    def _():
        y_ref[...] = jnp.zeros_like(y_ref)


def _moe(tile_expert, src_row, n_used, h2_all, w1, w3, w2):
    n_sorted = src_row.shape[0]
    n_tiles = n_sorted // TME
    grid_spec = pltpu.PrefetchScalarGridSpec(
        num_scalar_prefetch=3,
        grid=(n_tiles,),
        in_specs=[pl.BlockSpec(memory_space=pl.ANY),
                  pl.BlockSpec((None, D_MODEL, EXPERT_FF), lambda i, te, sr, nu: (te[i], 0, 0)),
                  pl.BlockSpec((None, D_MODEL, EXPERT_FF), lambda i, te, sr, nu: (te[i], 0, 0)),
                  pl.BlockSpec((None, EXPERT_FF, D_MODEL), lambda i, te, sr, nu: (te[i], 0, 0))],
        out_specs=pl.BlockSpec((TME, D_MODEL), lambda i, te, sr, nu: (i, 0)),
        scratch_shapes=[pltpu.VMEM((TME, D_MODEL), F32), pltpu.VMEM((TME, D_MODEL), F32),
                        pltpu.SemaphoreType.DMA((2,)),
                        pltpu.VMEM((D_MODEL, EXPERT_FF), BF16), pltpu.VMEM((D_MODEL, EXPERT_FF), BF16),
                        pltpu.VMEM((EXPERT_FF, D_MODEL), BF16)],
    )
    return pl.pallas_call(
        _moe_kernel,
        name="moe",
        grid_spec=grid_spec,
        out_shape=jax.ShapeDtypeStruct((n_sorted, D_MODEL), F32),
        compiler_params=_cparams(1),
    )(tile_expert, src_row, n_used, h2_all, w1, w3, w2)


def _finish_kernel(grp, pos_ref, x1_ref, mod_ref, rt_ref, nf_ref, y_hbm, out_ref, buf_a, buf_b, sem):
    i = pl.program_id(0)
    n_steps = pl.num_programs(0)
    slot = i % 2

    def issue(tile, slot_):
        base = (tile * TM + grp.row_off) * 2

        def body(r, carry):
            _row_copy(y_hbm, pos_ref[base + 2 * r], buf_a.at[slot_], r, sem.at[slot_]).start()
            _row_copy(y_hbm, pos_ref[base + 2 * r + 1], buf_b.at[slot_], r, sem.at[slot_]).start()
            return carry

        lax.fori_loop(0, TM, body, 0, unroll=8)

    @pl.when(i == 0)
    def _():
        issue(0, 0)

    @pl.when(i + 1 < n_steps)
    def _():
        issue(i + 1, 1 - slot)

    _tile_wait(y_hbm, buf_a.at[slot], sem.at[slot])
    _tile_wait(y_hbm, buf_b.at[slot], sem.at[slot])
    rt = rt_ref[...]
    moe = rt[:, 2:3] * buf_a[slot] + rt[:, 3:4] * buf_b[slot]
    x2 = x1_ref[...] + _mod_rows(mod_ref, 5) * moe.reshape(grp.nb, grp.tt, D_MODEL)
    ms = jnp.mean(x2 * x2, axis=-1, keepdims=True)
    out_ref[...] = x2 * lax.rsqrt(ms + EPS) * nf_ref[...]


def _finish(grp, pos, x1, mod, route_all, norm_f, y_sorted):
    off = grp.tile_off
    grid_spec = pltpu.PrefetchScalarGridSpec(
        num_scalar_prefetch=1,
        grid=(grp.n_tiles,),
        in_specs=[pl.BlockSpec((grp.nb, grp.tt, D_MODEL), grp.x_map()),
                  pl.BlockSpec((grp.nb, 6, D_MODEL), grp.mod_map()),
                  pl.BlockSpec((TM, LANES), lambda i, p: (i + off, 0)),
                  pl.BlockSpec((1, 1, D_MODEL), lambda i, p: (0, 0, 0)),
                  pl.BlockSpec(memory_space=pl.ANY)],
        out_specs=pl.BlockSpec((grp.nb, grp.tt, D_MODEL), grp.x_map()),
        scratch_shapes=[pltpu.VMEM((2, TM, D_MODEL), F32), pltpu.VMEM((2, TM, D_MODEL), F32),
                        pltpu.SemaphoreType.DMA((2,))],
    )
    return pl.pallas_call(
        functools.partial(_finish_kernel, grp),
        name="finish",
        grid_spec=grid_spec,
        out_shape=jax.ShapeDtypeStruct(x1.shape, F32),
        compiler_params=_cparams(1),
    )(pos, x1, mod, route_all, norm_f.reshape(1, 1, D_MODEL), y_sorted)


def _sort_plan(route_all):
    n_tok = route_all.shape[0]
    n_pairs = 2 * n_tok
    n_sorted = n_pairs + N_EXPERTS * TME
    flat_e = route_all[:, 0:2].astype(jnp.int32).reshape(n_pairs)
    onehot = (flat_e[:, None] == jnp.arange(N_EXPERTS, dtype=jnp.int32)[None, :]).astype(jnp.int32)
    csum = jnp.cumsum(onehot, axis=0)
    rank = jnp.sum(onehot * csum, axis=1) - 1
    counts = csum[-1]
    padded = ((counts + TME - 1) // TME) * TME
    ends = jnp.cumsum(padded)
    starts = ends - padded
    pos = starts[flat_e] + rank
    token = jnp.arange(n_pairs, dtype=jnp.int32) // 2
    src_row = jnp.zeros((n_sorted,), jnp.int32).at[pos].set(token)
    tile_start = jnp.arange(n_sorted // TME, dtype=jnp.int32) * TME
    tile_expert = jnp.sum((tile_start[:, None] >= ends[None, :]).astype(jnp.int32), axis=1)
    tile_expert = jnp.minimum(tile_expert, N_EXPERTS - 1)
    n_used = (ends[-1] // TME).astype(jnp.int32).reshape(1)
    return pos.astype(jnp.int32), src_row, tile_expert, n_used


def kernel(x_prompt, x_sample, c_prompt, c_sample, state_gla, state_pool, w_ada, b_ada, norm1, norm2, w_in,
           gate_up, gate_bias, gla_norm, pool_w, pool_scale, w_out, w_group, b_group, w_expert, b_expert,
           w1, w3, w2, norm_f):
    assert w_ada.shape[0] == 1, "single-layer step"
    bp, tp, _ = x_prompt.shape
    bs, ts, _ = x_sample.shape
    grp_p = _Group(bp, tp, 0, bs)
    grp_s = _Group(bs, ts, bp * tp, 0)
    n_tok = bp * tp + bs * ts

    n_c = bp + bs
    n_c_pad = -(-n_c // 8) * 8
    c_all = jnp.concatenate([c_sample, c_prompt, jnp.zeros((n_c_pad - n_c, D_MODEL), F32)], axis=0)
    mod = _adaln(c_all, w_ada[0], b_ada[0]).reshape(n_c_pad, 6, D_MODEL)
    mod_p = mod_s = mod

    wi = w_in[0]
    wq = wi[:, :QKVR_WIDTH].astype(BF16)
    wg = jnp.pad(wi[:, QKVR_WIDTH:QKVR_WIDTH + GATE_RANK], ((0, 0), (0, LANES - GATE_RANK))).astype(BF16)
    wu = wi[:, QKVR_WIDTH + GATE_RANK:].astype(BF16)
    gup = jnp.pad(gate_up[0], ((0, LANES - GATE_RANK), (0, 0))).astype(BF16)
    gb = gate_bias[0].reshape(1, GLA_KEY_WIDTH)
    pw = pool_w[0].astype(BF16)
    wo = w_out[0].astype(BF16)
    w_router = jnp.concatenate(
        [w_group[0], jnp.transpose(w_expert[0], (1, 0, 2)).reshape(D_MODEL, N_EXPERTS),
         jnp.zeros((D_MODEL, LANES - N_GROUPS - N_EXPERTS), F32)], axis=1)
    b_router = jnp.concatenate([b_group[0], b_expert[0].reshape(N_EXPERTS),
                                jnp.zeros((LANES - N_GROUPS - N_EXPERTS,), F32)]).reshape(1, LANES)

    qkvr_p, la_p, u_p = _in_proj(grp_p, x_prompt, mod_p, norm1[0], wq, wg, wu, gup, gb)
    qkvr_s, la_s, u_s = _in_proj(grp_s, x_sample, mod_s, norm1[0], wq, wg, wu, gup, gb)

    o_p, gla_p = _gla_prompt(bp, tp, qkvr_p, la_p)
    o_s, gla_s = _gla_decode(bs, ts, qkvr_s, la_s, state_gla[0])

    halo_per_tile = TM // HALO
    halo_map_p = lambda i: (jnp.maximum(i * halo_per_tile - 1, 0), 0)
    x1_p, h2_all, route_all = _mix_out(grp_p, n_tok, 0, True, o_p, qkvr_p, u_p, u_p, (HALO, POOL_WIDTH),
                                       halo_map_p, x_prompt, mod_p, norm2[0], gla_norm[0], pw, pool_scale[0],
                                       wo, w_router, b_router)
    x1_s, h2_all, route_all = _mix_out(grp_s, n_tok, PAST_LEN, False, o_s, qkvr_s, u_s, state_pool[0],
                                       (grp_s.nb, POOL_BUF, POOL_WIDTH), lambda i: (i, 0, 0), x_sample, mod_s,
                                       norm2[0], gla_norm[0], pw, pool_scale[0], wo, w_router, b_router,
                                       shared=(h2_all, route_all))

    pos, src_row, tile_expert, n_used = _sort_plan(route_all)
    y_sorted = _moe(tile_expert, src_row, n_used, h2_all, w1[0], w3[0], w2[0])

    y_p = _finish(grp_p, pos, x1_p, mod_p, route_all, norm_f, y_sorted)
    y_s = _finish(grp_s, pos, x1_s, mod_s, route_all, norm_f, y_sorted)

    u_p3 = u_p.reshape(bp, tp, POOL_WIDTH)
    u_s3 = u_s.reshape(bs, ts, POOL_WIDTH)
    pool_p = jnp.concatenate([jnp.zeros((bp, POOL_BUF, POOL_WIDTH), F32), u_p3], axis=1)[:, -POOL_BUF:]
    pool_s = jnp.concatenate([state_pool[0], u_s3], axis=1)[:, -POOL_BUF:]
    return (y_p, y_s, gla_p[None], pool_p[None], gla_s[None], pool_s[None])
```

```python
import functools

import jax
import jax.numpy as jnp
from jax import lax
from jax.experimental import pallas as pl
from jax.experimental.pallas import tpu as pltpu

D_MODEL = 2048
GLA_HEADS = 4
GLA_DK = 128
GLA_DV = 256
GLA_KEY_WIDTH = GLA_HEADS * GLA_DK
GLA_WIDTH = GLA_HEADS * GLA_DV
POOL_WIDTH = 1024
POOL_WINDOWS = (2, 4, 8, 16)
POOL_GW = 256
POOL_BUF = 15
HALO = 16
GATE_RANK = 16
GATE_TEMP = 16.0
N_GROUPS = 4
EXPERTS_PER_GROUP = 8
N_EXPERTS = 32
EXPERT_FF = 512
EPS = 1e-6
PAST_LEN = 16384
QKVR_WIDTH = 2 * GLA_KEY_WIDTH + 2 * GLA_WIDTH

LANES = 128
TM = 256
TME = 256
GLA_CHUNK = 64
GLA_SUB = 16
GLA_STEP = 512
VMEM_LIMIT = 56 * 1024 * 1024

BF16 = jnp.bfloat16
F32 = jnp.float32
NEG = -1e30


def _cparams(n_axes):
    return pltpu.CompilerParams(dimension_semantics=("arbitrary",) * n_axes,
                                vmem_limit_bytes=VMEM_LIMIT)


def _silu(x):
    return x / (1.0 + jnp.exp(-x))


def _bdot(a, b):
    return jnp.dot(a.astype(BF16), b.astype(BF16), preferred_element_type=F32)


def _split3(a):
    a1 = a.astype(BF16)
    r1 = a - a1.astype(F32)
    a2 = r1.astype(BF16)
    a3 = (r1 - a2.astype(F32)).astype(BF16)
    return a1, a2, a3


def _dot_f32(a, b):
    a1, a2, a3 = _split3(a)
    b1, b2, b3 = _split3(b)
    d = lambda x, y: jnp.dot(x, y, preferred_element_type=F32)
    small = d(a2, b2) + d(a1, b3) + d(a3, b1)
    mid = d(a1, b2) + d(a2, b1)
    return d(a1, b1) + (mid + small)


def _dot_exact_lhs(tri_bf16, g):
    g1, g2, g3 = _split3(g)
    d = lambda y: jnp.dot(tri_bf16, y, preferred_element_type=F32)
    return d(g1) + (d(g2) + d(g3))


def _adaln_kernel(c_ref, w_ref, b_ref, o_ref):
    c = c_ref[...]
    o_ref[...] = _bdot(_silu(c), w_ref[...]) + b_ref[...]


def _adaln(c_all, w_ada, b_ada):
    n, d = c_all.shape
    width = w_ada.shape[1]
    tn = 1024
    return pl.pallas_call(
        _adaln_kernel,
        name="adaln",
        grid=(width // tn,),
        in_specs=[pl.BlockSpec((n, d), lambda j: (0, 0)),
                  pl.BlockSpec((d, tn), lambda j: (0, j)),
                  pl.BlockSpec((1, tn), lambda j: (0, j))],
        out_specs=pl.BlockSpec((n, tn), lambda j: (0, j)),
        out_shape=jax.ShapeDtypeStruct((n, width), F32),
        compiler_params=_cparams(1),
    )(c_all, w_ada, b_ada.reshape(1, width))


class _Group:
    def __init__(self, batch, seq, row_off, mod_off):
        self.batch, self.seq, self.row_off, self.mod_off = batch, seq, row_off, mod_off
        if seq >= TM:
            assert seq % TM == 0
            self.nb, self.tt = 1, TM
            self.tiles_per_batch = seq // TM
            self.n_tiles = batch * self.tiles_per_batch
        else:
            assert TM % seq == 0 and batch % (TM // seq) == 0
            self.nb, self.tt = TM // seq, seq
            self.tiles_per_batch = 1
            self.n_tiles = batch // self.nb
        self.rows = batch * seq
        self.tile_off = row_off // TM

    def x_map(self):
        if self.nb == 1:
            tpb = self.tiles_per_batch
            return lambda i, *_: (i // tpb, i % tpb, 0)
        return lambda i, *_: (i, 0, 0)

    def mod_map(self):
        assert self.mod_off % self.nb == 0
        off = self.mod_off // self.nb
        if self.nb == 1:
            tpb = self.tiles_per_batch
            return lambda i, *_: (i // tpb + off, 0, 0)
        return lambda i, *_: (i + off, 0, 0)


def _mod_rows(mod_ref, idx):
    return mod_ref[:, idx:idx + 1, :]


def _rmsnorm_mod(x, gain, scale, shift):
    ms = jnp.mean(x * x, axis=-1, keepdims=True)
    y = x * lax.rsqrt(ms + EPS) * gain
    return y * (1.0 + scale) + shift


def _in_proj_kernel(x_ref, mod_ref, n1_ref, wq_ref, wg_ref, wu_ref, gup_ref, gb_ref,
                    qkvr_ref, la_ref, u_ref):
    x = x_ref[...]
    h = _rmsnorm_mod(x, n1_ref[...], _mod_rows(mod_ref, 1), _mod_rows(mod_ref, 0))
    hb = h.reshape(TM, D_MODEL).astype(BF16)
    qkvr_ref[...] = jnp.dot(hb, wq_ref[...], preferred_element_type=F32)
    u_ref[...] = jnp.dot(hb, wu_ref[...], preferred_element_type=F32)
    g_lr = jnp.dot(hb, wg_ref[...], preferred_element_type=F32)
    pre = jnp.dot(g_lr.astype(BF16), gup_ref[...], preferred_element_type=F32) + gb_ref[...]
    log_sig = jnp.minimum(pre, 0.0) - jnp.log1p(jnp.exp(-jnp.abs(pre)))
    la_ref[...] = log_sig / GATE_TEMP


def _in_proj(grp, x, mod, norm1, wq, wg, wu, gup, gb):
    const = lambda i: (0, 0)
    row = lambda i: (i, 0)
    return pl.pallas_call(
        _in_proj_kernel,
        name="in_proj",
        grid=(grp.n_tiles,),
        in_specs=[pl.BlockSpec((grp.nb, grp.tt, D_MODEL), grp.x_map()),
                  pl.BlockSpec((grp.nb, 6, D_MODEL), grp.mod_map()),
                  pl.BlockSpec((1, 1, D_MODEL), lambda i: (0, 0, 0)),
                  pl.BlockSpec(wq.shape, const, pipeline_mode=pl.Buffered(1)),
                  pl.BlockSpec(wg.shape, const, pipeline_mode=pl.Buffered(1)),
                  pl.BlockSpec(wu.shape, const, pipeline_mode=pl.Buffered(1)),
                  pl.BlockSpec(gup.shape, const, pipeline_mode=pl.Buffered(1)),
                  pl.BlockSpec(gb.shape, const, pipeline_mode=pl.Buffered(1))],
        out_specs=[pl.BlockSpec((TM, QKVR_WIDTH), row),
                   pl.BlockSpec((TM, GLA_KEY_WIDTH), row),
                   pl.BlockSpec((TM, POOL_WIDTH), row)],
        out_shape=[jax.ShapeDtypeStruct((grp.rows, QKVR_WIDTH), F32),
                   jax.ShapeDtypeStruct((grp.rows, GLA_KEY_WIDTH), F32),
                   jax.ShapeDtypeStruct((grp.rows, POOL_WIDTH), F32)],
        compiler_params=_cparams(1),
    )(x, mod, norm1.reshape(1, 1, D_MODEL), wq, wg, wu, gup, gb)


def _gla_head_chunk(q, k, v, g, state_t, chunk, sub):
    rows = lax.broadcasted_iota(jnp.int32, (chunk, chunk), 0)
    cols = lax.broadcasted_iota(jnp.int32, (chunk, chunk), 1)
    tri = (rows >= cols).astype(BF16)
    b = _dot_exact_lhs(tri, g)
    q = q * (GLA_DK ** -0.5)
    nt = (((1,), (1,)), ((), ()))
    tn = (((0,), (0,)), ((), ()))
    o = lax.dot_general((q * jnp.exp(b)).astype(BF16), state_t.astype(BF16), nt,
                        preferred_element_type=F32)

    b_last = b[chunk - 1:chunk, :]
    k_dec = k * jnp.exp(b_last - b)
    new_state_t = jnp.exp(b_last) * state_t + lax.dot_general(
        v.astype(BF16), k_dec.astype(BF16), tn, preferred_element_type=F32)

    lane = lax.broadcasted_iota(jnp.int32, (sub, chunk), 1)
    sub_row = lax.broadcasted_iota(jnp.int32, (sub, LANES), 0)
    key_row = lax.broadcasted_iota(jnp.int32, (chunk, LANES), 0)
    p_blocks = []
    for s in range(chunk // sub):
        lo = s * sub
        q_s = q[lo:lo + sub, :]
        b_s = b[lo:lo + sub, :]
        p = jnp.zeros((sub, chunk), F32)
        if s > 0:
            ref_row = b[lo - 1:lo, :]
            q_rel = q_s * jnp.exp(b_s - ref_row)
            k_rel = k * jnp.exp(jnp.where(key_row < lo, ref_row - b, NEG))
            p = lax.dot_general(q_rel.astype(BF16), k_rel.astype(BF16), nt, preferred_element_type=F32)
        for jl in range(sub):
            j = lo + jl
            expo = jnp.where(sub_row >= jl, b_s - b[j:j + 1, :], NEG)
            col = jnp.sum(q_s * k[j:j + 1, :] * jnp.exp(expo), axis=-1, keepdims=True)
            p = jnp.where(lane == j, col, p)
        p_blocks.append(p)
    p_full = p_blocks[0] if len(p_blocks) == 1 else jnp.concatenate(p_blocks, axis=0)
    o = o + _bdot(p_full, v)
    return o, new_state_t


def _gla_prompt_kernel(q_ref, k_ref, v_ref, la_ref, o_ref, s_ref, st_ref):
    step = pl.program_id(2)

    @pl.when(step == 0)
    def _():
        st_ref[...] = jnp.zeros_like(st_ref)

    def body(c, carry):
        r0 = pl.multiple_of(c * GLA_CHUNK, GLA_CHUNK)
        sl = pl.ds(r0, GLA_CHUNK)
        o, new_state_t = _gla_head_chunk(q_ref[sl, :], k_ref[sl, :], v_ref[sl, :], la_ref[sl, :],
                                         st_ref[...], GLA_CHUNK, GLA_SUB)
        o_ref[sl, :] = o
        st_ref[...] = new_state_t
        return carry

    lax.fori_loop(0, GLA_STEP // GLA_CHUNK, body, 0)

    @pl.when(step == pl.num_programs(2) - 1)
    def _():
        s_ref[0, 0] = st_ref[...].T


def _gla_prompt(batch, seq, qkvr, log_a):
    steps = seq // GLA_STEP
    row = lambda b, h, s: b * steps + s
    return pl.pallas_call(
        _gla_prompt_kernel,
        name="gla_prompt",
        grid=(batch, GLA_HEADS, steps),
        in_specs=[pl.BlockSpec((GLA_STEP, GLA_DK), lambda b, h, s: (row(b, h, s), h)),
                  pl.BlockSpec((GLA_STEP, GLA_DK), lambda b, h, s: (row(b, h, s), GLA_HEADS + h)),
                  pl.BlockSpec((GLA_STEP, GLA_DV), lambda b, h, s: (row(b, h, s), GLA_HEADS + h)),
                  pl.BlockSpec((GLA_STEP, GLA_DK), lambda b, h, s: (row(b, h, s), h))],
        out_specs=[pl.BlockSpec((GLA_STEP, GLA_DV), lambda b, h, s: (row(b, h, s), h)),
                   pl.BlockSpec((1, 1, GLA_DK, GLA_DV), lambda b, h, s: (b, h, 0, 0))],
        out_shape=[jax.ShapeDtypeStruct((batch * seq, GLA_WIDTH), F32),
                   jax.ShapeDtypeStruct((batch, GLA_HEADS, GLA_DK, GLA_DV), F32)],
        scratch_shapes=[pltpu.VMEM((GLA_DV, GLA_DK), F32)],
        compiler_params=_cparams(3),
    )(qkvr, qkvr, qkvr, log_a)


GLA_DEC_BB = 16


def _gla_decode_kernel(seq, q_ref, k_ref, v_ref, la_ref, s0_ref, o_ref, s_ref):
    def body(i, carry):
        r0 = pl.multiple_of(i * seq, seq)
        sl = pl.ds(r0, seq)
        o, new_state_t = _gla_head_chunk(q_ref[sl, :], k_ref[sl, :], v_ref[sl, :], la_ref[sl, :],
                                         s0_ref[i, 0].T, seq, seq)
        o_ref[sl, :] = o
        s_ref[i, 0] = new_state_t.T
        return carry

    lax.fori_loop(0, GLA_DEC_BB, body, 0)


def _gla_decode(batch, seq, qkvr, log_a, state):
    rows = GLA_DEC_BB * seq
    return pl.pallas_call(
        functools.partial(_gla_decode_kernel, seq),
        name="gla_decode",
        grid=(batch // GLA_DEC_BB, GLA_HEADS),
        in_specs=[pl.BlockSpec((rows, GLA_DK), lambda i, h: (i, h)),
                  pl.BlockSpec((rows, GLA_DK), lambda i, h: (i, GLA_HEADS + h)),
                  pl.BlockSpec((rows, GLA_DV), lambda i, h: (i, GLA_HEADS + h)),
                  pl.BlockSpec((rows, GLA_DK), lambda i, h: (i, h)),
                  pl.BlockSpec((GLA_DEC_BB, 1, GLA_DK, GLA_DV), lambda i, h: (i, h, 0, 0))],
        out_specs=[pl.BlockSpec((rows, GLA_DV), lambda i, h: (i, h)),
                   pl.BlockSpec((GLA_DEC_BB, 1, GLA_DK, GLA_DV), lambda i, h: (i, h, 0, 0))],
        out_shape=[jax.ShapeDtypeStruct((batch * seq, GLA_WIDTH), F32),
                   jax.ShapeDtypeStruct((batch, GLA_HEADS, GLA_DK, GLA_DV), F32)],
        compiler_params=_cparams(2),
    )(qkvr, qkvr, qkvr, log_a, state)


def _route(logits):
    lane = lax.broadcasted_iota(jnp.int32, logits.shape, 1)
    big = jnp.int32(10 ** 6)
    is_group = lane < N_GROUPS
    lg = jnp.where(is_group, logits, NEG)
    mg = jnp.max(lg, axis=-1, keepdims=True)
    g_idx = jnp.min(jnp.where(is_group & (lg == mg), lane, big), axis=-1, keepdims=True)
    denom = jnp.sum(jnp.where(is_group, jnp.exp(lg - mg), 0.0), axis=-1, keepdims=True)
    p_sel = 1.0 / denom
    first = N_GROUPS + EXPERTS_PER_GROUP * g_idx
    in_grp = (lane >= first) & (lane < first + EXPERTS_PER_GROUP)
    le = jnp.where(in_grp, logits, NEG)
    m1 = jnp.max(le, axis=-1, keepdims=True)
    i1 = jnp.min(jnp.where(in_grp & (le == m1), lane, big), axis=-1, keepdims=True)
    rest = in_grp & (lane != i1)
    le2 = jnp.where(rest, logits, NEG)
    m2 = jnp.max(le2, axis=-1, keepdims=True)
    i2 = jnp.min(jnp.where(rest & (le2 == m2), lane, big), axis=-1, keepdims=True)
    e2 = jnp.exp(m2 - m1)
    w1 = p_sel / (1.0 + e2)
    w2 = p_sel * e2 / (1.0 + e2)
    ex1 = (i1 - N_GROUPS).astype(F32)
    ex2 = (i2 - N_GROUPS).astype(F32)
    return jnp.where(lane == 0, ex1, jnp.where(lane == 1, ex2,
                     jnp.where(lane == 2, w1, jnp.where(lane == 3, w2, 0.0))))


def _mix_out_kernel(grp, pos0, zero_first_halo, n_alias,
                    o_ref, r_ref, u_ref, halo_ref, x_ref, mod_ref, n2_ref, gn_ref, pw_ref, ps_ref,
                    wo_ref, wr_ref, br_ref, *rest):
    x1_ref, h2_ref, rt_ref, ext_ref, ymix_ref = rest[n_alias:]
    i = pl.program_id(0)

    @pl.when(i < grp.n_tiles)
    def _():
        _mix_out_tile(grp, pos0, zero_first_halo, i, o_ref, r_ref, u_ref, halo_ref, x_ref, mod_ref, n2_ref,
                      gn_ref, pw_ref, ps_ref, wo_ref, wr_ref, br_ref, x1_ref, h2_ref, rt_ref, ext_ref, ymix_ref)

    @pl.when(i >= grp.n_tiles)
    def _():
        h2_ref[...] = jnp.zeros_like(h2_ref)
        rt_ref[...] = jnp.zeros_like(rt_ref)


def _mix_out_tile(grp, pos0, zero_first_halo, i, o_ref, r_ref, u_ref, halo_ref, x_ref, mod_ref, n2_ref,
                  gn_ref, pw_ref, ps_ref, wo_ref, wr_ref, br_ref, x1_ref, h2_ref, rt_ref, ext_ref, ymix_ref):
    nb, tt = grp.nb, grp.tt
    hist = halo_ref.shape[-2]

    for h in range(GLA_HEADS):
        cs = slice(h * GLA_DV, (h + 1) * GLA_DV)
        oh = o_ref[:, cs]
        ms = jnp.mean(oh * oh, axis=-1, keepdims=True)
        yh = oh * lax.rsqrt(ms + EPS) * gn_ref[:, cs] * _silu(r_ref[:, cs])
        ymix_ref[:, cs] = yh.astype(BF16)

    halo = halo_ref[...]
    if zero_first_halo:
        halo = jnp.where(i % grp.tiles_per_batch == 0, 0.0, halo)
    ext_ref[:, HALO - hist:HALO, :] = halo.reshape(nb, hist, POOL_WIDTH)
    u = u_ref[...].reshape(nb, tt, POOL_WIDTH)
    ext_ref[:, HALO:HALO + tt, :] = u
    t_idx = lax.broadcasted_iota(jnp.int32, (nb, tt, POOL_GW), 1)
    if grp.nb == 1:
        pos = (i % grp.tiles_per_batch) * TM + t_idx + pos0
    else:
        pos = t_idx + pos0
    for gi, w in enumerate(POOL_WINDOWS):
        cs = slice(gi * POOL_GW, (gi + 1) * POOL_GW)
        acc = ext_ref[:, pl.ds(HALO, tt), cs]
        for kk in range(1, w):
            acc = acc + ext_ref[:, pl.ds(HALO - kk, tt), cs]
        cnt = jnp.minimum(pos + 1, w).astype(F32)
        pooled = acc / cnt - u[:, :, cs]
        yp = _bdot(pooled.reshape(TM, POOL_GW), pw_ref[gi]) * ps_ref[:, cs]
        ymix_ref[:, GLA_WIDTH + gi * POOL_GW:GLA_WIDTH + (gi + 1) * POOL_GW] = yp.astype(BF16)

    y = jnp.dot(ymix_ref[...], wo_ref[...], preferred_element_type=F32)
    x1 = x_ref[...] + _mod_rows(mod_ref, 2) * y.reshape(nb, tt, D_MODEL)
    x1_ref[...] = x1
    h2 = _rmsnorm_mod(x1, n2_ref[...], _mod_rows(mod_ref, 4), _mod_rows(mod_ref, 3)).reshape(TM, D_MODEL)
    _pack_bf16_pairs(h2, h2_ref)
    logits = _dot_f32(h2, wr_ref[...]) + br_ref[...]
    rt_ref[...] = _route(logits)


def _mix_out(grp, n_tok, pos0, zero_first_halo, o, qkvr, u, halo_src, halo_block, halo_map, x, mod, norm2,
             gla_norm, pool_w, pool_scale, w_out, w_router, b_router, shared=()):
    n_alias = len(shared)
    n = grp.n_tiles
    n_fill = 0 if shared else n_tok // TM - n
    assert n_fill == 0 or grp.tile_off == 0
    clamp = lambda f: (lambda i: f(jnp.minimum(i, n - 1)))
    const2 = lambda i: (0, 0)
    row = clamp(lambda i: (i, 0))
    off = grp.tile_off
    kern = functools.partial(_mix_out_kernel, grp, pos0, zero_first_halo, n_alias)
    return pl.pallas_call(
        kern,
        name="mix_out",
        grid=(n + n_fill,),
        in_specs=[pl.BlockSpec((TM, GLA_WIDTH), row),
                  pl.BlockSpec((TM, GLA_WIDTH), clamp(lambda i: (i, 2))),
                  pl.BlockSpec((TM, POOL_WIDTH), row),
                  pl.BlockSpec(halo_block, clamp(halo_map)),
                  pl.BlockSpec((grp.nb, grp.tt, D_MODEL), clamp(grp.x_map())),
                  pl.BlockSpec((grp.nb, 6, D_MODEL), clamp(grp.mod_map())),
                  pl.BlockSpec((1, 1, D_MODEL), lambda i: (0, 0, 0)),
                  pl.BlockSpec((1, GLA_WIDTH), const2),
                  pl.BlockSpec(pool_w.shape, lambda i: (0, 0, 0), pipeline_mode=pl.Buffered(1)),
                  pl.BlockSpec((1, POOL_WIDTH), const2),
                  pl.BlockSpec(w_out.shape, const2, pipeline_mode=pl.Buffered(1)),
                  pl.BlockSpec(w_router.shape, const2, pipeline_mode=pl.Buffered(1)),
                  pl.BlockSpec((1, LANES), const2)]
                 + [pl.BlockSpec(memory_space=pl.ANY)] * n_alias,
        out_specs=[pl.BlockSpec((grp.nb, grp.tt, D_MODEL), clamp(grp.x_map())),
                   pl.BlockSpec((TM * PACK_ROWS, LANES), lambda i: (i + off, 0)),
                   pl.BlockSpec((TM, LANES), lambda i: (i + off, 0))],
        out_shape=[jax.ShapeDtypeStruct(x.shape, F32),
                   jax.ShapeDtypeStruct((n_tok * PACK_ROWS, LANES), jnp.int32),
                   jax.ShapeDtypeStruct((n_tok, LANES), F32)],
        scratch_shapes=[pltpu.VMEM((grp.nb, HALO + grp.tt, POOL_WIDTH), F32),
                        pltpu.VMEM((TM, D_MODEL), BF16)],
        input_output_aliases={13 + k: 1 + k for k in range(n_alias)},
        compiler_params=_cparams(1),
    )(o, qkvr, u, halo_src, x, mod, norm2.reshape(1, 1, D_MODEL), gla_norm.reshape(1, GLA_WIDTH),
      pool_w, pool_scale.reshape(1, POOL_WIDTH), w_out, w_router, b_router, *shared)


def _row_copy(src_hbm, src_row, dst, dst_row, sem):
    return pltpu.make_async_copy(src_hbm.at[pl.ds(src_row, 1), :], dst.at[pl.ds(dst_row, 1), :], sem)


def _tile_wait(src_hbm, dst, sem):
    pltpu.make_async_copy(src_hbm.at[pl.ds(0, dst.shape[0]), :], dst, sem).wait()


PACK_ROWS = D_MODEL // (2 * LANES)
HI_MASK = -65536


def _pack_bf16_pairs(h, dst_ref):
    rows = h.shape[0]
    half = D_MODEL // 2
    for s in range(PACK_ROWS):
        lo = h[:, s * LANES:(s + 1) * LANES].astype(BF16).astype(F32)
        hi = h[:, half + s * LANES:half + (s + 1) * LANES].astype(BF16).astype(F32)
        lo_bits = lax.shift_right_logical(lax.bitcast_convert_type(lo, jnp.int32), 16)
        hi_bits = lax.bitcast_convert_type(hi, jnp.int32) & HI_MASK
        dst_ref[pl.ds(s, rows, stride=PACK_ROWS), :] = lo_bits | hi_bits


def _unpack_bf16_pairs(src_ref, dst_ref):
    rows = dst_ref.shape[0]
    half = D_MODEL // 2
    for s in range(PACK_ROWS):
        word = src_ref[pl.ds(s, rows, stride=PACK_ROWS), :]
        lo = lax.bitcast_convert_type(lax.shift_left(word, 16), F32)
        hi = lax.bitcast_convert_type(word & HI_MASK, F32)
        dst_ref[:, s * LANES:(s + 1) * LANES] = lo.astype(BF16)
        dst_ref[:, half + s * LANES:half + (s + 1) * LANES] = hi.astype(BF16)


def _packed_row_copy(src_hbm, src_row, dst, dst_row, sem):
    return pltpu.make_async_copy(src_hbm.at[pl.ds(src_row * PACK_ROWS, PACK_ROWS), :],
                                 dst.at[pl.ds(dst_row * PACK_ROWS, PACK_ROWS), :], sem)


def _moe_kernel(te_ref, src_ref, nu_ref, seg_ref, nxt_ref, h_hbm, w1_hbm, w3_hbm, w2_hbm, y_ref,
                buf0, buf1, sem, hb, wf1, wf3, wf2, wsem, w1b, w3b, w2b):
    i = pl.program_id(0)
    n_used = nu_ref[0]
    bufs = (buf0, buf1)

    def issue(tile, slot):
        base = tile * TME
        for r in range(TME):
            _packed_row_copy(h_hbm, src_ref[base + r], bufs[slot], r, sem.at[slot]).start()

    def weight_copies(expert, wslot):
        return (pltpu.make_async_copy(w1_hbm.at[expert], wf1.at[wslot], wsem.at[wslot]),
                pltpu.make_async_copy(w3_hbm.at[expert], wf3.at[wslot], wsem.at[wslot]),
                pltpu.make_async_copy(w2_hbm.at[expert], wf2.at[wslot], wsem.at[wslot]))

    def compute(slot):
        _tile_wait(h_hbm, bufs[slot], sem.at[slot])
        _unpack_bf16_pairs(bufs[slot], hb)
        x = hb[...]
        a = jnp.dot(x, w1b[...], preferred_element_type=F32)
        b = jnp.dot(x, w3b[...], preferred_element_type=F32)
        hid = _silu(a) * b
        y_ref[...] = jnp.dot(hid.astype(BF16), w2b[...], preferred_element_type=F32)

    @pl.when(i == 0)
    def _():
        issue(0, 0)
        for c in weight_copies(te_ref[0], 0):
            c.start()

    prev = jnp.maximum(i - 1, 0)

    @pl.when((i < n_used) & ((i == 0) | (te_ref[i] != te_ref[prev])))
    def _():
        wslot = seg_ref[i] % 2
        for c in weight_copies(te_ref[i], wslot):
            c.wait()
        w1b[...] = wf1[wslot].astype(BF16)
        w3b[...] = wf3[wslot].astype(BF16)
        w2b[...] = wf2[wslot].astype(BF16)

        @pl.when(nxt_ref[i] >= 0)
        def _():
            for c in weight_copies(nxt_ref[i], 1 - wslot):
                c.start()

    for slot in range(2):
        @pl.when((i < n_used - 1) & (i % 2 == slot))
        def _():
            issue(i + 1, 1 - slot)
            compute(slot)

        @pl.when((i == n_used - 1) & (i % 2 == slot))
        def _():
            compute(slot)

    @pl.when(i >= n_used)
    def _():
        y_ref[...] = jnp.zeros_like(y_ref)


def _moe(plan, h2_packed, w1, w3, w2):
    n_sorted = plan["src_row"].shape[0]
    n_tiles = n_sorted // TME
    grid_spec = pltpu.PrefetchScalarGridSpec(
        num_scalar_prefetch=5,
        grid=(n_tiles,),
        in_specs=[pl.BlockSpec(memory_space=pl.ANY)] * 4,
        out_specs=pl.BlockSpec((TME, D_MODEL), lambda i, *_: (i, 0)),
        scratch_shapes=[pltpu.VMEM((TME * PACK_ROWS, LANES), jnp.int32),
                        pltpu.VMEM((TME * PACK_ROWS, LANES), jnp.int32),
                        pltpu.SemaphoreType.DMA((2,)),
                        pltpu.VMEM((TME, D_MODEL), BF16),
                        pltpu.VMEM((2, D_MODEL, EXPERT_FF), F32), pltpu.VMEM((2, D_MODEL, EXPERT_FF), F32),
                        pltpu.VMEM((2, EXPERT_FF, D_MODEL), F32),
                        pltpu.SemaphoreType.DMA((2,)),
                        pltpu.VMEM((D_MODEL, EXPERT_FF), BF16), pltpu.VMEM((D_MODEL, EXPERT_FF), BF16),
                        pltpu.VMEM((EXPERT_FF, D_MODEL), BF16)],
    )
    return pl.pallas_call(
        _moe_kernel,
        name="moe",
        grid_spec=grid_spec,
        out_shape=jax.ShapeDtypeStruct((n_sorted, D_MODEL), F32),
        compiler_params=_cparams(1),
    )(plan["tile_expert"], plan["src_row"], plan["n_used"], plan["segment"], plan["next_expert"],
      h2_packed, w1, w3, w2)


def _finish_kernel(grp, pos_ref, x1_ref, mod_ref, rt_ref, nf_ref, y_hbm, out_ref, buf_a, buf_b, sem):
    i = pl.program_id(0)
    n_steps = pl.num_programs(0)
    slot = i % 2

    def issue(tile, slot_):
        base = (tile * TM + grp.row_off) * 2

        def body(r, carry):
            _row_copy(y_hbm, pos_ref[base + 2 * r], buf_a.at[slot_], r, sem.at[slot_]).start()
            _row_copy(y_hbm, pos_ref[base + 2 * r + 1], buf_b.at[slot_], r, sem.at[slot_]).start()
            return carry

        lax.fori_loop(0, TM, body, 0, unroll=8)

    @pl.when(i == 0)
    def _():
        issue(0, 0)

    @pl.when(i + 1 < n_steps)
    def _():
        issue(i + 1, 1 - slot)

    _tile_wait(y_hbm, buf_a.at[slot], sem.at[slot])
    _tile_wait(y_hbm, buf_b.at[slot], sem.at[slot])
    rt = rt_ref[...]
    moe = rt[:, 2:3] * buf_a[slot] + rt[:, 3:4] * buf_b[slot]
    x2 = x1_ref[...] + _mod_rows(mod_ref, 5) * moe.reshape(grp.nb, grp.tt, D_MODEL)
    ms = jnp.mean(x2 * x2, axis=-1, keepdims=True)
    out_ref[...] = x2 * lax.rsqrt(ms + EPS) * nf_ref[...]


def _finish(grp, pos, x1, mod, route_all, norm_f, y_sorted):
    off = grp.tile_off
    grid_spec = pltpu.PrefetchScalarGridSpec(
        num_scalar_prefetch=1,
        grid=(grp.n_tiles,),
        in_specs=[pl.BlockSpec((grp.nb, grp.tt, D_MODEL), grp.x_map()),
                  pl.BlockSpec((grp.nb, 6, D_MODEL), grp.mod_map()),
                  pl.BlockSpec((TM, LANES), lambda i, p: (i + off, 0)),
                  pl.BlockSpec((1, 1, D_MODEL), lambda i, p: (0, 0, 0)),
                  pl.BlockSpec(memory_space=pl.ANY)],
        out_specs=pl.BlockSpec((grp.nb, grp.tt, D_MODEL), grp.x_map()),
        scratch_shapes=[pltpu.VMEM((2, TM, D_MODEL), F32), pltpu.VMEM((2, TM, D_MODEL), F32),
                        pltpu.SemaphoreType.DMA((2,))],
    )
    return pl.pallas_call(
        functools.partial(_finish_kernel, grp),
        name="finish",
        grid_spec=grid_spec,
        out_shape=jax.ShapeDtypeStruct(x1.shape, F32),
        compiler_params=_cparams(1),
    )(pos, x1, mod, route_all, norm_f.reshape(1, 1, D_MODEL), y_sorted)


def _sort_plan(route_all):
    n_tok = route_all.shape[0]
    n_pairs = 2 * n_tok
    n_sorted = n_pairs + N_EXPERTS * TME
    flat_e = route_all[:, 0:2].astype(jnp.int32).reshape(n_pairs)
    onehot = (flat_e[:, None] == jnp.arange(N_EXPERTS, dtype=jnp.int32)[None, :]).astype(jnp.int32)
    csum = jnp.cumsum(onehot, axis=0)
    rank = jnp.sum(onehot * csum, axis=1) - 1
    counts = csum[-1]
    padded = ((counts + TME - 1) // TME) * TME
    ends = jnp.cumsum(padded)
    starts = ends - padded
    pos = starts[flat_e] + rank
    token = jnp.arange(n_pairs, dtype=jnp.int32) // 2
    src_row = jnp.zeros((n_sorted,), jnp.int32).at[pos].set(token)
    tile_start = jnp.arange(n_sorted // TME, dtype=jnp.int32) * TME
    tile_expert = jnp.sum((tile_start[:, None] >= ends[None, :]).astype(jnp.int32), axis=1)
    tile_expert = jnp.minimum(tile_expert, N_EXPERTS - 1)
    n_used = ends[-1] // TME
    is_first = jnp.concatenate([jnp.ones((1,), jnp.int32),
                                (tile_expert[1:] != tile_expert[:-1]).astype(jnp.int32)])
    segment = jnp.cumsum(is_first) - 1
    next_tile = ends[tile_expert] // TME
    next_expert = jnp.where(next_tile < n_used, tile_expert[jnp.minimum(next_tile, n_sorted // TME - 1)], -1)
    return dict(pos=pos.astype(jnp.int32), src_row=src_row, tile_expert=tile_expert.astype(jnp.int32),
                n_used=n_used.astype(jnp.int32).reshape(1), segment=segment.astype(jnp.int32),
                next_expert=next_expert.astype(jnp.int32))


def kernel(x_prompt, x_sample, c_prompt, c_sample, state_gla, state_pool, w_ada, b_ada, norm1, norm2, w_in,
           gate_up, gate_bias, gla_norm, pool_w, pool_scale, w_out, w_group, b_group, w_expert, b_expert,
           w1, w3, w2, norm_f):
    assert w_ada.shape[0] == 1, "single-layer step"
    bp, tp, _ = x_prompt.shape
    bs, ts, _ = x_sample.shape
    grp_p = _Group(bp, tp, 0, bs)
    grp_s = _Group(bs, ts, bp * tp, 0)
    n_tok = bp * tp + bs * ts

    n_c = bp + bs
    n_c_pad = -(-n_c // 8) * 8
    c_all = jnp.concatenate([c_sample, c_prompt, jnp.zeros((n_c_pad - n_c, D_MODEL), F32)], axis=0)
    mod = _adaln(c_all, w_ada[0], b_ada[0]).reshape(n_c_pad, 6, D_MODEL)
    mod_p = mod_s = mod

    wi = w_in[0]
    wq = wi[:, :QKVR_WIDTH].astype(BF16)
    wg = jnp.pad(wi[:, QKVR_WIDTH:QKVR_WIDTH + GATE_RANK], ((0, 0), (0, LANES - GATE_RANK))).astype(BF16)
    wu = wi[:, QKVR_WIDTH + GATE_RANK:].astype(BF16)
    gup = jnp.pad(gate_up[0], ((0, LANES - GATE_RANK), (0, 0))).astype(BF16)
    gb = gate_bias[0].reshape(1, GLA_KEY_WIDTH)
    pw = pool_w[0].astype(BF16)
    wo = w_out[0].astype(BF16)
    w_router = jnp.concatenate(
        [w_group[0], jnp.transpose(w_expert[0], (1, 0, 2)).reshape(D_MODEL, N_EXPERTS),
         jnp.zeros((D_MODEL, LANES - N_GROUPS - N_EXPERTS), F32)], axis=1)
    b_router = jnp.concatenate([b_group[0], b_expert[0].reshape(N_EXPERTS),
                                jnp.zeros((LANES - N_GROUPS - N_EXPERTS,), F32)]).reshape(1, LANES)

    qkvr_p, la_p, u_p = _in_proj(grp_p, x_prompt, mod_p, norm1[0], wq, wg, wu, gup, gb)
    qkvr_s, la_s, u_s = _in_proj(grp_s, x_sample, mod_s, norm1[0], wq, wg, wu, gup, gb)

    o_p, gla_p = _gla_prompt(bp, tp, qkvr_p, la_p)
    o_s, gla_s = _gla_decode(bs, ts, qkvr_s, la_s, state_gla[0])

    halo_per_tile = TM // HALO
    halo_map_p = lambda i: (jnp.maximum(i * halo_per_tile - 1, 0), 0)
    x1_p, h2_all, route_all = _mix_out(grp_p, n_tok, 0, True, o_p, qkvr_p, u_p, u_p, (HALO, POOL_WIDTH),
                                       halo_map_p, x_prompt, mod_p, norm2[0], gla_norm[0], pw, pool_scale[0],
                                       wo, w_router, b_router)
    x1_s, h2_all, route_all = _mix_out(grp_s, n_tok, PAST_LEN, False, o_s, qkvr_s, u_s, state_pool[0],
                                       (grp_s.nb, POOL_BUF, POOL_WIDTH), lambda i: (i, 0, 0), x_sample, mod_s,
                                       norm2[0], gla_norm[0], pw, pool_scale[0], wo, w_router, b_router,
                                       shared=(h2_all, route_all))

    plan = _sort_plan(route_all)
    y_sorted = _moe(plan, h2_all, w1[0], w3[0], w2[0])

    y_p = _finish(grp_p, plan["pos"], x1_p, mod_p, route_all, norm_f, y_sorted)
    y_s = _finish(grp_s, plan["pos"], x1_s, mod_s, route_all, norm_f, y_sorted)

    u_p3 = u_p.reshape(bp, tp, POOL_WIDTH)
    u_s3 = u_s.reshape(bs, ts, POOL_WIDTH)
    pool_p = jnp.concatenate([jnp.zeros((bp, POOL_BUF, POOL_WIDTH), F32), u_p3], axis=1)[:, -POOL_BUF:]
    pool_s = jnp.concatenate([state_pool[0], u_s3], axis=1)[:, -POOL_BUF:]
    return (y_p, y_s, gla_p[None], pool_p[None], gla_s[None], pool_s[None])
```

```python
import functools

import jax
import jax.numpy as jnp
from jax import lax
from jax.experimental import pallas as pl
from jax.experimental.pallas import tpu as pltpu

D_MODEL = 2048
GLA_HEADS = 4
GLA_DK = 128
GLA_DV = 256
GLA_KEY_WIDTH = GLA_HEADS * GLA_DK
GLA_WIDTH = GLA_HEADS * GLA_DV
POOL_WIDTH = 1024
POOL_WINDOWS = (2, 4, 8, 16)
POOL_GW = 256
POOL_BUF = 15
HALO = 16
GATE_RANK = 16
GATE_TEMP = 16.0
N_GROUPS = 4
EXPERTS_PER_GROUP = 8
N_EXPERTS = 32
EXPERT_FF = 512
EPS = 1e-6
PAST_LEN = 16384
QKVR_WIDTH = 2 * GLA_KEY_WIDTH + 2 * GLA_WIDTH

LANES = 128
TM = 256
TME = 256
GLA_CHUNK = 64
GLA_SUB = 16
GLA_STEP = 512
VMEM_LIMIT = 56 * 1024 * 1024

BF16 = jnp.bfloat16
F32 = jnp.float32
NEG = -1e30


def _cparams(n_axes):
    return pltpu.CompilerParams(dimension_semantics=("arbitrary",) * n_axes,
                                vmem_limit_bytes=VMEM_LIMIT)


def _silu(x):
    return x / (1.0 + jnp.exp(-x))


def _bdot(a, b):
    return jnp.dot(a.astype(BF16), b.astype(BF16), preferred_element_type=F32)


def _split3(a):
    a1 = a.astype(BF16)
    r1 = a - a1.astype(F32)
    a2 = r1.astype(BF16)
    a3 = (r1 - a2.astype(F32)).astype(BF16)
    return a1, a2, a3


def _dot_f32(a, b):
    a1, a2, a3 = _split3(a)
    b1, b2, b3 = _split3(b)
    d = lambda x, y: jnp.dot(x, y, preferred_element_type=F32)
    small = d(a2, b2) + d(a1, b3) + d(a3, b1)
    mid = d(a1, b2) + d(a2, b1)
    return d(a1, b1) + (mid + small)


def _dot_exact_lhs(tri_bf16, g):
    g1, g2, g3 = _split3(g)
    d = lambda y: jnp.dot(tri_bf16, y, preferred_element_type=F32)
    return d(g1) + (d(g2) + d(g3))


def _adaln_kernel(c_ref, w_ref, b_ref, o_ref):
    c = c_ref[...]
    o_ref[...] = _bdot(_silu(c), w_ref[...]) + b_ref[...]


def _adaln(c_all, w_ada, b_ada):
    n, d = c_all.shape
    width = w_ada.shape[1]
    tn = 1024
    return pl.pallas_call(
        _adaln_kernel,
        name="adaln",
        grid=(width // tn,),
        in_specs=[pl.BlockSpec((n, d), lambda j: (0, 0)),
                  pl.BlockSpec((d, tn), lambda j: (0, j)),
                  pl.BlockSpec((1, tn), lambda j: (0, j))],
        out_specs=pl.BlockSpec((n, tn), lambda j: (0, j)),
        out_shape=jax.ShapeDtypeStruct((n, width), F32),
        compiler_params=_cparams(1),
    )(c_all, w_ada, b_ada.reshape(1, width))


class _Group:
    def __init__(self, batch, seq, row_off, mod_off):
        self.batch, self.seq, self.row_off, self.mod_off = batch, seq, row_off, mod_off
        if seq >= TM:
            assert seq % TM == 0
            self.nb, self.tt = 1, TM
            self.tiles_per_batch = seq // TM
            self.n_tiles = batch * self.tiles_per_batch
        else:
            assert TM % seq == 0 and batch % (TM // seq) == 0
            self.nb, self.tt = TM // seq, seq
            self.tiles_per_batch = 1
            self.n_tiles = batch // self.nb
        self.rows = batch * seq
        self.tile_off = row_off // TM

    def x_map(self):
        if self.nb == 1:
            tpb = self.tiles_per_batch
            return lambda i, *_: (i // tpb, i % tpb, 0)
        return lambda i, *_: (i, 0, 0)

    def mod_map(self):
        assert self.mod_off % self.nb == 0
        off = self.mod_off // self.nb
        if self.nb == 1:
            tpb = self.tiles_per_batch
            return lambda i, *_: (i // tpb + off, 0, 0)
        return lambda i, *_: (i + off, 0, 0)


def _mod_rows(mod_ref, idx):
    return mod_ref[:, idx:idx + 1, :]


def _rmsnorm_mod(x, gain, scale, shift):
    ms = jnp.mean(x * x, axis=-1, keepdims=True)
    y = x * lax.rsqrt(ms + EPS) * gain
    return y * (1.0 + scale) + shift


def _in_proj_kernel(x_ref, mod_ref, n1_ref, wq_ref, wg_ref, wu_ref, gup_ref, gb_ref,
                    qkvr_ref, la_ref, u_ref):
    x = x_ref[...]
    h = _rmsnorm_mod(x, n1_ref[...], _mod_rows(mod_ref, 1), _mod_rows(mod_ref, 0))
    hb = h.reshape(TM, D_MODEL).astype(BF16)
    qkvr_ref[...] = jnp.dot(hb, wq_ref[...], preferred_element_type=F32)
    u_ref[...] = jnp.dot(hb, wu_ref[...], preferred_element_type=F32)
    g_lr = jnp.dot(hb, wg_ref[...], preferred_element_type=F32)
    pre = jnp.dot(g_lr.astype(BF16), gup_ref[...], preferred_element_type=F32) + gb_ref[...]
    log_sig = jnp.minimum(pre, 0.0) - jnp.log1p(jnp.exp(-jnp.abs(pre)))
    la_ref[...] = log_sig / GATE_TEMP


def _in_proj(grp, x, mod, norm1, wq, wg, wu, gup, gb):
    const = lambda i: (0, 0)
    row = lambda i: (i, 0)
    return pl.pallas_call(
        _in_proj_kernel,
        name="in_proj",
        grid=(grp.n_tiles,),
        in_specs=[pl.BlockSpec((grp.nb, grp.tt, D_MODEL), grp.x_map()),
                  pl.BlockSpec((grp.nb, 6, D_MODEL), grp.mod_map()),
                  pl.BlockSpec((1, 1, D_MODEL), lambda i: (0, 0, 0)),
                  pl.BlockSpec(wq.shape, const, pipeline_mode=pl.Buffered(1)),
                  pl.BlockSpec(wg.shape, const, pipeline_mode=pl.Buffered(1)),
                  pl.BlockSpec(wu.shape, const, pipeline_mode=pl.Buffered(1)),
                  pl.BlockSpec(gup.shape, const, pipeline_mode=pl.Buffered(1)),
                  pl.BlockSpec(gb.shape, const, pipeline_mode=pl.Buffered(1))],
        out_specs=[pl.BlockSpec((TM, QKVR_WIDTH), row),
                   pl.BlockSpec((TM, GLA_KEY_WIDTH), row),
                   pl.BlockSpec((TM, POOL_WIDTH), row)],
        out_shape=[jax.ShapeDtypeStruct((grp.rows, QKVR_WIDTH), F32),
                   jax.ShapeDtypeStruct((grp.rows, GLA_KEY_WIDTH), F32),
                   jax.ShapeDtypeStruct((grp.rows, POOL_WIDTH), F32)],
        compiler_params=_cparams(1),
    )(x, mod, norm1.reshape(1, 1, D_MODEL), wq, wg, wu, gup, gb)


def _gla_head_chunk(q, k, v, g, state_t, chunk, sub):
    rows = lax.broadcasted_iota(jnp.int32, (chunk, chunk), 0)
    cols = lax.broadcasted_iota(jnp.int32, (chunk, chunk), 1)
    tri = (rows >= cols).astype(BF16)
    b = _dot_exact_lhs(tri, g)
    q = q * (GLA_DK ** -0.5)
    nt = (((1,), (1,)), ((), ()))
    tn = (((0,), (0,)), ((), ()))
    o = lax.dot_general((q * jnp.exp(b)).astype(BF16), state_t.astype(BF16), nt,
                        preferred_element_type=F32)

    b_last = b[chunk - 1:chunk, :]
    k_dec = k * jnp.exp(b_last - b)
    new_state_t = jnp.exp(b_last) * state_t + lax.dot_general(
        v.astype(BF16), k_dec.astype(BF16), tn, preferred_element_type=F32)

    lane = lax.broadcasted_iota(jnp.int32, (sub, chunk), 1)
    sub_row = lax.broadcasted_iota(jnp.int32, (sub, LANES), 0)
    key_row = lax.broadcasted_iota(jnp.int32, (chunk, LANES), 0)
    p_blocks = []
    for s in range(chunk // sub):
        lo = s * sub
        q_s = q[lo:lo + sub, :]
        b_s = b[lo:lo + sub, :]
        p = jnp.zeros((sub, chunk), F32)
        if s > 0:
            ref_row = b[lo - 1:lo, :]
            q_rel = q_s * jnp.exp(b_s - ref_row)
            k_rel = k * jnp.exp(jnp.where(key_row < lo, ref_row - b, NEG))
            p = lax.dot_general(q_rel.astype(BF16), k_rel.astype(BF16), nt, preferred_element_type=F32)
        for jl in range(sub):
            j = lo + jl
            expo = jnp.where(sub_row >= jl, b_s - b[j:j + 1, :], NEG)
            col = jnp.sum(q_s * k[j:j + 1, :] * jnp.exp(expo), axis=-1, keepdims=True)
            p = jnp.where(lane == j, col, p)
        p_blocks.append(p)
    p_full = p_blocks[0] if len(p_blocks) == 1 else jnp.concatenate(p_blocks, axis=0)
    o = o + _bdot(p_full, v)
    return o, new_state_t


def _gla_prompt_kernel(q_ref, k_ref, v_ref, la_ref, o_ref, s_ref, st_ref):
    step = pl.program_id(2)

    @pl.when(step == 0)
    def _():
        st_ref[...] = jnp.zeros_like(st_ref)

    def body(c, carry):
        r0 = pl.multiple_of(c * GLA_CHUNK, GLA_CHUNK)
        sl = pl.ds(r0, GLA_CHUNK)
        o, new_state_t = _gla_head_chunk(q_ref[sl, :], k_ref[sl, :], v_ref[sl, :], la_ref[sl, :],
                                         st_ref[...], GLA_CHUNK, GLA_SUB)
        o_ref[sl, :] = o
        st_ref[...] = new_state_t
        return carry

    lax.fori_loop(0, GLA_STEP // GLA_CHUNK, body, 0)

    @pl.when(step == pl.num_programs(2) - 1)
    def _():
        s_ref[0, 0] = st_ref[...].T


def _gla_prompt(batch, seq, qkvr, log_a):
    steps = seq // GLA_STEP
    row = lambda b, h, s: b * steps + s
    return pl.pallas_call(
        _gla_prompt_kernel,
        name="gla_prompt",
        grid=(batch, GLA_HEADS, steps),
        in_specs=[pl.BlockSpec((GLA_STEP, GLA_DK), lambda b, h, s: (row(b, h, s), h)),
                  pl.BlockSpec((GLA_STEP, GLA_DK), lambda b, h, s: (row(b, h, s), GLA_HEADS + h)),
                  pl.BlockSpec((GLA_STEP, GLA_DV), lambda b, h, s: (row(b, h, s), GLA_HEADS + h)),
                  pl.BlockSpec((GLA_STEP, GLA_DK), lambda b, h, s: (row(b, h, s), h))],
        out_specs=[pl.BlockSpec((GLA_STEP, GLA_DV), lambda b, h, s: (row(b, h, s), h)),
                   pl.BlockSpec((1, 1, GLA_DK, GLA_DV), lambda b, h, s: (b, h, 0, 0))],
        out_shape=[jax.ShapeDtypeStruct((batch * seq, GLA_WIDTH), F32),
                   jax.ShapeDtypeStruct((batch, GLA_HEADS, GLA_DK, GLA_DV), F32)],
        scratch_shapes=[pltpu.VMEM((GLA_DV, GLA_DK), F32)],
        compiler_params=_cparams(3),
    )(qkvr, qkvr, qkvr, log_a)


GLA_DEC_BB = 16


def _gla_decode_kernel(seq, q_ref, k_ref, v_ref, la_ref, s0_ref, o_ref, s_ref):
    def body(i, carry):
        r0 = pl.multiple_of(i * seq, seq)
        sl = pl.ds(r0, seq)
        o, new_state_t = _gla_head_chunk(q_ref[sl, :], k_ref[sl, :], v_ref[sl, :], la_ref[sl, :],
                                         s0_ref[i, 0].T, seq, seq)
        o_ref[sl, :] = o
        s_ref[i, 0] = new_state_t.T
        return carry

    lax.fori_loop(0, GLA_DEC_BB, body, 0)


def _gla_decode(batch, seq, qkvr, log_a, state):
    rows = GLA_DEC_BB * seq
    return pl.pallas_call(
        functools.partial(_gla_decode_kernel, seq),
        name="gla_decode",
        grid=(batch // GLA_DEC_BB, GLA_HEADS),
        in_specs=[pl.BlockSpec((rows, GLA_DK), lambda i, h: (i, h)),
                  pl.BlockSpec((rows, GLA_DK), lambda i, h: (i, GLA_HEADS + h)),
                  pl.BlockSpec((rows, GLA_DV), lambda i, h: (i, GLA_HEADS + h)),
                  pl.BlockSpec((rows, GLA_DK), lambda i, h: (i, h)),
                  pl.BlockSpec((GLA_DEC_BB, 1, GLA_DK, GLA_DV), lambda i, h: (i, h, 0, 0))],
        out_specs=[pl.BlockSpec((rows, GLA_DV), lambda i, h: (i, h)),
                   pl.BlockSpec((GLA_DEC_BB, 1, GLA_DK, GLA_DV), lambda i, h: (i, h, 0, 0))],
        out_shape=[jax.ShapeDtypeStruct((batch * seq, GLA_WIDTH), F32),
                   jax.ShapeDtypeStruct((batch, GLA_HEADS, GLA_DK, GLA_DV), F32)],
        compiler_params=_cparams(2),
    )(qkvr, qkvr, qkvr, log_a, state)


def _route(logits):
    lane = lax.broadcasted_iota(jnp.int32, logits.shape, 1)
    big = jnp.int32(10 ** 6)
    is_group = lane < N_GROUPS
    lg = jnp.where(is_group, logits, NEG)
    mg = jnp.max(lg, axis=-1, keepdims=True)
    g_idx = jnp.min(jnp.where(is_group & (lg == mg), lane, big), axis=-1, keepdims=True)
    denom = jnp.sum(jnp.where(is_group, jnp.exp(lg - mg), 0.0), axis=-1, keepdims=True)
    p_sel = 1.0 / denom
    first = N_GROUPS + EXPERTS_PER_GROUP * g_idx
    in_grp = (lane >= first) & (lane < first + EXPERTS_PER_GROUP)
    le = jnp.where(in_grp, logits, NEG)
    m1 = jnp.max(le, axis=-1, keepdims=True)
    i1 = jnp.min(jnp.where(in_grp & (le == m1), lane, big), axis=-1, keepdims=True)
    rest = in_grp & (lane != i1)
    le2 = jnp.where(rest, logits, NEG)
    m2 = jnp.max(le2, axis=-1, keepdims=True)
    i2 = jnp.min(jnp.where(rest & (le2 == m2), lane, big), axis=-1, keepdims=True)
    e2 = jnp.exp(m2 - m1)
    w1 = p_sel / (1.0 + e2)
    w2 = p_sel * e2 / (1.0 + e2)
    ex1 = (i1 - N_GROUPS).astype(F32)
    ex2 = (i2 - N_GROUPS).astype(F32)
    return jnp.where(lane == 0, ex1, jnp.where(lane == 1, ex2,
                     jnp.where(lane == 2, w1, jnp.where(lane == 3, w2, 0.0))))


def _mix_out_kernel(grp, pos0, zero_first_halo, n_alias,
                    o_ref, r_ref, u_ref, halo_ref, x_ref, mod_ref, n2_ref, gn_ref, pw_ref, ps_ref,
                    wo_ref, wr_ref, br_ref, *rest):
    x1_ref, h2_ref, rt_ref, ext_ref, ymix_ref = rest[n_alias:]
    i = pl.program_id(0)

    @pl.when(i < grp.n_tiles)
    def _():
        _mix_out_tile(grp, pos0, zero_first_halo, i, o_ref, r_ref, u_ref, halo_ref, x_ref, mod_ref, n2_ref,
                      gn_ref, pw_ref, ps_ref, wo_ref, wr_ref, br_ref, x1_ref, h2_ref, rt_ref, ext_ref, ymix_ref)

    @pl.when(i >= grp.n_tiles)
    def _():
        h2_ref[...] = jnp.zeros_like(h2_ref)
        rt_ref[...] = jnp.zeros_like(rt_ref)


def _mix_out_tile(grp, pos0, zero_first_halo, i, o_ref, r_ref, u_ref, halo_ref, x_ref, mod_ref, n2_ref,
                  gn_ref, pw_ref, ps_ref, wo_ref, wr_ref, br_ref, x1_ref, h2_ref, rt_ref, ext_ref, ymix_ref):
    nb, tt = grp.nb, grp.tt
    hist = halo_ref.shape[-2]

    for h in range(GLA_HEADS):
        cs = slice(h * GLA_DV, (h + 1) * GLA_DV)
        oh = o_ref[:, cs]
        ms = jnp.mean(oh * oh, axis=-1, keepdims=True)
        yh = oh * lax.rsqrt(ms + EPS) * gn_ref[:, cs] * _silu(r_ref[:, cs])
        ymix_ref[:, cs] = yh.astype(BF16)

    halo = halo_ref[...]
    if zero_first_halo:
        halo = jnp.where(i % grp.tiles_per_batch == 0, 0.0, halo)
    ext_ref[:, HALO - hist:HALO, :] = halo.reshape(nb, hist, POOL_WIDTH)
    u = u_ref[...].reshape(nb, tt, POOL_WIDTH)
    ext_ref[:, HALO:HALO + tt, :] = u
    t_idx = lax.broadcasted_iota(jnp.int32, (nb, tt, POOL_GW), 1)
    if grp.nb == 1:
        pos = (i % grp.tiles_per_batch) * TM + t_idx + pos0
    else:
        pos = t_idx + pos0
    for gi, w in enumerate(POOL_WINDOWS):
        cs = slice(gi * POOL_GW, (gi + 1) * POOL_GW)
        acc = ext_ref[:, pl.ds(HALO, tt), cs]
        for kk in range(1, w):
            acc = acc + ext_ref[:, pl.ds(HALO - kk, tt), cs]
        cnt = jnp.minimum(pos + 1, w).astype(F32)
        pooled = acc / cnt - u[:, :, cs]
        yp = _bdot(pooled.reshape(TM, POOL_GW), pw_ref[gi]) * ps_ref[:, cs]
        ymix_ref[:, GLA_WIDTH + gi * POOL_GW:GLA_WIDTH + (gi + 1) * POOL_GW] = yp.astype(BF16)

    y = jnp.dot(ymix_ref[...], wo_ref[...], preferred_element_type=F32)
    x1 = x_ref[...] + _mod_rows(mod_ref, 2) * y.reshape(nb, tt, D_MODEL)
    x1_ref[...] = x1
    h2 = _rmsnorm_mod(x1, n2_ref[...], _mod_rows(mod_ref, 4), _mod_rows(mod_ref, 3)).reshape(TM, D_MODEL)
    h2_ref[...] = h2
    logits = _dot_f32(h2, wr_ref[...]) + br_ref[...]
    rt_ref[...] = _route(logits)


def _mix_out(grp, n_tok, pos0, zero_first_halo, o, qkvr, u, halo_src, halo_block, halo_map, x, mod, norm2,
             gla_norm, pool_w, pool_scale, w_out, w_router, b_router, shared=()):
    n_alias = len(shared)
    n = grp.n_tiles
    n_fill = 0 if shared else n_tok // TM - n
    assert n_fill == 0 or grp.tile_off == 0
    clamp = lambda f: (lambda i: f(jnp.minimum(i, n - 1)))
    const2 = lambda i: (0, 0)
    row = clamp(lambda i: (i, 0))
    off = grp.tile_off
    kern = functools.partial(_mix_out_kernel, grp, pos0, zero_first_halo, n_alias)
    return pl.pallas_call(
        kern,
        name="mix_out",
        grid=(n + n_fill,),
        in_specs=[pl.BlockSpec((TM, GLA_WIDTH), row),
                  pl.BlockSpec((TM, GLA_WIDTH), clamp(lambda i: (i, 2))),
                  pl.BlockSpec((TM, POOL_WIDTH), row),
                  pl.BlockSpec(halo_block, clamp(halo_map)),
                  pl.BlockSpec((grp.nb, grp.tt, D_MODEL), clamp(grp.x_map())),
                  pl.BlockSpec((grp.nb, 6, D_MODEL), clamp(grp.mod_map())),
                  pl.BlockSpec((1, 1, D_MODEL), lambda i: (0, 0, 0)),
                  pl.BlockSpec((1, GLA_WIDTH), const2),
                  pl.BlockSpec(pool_w.shape, lambda i: (0, 0, 0), pipeline_mode=pl.Buffered(1)),
                  pl.BlockSpec((1, POOL_WIDTH), const2),
                  pl.BlockSpec(w_out.shape, const2, pipeline_mode=pl.Buffered(1)),
                  pl.BlockSpec(w_router.shape, const2, pipeline_mode=pl.Buffered(1)),
                  pl.BlockSpec((1, LANES), const2)]
                 + [pl.BlockSpec(memory_space=pl.ANY)] * n_alias,
        out_specs=[pl.BlockSpec((grp.nb, grp.tt, D_MODEL), clamp(grp.x_map())),
                   pl.BlockSpec((TM, D_MODEL), lambda i: (i + off, 0)),
                   pl.BlockSpec((TM, LANES), lambda i: (i + off, 0))],
        out_shape=[jax.ShapeDtypeStruct(x.shape, F32),
                   jax.ShapeDtypeStruct((n_tok, D_MODEL), F32),
                   jax.ShapeDtypeStruct((n_tok, LANES), F32)],
        scratch_shapes=[pltpu.VMEM((grp.nb, HALO + grp.tt, POOL_WIDTH), F32),
                        pltpu.VMEM((TM, D_MODEL), BF16)],
        input_output_aliases={13 + k: 1 + k for k in range(n_alias)},
        compiler_params=_cparams(1),
    )(o, qkvr, u, halo_src, x, mod, norm2.reshape(1, 1, D_MODEL), gla_norm.reshape(1, GLA_WIDTH),
      pool_w, pool_scale.reshape(1, POOL_WIDTH), w_out, w_router, b_router, *shared)


def _row_copy(src_hbm, src_row, dst, dst_row, sem):
    return pltpu.make_async_copy(src_hbm.at[pl.ds(src_row, 1), :], dst.at[pl.ds(dst_row, 1), :], sem)


def _tile_wait(src_hbm, dst, sem):
    pltpu.make_async_copy(src_hbm.at[pl.ds(0, dst.shape[0]), :], dst, sem).wait()


def _moe_kernel(te_ref, src_ref, nu_ref, seg_ref, nxt_ref, h_hbm, w1_hbm, w3_hbm, w2_hbm, y_ref,
                buf0, buf1, sem, wf1, wf3, wf2, wsem, w1b, w3b, w2b):
    i = pl.program_id(0)
    n_used = nu_ref[0]
    bufs = (buf0, buf1)

    def issue(tile, slot, part=0, n_parts=1):
        base = tile * TME
        per = TME // n_parts
        for r in range(part * per, (part + 1) * per):
            _row_copy(h_hbm, src_ref[base + r], bufs[slot], r, sem.at[slot]).start()

    def weight_copies(expert, wslot):
        return (pltpu.make_async_copy(w1_hbm.at[expert], wf1.at[wslot], wsem.at[wslot]),
                pltpu.make_async_copy(w3_hbm.at[expert], wf3.at[wslot], wsem.at[wslot]),
                pltpu.make_async_copy(w2_hbm.at[expert], wf2.at[wslot], wsem.at[wslot]))

    def compute(slot, next_tile=None):
        nxt = (lambda part: issue(next_tile, 1 - slot, part, 4)) if next_tile is not None else (lambda part: None)
        _tile_wait(h_hbm, bufs[slot], sem.at[slot])
        nxt(0)
        x = bufs[slot][...].astype(BF16)
        nxt(1)
        a = jnp.dot(x, w1b[...], preferred_element_type=F32)
        nxt(2)
        b = jnp.dot(x, w3b[...], preferred_element_type=F32)
        nxt(3)
        hid = _silu(a) * b
        y_ref[...] = jnp.dot(hid.astype(BF16), w2b[...], preferred_element_type=F32)

    @pl.when(i == 0)
    def _():
        issue(0, 0)
        for c in weight_copies(te_ref[0], 0):
            c.start(priority=1)

    prev = jnp.maximum(i - 1, 0)

    @pl.when((i < n_used) & ((i == 0) | (te_ref[i] != te_ref[prev])))
    def _():
        wslot = seg_ref[i] % 2
        for c in weight_copies(te_ref[i], wslot):
            c.wait()
        w1b[...] = wf1[wslot].astype(BF16)
        w3b[...] = wf3[wslot].astype(BF16)
        w2b[...] = wf2[wslot].astype(BF16)

        @pl.when(nxt_ref[i] >= 0)
        def _():
            for c in weight_copies(nxt_ref[i], 1 - wslot):
                c.start(priority=1)

    for slot in range(2):
        @pl.when((i < n_used - 1) & (i % 2 == slot))
        def _():
            compute(slot, next_tile=i + 1)

        @pl.when((i == n_used - 1) & (i % 2 == slot))
        def _():
            compute(slot)

    @pl.when(i >= n_used)
    def _():
        y_ref[...] = jnp.zeros_like(y_ref)


def _moe(plan, h2_packed, w1, w3, w2):
    n_sorted = plan["src_row"].shape[0]
    n_tiles = n_sorted // TME
    grid_spec = pltpu.PrefetchScalarGridSpec(
        num_scalar_prefetch=5,
        grid=(n_tiles,),
        in_specs=[pl.BlockSpec(memory_space=pl.ANY)] * 4,
        out_specs=pl.BlockSpec((TME, D_MODEL), lambda i, *_: (i, 0)),
        scratch_shapes=[pltpu.VMEM((TME, D_MODEL), F32), pltpu.VMEM((TME, D_MODEL), F32),
                        pltpu.SemaphoreType.DMA((2,)),
                        pltpu.VMEM((2, D_MODEL, EXPERT_FF), F32), pltpu.VMEM((2, D_MODEL, EXPERT_FF), F32),
                        pltpu.VMEM((2, EXPERT_FF, D_MODEL), F32),
                        pltpu.SemaphoreType.DMA((2,)),
                        pltpu.VMEM((D_MODEL, EXPERT_FF), BF16), pltpu.VMEM((D_MODEL, EXPERT_FF), BF16),
                        pltpu.VMEM((EXPERT_FF, D_MODEL), BF16)],
    )
    return pl.pallas_call(
        _moe_kernel,
        name="moe",
        grid_spec=grid_spec,
        out_shape=jax.ShapeDtypeStruct((n_sorted, D_MODEL), F32),
        compiler_params=_cparams(1),
    )(plan["tile_expert"], plan["src_row"], plan["n_used"], plan["segment"], plan["next_expert"],
      h2_packed, w1, w3, w2)


def _finish_kernel(grp, pos_ref, x1_ref, mod_ref, rt_ref, nf_ref, y_hbm, out_ref, buf_a, buf_b, sem):
    i = pl.program_id(0)
    n_steps = pl.num_programs(0)
    slot = i % 2

    def issue(tile, slot_):
        base = (tile * TM + grp.row_off) * 2

        def body(r, carry):
            _row_copy(y_hbm, pos_ref[base + 2 * r], buf_a.at[slot_], r, sem.at[slot_]).start()
            _row_copy(y_hbm, pos_ref[base + 2 * r + 1], buf_b.at[slot_], r, sem.at[slot_]).start()
            return carry

        lax.fori_loop(0, TM, body, 0, unroll=8)

    @pl.when(i == 0)
    def _():
        issue(0, 0)

    @pl.when(i + 1 < n_steps)
    def _():
        issue(i + 1, 1 - slot)

    _tile_wait(y_hbm, buf_a.at[slot], sem.at[slot])
    _tile_wait(y_hbm, buf_b.at[slot], sem.at[slot])
    rt = rt_ref[...]
    moe = rt[:, 2:3] * buf_a[slot] + rt[:, 3:4] * buf_b[slot]
    x2 = x1_ref[...] + _mod_rows(mod_ref, 5) * moe.reshape(grp.nb, grp.tt, D_MODEL)
    ms = jnp.mean(x2 * x2, axis=-1, keepdims=True)
    out_ref[...] = x2 * lax.rsqrt(ms + EPS) * nf_ref[...]


def _finish(grp, pos, x1, mod, route_all, norm_f, y_sorted):
    off = grp.tile_off
    grid_spec = pltpu.PrefetchScalarGridSpec(
        num_scalar_prefetch=1,
        grid=(grp.n_tiles,),
        in_specs=[pl.BlockSpec((grp.nb, grp.tt, D_MODEL), grp.x_map()),
                  pl.BlockSpec((grp.nb, 6, D_MODEL), grp.mod_map()),
                  pl.BlockSpec((TM, LANES), lambda i, p: (i + off, 0)),
                  pl.BlockSpec((1, 1, D_MODEL), lambda i, p: (0, 0, 0)),
                  pl.BlockSpec(memory_space=pl.ANY)],
        out_specs=pl.BlockSpec((grp.nb, grp.tt, D_MODEL), grp.x_map()),
        scratch_shapes=[pltpu.VMEM((2, TM, D_MODEL), F32), pltpu.VMEM((2, TM, D_MODEL), F32),
                        pltpu.SemaphoreType.DMA((2,))],
    )
    return pl.pallas_call(
        functools.partial(_finish_kernel, grp),
        name="finish",
        grid_spec=grid_spec,
        out_shape=jax.ShapeDtypeStruct(x1.shape, F32),
        compiler_params=_cparams(1),
    )(pos, x1, mod, route_all, norm_f.reshape(1, 1, D_MODEL), y_sorted)


def _sort_plan(route_all):
    n_tok = route_all.shape[0]
    n_pairs = 2 * n_tok
    n_sorted = n_pairs + N_EXPERTS * TME
    flat_e = route_all[:, 0:2].astype(jnp.int32).reshape(n_pairs)
    onehot = (flat_e[:, None] == jnp.arange(N_EXPERTS, dtype=jnp.int32)[None, :]).astype(jnp.int32)
    csum = jnp.cumsum(onehot, axis=0)
    rank = jnp.sum(onehot * csum, axis=1) - 1
    counts = csum[-1]
    padded = ((counts + TME - 1) // TME) * TME
    ends = jnp.cumsum(padded)
    starts = ends - padded
    pos = starts[flat_e] + rank
    token = jnp.arange(n_pairs, dtype=jnp.int32) // 2
    src_row = jnp.zeros((n_sorted,), jnp.int32).at[pos].set(token)
    tile_start = jnp.arange(n_sorted // TME, dtype=jnp.int32) * TME
    tile_expert = jnp.sum((tile_start[:, None] >= ends[None, :]).astype(jnp.int32), axis=1)
    tile_expert = jnp.minimum(tile_expert, N_EXPERTS - 1)
    n_used = ends[-1] // TME
    is_first = jnp.concatenate([jnp.ones((1,), jnp.int32),
                                (tile_expert[1:] != tile_expert[:-1]).astype(jnp.int32)])
    segment = jnp.cumsum(is_first) - 1
    next_tile = ends[tile_expert] // TME
    next_expert = jnp.where(next_tile < n_used, tile_expert[jnp.minimum(next_tile, n_sorted // TME - 1)], -1)
    return dict(pos=pos.astype(jnp.int32), src_row=src_row, tile_expert=tile_expert.astype(jnp.int32),
                n_used=n_used.astype(jnp.int32).reshape(1), segment=segment.astype(jnp.int32),
                next_expert=next_expert.astype(jnp.int32))


def kernel(x_prompt, x_sample, c_prompt, c_sample, state_gla, state_pool, w_ada, b_ada, norm1, norm2, w_in,
           gate_up, gate_bias, gla_norm, pool_w, pool_scale, w_out, w_group, b_group, w_expert, b_expert,
           w1, w3, w2, norm_f):
    assert w_ada.shape[0] == 1, "single-layer step"
    bp, tp, _ = x_prompt.shape
    bs, ts, _ = x_sample.shape
    grp_p = _Group(bp, tp, 0, bs)
    grp_s = _Group(bs, ts, bp * tp, 0)
    n_tok = bp * tp + bs * ts

    n_c = bp + bs
    n_c_pad = -(-n_c // 8) * 8
    c_all = jnp.concatenate([c_sample, c_prompt, jnp.zeros((n_c_pad - n_c, D_MODEL), F32)], axis=0)
    mod = _adaln(c_all, w_ada[0], b_ada[0]).reshape(n_c_pad, 6, D_MODEL)
    mod_p = mod_s = mod

    wi = w_in[0]
    wq = wi[:, :QKVR_WIDTH].astype(BF16)
    wg = jnp.pad(wi[:, QKVR_WIDTH:QKVR_WIDTH + GATE_RANK], ((0, 0), (0, LANES - GATE_RANK))).astype(BF16)
    wu = wi[:, QKVR_WIDTH + GATE_RANK:].astype(BF16)
    gup = jnp.pad(gate_up[0], ((0, LANES - GATE_RANK), (0, 0))).astype(BF16)
    gb = gate_bias[0].reshape(1, GLA_KEY_WIDTH)
    pw = pool_w[0].astype(BF16)
    wo = w_out[0].astype(BF16)
    w_router = jnp.concatenate(
        [w_group[0], jnp.transpose(w_expert[0], (1, 0, 2)).reshape(D_MODEL, N_EXPERTS),
         jnp.zeros((D_MODEL, LANES - N_GROUPS - N_EXPERTS), F32)], axis=1)
    b_router = jnp.concatenate([b_group[0], b_expert[0].reshape(N_EXPERTS),
                                jnp.zeros((LANES - N_GROUPS - N_EXPERTS,), F32)]).reshape(1, LANES)

    qkvr_p, la_p, u_p = _in_proj(grp_p, x_prompt, mod_p, norm1[0], wq, wg, wu, gup, gb)
    qkvr_s, la_s, u_s = _in_proj(grp_s, x_sample, mod_s, norm1[0], wq, wg, wu, gup, gb)

    o_p, gla_p = _gla_prompt(bp, tp, qkvr_p, la_p)
    o_s, gla_s = _gla_decode(bs, ts, qkvr_s, la_s, state_gla[0])

    halo_per_tile = TM // HALO
    halo_map_p = lambda i: (jnp.maximum(i * halo_per_tile - 1, 0), 0)
    x1_p, h2_all, route_all = _mix_out(grp_p, n_tok, 0, True, o_p, qkvr_p, u_p, u_p, (HALO, POOL_WIDTH),
                                       halo_map_p, x_prompt, mod_p, norm2[0], gla_norm[0], pw, pool_scale[0],
                                       wo, w_router, b_router)
    x1_s, h2_all, route_all = _mix_out(grp_s, n_tok, PAST_LEN, False, o_s, qkvr_s, u_s, state_pool[0],
                                       (grp_s.nb, POOL_BUF, POOL_WIDTH), lambda i: (i, 0, 0), x_sample, mod_s,
                                       norm2[0], gla_norm[0], pw, pool_scale[0], wo, w_router, b_router,
                                       shared=(h2_all, route_all))

    plan = _sort_plan(route_all)
    y_sorted = _moe(plan, h2_all, w1[0], w3[0], w2[0])

    y_p = _finish(grp_p, plan["pos"], x1_p, mod_p, route_all, norm_f, y_sorted)
    y_s = _finish(grp_s, plan["pos"], x1_s, mod_s, route_all, norm_f, y_sorted)

    u_p3 = u_p.reshape(bp, tp, POOL_WIDTH)
    u_s3 = u_s.reshape(bs, ts, POOL_WIDTH)
    pool_p = jnp.concatenate([jnp.zeros((bp, POOL_BUF, POOL_WIDTH), F32), u_p3], axis=1)[:, -POOL_BUF:]
    pool_s = jnp.concatenate([state_pool[0], u_s3], axis=1)[:, -POOL_BUF:]
    return (y_p, y_s, gla_p[None], pool_p[None], gla_s[None], pool_s[None])
```

```python
import functools

import jax
import jax.numpy as jnp
from jax import lax
from jax.experimental import pallas as pl
from jax.experimental.pallas import tpu as pltpu

D_MODEL = 2048
GLA_HEADS = 4
GLA_DK = 128
GLA_DV = 256
GLA_KEY_WIDTH = GLA_HEADS * GLA_DK
GLA_WIDTH = GLA_HEADS * GLA_DV
POOL_WIDTH = 1024
POOL_WINDOWS = (2, 4, 8, 16)
POOL_GW = 256
POOL_BUF = 15
HALO = 16
GATE_RANK = 16
GATE_TEMP = 16.0
N_GROUPS = 4
EXPERTS_PER_GROUP = 8
N_EXPERTS = 32
EXPERT_FF = 512
EPS = 1e-6
PAST_LEN = 16384
QKVR_WIDTH = 2 * GLA_KEY_WIDTH + 2 * GLA_WIDTH

LANES = 128
TM = 256
TME = 256
GLA_CHUNK = 64
GLA_SUB = 16
GLA_STEP = 512
VMEM_LIMIT = 56 * 1024 * 1024

BF16 = jnp.bfloat16
F32 = jnp.float32
NEG = -1e30


def _cparams(n_axes):
    return pltpu.CompilerParams(dimension_semantics=("arbitrary",) * n_axes,
                                vmem_limit_bytes=VMEM_LIMIT)


def _silu(x):
    return x / (1.0 + jnp.exp(-x))


def _bdot(a, b):
    return jnp.dot(a.astype(BF16), b.astype(BF16), preferred_element_type=F32)


def _split3(a):
    a1 = a.astype(BF16)
    r1 = a - a1.astype(F32)
    a2 = r1.astype(BF16)
    a3 = (r1 - a2.astype(F32)).astype(BF16)
    return a1, a2, a3


def _dot_f32(a, b):
    a1, a2, a3 = _split3(a)
    b1, b2, b3 = _split3(b)
    d = lambda x, y: jnp.dot(x, y, preferred_element_type=F32)
    small = d(a2, b2) + d(a1, b3) + d(a3, b1)
    mid = d(a1, b2) + d(a2, b1)
    return d(a1, b1) + (mid + small)


def _dot_exact_lhs(tri_bf16, g):
    g1, g2, g3 = _split3(g)
    d = lambda y: jnp.dot(tri_bf16, y, preferred_element_type=F32)
    return d(g1) + (d(g2) + d(g3))


def _adaln_kernel(c_ref, w_ref, b_ref, o_ref):
    c = c_ref[...]
    o_ref[...] = _bdot(_silu(c), w_ref[...]) + b_ref[...]


def _adaln(c_all, w_ada, b_ada):
    n, d = c_all.shape
    width = w_ada.shape[1]
    tn = 1024
    return pl.pallas_call(
        _adaln_kernel,
        name="adaln",
        grid=(width // tn,),
        in_specs=[pl.BlockSpec((n, d), lambda j: (0, 0)),
                  pl.BlockSpec((d, tn), lambda j: (0, j)),
                  pl.BlockSpec((1, tn), lambda j: (0, j))],
        out_specs=pl.BlockSpec((n, tn), lambda j: (0, j)),
        out_shape=jax.ShapeDtypeStruct((n, width), F32),
        compiler_params=_cparams(1),
    )(c_all, w_ada, b_ada.reshape(1, width))


class _Group:
    def __init__(self, batch, seq, row_off, mod_off):
        self.batch, self.seq, self.row_off, self.mod_off = batch, seq, row_off, mod_off
        if seq >= TM:
            assert seq % TM == 0
            self.nb, self.tt = 1, TM
            self.tiles_per_batch = seq // TM
            self.n_tiles = batch * self.tiles_per_batch
        else:
            assert TM % seq == 0 and batch % (TM // seq) == 0
            self.nb, self.tt = TM // seq, seq
            self.tiles_per_batch = 1
            self.n_tiles = batch // self.nb
        self.rows = batch * seq
        self.tile_off = row_off // TM

    def x_map(self):
        if self.nb == 1:
            tpb = self.tiles_per_batch
            return lambda i, *_: (i // tpb, i % tpb, 0)
        return lambda i, *_: (i, 0, 0)

    def mod_map(self):
        assert self.mod_off % self.nb == 0
        off = self.mod_off // self.nb
        if self.nb == 1:
            tpb = self.tiles_per_batch
            return lambda i, *_: (i // tpb + off, 0, 0)
        return lambda i, *_: (i + off, 0, 0)


def _mod_rows(mod_ref, idx):
    return mod_ref[:, idx:idx + 1, :]


def _rmsnorm_mod(x, gain, scale, shift):
    ms = jnp.mean(x * x, axis=-1, keepdims=True)
    y = x * lax.rsqrt(ms + EPS) * gain
    return y * (1.0 + scale) + shift


def _in_proj_kernel(x_ref, mod_ref, n1_ref, wq_ref, wg_ref, wu_ref, gup_ref, gb_ref,
                    qkvr_ref, la_ref, u_ref):
    x = x_ref[...]
    h = _rmsnorm_mod(x, n1_ref[...], _mod_rows(mod_ref, 1), _mod_rows(mod_ref, 0))
    hb = h.reshape(TM, D_MODEL).astype(BF16)
    qkvr_ref[...] = jnp.dot(hb, wq_ref[...], preferred_element_type=F32)
    u_ref[...] = jnp.dot(hb, wu_ref[...], preferred_element_type=F32)
    g_lr = jnp.dot(hb, wg_ref[...], preferred_element_type=F32)
    pre = jnp.dot(g_lr.astype(BF16), gup_ref[...], preferred_element_type=F32) + gb_ref[...]
    log_sig = jnp.minimum(pre, 0.0) - jnp.log1p(jnp.exp(-jnp.abs(pre)))
    la_ref[...] = log_sig / GATE_TEMP


def _in_proj(grp, x, mod, norm1, wq, wg, wu, gup, gb):
    const = lambda i: (0, 0)
    row = lambda i: (i, 0)
    return pl.pallas_call(
        _in_proj_kernel,
        name="in_proj",
        grid=(grp.n_tiles,),
        in_specs=[pl.BlockSpec((grp.nb, grp.tt, D_MODEL), grp.x_map()),
                  pl.BlockSpec((grp.nb, 6, D_MODEL), grp.mod_map()),
                  pl.BlockSpec((1, 1, D_MODEL), lambda i: (0, 0, 0)),
                  pl.BlockSpec(wq.shape, const, pipeline_mode=pl.Buffered(1)),
                  pl.BlockSpec(wg.shape, const, pipeline_mode=pl.Buffered(1)),
                  pl.BlockSpec(wu.shape, const, pipeline_mode=pl.Buffered(1)),
                  pl.BlockSpec(gup.shape, const, pipeline_mode=pl.Buffered(1)),
                  pl.BlockSpec(gb.shape, const, pipeline_mode=pl.Buffered(1))],
        out_specs=[pl.BlockSpec((TM, QKVR_WIDTH), row),
                   pl.BlockSpec((TM, GLA_KEY_WIDTH), row),
                   pl.BlockSpec((TM, POOL_WIDTH), row)],
        out_shape=[jax.ShapeDtypeStruct((grp.rows, QKVR_WIDTH), F32),
                   jax.ShapeDtypeStruct((grp.rows, GLA_KEY_WIDTH), F32),
                   jax.ShapeDtypeStruct((grp.rows, POOL_WIDTH), F32)],
        compiler_params=_cparams(1),
    )(x, mod, norm1.reshape(1, 1, D_MODEL), wq, wg, wu, gup, gb)


def _gla_head_chunk(q, k, v, g, state_t, chunk, sub):
    rows = lax.broadcasted_iota(jnp.int32, (chunk, chunk), 0)
    cols = lax.broadcasted_iota(jnp.int32, (chunk, chunk), 1)
    tri = (rows >= cols).astype(BF16)
    b = _dot_exact_lhs(tri, g)
    q = q * (GLA_DK ** -0.5)
    nt = (((1,), (1,)), ((), ()))
    tn = (((0,), (0,)), ((), ()))
    o = lax.dot_general((q * jnp.exp(b)).astype(BF16), state_t.astype(BF16), nt,
                        preferred_element_type=F32)

    b_last = b[chunk - 1:chunk, :]
    k_dec = k * jnp.exp(b_last - b)
    new_state_t = jnp.exp(b_last) * state_t + lax.dot_general(
        v.astype(BF16), k_dec.astype(BF16), tn, preferred_element_type=F32)

    lane = lax.broadcasted_iota(jnp.int32, (sub, chunk), 1)
    sub_row = lax.broadcasted_iota(jnp.int32, (sub, LANES), 0)
    key_row = lax.broadcasted_iota(jnp.int32, (chunk, LANES), 0)
    p_blocks = []
    for s in range(chunk // sub):
        lo = s * sub
        q_s = q[lo:lo + sub, :]
        b_s = b[lo:lo + sub, :]
        p = jnp.zeros((sub, chunk), F32)
        if s > 0:
            ref_row = b[lo - 1:lo, :]
            q_rel = q_s * jnp.exp(b_s - ref_row)
            k_rel = k * jnp.exp(jnp.where(key_row < lo, ref_row - b, NEG))
            p = lax.dot_general(q_rel.astype(BF16), k_rel.astype(BF16), nt, preferred_element_type=F32)
        for jl in range(sub):
            j = lo + jl
            expo = jnp.where(sub_row >= jl, b_s - b[j:j + 1, :], NEG)
            col = jnp.sum(q_s * k[j:j + 1, :] * jnp.exp(expo), axis=-1, keepdims=True)
            p = jnp.where(lane == j, col, p)
        p_blocks.append(p)
    p_full = p_blocks[0] if len(p_blocks) == 1 else jnp.concatenate(p_blocks, axis=0)
    o = o + _bdot(p_full, v)
    return o, new_state_t


def _gla_prompt_kernel(q_ref, k_ref, v_ref, la_ref, o_ref, s_ref, st_ref):
    step = pl.program_id(2)

    @pl.when(step == 0)
    def _():
        st_ref[...] = jnp.zeros_like(st_ref)

    def body(c, carry):
        r0 = pl.multiple_of(c * GLA_CHUNK, GLA_CHUNK)
        sl = pl.ds(r0, GLA_CHUNK)
        o, new_state_t = _gla_head_chunk(q_ref[sl, :], k_ref[sl, :], v_ref[sl, :], la_ref[sl, :],
                                         st_ref[...], GLA_CHUNK, GLA_SUB)
        o_ref[sl, :] = o
        st_ref[...] = new_state_t
        return carry

    lax.fori_loop(0, GLA_STEP // GLA_CHUNK, body, 0)

    @pl.when(step == pl.num_programs(2) - 1)
    def _():
        s_ref[0, 0] = st_ref[...].T


def _gla_prompt(batch, seq, qkvr, log_a):
    steps = seq // GLA_STEP
    row = lambda b, h, s: b * steps + s
    return pl.pallas_call(
        _gla_prompt_kernel,
        name="gla_prompt",
        grid=(batch, GLA_HEADS, steps),
        in_specs=[pl.BlockSpec((GLA_STEP, GLA_DK), lambda b, h, s: (row(b, h, s), h)),
                  pl.BlockSpec((GLA_STEP, GLA_DK), lambda b, h, s: (row(b, h, s), GLA_HEADS + h)),
                  pl.BlockSpec((GLA_STEP, GLA_DV), lambda b, h, s: (row(b, h, s), GLA_HEADS + h)),
                  pl.BlockSpec((GLA_STEP, GLA_DK), lambda b, h, s: (row(b, h, s), h))],
        out_specs=[pl.BlockSpec((GLA_STEP, GLA_DV), lambda b, h, s: (row(b, h, s), h)),
                   pl.BlockSpec((1, 1, GLA_DK, GLA_DV), lambda b, h, s: (b, h, 0, 0))],
        out_shape=[jax.ShapeDtypeStruct((batch * seq, GLA_WIDTH), F32),
                   jax.ShapeDtypeStruct((batch, GLA_HEADS, GLA_DK, GLA_DV), F32)],
        scratch_shapes=[pltpu.VMEM((GLA_DV, GLA_DK), F32)],
        compiler_params=_cparams(3),
    )(qkvr, qkvr, qkvr, log_a)


GLA_DEC_BB = 16


def _gla_decode_kernel(seq, q_ref, k_ref, v_ref, la_ref, s0_ref, o_ref, s_ref):
    def body(i, carry):
        r0 = pl.multiple_of(i * seq, seq)
        sl = pl.ds(r0, seq)
        o, new_state_t = _gla_head_chunk(q_ref[sl, :], k_ref[sl, :], v_ref[sl, :], la_ref[sl, :],
                                         s0_ref[i, 0].T, seq, seq)
        o_ref[sl, :] = o
        s_ref[i, 0] = new_state_t.T
        return carry

    lax.fori_loop(0, GLA_DEC_BB, body, 0)


def _gla_decode(batch, seq, qkvr, log_a, state):
    rows = GLA_DEC_BB * seq
    return pl.pallas_call(
        functools.partial(_gla_decode_kernel, seq),
        name="gla_decode",
        grid=(batch // GLA_DEC_BB, GLA_HEADS),
        in_specs=[pl.BlockSpec((rows, GLA_DK), lambda i, h: (i, h)),
                  pl.BlockSpec((rows, GLA_DK), lambda i, h: (i, GLA_HEADS + h)),
                  pl.BlockSpec((rows, GLA_DV), lambda i, h: (i, GLA_HEADS + h)),
                  pl.BlockSpec((rows, GLA_DK), lambda i, h: (i, h)),
                  pl.BlockSpec((GLA_DEC_BB, 1, GLA_DK, GLA_DV), lambda i, h: (i, h, 0, 0))],
        out_specs=[pl.BlockSpec((rows, GLA_DV), lambda i, h: (i, h)),
                   pl.BlockSpec((GLA_DEC_BB, 1, GLA_DK, GLA_DV), lambda i, h: (i, h, 0, 0))],
        out_shape=[jax.ShapeDtypeStruct((batch * seq, GLA_WIDTH), F32),
                   jax.ShapeDtypeStruct((batch, GLA_HEADS, GLA_DK, GLA_DV), F32)],
        compiler_params=_cparams(2),
    )(qkvr, qkvr, qkvr, log_a, state)


def _route(logits):
    lane = lax.broadcasted_iota(jnp.int32, logits.shape, 1)
    big = jnp.int32(10 ** 6)
    is_group = lane < N_GROUPS
    lg = jnp.where(is_group, logits, NEG)
    mg = jnp.max(lg, axis=-1, keepdims=True)
    g_idx = jnp.min(jnp.where(is_group & (lg == mg), lane, big), axis=-1, keepdims=True)
    denom = jnp.sum(jnp.where(is_group, jnp.exp(lg - mg), 0.0), axis=-1, keepdims=True)
    p_sel = 1.0 / denom
    first = N_GROUPS + EXPERTS_PER_GROUP * g_idx
    in_grp = (lane >= first) & (lane < first + EXPERTS_PER_GROUP)
    le = jnp.where(in_grp, logits, NEG)
    m1 = jnp.max(le, axis=-1, keepdims=True)
    i1 = jnp.min(jnp.where(in_grp & (le == m1), lane, big), axis=-1, keepdims=True)
    rest = in_grp & (lane != i1)
    le2 = jnp.where(rest, logits, NEG)
    m2 = jnp.max(le2, axis=-1, keepdims=True)
    i2 = jnp.min(jnp.where(rest & (le2 == m2), lane, big), axis=-1, keepdims=True)
    e2 = jnp.exp(m2 - m1)
    w1 = p_sel / (1.0 + e2)
    w2 = p_sel * e2 / (1.0 + e2)
    ex1 = (i1 - N_GROUPS).astype(F32)
    ex2 = (i2 - N_GROUPS).astype(F32)
    return jnp.where(lane == 0, ex1, jnp.where(lane == 1, ex2,
                     jnp.where(lane == 2, w1, jnp.where(lane == 3, w2, 0.0))))


def _mix_out_kernel(grp, pos0, zero_first_halo, n_alias,
                    o_ref, r_ref, u_ref, halo_ref, x_ref, mod_ref, n2_ref, gn_ref, pw_ref, ps_ref,
                    wo_ref, wr_ref, br_ref, *rest):
    x1_ref, h2_ref, rt_ref, ext_ref, ymix_ref = rest[n_alias:]
    i = pl.program_id(0)

    @pl.when(i < grp.n_tiles)
    def _():
        _mix_out_tile(grp, pos0, zero_first_halo, i, o_ref, r_ref, u_ref, halo_ref, x_ref, mod_ref, n2_ref,
                      gn_ref, pw_ref, ps_ref, wo_ref, wr_ref, br_ref, x1_ref, h2_ref, rt_ref, ext_ref, ymix_ref)

    @pl.when(i >= grp.n_tiles)
    def _():
        h2_ref[...] = jnp.zeros_like(h2_ref)
        rt_ref[...] = jnp.zeros_like(rt_ref)


def _mix_out_tile(grp, pos0, zero_first_halo, i, o_ref, r_ref, u_ref, halo_ref, x_ref, mod_ref, n2_ref,
                  gn_ref, pw_ref, ps_ref, wo_ref, wr_ref, br_ref, x1_ref, h2_ref, rt_ref, ext_ref, ymix_ref):
    nb, tt = grp.nb, grp.tt
    hist = halo_ref.shape[-2]

    for h in range(GLA_HEADS):
        cs = slice(h * GLA_DV, (h + 1) * GLA_DV)
        oh = o_ref[:, cs]
        ms = jnp.mean(oh * oh, axis=-1, keepdims=True)
        yh = oh * lax.rsqrt(ms + EPS) * gn_ref[:, cs] * _silu(r_ref[:, cs])
        ymix_ref[:, cs] = yh.astype(BF16)

    halo = halo_ref[...]
    if zero_first_halo:
        halo = jnp.where(i % grp.tiles_per_batch == 0, 0.0, halo)
    ext_ref[:, HALO - hist:HALO, :] = halo.reshape(nb, hist, POOL_WIDTH)
    u = u_ref[...].reshape(nb, tt, POOL_WIDTH)
    ext_ref[:, HALO:HALO + tt, :] = u
    t_idx = lax.broadcasted_iota(jnp.int32, (nb, tt, POOL_GW), 1)
    if grp.nb == 1:
        pos = (i % grp.tiles_per_batch) * TM + t_idx + pos0
    else:
        pos = t_idx + pos0
    for gi, w in enumerate(POOL_WINDOWS):
        cs = slice(gi * POOL_GW, (gi + 1) * POOL_GW)
        acc = ext_ref[:, pl.ds(HALO, tt), cs]
        for kk in range(1, w):
            acc = acc + ext_ref[:, pl.ds(HALO - kk, tt), cs]
        cnt = jnp.minimum(pos + 1, w).astype(F32)
        pooled = acc / cnt - u[:, :, cs]
        yp = _bdot(pooled.reshape(TM, POOL_GW), pw_ref[gi]) * ps_ref[:, cs]
        ymix_ref[:, GLA_WIDTH + gi * POOL_GW:GLA_WIDTH + (gi + 1) * POOL_GW] = yp.astype(BF16)

    y = jnp.dot(ymix_ref[...], wo_ref[...], preferred_element_type=F32)
    x1 = x_ref[...] + _mod_rows(mod_ref, 2) * y.reshape(nb, tt, D_MODEL)
    x1_ref[...] = x1
    h2 = _rmsnorm_mod(x1, n2_ref[...], _mod_rows(mod_ref, 4), _mod_rows(mod_ref, 3)).reshape(TM, D_MODEL)
    h2_ref[...] = h2
    logits = _dot_f32(h2, wr_ref[...]) + br_ref[...]
    rt_ref[...] = _route(logits)


def _mix_out(grp, n_tok, pos0, zero_first_halo, o, qkvr, u, halo_src, halo_block, halo_map, x, mod, norm2,
             gla_norm, pool_w, pool_scale, w_out, w_router, b_router, shared=()):
    n_alias = len(shared)
    n = grp.n_tiles
    n_fill = 0 if shared else n_tok // TM - n
    assert n_fill == 0 or grp.tile_off == 0
    clamp = lambda f: (lambda i: f(jnp.minimum(i, n - 1)))
    const2 = lambda i: (0, 0)
    row = clamp(lambda i: (i, 0))
    off = grp.tile_off
    kern = functools.partial(_mix_out_kernel, grp, pos0, zero_first_halo, n_alias)
    return pl.pallas_call(
        kern,
        name="mix_out",
        grid=(n + n_fill,),
        in_specs=[pl.BlockSpec((TM, GLA_WIDTH), row),
                  pl.BlockSpec((TM, GLA_WIDTH), clamp(lambda i: (i, 2))),
                  pl.BlockSpec((TM, POOL_WIDTH), row),
                  pl.BlockSpec(halo_block, clamp(halo_map)),
                  pl.BlockSpec((grp.nb, grp.tt, D_MODEL), clamp(grp.x_map())),
                  pl.BlockSpec((grp.nb, 6, D_MODEL), clamp(grp.mod_map())),
                  pl.BlockSpec((1, 1, D_MODEL), lambda i: (0, 0, 0)),
                  pl.BlockSpec((1, GLA_WIDTH), const2),
                  pl.BlockSpec(pool_w.shape, lambda i: (0, 0, 0), pipeline_mode=pl.Buffered(1)),
                  pl.BlockSpec((1, POOL_WIDTH), const2),
                  pl.BlockSpec(w_out.shape, const2, pipeline_mode=pl.Buffered(1)),
                  pl.BlockSpec(w_router.shape, const2, pipeline_mode=pl.Buffered(1)),
                  pl.BlockSpec((1, LANES), const2)]
                 + [pl.BlockSpec(memory_space=pl.ANY)] * n_alias,
        out_specs=[pl.BlockSpec((grp.nb, grp.tt, D_MODEL), clamp(grp.x_map())),
                   pl.BlockSpec((TM, D_MODEL), lambda i: (i + off, 0)),
                   pl.BlockSpec((TM, LANES), lambda i: (i + off, 0))],
        out_shape=[jax.ShapeDtypeStruct(x.shape, F32),
                   jax.ShapeDtypeStruct((n_tok, D_MODEL), F32),
                   jax.ShapeDtypeStruct((n_tok, LANES), F32)],
        scratch_shapes=[pltpu.VMEM((grp.nb, HALO + grp.tt, POOL_WIDTH), F32),
                        pltpu.VMEM((TM, D_MODEL), BF16)],
        input_output_aliases={13 + k: 1 + k for k in range(n_alias)},
        compiler_params=_cparams(1),
    )(o, qkvr, u, halo_src, x, mod, norm2.reshape(1, 1, D_MODEL), gla_norm.reshape(1, GLA_WIDTH),
      pool_w, pool_scale.reshape(1, POOL_WIDTH), w_out, w_router, b_router, *shared)


def _row_copy(src_hbm, src_row, dst, dst_row, sem):
    return pltpu.make_async_copy(src_hbm.at[pl.ds(src_row, 1), :], dst.at[pl.ds(dst_row, 1), :], sem)


def _tile_wait(src_hbm, dst, sem):
    pltpu.make_async_copy(src_hbm.at[pl.ds(0, dst.shape[0]), :], dst, sem).wait()


def _dispatch_kernel(src_ref, nu_ref, h_hbm, hs_ref, buf, sem):
    i = pl.program_id(0)
    n_used = nu_ref[0]
    slot = i % 2

    def issue(tile, slot_):
        base = tile * TME

        def body(r, carry):
            _row_copy(h_hbm, src_ref[base + 2 * r], buf.at[slot_], 2 * r, sem.at[slot_]).start(priority=0)
            _row_copy(h_hbm, src_ref[base + 2 * r + 1], buf.at[slot_], 2 * r + 1, sem.at[slot_]).start(priority=1)
            return carry

        lax.fori_loop(0, TME // 2, body, 0, unroll=4)

    @pl.when(i == 0)
    def _():
        issue(0, 0)

    @pl.when(i + 1 < n_used)
    def _():
        issue(i + 1, 1 - slot)

    @pl.when(i < n_used)
    def _():
        _tile_wait(h_hbm, buf.at[slot], sem.at[slot])
        hs_ref[...] = buf[slot].astype(BF16)

    @pl.when(i >= n_used)
    def _():
        hs_ref[...] = jnp.zeros_like(hs_ref)


def _dispatch(plan, h2_all):
    n_sorted = plan["src_row"].shape[0]
    grid_spec = pltpu.PrefetchScalarGridSpec(
        num_scalar_prefetch=2,
        grid=(n_sorted // TME,),
        in_specs=[pl.BlockSpec(memory_space=pl.ANY)],
        out_specs=pl.BlockSpec((TME, D_MODEL), lambda i, *_: (i, 0)),
        scratch_shapes=[pltpu.VMEM((2, TME, D_MODEL), F32), pltpu.SemaphoreType.DMA((2,))],
    )
    return pl.pallas_call(
        _dispatch_kernel,
        name="dispatch",
        grid_spec=grid_spec,
        out_shape=jax.ShapeDtypeStruct((n_sorted, D_MODEL), BF16),
        compiler_params=_cparams(1),
    )(plan["src_row"], plan["n_used"], h2_all)


def _moe_kernel(te_ref, nu_ref, seg_ref, nxt_ref, hs_ref, w1_hbm, w3_hbm, w2_hbm, y_ref,
                wf1, wf3, wf2, wsem, w1b, w3b, w2b):
    i = pl.program_id(0)
    n_used = nu_ref[0]

    def weight_copies(expert, wslot):
        return (pltpu.make_async_copy(w1_hbm.at[expert], wf1.at[wslot], wsem.at[wslot]),
                pltpu.make_async_copy(w3_hbm.at[expert], wf3.at[wslot], wsem.at[wslot]),
                pltpu.make_async_copy(w2_hbm.at[expert], wf2.at[wslot], wsem.at[wslot]))

    @pl.when(i == 0)
    def _():
        for c in weight_copies(te_ref[0], 0):
            c.start()

    prev = jnp.maximum(i - 1, 0)

    @pl.when((i < n_used) & ((i == 0) | (te_ref[i] != te_ref[prev])))
    def _():
        wslot = seg_ref[i] % 2
        for c in weight_copies(te_ref[i], wslot):
            c.wait()
        w1b[...] = wf1[wslot].astype(BF16)
        w3b[...] = wf3[wslot].astype(BF16)
        w2b[...] = wf2[wslot].astype(BF16)

        @pl.when(nxt_ref[i] >= 0)
        def _():
            for c in weight_copies(nxt_ref[i], 1 - wslot):
                c.start()

    @pl.when(i < n_used)
    def _():
        x = hs_ref[...]
        a = jnp.dot(x, w1b[...], preferred_element_type=F32)
        b = jnp.dot(x, w3b[...], preferred_element_type=F32)
        hid = _silu(a) * b
        y_ref[...] = jnp.dot(hid.astype(BF16), w2b[...], preferred_element_type=F32)

    @pl.when(i >= n_used)
    def _():
        y_ref[...] = jnp.zeros_like(y_ref)


def _moe(plan, h_sorted, w1, w3, w2):
    n_sorted = h_sorted.shape[0]
    n_tiles = n_sorted // TME
    used = lambda i, te, nu, seg, nxt: (jnp.minimum(i, nu[0] - 1), 0)
    grid_spec = pltpu.PrefetchScalarGridSpec(
        num_scalar_prefetch=4,
        grid=(n_tiles,),
        in_specs=[pl.BlockSpec((TME, D_MODEL), used)] + [pl.BlockSpec(memory_space=pl.ANY)] * 3,
        out_specs=pl.BlockSpec((TME, D_MODEL), lambda i, *_: (i, 0)),
        scratch_shapes=[pltpu.VMEM((2, D_MODEL, EXPERT_FF), F32), pltpu.VMEM((2, D_MODEL, EXPERT_FF), F32),
                        pltpu.VMEM((2, EXPERT_FF, D_MODEL), F32),
                        pltpu.SemaphoreType.DMA((2,)),
                        pltpu.VMEM((D_MODEL, EXPERT_FF), BF16), pltpu.VMEM((D_MODEL, EXPERT_FF), BF16),
                        pltpu.VMEM((EXPERT_FF, D_MODEL), BF16)],
    )
    return pl.pallas_call(
        _moe_kernel,
        name="moe",
        grid_spec=grid_spec,
        out_shape=jax.ShapeDtypeStruct((n_sorted, D_MODEL), F32),
        compiler_params=_cparams(1),
    )(plan["tile_expert"], plan["n_used"], plan["segment"], plan["next_expert"], h_sorted, w1, w3, w2)


def _finish_kernel(grp, pos_ref, x1_ref, mod_ref, rt_ref, nf_ref, y_hbm, out_ref, buf_a, buf_b, sem):
    i = pl.program_id(0)
    n_steps = pl.num_programs(0)
    slot = i % 2

    def issue(tile, slot_):
        base = (tile * TM + grp.row_off) * 2

        def body(r, carry):
            _row_copy(y_hbm, pos_ref[base + 2 * r], buf_a.at[slot_], r, sem.at[slot_]).start(priority=0)
            _row_copy(y_hbm, pos_ref[base + 2 * r + 1], buf_b.at[slot_], r, sem.at[slot_]).start(priority=1)
            return carry

        lax.fori_loop(0, TM, body, 0, unroll=8)

    @pl.when(i == 0)
    def _():
        issue(0, 0)

    @pl.when(i + 1 < n_steps)
    def _():
        issue(i + 1, 1 - slot)

    _tile_wait(y_hbm, buf_a.at[slot], sem.at[slot])
    _tile_wait(y_hbm, buf_b.at[slot], sem.at[slot])
    rt = rt_ref[...]
    moe = rt[:, 2:3] * buf_a[slot] + rt[:, 3:4] * buf_b[slot]
    x2 = x1_ref[...] + _mod_rows(mod_ref, 5) * moe.reshape(grp.nb, grp.tt, D_MODEL)
    ms = jnp.mean(x2 * x2, axis=-1, keepdims=True)
    out_ref[...] = x2 * lax.rsqrt(ms + EPS) * nf_ref[...]


def _finish(grp, pos, x1, mod, route_all, norm_f, y_sorted):
    off = grp.tile_off
    grid_spec = pltpu.PrefetchScalarGridSpec(
        num_scalar_prefetch=1,
        grid=(grp.n_tiles,),
        in_specs=[pl.BlockSpec((grp.nb, grp.tt, D_MODEL), grp.x_map()),
                  pl.BlockSpec((grp.nb, 6, D_MODEL), grp.mod_map()),
                  pl.BlockSpec((TM, LANES), lambda i, p: (i + off, 0)),
                  pl.BlockSpec((1, 1, D_MODEL), lambda i, p: (0, 0, 0)),
                  pl.BlockSpec(memory_space=pl.ANY)],
        out_specs=pl.BlockSpec((grp.nb, grp.tt, D_MODEL), grp.x_map()),
        scratch_shapes=[pltpu.VMEM((2, TM, D_MODEL), F32), pltpu.VMEM((2, TM, D_MODEL), F32),
                        pltpu.SemaphoreType.DMA((2,))],
    )
    return pl.pallas_call(
        functools.partial(_finish_kernel, grp),
        name="finish",
        grid_spec=grid_spec,
        out_shape=jax.ShapeDtypeStruct(x1.shape, F32),
        compiler_params=_cparams(1),
    )(pos, x1, mod, route_all, norm_f.reshape(1, 1, D_MODEL), y_sorted)


def _sort_plan(route_all):
    n_tok = route_all.shape[0]
    n_pairs = 2 * n_tok
    n_sorted = n_pairs + N_EXPERTS * TME
    flat_e = route_all[:, 0:2].astype(jnp.int32).reshape(n_pairs)
    onehot = (flat_e[:, None] == jnp.arange(N_EXPERTS, dtype=jnp.int32)[None, :]).astype(jnp.int32)
    csum = jnp.cumsum(onehot, axis=0)
    rank = jnp.sum(onehot * csum, axis=1) - 1
    counts = csum[-1]
    padded = ((counts + TME - 1) // TME) * TME
    ends = jnp.cumsum(padded)
    starts = ends - padded
    pos = starts[flat_e] + rank
    token = jnp.arange(n_pairs, dtype=jnp.int32) // 2
    src_row = jnp.zeros((n_sorted,), jnp.int32).at[pos].set(token)
    tile_start = jnp.arange(n_sorted // TME, dtype=jnp.int32) * TME
    tile_expert = jnp.sum((tile_start[:, None] >= ends[None, :]).astype(jnp.int32), axis=1)
    tile_expert = jnp.minimum(tile_expert, N_EXPERTS - 1)
    n_used = ends[-1] // TME
    is_first = jnp.concatenate([jnp.ones((1,), jnp.int32),
                                (tile_expert[1:] != tile_expert[:-1]).astype(jnp.int32)])
    segment = jnp.cumsum(is_first) - 1
    next_tile = ends[tile_expert] // TME
    next_expert = jnp.where(next_tile < n_used, tile_expert[jnp.minimum(next_tile, n_sorted // TME - 1)], -1)
    return dict(pos=pos.astype(jnp.int32), src_row=src_row, tile_expert=tile_expert.astype(jnp.int32),
                n_used=n_used.astype(jnp.int32).reshape(1), segment=segment.astype(jnp.int32),
                next_expert=next_expert.astype(jnp.int32))


def kernel(x_prompt, x_sample, c_prompt, c_sample, state_gla, state_pool, w_ada, b_ada, norm1, norm2, w_in,
           gate_up, gate_bias, gla_norm, pool_w, pool_scale, w_out, w_group, b_group, w_expert, b_expert,
           w1, w3, w2, norm_f):
    assert w_ada.shape[0] == 1, "single-layer step"
    bp, tp, _ = x_prompt.shape
    bs, ts, _ = x_sample.shape
    grp_p = _Group(bp, tp, 0, bs)
    grp_s = _Group(bs, ts, bp * tp, 0)
    n_tok = bp * tp + bs * ts

    n_c = bp + bs
    n_c_pad = -(-n_c // 8) * 8
    c_all = jnp.concatenate([c_sample, c_prompt, jnp.zeros((n_c_pad - n_c, D_MODEL), F32)], axis=0)
    mod = _adaln(c_all, w_ada[0], b_ada[0]).reshape(n_c_pad, 6, D_MODEL)
    mod_p = mod_s = mod

    wi = w_in[0]
    wq = wi[:, :QKVR_WIDTH].astype(BF16)
    wg = jnp.pad(wi[:, QKVR_WIDTH:QKVR_WIDTH + GATE_RANK], ((0, 0), (0, LANES - GATE_RANK))).astype(BF16)
    wu = wi[:, QKVR_WIDTH + GATE_RANK:].astype(BF16)
    gup = jnp.pad(gate_up[0], ((0, LANES - GATE_RANK), (0, 0))).astype(BF16)
    gb = gate_bias[0].reshape(1, GLA_KEY_WIDTH)
    pw = pool_w[0].astype(BF16)
    wo = w_out[0].astype(BF16)
    w_router = jnp.concatenate(
        [w_group[0], jnp.transpose(w_expert[0], (1, 0, 2)).reshape(D_MODEL, N_EXPERTS),
         jnp.zeros((D_MODEL, LANES - N_GROUPS - N_EXPERTS), F32)], axis=1)
    b_router = jnp.concatenate([b_group[0], b_expert[0].reshape(N_EXPERTS),
                                jnp.zeros((LANES - N_GROUPS - N_EXPERTS,), F32)]).reshape(1, LANES)

    qkvr_p, la_p, u_p = _in_proj(grp_p, x_prompt, mod_p, norm1[0], wq, wg, wu, gup, gb)
    qkvr_s, la_s, u_s = _in_proj(grp_s, x_sample, mod_s, norm1[0], wq, wg, wu, gup, gb)

    o_p, gla_p = _gla_prompt(bp, tp, qkvr_p, la_p)
    o_s, gla_s = _gla_decode(bs, ts, qkvr_s, la_s, state_gla[0])

    halo_per_tile = TM // HALO
    halo_map_p = lambda i: (jnp.maximum(i * halo_per_tile - 1, 0), 0)
    x1_p, h2_all, route_all = _mix_out(grp_p, n_tok, 0, True, o_p, qkvr_p, u_p, u_p, (HALO, POOL_WIDTH),
                                       halo_map_p, x_prompt, mod_p, norm2[0], gla_norm[0], pw, pool_scale[0],
                                       wo, w_router, b_router)
    x1_s, h2_all, route_all = _mix_out(grp_s, n_tok, PAST_LEN, False, o_s, qkvr_s, u_s, state_pool[0],
                                       (grp_s.nb, POOL_BUF, POOL_WIDTH), lambda i: (i, 0, 0), x_sample, mod_s,
                                       norm2[0], gla_norm[0], pw, pool_scale[0], wo, w_router, b_router,
                                       shared=(h2_all, route_all))

    plan = _sort_plan(route_all)
    y_sorted = _moe(plan, _dispatch(plan, h2_all), w1[0], w3[0], w2[0])

    y_p = _finish(grp_p, plan["pos"], x1_p, mod_p, route_all, norm_f, y_sorted)
    y_s = _finish(grp_s, plan["pos"], x1_s, mod_s, route_all, norm_f, y_sorted)

    u_p3 = u_p.reshape(bp, tp, POOL_WIDTH)
    u_s3 = u_s.reshape(bs, ts, POOL_WIDTH)
    pool_p = jnp.concatenate([jnp.zeros((bp, POOL_BUF, POOL_WIDTH), F32), u_p3], axis=1)[:, -POOL_BUF:]
    pool_s = jnp.concatenate([state_pool[0], u_s3], axis=1)[:, -POOL_BUF:]
    return (y_p, y_s, gla_p[None], pool_p[None], gla_s[None], pool_s[None])
```

```python
import functools

import jax
import jax.numpy as jnp
from jax import lax
from jax.experimental import pallas as pl
from jax.experimental.pallas import tpu as pltpu

D_MODEL = 2048
GLA_HEADS = 4
GLA_DK = 128
GLA_DV = 256
GLA_KEY_WIDTH = GLA_HEADS * GLA_DK
GLA_WIDTH = GLA_HEADS * GLA_DV
POOL_WIDTH = 1024
POOL_WINDOWS = (2, 4, 8, 16)
POOL_GW = 256
POOL_BUF = 15
HALO = 16
GATE_RANK = 16
GATE_TEMP = 16.0
N_GROUPS = 4
EXPERTS_PER_GROUP = 8
N_EXPERTS = 32
EXPERT_FF = 512
EPS = 1e-6
PAST_LEN = 16384
QKVR_WIDTH = 2 * GLA_KEY_WIDTH + 2 * GLA_WIDTH

LANES = 128
TM = 256
TME = 256
TD = 1024
FINISH_AHEAD = 3
GLA_CHUNK = 64
GLA_SUB = 16
GLA_STEP = 512
VMEM_LIMIT = 56 * 1024 * 1024

BF16 = jnp.bfloat16
F32 = jnp.float32
NEG = -1e30


def _cparams(n_axes):
    return pltpu.CompilerParams(dimension_semantics=("arbitrary",) * n_axes,
                                vmem_limit_bytes=VMEM_LIMIT)


def _silu(x):
    return x / (1.0 + jnp.exp(-x))


def _bdot(a, b):
    return jnp.dot(a.astype(BF16), b.astype(BF16), preferred_element_type=F32)


def _split3(a):
    a1 = a.astype(BF16)
    r1 = a - a1.astype(F32)
    a2 = r1.astype(BF16)
    a3 = (r1 - a2.astype(F32)).astype(BF16)
    return a1, a2, a3


def _dot_f32(a, b):
    a1, a2, a3 = _split3(a)
    b1, b2, b3 = _split3(b)
    d = lambda x, y: jnp.dot(x, y, preferred_element_type=F32)
    small = d(a2, b2) + d(a1, b3) + d(a3, b1)
    mid = d(a1, b2) + d(a2, b1)
    return d(a1, b1) + (mid + small)


def _dot_exact_lhs(tri_bf16, g):
    g1, g2, g3 = _split3(g)
    d = lambda y: jnp.dot(tri_bf16, y, preferred_element_type=F32)
    return d(g1) + (d(g2) + d(g3))


def _adaln_kernel(c_ref, w_ref, b_ref, o_ref):
    c = c_ref[...]
    o_ref[...] = _bdot(_silu(c), w_ref[...]) + b_ref[...]


def _adaln(c_all, w_ada, b_ada):
    n, d = c_all.shape
    width = w_ada.shape[1]
    tn = 1024
    return pl.pallas_call(
        _adaln_kernel,
        name="adaln",
        grid=(width // tn,),
        in_specs=[pl.BlockSpec((n, d), lambda j: (0, 0)),
                  pl.BlockSpec((d, tn), lambda j: (0, j)),
                  pl.BlockSpec((1, tn), lambda j: (0, j))],
        out_specs=pl.BlockSpec((n, tn), lambda j: (0, j)),
        out_shape=jax.ShapeDtypeStruct((n, width), F32),
        compiler_params=_cparams(1),
    )(c_all, w_ada, b_ada.reshape(1, width))


class _Group:
    def __init__(self, batch, seq, row_off, mod_off):
        self.batch, self.seq, self.row_off, self.mod_off = batch, seq, row_off, mod_off
        if seq >= TM:
            assert seq % TM == 0
            self.nb, self.tt = 1, TM
            self.tiles_per_batch = seq // TM
            self.n_tiles = batch * self.tiles_per_batch
        else:
            assert TM % seq == 0 and batch % (TM // seq) == 0
            self.nb, self.tt = TM // seq, seq
            self.tiles_per_batch = 1
            self.n_tiles = batch // self.nb
        self.rows = batch * seq
        self.tile_off = row_off // TM

    def x_map(self):
        if self.nb == 1:
            tpb = self.tiles_per_batch
            return lambda i, *_: (i // tpb, i % tpb, 0)
        return lambda i, *_: (i, 0, 0)

    def mod_map(self):
        assert self.mod_off % self.nb == 0
        off = self.mod_off // self.nb
        if self.nb == 1:
            tpb = self.tiles_per_batch
            return lambda i, *_: (i // tpb + off, 0, 0)
        return lambda i, *_: (i + off, 0, 0)


def _mod_rows(mod_ref, idx):
    return mod_ref[:, idx:idx + 1, :]


def _rmsnorm_mod(x, gain, scale, shift):
    ms = jnp.mean(x * x, axis=-1, keepdims=True)
    y = x * lax.rsqrt(ms + EPS) * gain
    return y * (1.0 + scale) + shift


def _in_proj_kernel(x_ref, mod_ref, n1_ref, wq_ref, wg_ref, wu_ref, gup_ref, gb_ref,
                    qkvr_ref, la_ref, u_ref):
    x = x_ref[...]
    h = _rmsnorm_mod(x, n1_ref[...], _mod_rows(mod_ref, 1), _mod_rows(mod_ref, 0))
    hb = h.reshape(TM, D_MODEL).astype(BF16)
    qkvr_ref[...] = jnp.dot(hb, wq_ref[...], preferred_element_type=F32)
    u_ref[...] = jnp.dot(hb, wu_ref[...], preferred_element_type=F32)
    g_lr = jnp.dot(hb, wg_ref[...], preferred_element_type=F32)
    pre = jnp.dot(g_lr.astype(BF16), gup_ref[...], preferred_element_type=F32) + gb_ref[...]
    log_sig = jnp.minimum(pre, 0.0) - jnp.log1p(jnp.exp(-jnp.abs(pre)))
    la_ref[...] = log_sig / GATE_TEMP


def _in_proj(grp, x, mod, norm1, wq, wg, wu, gup, gb):
    const = lambda i: (0, 0)
    row = lambda i: (i, 0)
    return pl.pallas_call(
        _in_proj_kernel,
        name="in_proj",
        grid=(grp.n_tiles,),
        in_specs=[pl.BlockSpec((grp.nb, grp.tt, D_MODEL), grp.x_map()),
                  pl.BlockSpec((grp.nb, 6, D_MODEL), grp.mod_map()),
                  pl.BlockSpec((1, 1, D_MODEL), lambda i: (0, 0, 0)),
                  pl.BlockSpec(wq.shape, const, pipeline_mode=pl.Buffered(1)),
                  pl.BlockSpec(wg.shape, const, pipeline_mode=pl.Buffered(1)),
                  pl.BlockSpec(wu.shape, const, pipeline_mode=pl.Buffered(1)),
                  pl.BlockSpec(gup.shape, const, pipeline_mode=pl.Buffered(1)),
                  pl.BlockSpec(gb.shape, const, pipeline_mode=pl.Buffered(1))],
        out_specs=[pl.BlockSpec((TM, QKVR_WIDTH), row),
                   pl.BlockSpec((TM, GLA_KEY_WIDTH), row),
                   pl.BlockSpec((TM, POOL_WIDTH), row)],
        out_shape=[jax.ShapeDtypeStruct((grp.rows, QKVR_WIDTH), F32),
                   jax.ShapeDtypeStruct((grp.rows, GLA_KEY_WIDTH), F32),
                   jax.ShapeDtypeStruct((grp.rows, POOL_WIDTH), F32)],
        compiler_params=_cparams(1),
    )(x, mod, norm1.reshape(1, 1, D_MODEL), wq, wg, wu, gup, gb)


def _gla_head_chunk(q, k, v, g, state_t, chunk, sub):
    rows = lax.broadcasted_iota(jnp.int32, (chunk, chunk), 0)
    cols = lax.broadcasted_iota(jnp.int32, (chunk, chunk), 1)
    tri = (rows >= cols).astype(BF16)
    b = _dot_exact_lhs(tri, g)
    q = q * (GLA_DK ** -0.5)
    nt = (((1,), (1,)), ((), ()))
    tn = (((0,), (0,)), ((), ()))
    o = lax.dot_general((q * jnp.exp(b)).astype(BF16), state_t.astype(BF16), nt,
                        preferred_element_type=F32)

    b_last = b[chunk - 1:chunk, :]
    k_dec = k * jnp.exp(b_last - b)
    new_state_t = jnp.exp(b_last) * state_t + lax.dot_general(
        v.astype(BF16), k_dec.astype(BF16), tn, preferred_element_type=F32)

    lane = lax.broadcasted_iota(jnp.int32, (sub, chunk), 1)
    sub_row = lax.broadcasted_iota(jnp.int32, (sub, LANES), 0)
    key_row = lax.broadcasted_iota(jnp.int32, (chunk, LANES), 0)
    p_blocks = []
    for s in range(chunk // sub):
        lo = s * sub
        q_s = q[lo:lo + sub, :]
        b_s = b[lo:lo + sub, :]
        p = jnp.zeros((sub, chunk), F32)
        if s > 0:
            ref_row = b[lo - 1:lo, :]
            q_rel = q_s * jnp.exp(b_s - ref_row)
            k_rel = k * jnp.exp(jnp.where(key_row < lo, ref_row - b, NEG))
            p = lax.dot_general(q_rel.astype(BF16), k_rel.astype(BF16), nt, preferred_element_type=F32)
        for jl in range(sub):
            j = lo + jl
            expo = jnp.where(sub_row >= jl, b_s - b[j:j + 1, :], NEG)
            col = jnp.sum(q_s * k[j:j + 1, :] * jnp.exp(expo), axis=-1, keepdims=True)
            p = jnp.where(lane == j, col, p)
        p_blocks.append(p)
    p_full = p_blocks[0] if len(p_blocks) == 1 else jnp.concatenate(p_blocks, axis=0)
    o = o + _bdot(p_full, v)
    return o, new_state_t


def _gla_prompt_kernel(q_ref, k_ref, v_ref, la_ref, o_ref, s_ref, st_ref):
    step = pl.program_id(2)

    @pl.when(step == 0)
    def _():
        st_ref[...] = jnp.zeros_like(st_ref)

    def body(c, carry):
        r0 = pl.multiple_of(c * GLA_CHUNK, GLA_CHUNK)
        sl = pl.ds(r0, GLA_CHUNK)
        o, new_state_t = _gla_head_chunk(q_ref[sl, :], k_ref[sl, :], v_ref[sl, :], la_ref[sl, :],
                                         st_ref[...], GLA_CHUNK, GLA_SUB)
        o_ref[sl, :] = o
        st_ref[...] = new_state_t
        return carry

    lax.fori_loop(0, GLA_STEP // GLA_CHUNK, body, 0)

    @pl.when(step == pl.num_programs(2) - 1)
    def _():
        s_ref[0, 0] = st_ref[...].T


def _gla_prompt(batch, seq, qkvr, log_a):
    steps = seq // GLA_STEP
    row = lambda b, h, s: b * steps + s
    return pl.pallas_call(
        _gla_prompt_kernel,
        name="gla_prompt",
        grid=(batch, GLA_HEADS, steps),
        in_specs=[pl.BlockSpec((GLA_STEP, GLA_DK), lambda b, h, s: (row(b, h, s), h)),
                  pl.BlockSpec((GLA_STEP, GLA_DK), lambda b, h, s: (row(b, h, s), GLA_HEADS + h)),
                  pl.BlockSpec((GLA_STEP, GLA_DV), lambda b, h, s: (row(b, h, s), GLA_HEADS + h)),
                  pl.BlockSpec((GLA_STEP, GLA_DK), lambda b, h, s: (row(b, h, s), h))],
        out_specs=[pl.BlockSpec((GLA_STEP, GLA_DV), lambda b, h, s: (row(b, h, s), h)),
                   pl.BlockSpec((1, 1, GLA_DK, GLA_DV), lambda b, h, s: (b, h, 0, 0))],
        out_shape=[jax.ShapeDtypeStruct((batch * seq, GLA_WIDTH), F32),
                   jax.ShapeDtypeStruct((batch, GLA_HEADS, GLA_DK, GLA_DV), F32)],
        scratch_shapes=[pltpu.VMEM((GLA_DV, GLA_DK), F32)],
        compiler_params=_cparams(3),
    )(qkvr, qkvr, qkvr, log_a)


GLA_DEC_BB = 16


def _gla_decode_kernel(seq, q_ref, k_ref, v_ref, la_ref, s0_ref, o_ref, s_ref):
    def body(i, carry):
        r0 = pl.multiple_of(i * seq, seq)
        sl = pl.ds(r0, seq)
        o, new_state_t = _gla_head_chunk(q_ref[sl, :], k_ref[sl, :], v_ref[sl, :], la_ref[sl, :],
                                         s0_ref[i, 0].T, seq, seq)
        o_ref[sl, :] = o
        s_ref[i, 0] = new_state_t.T
        return carry

    lax.fori_loop(0, GLA_DEC_BB, body, 0)


def _gla_decode(batch, seq, qkvr, log_a, state):
    rows = GLA_DEC_BB * seq
    return pl.pallas_call(
        functools.partial(_gla_decode_kernel, seq),
        name="gla_decode",
        grid=(batch // GLA_DEC_BB, GLA_HEADS),
        in_specs=[pl.BlockSpec((rows, GLA_DK), lambda i, h: (i, h)),
                  pl.BlockSpec((rows, GLA_DK), lambda i, h: (i, GLA_HEADS + h)),
                  pl.BlockSpec((rows, GLA_DV), lambda i, h: (i, GLA_HEADS + h)),
                  pl.BlockSpec((rows, GLA_DK), lambda i, h: (i, h)),
                  pl.BlockSpec((GLA_DEC_BB, 1, GLA_DK, GLA_DV), lambda i, h: (i, h, 0, 0))],
        out_specs=[pl.BlockSpec((rows, GLA_DV), lambda i, h: (i, h)),
                   pl.BlockSpec((GLA_DEC_BB, 1, GLA_DK, GLA_DV), lambda i, h: (i, h, 0, 0))],
        out_shape=[jax.ShapeDtypeStruct((batch * seq, GLA_WIDTH), F32),
                   jax.ShapeDtypeStruct((batch, GLA_HEADS, GLA_DK, GLA_DV), F32)],
        compiler_params=_cparams(2),
    )(qkvr, qkvr, qkvr, log_a, state)


def _route(logits):
    lane = lax.broadcasted_iota(jnp.int32, logits.shape, 1)
    big = jnp.int32(10 ** 6)
    is_group = lane < N_GROUPS
    lg = jnp.where(is_group, logits, NEG)
    mg = jnp.max(lg, axis=-1, keepdims=True)
    g_idx = jnp.min(jnp.where(is_group & (lg == mg), lane, big), axis=-1, keepdims=True)
    denom = jnp.sum(jnp.where(is_group, jnp.exp(lg - mg), 0.0), axis=-1, keepdims=True)
    p_sel = 1.0 / denom
    first = N_GROUPS + EXPERTS_PER_GROUP * g_idx
    in_grp = (lane >= first) & (lane < first + EXPERTS_PER_GROUP)
    le = jnp.where(in_grp, logits, NEG)
    m1 = jnp.max(le, axis=-1, keepdims=True)
    i1 = jnp.min(jnp.where(in_grp & (le == m1), lane, big), axis=-1, keepdims=True)
    rest = in_grp & (lane != i1)
    le2 = jnp.where(rest, logits, NEG)
    m2 = jnp.max(le2, axis=-1, keepdims=True)
    i2 = jnp.min(jnp.where(rest & (le2 == m2), lane, big), axis=-1, keepdims=True)
    e2 = jnp.exp(m2 - m1)
    w1 = p_sel / (1.0 + e2)
    w2 = p_sel * e2 / (1.0 + e2)
    ex1 = (i1 - N_GROUPS).astype(F32)
    ex2 = (i2 - N_GROUPS).astype(F32)
    return jnp.where(lane == 0, ex1, jnp.where(lane == 1, ex2,
                     jnp.where(lane == 2, w1, jnp.where(lane == 3, w2, 0.0))))


def _mix_out_kernel(grp, pos0, zero_first_halo, n_alias,
                    o_ref, r_ref, u_ref, halo_ref, x_ref, mod_ref, n2_ref, gn_ref, pw_ref, ps_ref,
                    wo_ref, wr_ref, br_ref, *rest):
    x1_ref, h2_ref, rt_ref, ext_ref, ymix_ref = rest[n_alias:]
    i = pl.program_id(0)

    @pl.when(i < grp.n_tiles)
    def _():
        _mix_out_tile(grp, pos0, zero_first_halo, i, o_ref, r_ref, u_ref, halo_ref, x_ref, mod_ref, n2_ref,
                      gn_ref, pw_ref, ps_ref, wo_ref, wr_ref, br_ref, x1_ref, h2_ref, rt_ref, ext_ref, ymix_ref)

    @pl.when(i >= grp.n_tiles)
    def _():
        h2_ref[...] = jnp.zeros_like(h2_ref)
        rt_ref[...] = jnp.zeros_like(rt_ref)


def _mix_out_tile(grp, pos0, zero_first_halo, i, o_ref, r_ref, u_ref, halo_ref, x_ref, mod_ref, n2_ref,
                  gn_ref, pw_ref, ps_ref, wo_ref, wr_ref, br_ref, x1_ref, h2_ref, rt_ref, ext_ref, ymix_ref):
    nb, tt = grp.nb, grp.tt
    hist = halo_ref.shape[-2]

    for h in range(GLA_HEADS):
        cs = slice(h * GLA_DV, (h + 1) * GLA_DV)
        oh = o_ref[:, cs]
        ms = jnp.mean(oh * oh, axis=-1, keepdims=True)
        yh = oh * lax.rsqrt(ms + EPS) * gn_ref[:, cs] * _silu(r_ref[:, cs])
        ymix_ref[:, cs] = yh.astype(BF16)

    halo = halo_ref[...]
    if zero_first_halo:
        halo = jnp.where(i % grp.tiles_per_batch == 0, 0.0, halo)
    ext_ref[:, HALO - hist:HALO, :] = halo.reshape(nb, hist, POOL_WIDTH)
    u = u_ref[...].reshape(nb, tt, POOL_WIDTH)
    ext_ref[:, HALO:HALO + tt, :] = u
    t_idx = lax.broadcasted_iota(jnp.int32, (nb, tt, POOL_GW), 1)
    if grp.nb == 1:
        pos = (i % grp.tiles_per_batch) * TM + t_idx + pos0
    else:
        pos = t_idx + pos0
    for gi, w in enumerate(POOL_WINDOWS):
        cs = slice(gi * POOL_GW, (gi + 1) * POOL_GW)
        acc = ext_ref[:, pl.ds(HALO, tt), cs]
        for kk in range(1, w):
            acc = acc + ext_ref[:, pl.ds(HALO - kk, tt), cs]
        cnt = jnp.minimum(pos + 1, w).astype(F32)
        pooled = acc / cnt - u[:, :, cs]
        yp = _bdot(pooled.reshape(TM, POOL_GW), pw_ref[gi]) * ps_ref[:, cs]
        ymix_ref[:, GLA_WIDTH + gi * POOL_GW:GLA_WIDTH + (gi + 1) * POOL_GW] = yp.astype(BF16)

    y = jnp.dot(ymix_ref[...], wo_ref[...], preferred_element_type=F32)
    x1 = x_ref[...] + _mod_rows(mod_ref, 2) * y.reshape(nb, tt, D_MODEL)
    x1_ref[...] = x1
    h2 = _rmsnorm_mod(x1, n2_ref[...], _mod_rows(mod_ref, 4), _mod_rows(mod_ref, 3)).reshape(TM, D_MODEL)
    h2_ref[...] = h2
    logits = _dot_f32(h2, wr_ref[...]) + br_ref[...]
    rt_ref[...] = _route(logits)


def _mix_out(grp, n_tok, pos0, zero_first_halo, o, qkvr, u, halo_src, halo_block, halo_map, x, mod, norm2,
             gla_norm, pool_w, pool_scale, w_out, w_router, b_router, shared=()):
    n_alias = len(shared)
    n = grp.n_tiles
    n_fill = 0 if shared else n_tok // TM - n
    assert n_fill == 0 or grp.tile_off == 0
    clamp = lambda f: (lambda i: f(jnp.minimum(i, n - 1)))
    const2 = lambda i: (0, 0)
    row = clamp(lambda i: (i, 0))
    off = grp.tile_off
    kern = functools.partial(_mix_out_kernel, grp, pos0, zero_first_halo, n_alias)
    return pl.pallas_call(
        kern,
        name="mix_out",
        grid=(n + n_fill,),
        in_specs=[pl.BlockSpec((TM, GLA_WIDTH), row),
                  pl.BlockSpec((TM, GLA_WIDTH), clamp(lambda i: (i, 2))),
                  pl.BlockSpec((TM, POOL_WIDTH), row),
                  pl.BlockSpec(halo_block, clamp(halo_map)),
                  pl.BlockSpec((grp.nb, grp.tt, D_MODEL), clamp(grp.x_map())),
                  pl.BlockSpec((grp.nb, 6, D_MODEL), clamp(grp.mod_map())),
                  pl.BlockSpec((1, 1, D_MODEL), lambda i: (0, 0, 0)),
                  pl.BlockSpec((1, GLA_WIDTH), const2),
                  pl.BlockSpec(pool_w.shape, lambda i: (0, 0, 0), pipeline_mode=pl.Buffered(1)),
                  pl.BlockSpec((1, POOL_WIDTH), const2),
                  pl.BlockSpec(w_out.shape, const2, pipeline_mode=pl.Buffered(1)),
                  pl.BlockSpec(w_router.shape, const2, pipeline_mode=pl.Buffered(1)),
                  pl.BlockSpec((1, LANES), const2)]
                 + [pl.BlockSpec(memory_space=pl.ANY)] * n_alias,
        out_specs=[pl.BlockSpec((grp.nb, grp.tt, D_MODEL), clamp(grp.x_map())),
                   pl.BlockSpec((TM, D_MODEL), lambda i: (i + off, 0)),
                   pl.BlockSpec((TM, LANES), lambda i: (i + off, 0))],
        out_shape=[jax.ShapeDtypeStruct(x.shape, F32),
                   jax.ShapeDtypeStruct((n_tok, D_MODEL), F32),
                   jax.ShapeDtypeStruct((n_tok, LANES), F32)],
        scratch_shapes=[pltpu.VMEM((grp.nb, HALO + grp.tt, POOL_WIDTH), F32),
                        pltpu.VMEM((TM, D_MODEL), BF16)],
        input_output_aliases={13 + k: 1 + k for k in range(n_alias)},
        compiler_params=_cparams(1),
    )(o, qkvr, u, halo_src, x, mod, norm2.reshape(1, 1, D_MODEL), gla_norm.reshape(1, GLA_WIDTH),
      pool_w, pool_scale.reshape(1, POOL_WIDTH), w_out, w_router, b_router, *shared)


def _row_copy(src_hbm, src_row, dst, dst_row, sem):
    return pltpu.make_async_copy(src_hbm.at[pl.ds(src_row, 1), :], dst.at[pl.ds(dst_row, 1), :], sem)


def _tile_wait(src_hbm, dst, sem):
    pltpu.make_async_copy(src_hbm.at[pl.ds(0, dst.shape[0]), :], dst, sem).wait()


def _dispatch_kernel(src_ref, nu_ref, h_hbm, hs_ref, buf, sem):
    i = pl.program_id(0)
    n_used = (nu_ref[0] * TME + TD - 1) // TD
    slot = i % 2

    def issue(tile, slot_):
        base = tile * TD

        def body(r, carry):
            _row_copy(h_hbm, src_ref[base + 2 * r], buf.at[slot_], 2 * r, sem.at[slot_]).start(priority=0)
            _row_copy(h_hbm, src_ref[base + 2 * r + 1], buf.at[slot_], 2 * r + 1, sem.at[slot_]).start(priority=1)
            return carry

        lax.fori_loop(0, TD // 2, body, 0, unroll=4)

    @pl.when(i == 0)
    def _():
        issue(0, 0)

    @pl.when(i + 1 < n_used)
    def _():
        issue(i + 1, 1 - slot)

    @pl.when(i < n_used)
    def _():
        _tile_wait(h_hbm, buf.at[slot], sem.at[slot])
        hs_ref[...] = buf[slot].astype(BF16)

    @pl.when(i >= n_used)
    def _():
        hs_ref[...] = jnp.zeros_like(hs_ref)


def _dispatch(plan, h2_all):
    n_sorted = plan["src_row"].shape[0]
    assert n_sorted % TD == 0
    grid_spec = pltpu.PrefetchScalarGridSpec(
        num_scalar_prefetch=2,
        grid=(n_sorted // TD,),
        in_specs=[pl.BlockSpec(memory_space=pl.ANY)],
        out_specs=pl.BlockSpec((TD, D_MODEL), lambda i, *_: (i, 0)),
        scratch_shapes=[pltpu.VMEM((2, TD, D_MODEL), F32), pltpu.SemaphoreType.DMA((2,))],
    )
    return pl.pallas_call(
        _dispatch_kernel,
        name="dispatch",
        grid_spec=grid_spec,
        out_shape=jax.ShapeDtypeStruct((n_sorted, D_MODEL), BF16),
        compiler_params=_cparams(1),
    )(plan["src_row"], plan["n_used"], h2_all)


def _moe_kernel(te_ref, nu_ref, seg_ref, nxt_ref, hs_ref, w1_hbm, w3_hbm, w2_hbm, y_ref,
                wf1, wf3, wf2, wsem, w1b, w3b, w2b):
    i = pl.program_id(0)
    n_used = nu_ref[0]

    def weight_copies(expert, wslot):
        return (pltpu.make_async_copy(w1_hbm.at[expert], wf1.at[wslot], wsem.at[wslot]),
                pltpu.make_async_copy(w3_hbm.at[expert], wf3.at[wslot], wsem.at[wslot]),
                pltpu.make_async_copy(w2_hbm.at[expert], wf2.at[wslot], wsem.at[wslot]))

    @pl.when(i == 0)
    def _():
        for c in weight_copies(te_ref[0], 0):
            c.start()

    prev = jnp.maximum(i - 1, 0)

    @pl.when((i < n_used) & ((i == 0) | (te_ref[i] != te_ref[prev])))
    def _():
        wslot = seg_ref[i] % 2
        for c in weight_copies(te_ref[i], wslot):
            c.wait()
        w1b[...] = wf1[wslot].astype(BF16)
        w3b[...] = wf3[wslot].astype(BF16)
        w2b[...] = wf2[wslot].astype(BF16)

        @pl.when(nxt_ref[i] >= 0)
        def _():
            for c in weight_copies(nxt_ref[i], 1 - wslot):
                c.start()

    @pl.when(i < n_used)
    def _():
        x = hs_ref[...]
        a = jnp.dot(x, w1b[...], preferred_element_type=F32)
        b = jnp.dot(x, w3b[...], preferred_element_type=F32)
        hid = _silu(a) * b
        y_ref[...] = jnp.dot(hid.astype(BF16), w2b[...], preferred_element_type=F32)

    @pl.when(i >= n_used)
    def _():
        y_ref[...] = jnp.zeros_like(y_ref)


def _moe(plan, h_sorted, w1, w3, w2):
    n_sorted = h_sorted.shape[0]
    n_tiles = n_sorted // TME
    used = lambda i, te, nu, seg, nxt: (jnp.minimum(i, nu[0] - 1), 0)
    grid_spec = pltpu.PrefetchScalarGridSpec(
        num_scalar_prefetch=4,
        grid=(n_tiles,),
        in_specs=[pl.BlockSpec((TME, D_MODEL), used)] + [pl.BlockSpec(memory_space=pl.ANY)] * 3,
        out_specs=pl.BlockSpec((TME, D_MODEL), lambda i, *_: (i, 0)),
        scratch_shapes=[pltpu.VMEM((2, D_MODEL, EXPERT_FF), F32), pltpu.VMEM((2, D_MODEL, EXPERT_FF), F32),
                        pltpu.VMEM((2, EXPERT_FF, D_MODEL), F32),
                        pltpu.SemaphoreType.DMA((2,)),
                        pltpu.VMEM((D_MODEL, EXPERT_FF), BF16), pltpu.VMEM((D_MODEL, EXPERT_FF), BF16),
                        pltpu.VMEM((EXPERT_FF, D_MODEL), BF16)],
    )
    return pl.pallas_call(
        _moe_kernel,
        name="moe",
        grid_spec=grid_spec,
        out_shape=jax.ShapeDtypeStruct((n_sorted, D_MODEL), F32),
        compiler_params=_cparams(1),
    )(plan["tile_expert"], plan["n_used"], plan["segment"], plan["next_expert"], h_sorted, w1, w3, w2)


def _finish_kernel(grp, pos_ref, x1_ref, mod_ref, rt_ref, nf_ref, y_hbm, out_ref, buf_a, buf_b, sem):
    i = pl.program_id(0)
    n_steps = pl.num_programs(0)
    n_slots = FINISH_AHEAD + 1
    slot = i % n_slots

    def issue(tile, slot_):
        base = (tile * TM + grp.row_off) * 2

        def body(r, carry):
            _row_copy(y_hbm, pos_ref[base + 2 * r], buf_a.at[slot_], r, sem.at[slot_]).start(priority=0)
            _row_copy(y_hbm, pos_ref[base + 2 * r + 1], buf_b.at[slot_], r, sem.at[slot_]).start(priority=1)
            return carry

        lax.fori_loop(0, TM, body, 0, unroll=8)

    @pl.when(i == 0)
    def _():
        for t in range(FINISH_AHEAD):
            @pl.when(t < n_steps)
            def _():
                issue(t, t)

    @pl.when(i + FINISH_AHEAD < n_steps)
    def _():
        issue(i + FINISH_AHEAD, (i + FINISH_AHEAD) % n_slots)

    _tile_wait(y_hbm, buf_a.at[slot], sem.at[slot])
    _tile_wait(y_hbm, buf_b.at[slot], sem.at[slot])
    rt = rt_ref[...]
    moe = rt[:, 2:3] * buf_a[slot] + rt[:, 3:4] * buf_b[slot]
    x2 = x1_ref[...] + _mod_rows(mod_ref, 5) * moe.reshape(grp.nb, grp.tt, D_MODEL)
    ms = jnp.mean(x2 * x2, axis=-1, keepdims=True)
    out_ref[...] = x2 * lax.rsqrt(ms + EPS) * nf_ref[...]


def _finish(grp, pos, x1, mod, route_all, norm_f, y_sorted):
    off = grp.tile_off
    grid_spec = pltpu.PrefetchScalarGridSpec(
        num_scalar_prefetch=1,
        grid=(grp.n_tiles,),
        in_specs=[pl.BlockSpec((grp.nb, grp.tt, D_MODEL), grp.x_map()),
                  pl.BlockSpec((grp.nb, 6, D_MODEL), grp.mod_map()),
                  pl.BlockSpec((TM, LANES), lambda i, p: (i + off, 0)),
                  pl.BlockSpec((1, 1, D_MODEL), lambda i, p: (0, 0, 0)),
                  pl.BlockSpec(memory_space=pl.ANY)],
        out_specs=pl.BlockSpec((grp.nb, grp.tt, D_MODEL), grp.x_map()),
        scratch_shapes=[pltpu.VMEM((FINISH_AHEAD + 1, TM, D_MODEL), F32),
                        pltpu.VMEM((FINISH_AHEAD + 1, TM, D_MODEL), F32),
                        pltpu.SemaphoreType.DMA((FINISH_AHEAD + 1,))],
    )
    return pl.pallas_call(
        functools.partial(_finish_kernel, grp),
        name="finish",
        grid_spec=grid_spec,
        out_shape=jax.ShapeDtypeStruct(x1.shape, F32),
        compiler_params=_cparams(1),
    )(pos, x1, mod, route_all, norm_f.reshape(1, 1, D_MODEL), y_sorted)


def _sort_plan(route_all):
    n_tok = route_all.shape[0]
    n_pairs = 2 * n_tok
    n_sorted = n_pairs + N_EXPERTS * TME
    flat_e = route_all[:, 0:2].astype(jnp.int32).reshape(n_pairs)
    onehot = (flat_e[:, None] == jnp.arange(N_EXPERTS, dtype=jnp.int32)[None, :]).astype(jnp.int32)
    csum = jnp.cumsum(onehot, axis=0)
    rank = jnp.sum(onehot * csum, axis=1) - 1
    counts = csum[-1]
    padded = ((counts + TME - 1) // TME) * TME
    ends = jnp.cumsum(padded)
    starts = ends - padded
    pos = starts[flat_e] + rank
    token = jnp.arange(n_pairs, dtype=jnp.int32) // 2
    src_row = jnp.zeros((n_sorted,), jnp.int32).at[pos].set(token)
    tile_start = jnp.arange(n_sorted // TME, dtype=jnp.int32) * TME
    tile_expert = jnp.sum((tile_start[:, None] >= ends[None, :]).astype(jnp.int32), axis=1)
    tile_expert = jnp.minimum(tile_expert, N_EXPERTS - 1)
    n_used = ends[-1] // TME
    is_first = jnp.concatenate([jnp.ones((1,), jnp.int32),
                                (tile_expert[1:] != tile_expert[:-1]).astype(jnp.int32)])
    segment = jnp.cumsum(is_first) - 1
    next_tile = ends[tile_expert] // TME
    next_expert = jnp.where(next_tile < n_used, tile_expert[jnp.minimum(next_tile, n_sorted // TME - 1)], -1)
    return dict(pos=pos.astype(jnp.int32), src_row=src_row, tile_expert=tile_expert.astype(jnp.int32),
                n_used=n_used.astype(jnp.int32).reshape(1), segment=segment.astype(jnp.int32),
                next_expert=next_expert.astype(jnp.int32))


def kernel(x_prompt, x_sample, c_prompt, c_sample, state_gla, state_pool, w_ada, b_ada, norm1, norm2, w_in,
           gate_up, gate_bias, gla_norm, pool_w, pool_scale, w_out, w_group, b_group, w_expert, b_expert,
           w1, w3, w2, norm_f):
    assert w_ada.shape[0] == 1, "single-layer step"
    bp, tp, _ = x_prompt.shape
    bs, ts, _ = x_sample.shape
    grp_p = _Group(bp, tp, 0, bs)
    grp_s = _Group(bs, ts, bp * tp, 0)
    n_tok = bp * tp + bs * ts

    n_c = bp + bs
    n_c_pad = -(-n_c // 8) * 8
    c_all = jnp.concatenate([c_sample, c_prompt, jnp.zeros((n_c_pad - n_c, D_MODEL), F32)], axis=0)
    mod = _adaln(c_all, w_ada[0], b_ada[0]).reshape(n_c_pad, 6, D_MODEL)
    mod_p = mod_s = mod

    wi = w_in[0]
    wq = wi[:, :QKVR_WIDTH].astype(BF16)
    wg = jnp.pad(wi[:, QKVR_WIDTH:QKVR_WIDTH + GATE_RANK], ((0, 0), (0, LANES - GATE_RANK))).astype(BF16)
    wu = wi[:, QKVR_WIDTH + GATE_RANK:].astype(BF16)
    gup = jnp.pad(gate_up[0], ((0, LANES - GATE_RANK), (0, 0))).astype(BF16)
    gb = gate_bias[0].reshape(1, GLA_KEY_WIDTH)
    pw = pool_w[0].astype(BF16)
    wo = w_out[0].astype(BF16)
    w_router = jnp.concatenate(
        [w_group[0], jnp.transpose(w_expert[0], (1, 0, 2)).reshape(D_MODEL, N_EXPERTS),
         jnp.zeros((D_MODEL, LANES - N_GROUPS - N_EXPERTS), F32)], axis=1)
    b_router = jnp.concatenate([b_group[0], b_expert[0].reshape(N_EXPERTS),
                                jnp.zeros((LANES - N_GROUPS - N_EXPERTS,), F32)]).reshape(1, LANES)

    qkvr_p, la_p, u_p = _in_proj(grp_p, x_prompt, mod_p, norm1[0], wq, wg, wu, gup, gb)
    qkvr_s, la_s, u_s = _in_proj(grp_s, x_sample, mod_s, norm1[0], wq, wg, wu, gup, gb)

    o_p, gla_p = _gla_prompt(bp, tp, qkvr_p, la_p)
    o_s, gla_s = _gla_decode(bs, ts, qkvr_s, la_s, state_gla[0])

    halo_per_tile = TM // HALO
    halo_map_p = lambda i: (jnp.maximum(i * halo_per_tile - 1, 0), 0)
    x1_p, h2_all, route_all = _mix_out(grp_p, n_tok, 0, True, o_p, qkvr_p, u_p, u_p, (HALO, POOL_WIDTH),
                                       halo_map_p, x_prompt, mod_p, norm2[0], gla_norm[0], pw, pool_scale[0],
                                       wo, w_router, b_router)
    x1_s, h2_all, route_all = _mix_out(grp_s, n_tok, PAST_LEN, False, o_s, qkvr_s, u_s, state_pool[0],
                                       (grp_s.nb, POOL_BUF, POOL_WIDTH), lambda i: (i, 0, 0), x_sample, mod_s,
                                       norm2[0], gla_norm[0], pw, pool_scale[0], wo, w_router, b_router,
                                       shared=(h2_all, route_all))

    plan = _sort_plan(route_all)
    y_sorted = _moe(plan, _dispatch(plan, h2_all), w1[0], w3[0], w2[0])

    y_p = _finish(grp_p, plan["pos"], x1_p, mod_p, route_all, norm_f, y_sorted)
    y_s = _finish(grp_s, plan["pos"], x1_s, mod_s, route_all, norm_f, y_sorted)

    u_p3 = u_p.reshape(bp, tp, POOL_WIDTH)
    u_s3 = u_s.reshape(bs, ts, POOL_WIDTH)
    pool_p = jnp.concatenate([jnp.zeros((bp, POOL_BUF, POOL_WIDTH), F32), u_p3], axis=1)[:, -POOL_BUF:]
    pool_s = jnp.concatenate([state_pool[0], u_s3], axis=1)[:, -POOL_BUF:]
    return (y_p, y_s, gla_p[None], pool_p[None], gla_s[None], pool_s[None])
```

```python
import functools

import jax
import jax.numpy as jnp
from jax import lax
from jax.experimental import pallas as pl
from jax.experimental.pallas import tpu as pltpu

D_MODEL = 2048
GLA_HEADS = 4
GLA_DK = 128
GLA_DV = 256
GLA_KEY_WIDTH = GLA_HEADS * GLA_DK
GLA_WIDTH = GLA_HEADS * GLA_DV
POOL_WIDTH = 1024
POOL_WINDOWS = (2, 4, 8, 16)
POOL_GW = 256
POOL_BUF = 15
HALO = 16
GATE_RANK = 16
GATE_TEMP = 16.0
N_GROUPS = 4
EXPERTS_PER_GROUP = 8
N_EXPERTS = 32
EXPERT_FF = 512
EPS = 1e-6
PAST_LEN = 16384
QKVR_WIDTH = 2 * GLA_KEY_WIDTH + 2 * GLA_WIDTH

LANES = 128
TM = 256
TME = 256
TD = 1024
FINISH_AHEAD = 3
GLA_CHUNK = 64
GLA_SUB = 16
GLA_STEP = 512
VMEM_LIMIT = 56 * 1024 * 1024

BF16 = jnp.bfloat16
F32 = jnp.float32
NEG = -1e30


def _cparams(n_axes):
    return pltpu.CompilerParams(dimension_semantics=("arbitrary",) * n_axes,
                                vmem_limit_bytes=VMEM_LIMIT)


def _silu(x):
    return x / (1.0 + jnp.exp(-x))


def _bdot(a, b):
    return jnp.dot(a.astype(BF16), b.astype(BF16), preferred_element_type=F32)


def _split3(a):
    a1 = a.astype(BF16)
    r1 = a - a1.astype(F32)
    a2 = r1.astype(BF16)
    a3 = (r1 - a2.astype(F32)).astype(BF16)
    return a1, a2, a3


def _dot_f32(a, b):
    a1, a2, a3 = _split3(a)
    b1, b2, b3 = _split3(b)
    d = lambda x, y: jnp.dot(x, y, preferred_element_type=F32)
    small = d(a2, b2) + d(a1, b3) + d(a3, b1)
    mid = d(a1, b2) + d(a2, b1)
    return d(a1, b1) + (mid + small)


def _dot_exact_lhs(tri_bf16, g):
    g1, g2, g3 = _split3(g)
    d = lambda y: jnp.dot(tri_bf16, y, preferred_element_type=F32)
    return d(g1) + (d(g2) + d(g3))


def _adaln_kernel(c_ref, w_ref, b_ref, o_ref):
    c = c_ref[...]
    o_ref[...] = _bdot(_silu(c), w_ref[...]) + b_ref[...]


def _adaln(c_all, w_ada, b_ada):
    n, d = c_all.shape
    width = w_ada.shape[1]
    tn = 1024
    return pl.pallas_call(
        _adaln_kernel,
        name="adaln",
        grid=(width // tn,),
        in_specs=[pl.BlockSpec((n, d), lambda j: (0, 0)),
                  pl.BlockSpec((d, tn), lambda j: (0, j)),
                  pl.BlockSpec((1, tn), lambda j: (0, j))],
        out_specs=pl.BlockSpec((n, tn), lambda j: (0, j)),
        out_shape=jax.ShapeDtypeStruct((n, width), F32),
        compiler_params=_cparams(1),
    )(c_all, w_ada, b_ada.reshape(1, width))


class _Group:
    def __init__(self, batch, seq, row_off, mod_off):
        self.batch, self.seq, self.row_off, self.mod_off = batch, seq, row_off, mod_off
        if seq >= TM:
            assert seq % TM == 0
            self.nb, self.tt = 1, TM
            self.tiles_per_batch = seq // TM
            self.n_tiles = batch * self.tiles_per_batch
        else:
            assert TM % seq == 0 and batch % (TM // seq) == 0
            self.nb, self.tt = TM // seq, seq
            self.tiles_per_batch = 1
            self.n_tiles = batch // self.nb
        self.rows = batch * seq
        self.tile_off = row_off // TM

    def x_map(self):
        if self.nb == 1:
            tpb = self.tiles_per_batch
            return lambda i, *_: (i // tpb, i % tpb, 0)
        return lambda i, *_: (i, 0, 0)

    def mod_map(self):
        assert self.mod_off % self.nb == 0
        off = self.mod_off // self.nb
        if self.nb == 1:
            tpb = self.tiles_per_batch
            return lambda i, *_: (i // tpb + off, 0, 0)
        return lambda i, *_: (i + off, 0, 0)


def _mod_rows(mod_ref, idx):
    return mod_ref[:, idx:idx + 1, :]


def _rmsnorm_mod(x, gain, scale, shift):
    ms = jnp.mean(x * x, axis=-1, keepdims=True)
    y = x * lax.rsqrt(ms + EPS) * gain
    return y * (1.0 + scale) + shift


def _in_proj_kernel(x_ref, mod_ref, n1_ref, wq_ref, wg_ref, wu_ref, gup_ref, gb_ref,
                    qkvr_ref, la_ref, u_ref):
    x = x_ref[...]
    h = _rmsnorm_mod(x, n1_ref[...], _mod_rows(mod_ref, 1), _mod_rows(mod_ref, 0))
    hb = h.reshape(TM, D_MODEL).astype(BF16)
    qkvr_ref[...] = jnp.dot(hb, wq_ref[...], preferred_element_type=F32)
    u_ref[...] = jnp.dot(hb, wu_ref[...], preferred_element_type=F32)
    g_lr = jnp.dot(hb, wg_ref[...], preferred_element_type=F32)
    pre = jnp.dot(g_lr.astype(BF16), gup_ref[...], preferred_element_type=F32) + gb_ref[...]
    log_sig = jnp.minimum(pre, 0.0) - jnp.log1p(jnp.exp(-jnp.abs(pre)))
    la_ref[...] = log_sig / GATE_TEMP


def _in_proj(grp, x, mod, norm1, wq, wg, wu, gup, gb):
    const = lambda i: (0, 0)
    row = lambda i: (i, 0)
    return pl.pallas_call(
        _in_proj_kernel,
        name="in_proj",
        grid=(grp.n_tiles,),
        in_specs=[pl.BlockSpec((grp.nb, grp.tt, D_MODEL), grp.x_map()),
                  pl.BlockSpec((grp.nb, 6, D_MODEL), grp.mod_map()),
                  pl.BlockSpec((1, 1, D_MODEL), lambda i: (0, 0, 0)),
                  pl.BlockSpec(wq.shape, const, pipeline_mode=pl.Buffered(1)),
                  pl.BlockSpec(wg.shape, const, pipeline_mode=pl.Buffered(1)),
                  pl.BlockSpec(wu.shape, const, pipeline_mode=pl.Buffered(1)),
                  pl.BlockSpec(gup.shape, const, pipeline_mode=pl.Buffered(1)),
                  pl.BlockSpec(gb.shape, const, pipeline_mode=pl.Buffered(1))],
        out_specs=[pl.BlockSpec((TM, QKVR_WIDTH), row),
                   pl.BlockSpec((TM, GLA_KEY_WIDTH), row),
                   pl.BlockSpec((TM, POOL_WIDTH), row)],
        out_shape=[jax.ShapeDtypeStruct((grp.rows, QKVR_WIDTH), F32),
                   jax.ShapeDtypeStruct((grp.rows, GLA_KEY_WIDTH), F32),
                   jax.ShapeDtypeStruct((grp.rows, POOL_WIDTH), F32)],
        compiler_params=_cparams(1),
    )(x, mod, norm1.reshape(1, 1, D_MODEL), wq, wg, wu, gup, gb)


def _gla_head_chunk(q, k, v, g, state_t, chunk, sub):
    rows = lax.broadcasted_iota(jnp.int32, (chunk, chunk), 0)
    cols = lax.broadcasted_iota(jnp.int32, (chunk, chunk), 1)
    tri = (rows >= cols).astype(BF16)
    b = _dot_exact_lhs(tri, g)
    q = q * (GLA_DK ** -0.5)
    nt = (((1,), (1,)), ((), ()))
    tn = (((0,), (0,)), ((), ()))
    o = lax.dot_general((q * jnp.exp(b)).astype(BF16), state_t.astype(BF16), nt,
                        preferred_element_type=F32)

    b_last = b[chunk - 1:chunk, :]
    k_dec = k * jnp.exp(b_last - b)
    new_state_t = jnp.exp(b_last) * state_t + lax.dot_general(
        v.astype(BF16), k_dec.astype(BF16), tn, preferred_element_type=F32)

    lane = lax.broadcasted_iota(jnp.int32, (sub, chunk), 1)
    sub_row = lax.broadcasted_iota(jnp.int32, (sub, LANES), 0)
    key_row = lax.broadcasted_iota(jnp.int32, (chunk, LANES), 0)
    p_blocks = []
    for s in range(chunk // sub):
        lo = s * sub
        q_s = q[lo:lo + sub, :]
        b_s = b[lo:lo + sub, :]
        p = jnp.zeros((sub, chunk), F32)
        if s > 0:
            ref_row = b[lo - 1:lo, :]
            q_rel = q_s * jnp.exp(b_s - ref_row)
            k_rel = k * jnp.exp(jnp.where(key_row < lo, ref_row - b, NEG))
            p = lax.dot_general(q_rel.astype(BF16), k_rel.astype(BF16), nt, preferred_element_type=F32)
        for jl in range(sub):
            j = lo + jl
            expo = jnp.where(sub_row >= jl, b_s - b[j:j + 1, :], NEG)
            col = jnp.sum(q_s * k[j:j + 1, :] * jnp.exp(expo), axis=-1, keepdims=True)
            p = jnp.where(lane == j, col, p)
        p_blocks.append(p)
    p_full = p_blocks[0] if len(p_blocks) == 1 else jnp.concatenate(p_blocks, axis=0)
    o = o + _bdot(p_full, v)
    return o, new_state_t


def _gla_prompt_kernel(q_ref, k_ref, v_ref, la_ref, o_ref, s_ref, st_ref):
    step = pl.program_id(2)

    @pl.when(step == 0)
    def _():
        st_ref[...] = jnp.zeros_like(st_ref)

    def body(c, carry):
        r0 = pl.multiple_of(c * GLA_CHUNK, GLA_CHUNK)
        sl = pl.ds(r0, GLA_CHUNK)
        o, new_state_t = _gla_head_chunk(q_ref[sl, :], k_ref[sl, :], v_ref[sl, :], la_ref[sl, :],
                                         st_ref[...], GLA_CHUNK, GLA_SUB)
        o_ref[sl, :] = o
        st_ref[...] = new_state_t
        return carry

    lax.fori_loop(0, GLA_STEP // GLA_CHUNK, body, 0)

    @pl.when(step == pl.num_programs(2) - 1)
    def _():
        s_ref[0, 0] = st_ref[...].T


def _gla_prompt(batch, seq, qkvr, log_a):
    steps = seq // GLA_STEP
    row = lambda b, h, s: b * steps + s
    return pl.pallas_call(
        _gla_prompt_kernel,
        name="gla_prompt",
        grid=(batch, GLA_HEADS, steps),
        in_specs=[pl.BlockSpec((GLA_STEP, GLA_DK), lambda b, h, s: (row(b, h, s), h)),
                  pl.BlockSpec((GLA_STEP, GLA_DK), lambda b, h, s: (row(b, h, s), GLA_HEADS + h)),
                  pl.BlockSpec((GLA_STEP, GLA_DV), lambda b, h, s: (row(b, h, s), GLA_HEADS + h)),
                  pl.BlockSpec((GLA_STEP, GLA_DK), lambda b, h, s: (row(b, h, s), h))],
        out_specs=[pl.BlockSpec((GLA_STEP, GLA_DV), lambda b, h, s: (row(b, h, s), h)),
                   pl.BlockSpec((1, 1, GLA_DK, GLA_DV), lambda b, h, s: (b, h, 0, 0))],
        out_shape=[jax.ShapeDtypeStruct((batch * seq, GLA_WIDTH), F32),
                   jax.ShapeDtypeStruct((batch, GLA_HEADS, GLA_DK, GLA_DV), F32)],
        scratch_shapes=[pltpu.VMEM((GLA_DV, GLA_DK), F32)],
        compiler_params=_cparams(3),
    )(qkvr, qkvr, qkvr, log_a)


GLA_DEC_BB = 16


def _gla_decode_kernel(seq, q_ref, k_ref, v_ref, la_ref, s0_ref, o_ref, s_ref):
    def body(i, carry):
        r0 = pl.multiple_of(i * seq, seq)
        sl = pl.ds(r0, seq)
        o, new_state_t = _gla_head_chunk(q_ref[sl, :], k_ref[sl, :], v_ref[sl, :], la_ref[sl, :],
                                         s0_ref[i, 0].T, seq, seq)
        o_ref[sl, :] = o
        s_ref[i, 0] = new_state_t.T
        return carry

    lax.fori_loop(0, GLA_DEC_BB, body, 0)


def _gla_decode(batch, seq, qkvr, log_a, state):
    rows = GLA_DEC_BB * seq
    return pl.pallas_call(
        functools.partial(_gla_decode_kernel, seq),
        name="gla_decode",
        grid=(batch // GLA_DEC_BB, GLA_HEADS),
        in_specs=[pl.BlockSpec((rows, GLA_DK), lambda i, h: (i, h)),
                  pl.BlockSpec((rows, GLA_DK), lambda i, h: (i, GLA_HEADS + h)),
                  pl.BlockSpec((rows, GLA_DV), lambda i, h: (i, GLA_HEADS + h)),
                  pl.BlockSpec((rows, GLA_DK), lambda i, h: (i, h)),
                  pl.BlockSpec((GLA_DEC_BB, 1, GLA_DK, GLA_DV), lambda i, h: (i, h, 0, 0))],
        out_specs=[pl.BlockSpec((rows, GLA_DV), lambda i, h: (i, h)),
                   pl.BlockSpec((GLA_DEC_BB, 1, GLA_DK, GLA_DV), lambda i, h: (i, h, 0, 0))],
        out_shape=[jax.ShapeDtypeStruct((batch * seq, GLA_WIDTH), F32),
                   jax.ShapeDtypeStruct((batch, GLA_HEADS, GLA_DK, GLA_DV), F32)],
        compiler_params=_cparams(2),
    )(qkvr, qkvr, qkvr, log_a, state)


def _route(logits):
    lane = lax.broadcasted_iota(jnp.int32, logits.shape, 1)
    big = jnp.int32(10 ** 6)
    is_group = lane < N_GROUPS
    lg = jnp.where(is_group, logits, NEG)
    mg = jnp.max(lg, axis=-1, keepdims=True)
    g_idx = jnp.min(jnp.where(is_group & (lg == mg), lane, big), axis=-1, keepdims=True)
    denom = jnp.sum(jnp.where(is_group, jnp.exp(lg - mg), 0.0), axis=-1, keepdims=True)
    p_sel = 1.0 / denom
    first = N_GROUPS + EXPERTS_PER_GROUP * g_idx
    in_grp = (lane >= first) & (lane < first + EXPERTS_PER_GROUP)
    le = jnp.where(in_grp, logits, NEG)
    m1 = jnp.max(le, axis=-1, keepdims=True)
    i1 = jnp.min(jnp.where(in_grp & (le == m1), lane, big), axis=-1, keepdims=True)
    rest = in_grp & (lane != i1)
    le2 = jnp.where(rest, logits, NEG)
    m2 = jnp.max(le2, axis=-1, keepdims=True)
    i2 = jnp.min(jnp.where(rest & (le2 == m2), lane, big), axis=-1, keepdims=True)
    e2 = jnp.exp(m2 - m1)
    w1 = p_sel / (1.0 + e2)
    w2 = p_sel * e2 / (1.0 + e2)
    ex1 = (i1 - N_GROUPS).astype(F32)
    ex2 = (i2 - N_GROUPS).astype(F32)
    return jnp.where(lane == 0, ex1, jnp.where(lane == 1, ex2,
                     jnp.where(lane == 2, w1, jnp.where(lane == 3, w2, 0.0))))


def _mix_out_kernel(grp, pos0, zero_first_halo, n_alias,
                    o_ref, r_ref, u_ref, halo_ref, x_ref, mod_ref, n2_ref, gn_ref, pw_ref, ps_ref,
                    wo_ref, wr_ref, br_ref, *rest):
    x1_ref, h2_ref, rt_ref, ext_ref, ymix_ref = rest[n_alias:]
    i = pl.program_id(0)

    @pl.when(i < grp.n_tiles)
    def _():
        _mix_out_tile(grp, pos0, zero_first_halo, i, o_ref, r_ref, u_ref, halo_ref, x_ref, mod_ref, n2_ref,
                      gn_ref, pw_ref, ps_ref, wo_ref, wr_ref, br_ref, x1_ref, h2_ref, rt_ref, ext_ref, ymix_ref)

    @pl.when(i >= grp.n_tiles)
    def _():
        h2_ref[...] = jnp.zeros_like(h2_ref)
        rt_ref[...] = jnp.zeros_like(rt_ref)


def _mix_out_tile(grp, pos0, zero_first_halo, i, o_ref, r_ref, u_ref, halo_ref, x_ref, mod_ref, n2_ref,
                  gn_ref, pw_ref, ps_ref, wo_ref, wr_ref, br_ref, x1_ref, h2_ref, rt_ref, ext_ref, ymix_ref):
    nb, tt = grp.nb, grp.tt
    hist = halo_ref.shape[-2]

    for h in range(GLA_HEADS):
        cs = slice(h * GLA_DV, (h + 1) * GLA_DV)
        oh = o_ref[:, cs]
        ms = jnp.mean(oh * oh, axis=-1, keepdims=True)
        yh = oh * lax.rsqrt(ms + EPS) * gn_ref[:, cs] * _silu(r_ref[:, cs])
        ymix_ref[:, cs] = yh.astype(BF16)

    halo = halo_ref[...]
    if zero_first_halo:
        halo = jnp.where(i % grp.tiles_per_batch == 0, 0.0, halo)
    ext_ref[:, HALO - hist:HALO, :] = halo.reshape(nb, hist, POOL_WIDTH)
    u = u_ref[...].reshape(nb, tt, POOL_WIDTH)
    ext_ref[:, HALO:HALO + tt, :] = u
    t_idx = lax.broadcasted_iota(jnp.int32, (nb, tt, POOL_GW), 1)
    if grp.nb == 1:
        pos = (i % grp.tiles_per_batch) * TM + t_idx + pos0
    else:
        pos = t_idx + pos0
    for gi, w in enumerate(POOL_WINDOWS):
        cs = slice(gi * POOL_GW, (gi + 1) * POOL_GW)
        acc = ext_ref[:, pl.ds(HALO, tt), cs]
        for kk in range(1, w):
            acc = acc + ext_ref[:, pl.ds(HALO - kk, tt), cs]
        cnt = jnp.minimum(pos + 1, w).astype(F32)
        pooled = acc / cnt - u[:, :, cs]
        yp = _bdot(pooled.reshape(TM, POOL_GW), pw_ref[gi]) * ps_ref[:, cs]
        ymix_ref[:, GLA_WIDTH + gi * POOL_GW:GLA_WIDTH + (gi + 1) * POOL_GW] = yp.astype(BF16)

    y = jnp.dot(ymix_ref[...], wo_ref[...], preferred_element_type=F32)
    x1 = x_ref[...] + _mod_rows(mod_ref, 2) * y.reshape(nb, tt, D_MODEL)
    x1_ref[...] = x1
    h2 = _rmsnorm_mod(x1, n2_ref[...], _mod_rows(mod_ref, 4), _mod_rows(mod_ref, 3)).reshape(TM, D_MODEL)
    h2_ref[...] = h2
    logits = _dot_f32(h2, wr_ref[...]) + br_ref[...]
    rt_ref[...] = _route(logits)


def _mix_out(grp, n_tok, pos0, zero_first_halo, o, qkvr, u, halo_src, halo_block, halo_map, x, mod, norm2,
             gla_norm, pool_w, pool_scale, w_out, w_router, b_router, shared=()):
    n_alias = len(shared)
    n = grp.n_tiles
    n_fill = 0 if shared else n_tok // TM - n
    assert n_fill == 0 or grp.tile_off == 0
    clamp = lambda f: (lambda i: f(jnp.minimum(i, n - 1)))
    const2 = lambda i: (0, 0)
    row = clamp(lambda i: (i, 0))
    off = grp.tile_off
    kern = functools.partial(_mix_out_kernel, grp, pos0, zero_first_halo, n_alias)
    return pl.pallas_call(
        kern,
        name="mix_out",
        grid=(n + n_fill,),
        in_specs=[pl.BlockSpec((TM, GLA_WIDTH), row),
                  pl.BlockSpec((TM, GLA_WIDTH), clamp(lambda i: (i, 2))),
                  pl.BlockSpec((TM, POOL_WIDTH), row),
                  pl.BlockSpec(halo_block, clamp(halo_map)),
                  pl.BlockSpec((grp.nb, grp.tt, D_MODEL), clamp(grp.x_map())),
                  pl.BlockSpec((grp.nb, 6, D_MODEL), clamp(grp.mod_map())),
                  pl.BlockSpec((1, 1, D_MODEL), lambda i: (0, 0, 0)),
                  pl.BlockSpec((1, GLA_WIDTH), const2),
                  pl.BlockSpec(pool_w.shape, lambda i: (0, 0, 0), pipeline_mode=pl.Buffered(1)),
                  pl.BlockSpec((1, POOL_WIDTH), const2),
                  pl.BlockSpec(w_out.shape, const2, pipeline_mode=pl.Buffered(1)),
                  pl.BlockSpec(w_router.shape, const2, pipeline_mode=pl.Buffered(1)),
                  pl.BlockSpec((1, LANES), const2)]
                 + [pl.BlockSpec(memory_space=pl.ANY)] * n_alias,
        out_specs=[pl.BlockSpec((grp.nb, grp.tt, D_MODEL), clamp(grp.x_map())),
                   pl.BlockSpec((TM, D_MODEL), lambda i: (i + off, 0)),
                   pl.BlockSpec((TM, LANES), lambda i: (i + off, 0))],
        out_shape=[jax.ShapeDtypeStruct(x.shape, F32),
                   jax.ShapeDtypeStruct((n_tok, D_MODEL), F32),
                   jax.ShapeDtypeStruct((n_tok, LANES), F32)],
        scratch_shapes=[pltpu.VMEM((grp.nb, HALO + grp.tt, POOL_WIDTH), F32),
                        pltpu.VMEM((TM, D_MODEL), BF16)],
        input_output_aliases={13 + k: 1 + k for k in range(n_alias)},
        compiler_params=_cparams(1),
    )(o, qkvr, u, halo_src, x, mod, norm2.reshape(1, 1, D_MODEL), gla_norm.reshape(1, GLA_WIDTH),
      pool_w, pool_scale.reshape(1, POOL_WIDTH), w_out, w_router, b_router, *shared)


def _row_copy(src_hbm, src_row, dst, dst_row, sem):
    return pltpu.make_async_copy(src_hbm.at[pl.ds(src_row, 1), :], dst.at[pl.ds(dst_row, 1), :], sem)


def _tile_wait(src_hbm, dst, sem):
    pltpu.make_async_copy(src_hbm.at[pl.ds(0, dst.shape[0]), :], dst, sem).wait()


def _dispatch_kernel(src_ref, nu_ref, h_hbm, hs_ref, buf, sem):
    i = pl.program_id(0)
    n_used = (nu_ref[0] * TME + TD - 1) // TD
    slot = i % 2

    def issue(tile, slot_):
        base = tile * TD

        def body(r, carry):
            _row_copy(h_hbm, src_ref[base + 2 * r], buf.at[slot_], 2 * r, sem.at[slot_]).start(priority=0)
            _row_copy(h_hbm, src_ref[base + 2 * r + 1], buf.at[slot_], 2 * r + 1, sem.at[slot_]).start(priority=1)
            return carry

        lax.fori_loop(0, TD // 2, body, 0, unroll=4)

    @pl.when(i == 0)
    def _():
        issue(0, 0)

    @pl.when(i + 1 < n_used)
    def _():
        issue(i + 1, 1 - slot)

    @pl.when(i < n_used)
    def _():
        _tile_wait(h_hbm, buf.at[slot], sem.at[slot])
        hs_ref[...] = buf[slot].astype(BF16)

    @pl.when(i >= n_used)
    def _():
        hs_ref[...] = jnp.zeros_like(hs_ref)


def _dispatch(plan, h2_all):
    n_sorted = plan["src_row"].shape[0]
    assert n_sorted % TD == 0
    grid_spec = pltpu.PrefetchScalarGridSpec(
        num_scalar_prefetch=2,
        grid=(n_sorted // TD,),
        in_specs=[pl.BlockSpec(memory_space=pl.ANY)],
        out_specs=pl.BlockSpec((TD, D_MODEL), lambda i, *_: (i, 0)),
        scratch_shapes=[pltpu.VMEM((2, TD, D_MODEL), F32), pltpu.SemaphoreType.DMA((2,))],
    )
    return pl.pallas_call(
        _dispatch_kernel,
        name="dispatch",
        grid_spec=grid_spec,
        out_shape=jax.ShapeDtypeStruct((n_sorted, D_MODEL), BF16),
        compiler_params=_cparams(1),
    )(plan["src_row"], plan["n_used"], h2_all)


def _moe_kernel(te_ref, nu_ref, seg_ref, nxt_ref, hs_ref, w1_hbm, w3_hbm, w2_hbm, y_ref,
                wf1, wf3, wf2, wsem, w1b, w3b, w2b):
    i = pl.program_id(0)
    n_used = nu_ref[0]

    def weight_copies(expert, wslot):
        return (pltpu.make_async_copy(w1_hbm.at[expert], wf1.at[wslot], wsem.at[wslot]),
                pltpu.make_async_copy(w3_hbm.at[expert], wf3.at[wslot], wsem.at[wslot]),
                pltpu.make_async_copy(w2_hbm.at[expert], wf2.at[wslot], wsem.at[wslot]))

    @pl.when(i == 0)
    def _():
        for c in weight_copies(te_ref[0], 0):
            c.start()

    prev = jnp.maximum(i - 1, 0)

    @pl.when((i < n_used) & ((i == 0) | (te_ref[i] != te_ref[prev])))
    def _():
        wslot = seg_ref[i] % 2
        for c in weight_copies(te_ref[i], wslot):
            c.wait()
        w1b[...] = wf1[wslot].astype(BF16)
        w3b[...] = wf3[wslot].astype(BF16)
        w2b[...] = wf2[wslot].astype(BF16)

        @pl.when(nxt_ref[i] >= 0)
        def _():
            for c in weight_copies(nxt_ref[i], 1 - wslot):
                c.start()

    @pl.when(i < n_used)
    def _():
        x = hs_ref[...]
        a = jnp.dot(x, w1b[...], preferred_element_type=F32)
        b = jnp.dot(x, w3b[...], preferred_element_type=F32)
        hid = _silu(a) * b
        y_ref[...] = jnp.dot(hid.astype(BF16), w2b[...], preferred_element_type=F32)

    @pl.when(i >= n_used)
    def _():
        y_ref[...] = jnp.zeros_like(y_ref)


def _moe(plan, h_sorted, w1, w3, w2):
    n_sorted = h_sorted.shape[0]
    n_tiles = n_sorted // TME
    used = lambda i, te, nu, seg, nxt: (jnp.minimum(i, nu[0] - 1), 0)
    grid_spec = pltpu.PrefetchScalarGridSpec(
        num_scalar_prefetch=4,
        grid=(n_tiles,),
        in_specs=[pl.BlockSpec((TME, D_MODEL), used)] + [pl.BlockSpec(memory_space=pl.ANY)] * 3,
        out_specs=pl.BlockSpec((TME, D_MODEL), lambda i, *_: (i, 0)),
        scratch_shapes=[pltpu.VMEM((2, D_MODEL, EXPERT_FF), F32), pltpu.VMEM((2, D_MODEL, EXPERT_FF), F32),
                        pltpu.VMEM((2, EXPERT_FF, D_MODEL), F32),
                        pltpu.SemaphoreType.DMA((2,)),
                        pltpu.VMEM((D_MODEL, EXPERT_FF), BF16), pltpu.VMEM((D_MODEL, EXPERT_FF), BF16),
                        pltpu.VMEM((EXPERT_FF, D_MODEL), BF16)],
    )
    return pl.pallas_call(
        _moe_kernel,
        name="moe",
        grid_spec=grid_spec,
        out_shape=jax.ShapeDtypeStruct((n_sorted, D_MODEL), F32),
        compiler_params=_cparams(1),
    )(plan["tile_expert"], plan["n_used"], plan["segment"], plan["next_expert"], h_sorted, w1, w3, w2)


def _finish_kernel(grp, pos_ref, x1_ref, mod_ref, rt_ref, nf_ref, y_hbm, out_ref, buf_a, buf_b, sem):
    i = pl.program_id(0)
    n_steps = pl.num_programs(0)
    n_slots = FINISH_AHEAD + 1
    slot = i % n_slots

    def issue(tile, slot_):
        base = (tile * TM + grp.row_off) * 2

        def body(r, carry):
            _row_copy(y_hbm, pos_ref[base + 2 * r], buf_a.at[slot_], r, sem.at[slot_]).start(priority=0)
            _row_copy(y_hbm, pos_ref[base + 2 * r + 1], buf_b.at[slot_], r, sem.at[slot_]).start(priority=1)
            return carry

        lax.fori_loop(0, TM, body, 0, unroll=8)

    @pl.when(i == 0)
    def _():
        for t in range(FINISH_AHEAD):
            @pl.when(t < n_steps)
            def _():
                issue(t, t)

    @pl.when(i + FINISH_AHEAD < n_steps)
    def _():
        issue(i + FINISH_AHEAD, (i + FINISH_AHEAD) % n_slots)

    _tile_wait(y_hbm, buf_a.at[slot], sem.at[slot])
    _tile_wait(y_hbm, buf_b.at[slot], sem.at[slot])
    rt = rt_ref[...]
    moe = rt[:, 2:3] * buf_a[slot] + rt[:, 3:4] * buf_b[slot]
    x2 = x1_ref[...] + _mod_rows(mod_ref, 5) * moe.reshape(grp.nb, grp.tt, D_MODEL)
    ms = jnp.mean(x2 * x2, axis=-1, keepdims=True)
    out_ref[...] = x2 * lax.rsqrt(ms + EPS) * nf_ref[...]


def _finish(grp, pos, x1, mod, route_all, norm_f, y_sorted):
    off = grp.tile_off
    grid_spec = pltpu.PrefetchScalarGridSpec(
        num_scalar_prefetch=1,
        grid=(grp.n_tiles,),
        in_specs=[pl.BlockSpec((grp.nb, grp.tt, D_MODEL), grp.x_map()),
                  pl.BlockSpec((grp.nb, 6, D_MODEL), grp.mod_map()),
                  pl.BlockSpec((TM, LANES), lambda i, p: (i + off, 0)),
                  pl.BlockSpec((1, 1, D_MODEL), lambda i, p: (0, 0, 0)),
                  pl.BlockSpec(memory_space=pl.ANY)],
        out_specs=pl.BlockSpec((grp.nb, grp.tt, D_MODEL), grp.x_map()),
        scratch_shapes=[pltpu.VMEM((FINISH_AHEAD + 1, TM, D_MODEL), F32),
                        pltpu.VMEM((FINISH_AHEAD + 1, TM, D_MODEL), F32),
                        pltpu.SemaphoreType.DMA((FINISH_AHEAD + 1,))],
    )
    return pl.pallas_call(
        functools.partial(_finish_kernel, grp),
        name="finish",
        grid_spec=grid_spec,
        out_shape=jax.ShapeDtypeStruct(x1.shape, F32),
        compiler_params=_cparams(1),
    )(pos, x1, mod, route_all, norm_f.reshape(1, 1, D_MODEL), y_sorted)


def _sort_plan(route_all):
    n_tok = route_all.shape[0]
    n_pairs = 2 * n_tok
    n_sorted = n_pairs + N_EXPERTS * TME
    flat_e = route_all[:, 0:2].astype(jnp.int32).reshape(n_pairs)
    onehot = (flat_e[:, None] == jnp.arange(N_EXPERTS, dtype=jnp.int32)[None, :]).astype(jnp.int32)
    csum = jnp.cumsum(onehot, axis=0)
    rank = jnp.sum(onehot * csum, axis=1) - 1
    counts = csum[-1]
    padded = ((counts + TME - 1) // TME) * TME
    ends = jnp.cumsum(padded)
    starts = ends - padded
    pos = starts[flat_e] + rank
    token = jnp.arange(n_pairs, dtype=jnp.int32) // 2
    src_row = (jnp.arange(n_sorted, dtype=jnp.int32) % n_tok).at[pos].set(token)
    tile_start = jnp.arange(n_sorted // TME, dtype=jnp.int32) * TME
    tile_expert = jnp.sum((tile_start[:, None] >= ends[None, :]).astype(jnp.int32), axis=1)
    tile_expert = jnp.minimum(tile_expert, N_EXPERTS - 1)
    n_used = ends[-1] // TME
    is_first = jnp.concatenate([jnp.ones((1,), jnp.int32),
                                (tile_expert[1:] != tile_expert[:-1]).astype(jnp.int32)])
    segment = jnp.cumsum(is_first) - 1
    next_tile = ends[tile_expert] // TME
    next_expert = jnp.where(next_tile < n_used, tile_expert[jnp.minimum(next_tile, n_sorted // TME - 1)], -1)
    return dict(pos=pos.astype(jnp.int32), src_row=src_row, tile_expert=tile_expert.astype(jnp.int32),
                n_used=n_used.astype(jnp.int32).reshape(1), segment=segment.astype(jnp.int32),
                next_expert=next_expert.astype(jnp.int32))


def kernel(x_prompt, x_sample, c_prompt, c_sample, state_gla, state_pool, w_ada, b_ada, norm1, norm2, w_in,
           gate_up, gate_bias, gla_norm, pool_w, pool_scale, w_out, w_group, b_group, w_expert, b_expert,
           w1, w3, w2, norm_f):
    assert w_ada.shape[0] == 1, "single-layer step"
    bp, tp, _ = x_prompt.shape
    bs, ts, _ = x_sample.shape
    grp_p = _Group(bp, tp, 0, bs)
    grp_s = _Group(bs, ts, bp * tp, 0)
    n_tok = bp * tp + bs * ts

    n_c = bp + bs
    n_c_pad = -(-n_c // 8) * 8
    c_all = jnp.concatenate([c_sample, c_prompt, jnp.zeros((n_c_pad - n_c, D_MODEL), F32)], axis=0)
    mod = _adaln(c_all, w_ada[0], b_ada[0]).reshape(n_c_pad, 6, D_MODEL)
    mod_p = mod_s = mod

    wi = w_in[0]
    wq = wi[:, :QKVR_WIDTH].astype(BF16)
    wg = jnp.pad(wi[:, QKVR_WIDTH:QKVR_WIDTH + GATE_RANK], ((0, 0), (0, LANES - GATE_RANK))).astype(BF16)
    wu = wi[:, QKVR_WIDTH + GATE_RANK:].astype(BF16)
    gup = jnp.pad(gate_up[0], ((0, LANES - GATE_RANK), (0, 0))).astype(BF16)
    gb = gate_bias[0].reshape(1, GLA_KEY_WIDTH)
    pw = pool_w[0].astype(BF16)
    wo = w_out[0].astype(BF16)
    w_router = jnp.concatenate(
        [w_group[0], jnp.transpose(w_expert[0], (1, 0, 2)).reshape(D_MODEL, N_EXPERTS),
         jnp.zeros((D_MODEL, LANES - N_GROUPS - N_EXPERTS), F32)], axis=1)
    b_router = jnp.concatenate([b_group[0], b_expert[0].reshape(N_EXPERTS),
                                jnp.zeros((LANES - N_GROUPS - N_EXPERTS,), F32)]).reshape(1, LANES)

    qkvr_p, la_p, u_p = _in_proj(grp_p, x_prompt, mod_p, norm1[0], wq, wg, wu, gup, gb)
    qkvr_s, la_s, u_s = _in_proj(grp_s, x_sample, mod_s, norm1[0], wq, wg, wu, gup, gb)

    o_p, gla_p = _gla_prompt(bp, tp, qkvr_p, la_p)
    o_s, gla_s = _gla_decode(bs, ts, qkvr_s, la_s, state_gla[0])

    halo_per_tile = TM // HALO
    halo_map_p = lambda i: (jnp.maximum(i * halo_per_tile - 1, 0), 0)
    x1_p, h2_all, route_all = _mix_out(grp_p, n_tok, 0, True, o_p, qkvr_p, u_p, u_p, (HALO, POOL_WIDTH),
                                       halo_map_p, x_prompt, mod_p, norm2[0], gla_norm[0], pw, pool_scale[0],
                                       wo, w_router, b_router)
    x1_s, h2_all, route_all = _mix_out(grp_s, n_tok, PAST_LEN, False, o_s, qkvr_s, u_s, state_pool[0],
                                       (grp_s.nb, POOL_BUF, POOL_WIDTH), lambda i: (i, 0, 0), x_sample, mod_s,
                                       norm2[0], gla_norm[0], pw, pool_scale[0], wo, w_router, b_router,
                                       shared=(h2_all, route_all))

    plan = _sort_plan(route_all)
    y_sorted = _moe(plan, _dispatch(plan, h2_all), w1[0], w3[0], w2[0])

    y_p = _finish(grp_p, plan["pos"], x1_p, mod_p, route_all, norm_f, y_sorted)
    y_s = _finish(grp_s, plan["pos"], x1_s, mod_s, route_all, norm_f, y_sorted)

    u_p3 = u_p.reshape(bp, tp, POOL_WIDTH)
    u_s3 = u_s.reshape(bs, ts, POOL_WIDTH)
    pool_p = jnp.concatenate([jnp.zeros((bp, POOL_BUF, POOL_WIDTH), F32), u_p3], axis=1)[:, -POOL_BUF:]
    pool_s = jnp.concatenate([state_pool[0], u_s3], axis=1)[:, -POOL_BUF:]
    return (y_p, y_s, gla_p[None], pool_p[None], gla_s[None], pool_s[None])
```

```python
import functools

import jax
import jax.numpy as jnp
from jax import lax
from jax.experimental import pallas as pl
from jax.experimental.pallas import tpu as pltpu

D_MODEL = 2048
GLA_HEADS = 4
GLA_DK = 128
GLA_DV = 256
GLA_KEY_WIDTH = GLA_HEADS * GLA_DK
GLA_WIDTH = GLA_HEADS * GLA_DV
POOL_WIDTH = 1024
POOL_WINDOWS = (2, 4, 8, 16)
POOL_GW = 256
POOL_BUF = 15
HALO = 16
GATE_RANK = 16
GATE_TEMP = 16.0
N_GROUPS = 4
EXPERTS_PER_GROUP = 8
N_EXPERTS = 32
EXPERT_FF = 512
EPS = 1e-6
PAST_LEN = 16384
QKVR_WIDTH = 2 * GLA_KEY_WIDTH + 2 * GLA_WIDTH

LANES = 128
TM = 256
TME = 256
TD = 1024
FINISH_AHEAD = 3
GLA_CHUNK = 64
GLA_SUB = 16
GLA_STEP = 512
VMEM_LIMIT = 56 * 1024 * 1024

BF16 = jnp.bfloat16
F32 = jnp.float32
NEG = -1e30


def _cparams(n_axes):
    return pltpu.CompilerParams(dimension_semantics=("arbitrary",) * n_axes,
                                vmem_limit_bytes=VMEM_LIMIT)


def _silu(x):
    return x / (1.0 + jnp.exp(-x))


def _bdot(a, b):
    return jnp.dot(a.astype(BF16), b.astype(BF16), preferred_element_type=F32)


def _split3(a):
    a1 = a.astype(BF16)
    r1 = a - a1.astype(F32)
    a2 = r1.astype(BF16)
    a3 = (r1 - a2.astype(F32)).astype(BF16)
    return a1, a2, a3


def _dot_f32(a, b):
    a1, a2, a3 = _split3(a)
    b1, b2, b3 = _split3(b)
    d = lambda x, y: jnp.dot(x, y, preferred_element_type=F32)
    small = d(a2, b2) + d(a1, b3) + d(a3, b1)
    mid = d(a1, b2) + d(a2, b1)
    return d(a1, b1) + (mid + small)


def _dot_exact_lhs(tri_bf16, g):
    g1, g2, g3 = _split3(g)
    d = lambda y: jnp.dot(tri_bf16, y, preferred_element_type=F32)
    return d(g1) + (d(g2) + d(g3))


def _adaln_kernel(c_ref, w_ref, b_ref, o_ref):
    c = c_ref[...]
    o_ref[...] = _bdot(_silu(c), w_ref[...]) + b_ref[...]


def _adaln(c_all, w_ada, b_ada):
    n, d = c_all.shape
    width = w_ada.shape[1]
    tn = 1024
    return pl.pallas_call(
        _adaln_kernel,
        name="adaln",
        grid=(width // tn,),
        in_specs=[pl.BlockSpec((n, d), lambda j: (0, 0)),
                  pl.BlockSpec((d, tn), lambda j: (0, j)),
                  pl.BlockSpec((1, tn), lambda j: (0, j))],
        out_specs=pl.BlockSpec((n, tn), lambda j: (0, j)),
        out_shape=jax.ShapeDtypeStruct((n, width), F32),
        compiler_params=_cparams(1),
    )(c_all, w_ada, b_ada.reshape(1, width))


class _Group:
    def __init__(self, batch, seq, row_off, mod_off):
        self.batch, self.seq, self.row_off, self.mod_off = batch, seq, row_off, mod_off
        if seq >= TM:
            assert seq % TM == 0
            self.nb, self.tt = 1, TM
            self.tiles_per_batch = seq // TM
            self.n_tiles = batch * self.tiles_per_batch
        else:
            assert TM % seq == 0 and batch % (TM // seq) == 0
            self.nb, self.tt = TM // seq, seq
            self.tiles_per_batch = 1
            self.n_tiles = batch // self.nb
        self.rows = batch * seq
        self.tile_off = row_off // TM

    def x_map(self):
        if self.nb == 1:
            tpb = self.tiles_per_batch
            return lambda i, *_: (i // tpb, i % tpb, 0)
        return lambda i, *_: (i, 0, 0)

    def mod_map(self):
        assert self.mod_off % self.nb == 0
        off = self.mod_off // self.nb
        if self.nb == 1:
            tpb = self.tiles_per_batch
            return lambda i, *_: (i // tpb + off, 0, 0)
        return lambda i, *_: (i + off, 0, 0)


def _mod_rows(mod_ref, idx):
    return mod_ref[:, idx:idx + 1, :]


def _rmsnorm_mod(x, gain, scale, shift):
    ms = jnp.mean(x * x, axis=-1, keepdims=True)
    y = x * lax.rsqrt(ms + EPS) * gain
    return y * (1.0 + scale) + shift


def _in_proj_kernel(x_ref, mod_ref, n1_ref, wq_ref, wg_ref, wu_ref, gup_ref, gb_ref,
                    qkvr_ref, la_ref, u_ref):
    x = x_ref[...]
    h = _rmsnorm_mod(x, n1_ref[...], _mod_rows(mod_ref, 1), _mod_rows(mod_ref, 0))
    hb = h.reshape(TM, D_MODEL).astype(BF16)
    qkvr_ref[...] = jnp.dot(hb, wq_ref[...], preferred_element_type=F32)
    u_ref[...] = jnp.dot(hb, wu_ref[...], preferred_element_type=F32)
    g_lr = jnp.dot(hb, wg_ref[...], preferred_element_type=F32)
    pre = jnp.dot(g_lr.astype(BF16), gup_ref[...], preferred_element_type=F32) + gb_ref[...]
    log_sig = jnp.minimum(pre, 0.0) - jnp.log1p(jnp.exp(-jnp.abs(pre)))
    la_ref[...] = log_sig / GATE_TEMP


def _in_proj(grp, x, mod, norm1, wq, wg, wu, gup, gb):
    const = lambda i: (0, 0)
    row = lambda i: (i, 0)
    return pl.pallas_call(
        _in_proj_kernel,
        name="in_proj",
        grid=(grp.n_tiles,),
        in_specs=[pl.BlockSpec((grp.nb, grp.tt, D_MODEL), grp.x_map()),
                  pl.BlockSpec((grp.nb, 6, D_MODEL), grp.mod_map()),
                  pl.BlockSpec((1, 1, D_MODEL), lambda i: (0, 0, 0)),
                  pl.BlockSpec(wq.shape, const, pipeline_mode=pl.Buffered(1)),
                  pl.BlockSpec(wg.shape, const, pipeline_mode=pl.Buffered(1)),
                  pl.BlockSpec(wu.shape, const, pipeline_mode=pl.Buffered(1)),
                  pl.BlockSpec(gup.shape, const, pipeline_mode=pl.Buffered(1)),
                  pl.BlockSpec(gb.shape, const, pipeline_mode=pl.Buffered(1))],
        out_specs=[pl.BlockSpec((TM, QKVR_WIDTH), row),
                   pl.BlockSpec((TM, GLA_KEY_WIDTH), row),
                   pl.BlockSpec((TM, POOL_WIDTH), row)],
        out_shape=[jax.ShapeDtypeStruct((grp.rows, QKVR_WIDTH), F32),
                   jax.ShapeDtypeStruct((grp.rows, GLA_KEY_WIDTH), F32),
                   jax.ShapeDtypeStruct((grp.rows, POOL_WIDTH), F32)],
        compiler_params=_cparams(1),
    )(x, mod, norm1.reshape(1, 1, D_MODEL), wq, wg, wu, gup, gb)


def _gla_select_matrix(chunk, sub):
    r = jnp.arange(sub * GLA_DK, dtype=jnp.int32)[:, None] // GLA_DK
    l = jnp.arange(LANES, dtype=jnp.int32)[None, :]
    return ((l % sub == r) & (l < chunk)).astype(BF16)


def _gla_chunk(q4, k4, v4, g4, states, chunk, sub, wsel, t_ref):
    n_sub = chunk // sub
    rows = lax.broadcasted_iota(jnp.int32, (chunk, chunk), 0)
    cols = lax.broadcasted_iota(jnp.int32, (chunk, chunk), 1)
    tri = (rows >= cols).astype(BF16)
    b4 = _dot_exact_lhs(tri, g4)
    q4 = q4 * (GLA_DK ** -0.5)
    nt = (((1,), (1,)), ((), ()))
    tn = (((0,), (0,)), ((), ()))
    key_row = lax.broadcasted_iota(jnp.int32, (chunk, LANES), 0)

    for h in range(GLA_HEADS):
        hs = slice(h * GLA_DK, (h + 1) * GLA_DK)
        q, k, b = q4[:, hs], k4[:, hs], b4[:, hs]
        for s in range(n_sub):
            lo = s * sub
            r0 = (h * n_sub + s) * sub
            q_s, b_s = q[lo:lo + sub, :], b[lo:lo + sub, :]
            for jl in range(sub):
                j = lo + jl
                decay = jnp.exp(jnp.minimum(b_s - b[j:j + 1, :], 0.0))
                t_ref[r0:r0 + sub, jl * GLA_DK:(jl + 1) * GLA_DK] = (q_s * k[j:j + 1, :] * decay).astype(t_ref.dtype)
    p_diag = jnp.dot(t_ref[...].astype(BF16), wsel, preferred_element_type=F32)

    lane = lax.broadcasted_iota(jnp.int32, (sub, chunk), 1)
    row = lax.broadcasted_iota(jnp.int32, (sub, chunk), 0)
    outs, new_states = [], []
    for h in range(GLA_HEADS):
        hs = slice(h * GLA_DK, (h + 1) * GLA_DK)
        vs = slice(h * GLA_DV, (h + 1) * GLA_DV)
        q, k, b, v, state = q4[:, hs], k4[:, hs], b4[:, hs], v4[:, vs], states[h]
        o = _bdot(q * jnp.exp(b), state)

        b_last = b[chunk - 1:chunk, :]
        k_dec = k * jnp.exp(b_last - b)
        decay_col = jnp.exp(b[chunk - 8:chunk, :]).T[:, 7:8]
        new_states.append(decay_col * state + lax.dot_general(
            k_dec.astype(BF16), v.astype(BF16), tn, preferred_element_type=F32))

        p_blocks = []
        for s in range(n_sub):
            lo = s * sub
            r0 = (h * n_sub + s) * sub
            in_block = (lane >= lo) & (lane - lo <= row)
            p = jnp.where(in_block, p_diag[r0:r0 + sub, :chunk], 0.0)
            if s > 0:
                ref_row = b[lo - 1:lo, :]
                q_rel = q[lo:lo + sub, :] * jnp.exp(b[lo:lo + sub, :] - ref_row)
                k_rel = k * jnp.exp(jnp.where(key_row < lo, ref_row - b, NEG))
                p = p + lax.dot_general(q_rel.astype(BF16), k_rel.astype(BF16), nt, preferred_element_type=F32)
            p_blocks.append(p)
        p_full = p_blocks[0] if n_sub == 1 else jnp.concatenate(p_blocks, axis=0)
        outs.append(o + _bdot(p_full, v))
    return jnp.concatenate(outs, axis=1), new_states


def _gla_prompt_kernel(q_ref, k_ref, v_ref, la_ref, wsel_ref, o_ref, s_ref, t_ref):
    @pl.when(pl.program_id(1) == 0)
    def _():
        s_ref[...] = jnp.zeros_like(s_ref)

    def body(c, carry):
        r0 = pl.multiple_of(c * GLA_CHUNK, GLA_CHUNK)
        sl = pl.ds(r0, GLA_CHUNK)
        states = [s_ref[0, h] for h in range(GLA_HEADS)]
        o, new_states = _gla_chunk(q_ref[sl, :], k_ref[sl, :], v_ref[sl, :], la_ref[sl, :], states,
                                   GLA_CHUNK, GLA_SUB, wsel_ref[...], t_ref)
        o_ref[sl, :] = o
        for h in range(GLA_HEADS):
            s_ref[0, h] = new_states[h]
        return carry

    lax.fori_loop(0, GLA_STEP // GLA_CHUNK, body, 0)


def _gla_prompt(batch, seq, qkvr, log_a):
    steps = seq // GLA_STEP
    row = lambda b, s: b * steps + s
    wsel = _gla_select_matrix(GLA_CHUNK, GLA_SUB)
    return pl.pallas_call(
        _gla_prompt_kernel,
        name="gla_prompt",
        grid=(batch, steps),
        in_specs=[pl.BlockSpec((GLA_STEP, GLA_KEY_WIDTH), lambda b, s: (row(b, s), 0)),
                  pl.BlockSpec((GLA_STEP, GLA_KEY_WIDTH), lambda b, s: (row(b, s), 1)),
                  pl.BlockSpec((GLA_STEP, GLA_WIDTH), lambda b, s: (row(b, s), 1)),
                  pl.BlockSpec((GLA_STEP, GLA_KEY_WIDTH), lambda b, s: (row(b, s), 0)),
                  pl.BlockSpec(wsel.shape, lambda b, s: (0, 0))],
        out_specs=[pl.BlockSpec((GLA_STEP, GLA_WIDTH), lambda b, s: (row(b, s), 0)),
                   pl.BlockSpec((1, GLA_HEADS, GLA_DK, GLA_DV), lambda b, s: (b, 0, 0, 0))],
        out_shape=[jax.ShapeDtypeStruct((batch * seq, GLA_WIDTH), F32),
                   jax.ShapeDtypeStruct((batch, GLA_HEADS, GLA_DK, GLA_DV), F32)],
        scratch_shapes=[pltpu.VMEM((GLA_HEADS * GLA_CHUNK, GLA_SUB * GLA_DK), BF16)],
        compiler_params=_cparams(2),
    )(qkvr, qkvr, qkvr, log_a, wsel)


GLA_DEC_BB = 8


def _gla_decode_kernel(seq, q_ref, k_ref, v_ref, la_ref, wsel_ref, s0_ref, o_ref, s_ref, t_ref):
    def body(i, carry):
        r0 = pl.multiple_of(i * seq, seq)
        sl = pl.ds(r0, seq)
        states = [s0_ref[i, h] for h in range(GLA_HEADS)]
        o, new_states = _gla_chunk(q_ref[sl, :], k_ref[sl, :], v_ref[sl, :], la_ref[sl, :], states,
                                   seq, seq, wsel_ref[...], t_ref)
        o_ref[sl, :] = o
        for h in range(GLA_HEADS):
            s_ref[i, h] = new_states[h]
        return carry

    lax.fori_loop(0, GLA_DEC_BB, body, 0)


def _gla_decode(batch, seq, qkvr, log_a, state):
    rows = GLA_DEC_BB * seq
    wsel = _gla_select_matrix(seq, seq)
    state_spec = pl.BlockSpec((GLA_DEC_BB, GLA_HEADS, GLA_DK, GLA_DV), lambda i: (i, 0, 0, 0))
    return pl.pallas_call(
        functools.partial(_gla_decode_kernel, seq),
        name="gla_decode",
        grid=(batch // GLA_DEC_BB,),
        in_specs=[pl.BlockSpec((rows, GLA_KEY_WIDTH), lambda i: (i, 0)),
                  pl.BlockSpec((rows, GLA_KEY_WIDTH), lambda i: (i, 1)),
                  pl.BlockSpec((rows, GLA_WIDTH), lambda i: (i, 1)),
                  pl.BlockSpec((rows, GLA_KEY_WIDTH), lambda i: (i, 0)),
                  pl.BlockSpec(wsel.shape, lambda i: (0, 0)),
                  state_spec],
        out_specs=[pl.BlockSpec((rows, GLA_WIDTH), lambda i: (i, 0)), state_spec],
        out_shape=[jax.ShapeDtypeStruct((batch * seq, GLA_WIDTH), F32),
                   jax.ShapeDtypeStruct((batch, GLA_HEADS, GLA_DK, GLA_DV), F32)],
        scratch_shapes=[pltpu.VMEM((GLA_HEADS * seq, seq * GLA_DK), F32)],
        compiler_params=_cparams(1),
    )(qkvr, qkvr, qkvr, log_a, wsel, state)


def _route(logits):
    lane = lax.broadcasted_iota(jnp.int32, logits.shape, 1)
    big = jnp.int32(10 ** 6)
    is_group = lane < N_GROUPS
    lg = jnp.where(is_group, logits, NEG)
    mg = jnp.max(lg, axis=-1, keepdims=True)
    g_idx = jnp.min(jnp.where(is_group & (lg == mg), lane, big), axis=-1, keepdims=True)
    denom = jnp.sum(jnp.where(is_group, jnp.exp(lg - mg), 0.0), axis=-1, keepdims=True)
    p_sel = 1.0 / denom
    first = N_GROUPS + EXPERTS_PER_GROUP * g_idx
    in_grp = (lane >= first) & (lane < first + EXPERTS_PER_GROUP)
    le = jnp.where(in_grp, logits, NEG)
    m1 = jnp.max(le, axis=-1, keepdims=True)
    i1 = jnp.min(jnp.where(in_grp & (le == m1), lane, big), axis=-1, keepdims=True)
    rest = in_grp & (lane != i1)
    le2 = jnp.where(rest, logits, NEG)
    m2 = jnp.max(le2, axis=-1, keepdims=True)
    i2 = jnp.min(jnp.where(rest & (le2 == m2), lane, big), axis=-1, keepdims=True)
    e2 = jnp.exp(m2 - m1)
    w1 = p_sel / (1.0 + e2)
    w2 = p_sel * e2 / (1.0 + e2)
    ex1 = (i1 - N_GROUPS).astype(F32)
    ex2 = (i2 - N_GROUPS).astype(F32)
    return jnp.where(lane == 0, ex1, jnp.where(lane == 1, ex2,
                     jnp.where(lane == 2, w1, jnp.where(lane == 3, w2, 0.0))))


def _mix_out_kernel(grp, pos0, zero_first_halo, n_alias,
                    o_ref, r_ref, u_ref, halo_ref, x_ref, mod_ref, n2_ref, gn_ref, pw_ref, ps_ref,
                    wo_ref, wr_ref, br_ref, *rest):
    x1_ref, h2_ref, rt_ref, ext_ref, ymix_ref = rest[n_alias:]
    i = pl.program_id(0)

    @pl.when(i < grp.n_tiles)
    def _():
        _mix_out_tile(grp, pos0, zero_first_halo, i, o_ref, r_ref, u_ref, halo_ref, x_ref, mod_ref, n2_ref,
                      gn_ref, pw_ref, ps_ref, wo_ref, wr_ref, br_ref, x1_ref, h2_ref, rt_ref, ext_ref, ymix_ref)

    @pl.when(i >= grp.n_tiles)
    def _():
        h2_ref[...] = jnp.zeros_like(h2_ref)
        rt_ref[...] = jnp.zeros_like(rt_ref)


def _mix_out_tile(grp, pos0, zero_first_halo, i, o_ref, r_ref, u_ref, halo_ref, x_ref, mod_ref, n2_ref,
                  gn_ref, pw_ref, ps_ref, wo_ref, wr_ref, br_ref, x1_ref, h2_ref, rt_ref, ext_ref, ymix_ref):
    nb, tt = grp.nb, grp.tt
    hist = halo_ref.shape[-2]

    for h in range(GLA_HEADS):
        cs = slice(h * GLA_DV, (h + 1) * GLA_DV)
        oh = o_ref[:, cs]
        ms = jnp.mean(oh * oh, axis=-1, keepdims=True)
        yh = oh * lax.rsqrt(ms + EPS) * gn_ref[:, cs] * _silu(r_ref[:, cs])
        ymix_ref[:, cs] = yh.astype(BF16)

    halo = halo_ref[...]
    if zero_first_halo:
        halo = jnp.where(i % grp.tiles_per_batch == 0, 0.0, halo)
    ext_ref[:, HALO - hist:HALO, :] = halo.reshape(nb, hist, POOL_WIDTH)
    u = u_ref[...].reshape(nb, tt, POOL_WIDTH)
    ext_ref[:, HALO:HALO + tt, :] = u
    t_idx = lax.broadcasted_iota(jnp.int32, (nb, tt, POOL_GW), 1)
    if grp.nb == 1:
        pos = (i % grp.tiles_per_batch) * TM + t_idx + pos0
    else:
        pos = t_idx + pos0
    for gi, w in enumerate(POOL_WINDOWS):
        cs = slice(gi * POOL_GW, (gi + 1) * POOL_GW)
        acc = ext_ref[:, pl.ds(HALO, tt), cs]
        for kk in range(1, w):
            acc = acc + ext_ref[:, pl.ds(HALO - kk, tt), cs]
        cnt = jnp.minimum(pos + 1, w).astype(F32)
        pooled = acc / cnt - u[:, :, cs]
        yp = _bdot(pooled.reshape(TM, POOL_GW), pw_ref[gi]) * ps_ref[:, cs]
        ymix_ref[:, GLA_WIDTH + gi * POOL_GW:GLA_WIDTH + (gi + 1) * POOL_GW] = yp.astype(BF16)

    y = jnp.dot(ymix_ref[...], wo_ref[...], preferred_element_type=F32)
    x1 = x_ref[...] + _mod_rows(mod_ref, 2) * y.reshape(nb, tt, D_MODEL)
    x1_ref[...] = x1
    h2 = _rmsnorm_mod(x1, n2_ref[...], _mod_rows(mod_ref, 4), _mod_rows(mod_ref, 3)).reshape(TM, D_MODEL)
    h2_ref[...] = h2
    logits = _dot_f32(h2, wr_ref[...]) + br_ref[...]
    rt_ref[...] = _route(logits)


def _mix_out(grp, n_tok, pos0, zero_first_halo, o, qkvr, u, halo_src, halo_block, halo_map, x, mod, norm2,
             gla_norm, pool_w, pool_scale, w_out, w_router, b_router, shared=()):
    n_alias = len(shared)
    n = grp.n_tiles
    n_fill = 0 if shared else n_tok // TM - n
    assert n_fill == 0 or grp.tile_off == 0
    clamp = lambda f: (lambda i: f(jnp.minimum(i, n - 1)))
    const2 = lambda i: (0, 0)
    row = clamp(lambda i: (i, 0))
    off = grp.tile_off
    kern = functools.partial(_mix_out_kernel, grp, pos0, zero_first_halo, n_alias)
    return pl.pallas_call(
        kern,
        name="mix_out",
        grid=(n + n_fill,),
        in_specs=[pl.BlockSpec((TM, GLA_WIDTH), row),
                  pl.BlockSpec((TM, GLA_WIDTH), clamp(lambda i: (i, 2))),
                  pl.BlockSpec((TM, POOL_WIDTH), row),
                  pl.BlockSpec(halo_block, clamp(halo_map)),
                  pl.BlockSpec((grp.nb, grp.tt, D_MODEL), clamp(grp.x_map())),
                  pl.BlockSpec((grp.nb, 6, D_MODEL), clamp(grp.mod_map())),
                  pl.BlockSpec((1, 1, D_MODEL), lambda i: (0, 0, 0)),
                  pl.BlockSpec((1, GLA_WIDTH), const2),
                  pl.BlockSpec(pool_w.shape, lambda i: (0, 0, 0), pipeline_mode=pl.Buffered(1)),
                  pl.BlockSpec((1, POOL_WIDTH), const2),
                  pl.BlockSpec(w_out.shape, const2, pipeline_mode=pl.Buffered(1)),
                  pl.BlockSpec(w_router.shape, const2, pipeline_mode=pl.Buffered(1)),
                  pl.BlockSpec((1, LANES), const2)]
                 + [pl.BlockSpec(memory_space=pl.ANY)] * n_alias,
        out_specs=[pl.BlockSpec((grp.nb, grp.tt, D_MODEL), clamp(grp.x_map())),
                   pl.BlockSpec((TM, D_MODEL), lambda i: (i + off, 0)),
                   pl.BlockSpec((TM, LANES), lambda i: (i + off, 0))],
        out_shape=[jax.ShapeDtypeStruct(x.shape, F32),
                   jax.ShapeDtypeStruct((n_tok, D_MODEL), F32),
                   jax.ShapeDtypeStruct((n_tok, LANES), F32)],
        scratch_shapes=[pltpu.VMEM((grp.nb, HALO + grp.tt, POOL_WIDTH), F32),
                        pltpu.VMEM((TM, D_MODEL), BF16)],
        input_output_aliases={13 + k: 1 + k for k in range(n_alias)},
        compiler_params=_cparams(1),
    )(o, qkvr, u, halo_src, x, mod, norm2.reshape(1, 1, D_MODEL), gla_norm.reshape(1, GLA_WIDTH),
      pool_w, pool_scale.reshape(1, POOL_WIDTH), w_out, w_router, b_router, *shared)


def _row_copy(src_hbm, src_row, dst, dst_row, sem):
    return pltpu.make_async_copy(src_hbm.at[pl.ds(src_row, 1), :], dst.at[pl.ds(dst_row, 1), :], sem)


def _tile_wait(src_hbm, dst, sem):
    pltpu.make_async_copy(src_hbm.at[pl.ds(0, dst.shape[0]), :], dst, sem).wait()


def _dispatch_kernel(src_ref, nu_ref, h_hbm, hs_ref, buf, sem):
    i = pl.program_id(0)
    n_used = (nu_ref[0] * TME + TD - 1) // TD
    slot = i % 2

    def issue(tile, slot_):
        base = tile * TD

        def body(r, carry):
            _row_copy(h_hbm, src_ref[base + 2 * r], buf.at[slot_], 2 * r, sem.at[slot_]).start(priority=0)
            _row_copy(h_hbm, src_ref[base + 2 * r + 1], buf.at[slot_], 2 * r + 1, sem.at[slot_]).start(priority=1)
            return carry

        lax.fori_loop(0, TD // 2, body, 0, unroll=4)

    @pl.when(i == 0)
    def _():
        issue(0, 0)

    @pl.when(i + 1 < n_used)
    def _():
        issue(i + 1, 1 - slot)

    @pl.when(i < n_used)
    def _():
        _tile_wait(h_hbm, buf.at[slot], sem.at[slot])
        hs_ref[...] = buf[slot].astype(BF16)

    @pl.when(i >= n_used)
    def _():
        hs_ref[...] = jnp.zeros_like(hs_ref)


def _dispatch(plan, h2_all):
    n_sorted = plan["src_row"].shape[0]
    assert n_sorted % TD == 0
    grid_spec = pltpu.PrefetchScalarGridSpec(
        num_scalar_prefetch=2,
        grid=(n_sorted // TD,),
        in_specs=[pl.BlockSpec(memory_space=pl.ANY)],
        out_specs=pl.BlockSpec((TD, D_MODEL), lambda i, *_: (i, 0)),
        scratch_shapes=[pltpu.VMEM((2, TD, D_MODEL), F32), pltpu.SemaphoreType.DMA((2,))],
    )
    return pl.pallas_call(
        _dispatch_kernel,
        name="dispatch",
        grid_spec=grid_spec,
        out_shape=jax.ShapeDtypeStruct((n_sorted, D_MODEL), BF16),
        compiler_params=_cparams(1),
    )(plan["src_row"], plan["n_used"], h2_all)


def _moe_kernel(te_ref, nu_ref, seg_ref, nxt_ref, hs_ref, w1_hbm, w3_hbm, w2_hbm, y_ref,
                wf1, wf3, wf2, wsem, w1b, w3b, w2b):
    i = pl.program_id(0)
    n_used = nu_ref[0]

    def weight_copies(expert, wslot):
        return (pltpu.make_async_copy(w1_hbm.at[expert], wf1.at[wslot], wsem.at[wslot]),
                pltpu.make_async_copy(w3_hbm.at[expert], wf3.at[wslot], wsem.at[wslot]),
                pltpu.make_async_copy(w2_hbm.at[expert], wf2.at[wslot], wsem.at[wslot]))

    @pl.when(i == 0)
    def _():
        for c in weight_copies(te_ref[0], 0):
            c.start()

    prev = jnp.maximum(i - 1, 0)

    @pl.when((i < n_used) & ((i == 0) | (te_ref[i] != te_ref[prev])))
    def _():
        wslot = seg_ref[i] % 2
        for c in weight_copies(te_ref[i], wslot):
            c.wait()
        w1b[...] = wf1[wslot].astype(BF16)
        w3b[...] = wf3[wslot].astype(BF16)
        w2b[...] = wf2[wslot].astype(BF16)

        @pl.when(nxt_ref[i] >= 0)
        def _():
            for c in weight_copies(nxt_ref[i], 1 - wslot):
                c.start()

    @pl.when(i < n_used)
    def _():
        x = hs_ref[...]
        a = jnp.dot(x, w1b[...], preferred_element_type=F32)
        b = jnp.dot(x, w3b[...], preferred_element_type=F32)
        hid = _silu(a) * b
        y_ref[...] = jnp.dot(hid.astype(BF16), w2b[...], preferred_element_type=F32)

    @pl.when(i >= n_used)
    def _():
        y_ref[...] = jnp.zeros_like(y_ref)


def _moe(plan, h_sorted, w1, w3, w2):
    n_sorted = h_sorted.shape[0]
    n_tiles = n_sorted // TME
    used = lambda i, te, nu, seg, nxt: (jnp.minimum(i, nu[0] - 1), 0)
    grid_spec = pltpu.PrefetchScalarGridSpec(
        num_scalar_prefetch=4,
        grid=(n_tiles,),
        in_specs=[pl.BlockSpec((TME, D_MODEL), used)] + [pl.BlockSpec(memory_space=pl.ANY)] * 3,
        out_specs=pl.BlockSpec((TME, D_MODEL), lambda i, *_: (i, 0)),
        scratch_shapes=[pltpu.VMEM((2, D_MODEL, EXPERT_FF), F32), pltpu.VMEM((2, D_MODEL, EXPERT_FF), F32),
                        pltpu.VMEM((2, EXPERT_FF, D_MODEL), F32),
                        pltpu.SemaphoreType.DMA((2,)),
                        pltpu.VMEM((D_MODEL, EXPERT_FF), BF16), pltpu.VMEM((D_MODEL, EXPERT_FF), BF16),
                        pltpu.VMEM((EXPERT_FF, D_MODEL), BF16)],
    )
    return pl.pallas_call(
        _moe_kernel,
        name="moe",
        grid_spec=grid_spec,
        out_shape=jax.ShapeDtypeStruct((n_sorted, D_MODEL), F32),
        compiler_params=_cparams(1),
    )(plan["tile_expert"], plan["n_used"], plan["segment"], plan["next_expert"], h_sorted, w1, w3, w2)


def _finish_kernel(grp, pos_ref, x1_ref, mod_ref, rt_ref, nf_ref, y_hbm, out_ref, buf_a, buf_b, sem):
    i = pl.program_id(0)
    n_steps = pl.num_programs(0)
    n_slots = FINISH_AHEAD + 1
    slot = i % n_slots

    def issue(tile, slot_):
        base = (tile * TM + grp.row_off) * 2

        def body(r, carry):
            _row_copy(y_hbm, pos_ref[base + 2 * r], buf_a.at[slot_], r, sem.at[slot_]).start(priority=0)
            _row_copy(y_hbm, pos_ref[base + 2 * r + 1], buf_b.at[slot_], r, sem.at[slot_]).start(priority=1)
            return carry

        lax.fori_loop(0, TM, body, 0, unroll=8)

    @pl.when(i == 0)
    def _():
        for t in range(FINISH_AHEAD):
            @pl.when(t < n_steps)
            def _():
                issue(t, t)

    @pl.when(i + FINISH_AHEAD < n_steps)
    def _():
        issue(i + FINISH_AHEAD, (i + FINISH_AHEAD) % n_slots)

    _tile_wait(y_hbm, buf_a.at[slot], sem.at[slot])
    _tile_wait(y_hbm, buf_b.at[slot], sem.at[slot])
    rt = rt_ref[...]
    moe = rt[:, 2:3] * buf_a[slot] + rt[:, 3:4] * buf_b[slot]
    x2 = x1_ref[...] + _mod_rows(mod_ref, 5) * moe.reshape(grp.nb, grp.tt, D_MODEL)
    ms = jnp.mean(x2 * x2, axis=-1, keepdims=True)
    out_ref[...] = x2 * lax.rsqrt(ms + EPS) * nf_ref[...]


def _finish(grp, pos, x1, mod, route_all, norm_f, y_sorted):
    off = grp.tile_off
    grid_spec = pltpu.PrefetchScalarGridSpec(
        num_scalar_prefetch=1,
        grid=(grp.n_tiles,),
        in_specs=[pl.BlockSpec((grp.nb, grp.tt, D_MODEL), grp.x_map()),
                  pl.BlockSpec((grp.nb, 6, D_MODEL), grp.mod_map()),
                  pl.BlockSpec((TM, LANES), lambda i, p: (i + off, 0)),
                  pl.BlockSpec((1, 1, D_MODEL), lambda i, p: (0, 0, 0)),
                  pl.BlockSpec(memory_space=pl.ANY)],
        out_specs=pl.BlockSpec((grp.nb, grp.tt, D_MODEL), grp.x_map()),
        scratch_shapes=[pltpu.VMEM((FINISH_AHEAD + 1, TM, D_MODEL), F32),
                        pltpu.VMEM((FINISH_AHEAD + 1, TM, D_MODEL), F32),
                        pltpu.SemaphoreType.DMA((FINISH_AHEAD + 1,))],
    )
    return pl.pallas_call(
        functools.partial(_finish_kernel, grp),
        name="finish",
        grid_spec=grid_spec,
        out_shape=jax.ShapeDtypeStruct(x1.shape, F32),
        compiler_params=_cparams(1),
    )(pos, x1, mod, route_all, norm_f.reshape(1, 1, D_MODEL), y_sorted)


def _sort_plan(route_all):
    n_tok = route_all.shape[0]
    n_pairs = 2 * n_tok
    n_sorted = n_pairs + N_EXPERTS * TME
    flat_e = route_all[:, 0:2].astype(jnp.int32).reshape(n_pairs)
    onehot = (flat_e[:, None] == jnp.arange(N_EXPERTS, dtype=jnp.int32)[None, :]).astype(jnp.int32)
    csum = jnp.cumsum(onehot, axis=0)
    rank = jnp.sum(onehot * csum, axis=1) - 1
    counts = csum[-1]
    padded = ((counts + TME - 1) // TME) * TME
    ends = jnp.cumsum(padded)
    starts = ends - padded
    pos = starts[flat_e] + rank
    token = jnp.arange(n_pairs, dtype=jnp.int32) // 2
    src_row = (jnp.arange(n_sorted, dtype=jnp.int32) % n_tok).at[pos].set(token)
    tile_start = jnp.arange(n_sorted // TME, dtype=jnp.int32) * TME
    tile_expert = jnp.sum((tile_start[:, None] >= ends[None, :]).astype(jnp.int32), axis=1)
    tile_expert = jnp.minimum(tile_expert, N_EXPERTS - 1)
    n_used = ends[-1] // TME
    is_first = jnp.concatenate([jnp.ones((1,), jnp.int32),
                                (tile_expert[1:] != tile_expert[:-1]).astype(jnp.int32)])
    segment = jnp.cumsum(is_first) - 1
    next_tile = ends[tile_expert] // TME
    next_expert = jnp.where(next_tile < n_used, tile_expert[jnp.minimum(next_tile, n_sorted // TME - 1)], -1)
    return dict(pos=pos.astype(jnp.int32), src_row=src_row, tile_expert=tile_expert.astype(jnp.int32),
                n_used=n_used.astype(jnp.int32).reshape(1), segment=segment.astype(jnp.int32),
                next_expert=next_expert.astype(jnp.int32))


def kernel(x_prompt, x_sample, c_prompt, c_sample, state_gla, state_pool, w_ada, b_ada, norm1, norm2, w_in,
           gate_up, gate_bias, gla_norm, pool_w, pool_scale, w_out, w_group, b_group, w_expert, b_expert,
           w1, w3, w2, norm_f):
    assert w_ada.shape[0] == 1, "single-layer step"
    bp, tp, _ = x_prompt.shape
    bs, ts, _ = x_sample.shape
    grp_p = _Group(bp, tp, 0, bs)
    grp_s = _Group(bs, ts, bp * tp, 0)
    n_tok = bp * tp + bs * ts

    n_c = bp + bs
    n_c_pad = -(-n_c // 8) * 8
    c_all = jnp.concatenate([c_sample, c_prompt, jnp.zeros((n_c_pad - n_c, D_MODEL), F32)], axis=0)
    mod = _adaln(c_all, w_ada[0], b_ada[0]).reshape(n_c_pad, 6, D_MODEL)
    mod_p = mod_s = mod

    wi = w_in[0]
    wq = wi[:, :QKVR_WIDTH].astype(BF16)
    wg = jnp.pad(wi[:, QKVR_WIDTH:QKVR_WIDTH + GATE_RANK], ((0, 0), (0, LANES - GATE_RANK))).astype(BF16)
    wu = wi[:, QKVR_WIDTH + GATE_RANK:].astype(BF16)
    gup = jnp.pad(gate_up[0], ((0, LANES - GATE_RANK), (0, 0))).astype(BF16)
    gb = gate_bias[0].reshape(1, GLA_KEY_WIDTH)
    pw = pool_w[0].astype(BF16)
    wo = w_out[0].astype(BF16)
    w_router = jnp.concatenate(
        [w_group[0], jnp.transpose(w_expert[0], (1, 0, 2)).reshape(D_MODEL, N_EXPERTS),
         jnp.zeros((D_MODEL, LANES - N_GROUPS - N_EXPERTS), F32)], axis=1)
    b_router = jnp.concatenate([b_group[0], b_expert[0].reshape(N_EXPERTS),
                                jnp.zeros((LANES - N_GROUPS - N_EXPERTS,), F32)]).reshape(1, LANES)

    qkvr_p, la_p, u_p = _in_proj(grp_p, x_prompt, mod_p, norm1[0], wq, wg, wu, gup, gb)
    qkvr_s, la_s, u_s = _in_proj(grp_s, x_sample, mod_s, norm1[0], wq, wg, wu, gup, gb)

    o_p, gla_p = _gla_prompt(bp, tp, qkvr_p, la_p)
    o_s, gla_s = _gla_decode(bs, ts, qkvr_s, la_s, state_gla[0])

    halo_per_tile = TM // HALO
    halo_map_p = lambda i: (jnp.maximum(i * halo_per_tile - 1, 0), 0)
    x1_p, h2_all, route_all = _mix_out(grp_p, n_tok, 0, True, o_p, qkvr_p, u_p, u_p, (HALO, POOL_WIDTH),
                                       halo_map_p, x_prompt, mod_p, norm2[0], gla_norm[0], pw, pool_scale[0],
                                       wo, w_router, b_router)
    x1_s, h2_all, route_all = _mix_out(grp_s, n_tok, PAST_LEN, False, o_s, qkvr_s, u_s, state_pool[0],
                                       (grp_s.nb, POOL_BUF, POOL_WIDTH), lambda i: (i, 0, 0), x_sample, mod_s,
                                       norm2[0], gla_norm[0], pw, pool_scale[0], wo, w_router, b_router,
                                       shared=(h2_all, route_all))

    plan = _sort_plan(route_all)
    y_sorted = _moe(plan, _dispatch(plan, h2_all), w1[0], w3[0], w2[0])

    y_p = _finish(grp_p, plan["pos"], x1_p, mod_p, route_all, norm_f, y_sorted)
    y_s = _finish(grp_s, plan["pos"], x1_s, mod_s, route_all, norm_f, y_sorted)

    u_p3 = u_p.reshape(bp, tp, POOL_WIDTH)
    u_s3 = u_s.reshape(bs, ts, POOL_WIDTH)
    pool_p = jnp.concatenate([jnp.zeros((bp, POOL_BUF, POOL_WIDTH), F32), u_p3], axis=1)[:, -POOL_BUF:]
    pool_s = jnp.concatenate([state_pool[0], u_s3], axis=1)[:, -POOL_BUF:]
    return (y_p, y_s, gla_p[None], pool_p[None], gla_s[None], pool_s[None])
```

```python
import functools

import jax
import jax.numpy as jnp
from jax import lax
from jax.experimental import pallas as pl
from jax.experimental.pallas import tpu as pltpu

D_MODEL = 2048
GLA_HEADS = 4
GLA_DK = 128
GLA_DV = 256
GLA_KEY_WIDTH = GLA_HEADS * GLA_DK
GLA_WIDTH = GLA_HEADS * GLA_DV
POOL_WIDTH = 1024
POOL_WINDOWS = (2, 4, 8, 16)
POOL_GW = 256
POOL_BUF = 15
HALO = 16
EXT_PAD = 8
EXT_TOK0 = EXT_PAD + HALO
GATE_RANK = 16
GATE_TEMP = 16.0
N_GROUPS = 4
EXPERTS_PER_GROUP = 8
N_EXPERTS = 32
EXPERT_FF = 512
EPS = 1e-6
PAST_LEN = 16384
QKVR_WIDTH = 2 * GLA_KEY_WIDTH + 2 * GLA_WIDTH

LANES = 128
TM = 256
TME = 256
TD = 1024
FINISH_AHEAD = 3
GLA_CHUNK = 64
GLA_SUB = 16
GLA_STEP = 512
VMEM_LIMIT = 56 * 1024 * 1024

BF16 = jnp.bfloat16
F32 = jnp.float32
NEG = -1e30


def _cparams(n_axes):
    return pltpu.CompilerParams(dimension_semantics=("arbitrary",) * n_axes,
                                vmem_limit_bytes=VMEM_LIMIT)


def _silu(x):
    return x / (1.0 + jnp.exp(-x))


def _bdot(a, b):
    return jnp.dot(a.astype(BF16), b.astype(BF16), preferred_element_type=F32)


def _split3(a):
    a1 = a.astype(BF16)
    r1 = a - a1.astype(F32)
    a2 = r1.astype(BF16)
    a3 = (r1 - a2.astype(F32)).astype(BF16)
    return a1, a2, a3


def _split2(a):
    hi = a.astype(BF16)
    lo = (a - hi.astype(F32)).astype(BF16)
    return hi, lo


def _dot_3pass(a, b_hi, b_lo):
    a_hi, a_lo = _split2(a)
    d = lambda x, y: jnp.dot(x, y, preferred_element_type=F32)
    return d(a_hi, b_hi) + (d(a_hi, b_lo) + d(a_lo, b_hi))


def _dot_exact_lhs(tri_bf16, g):
    g1, g2, g3 = _split3(g)
    d = lambda y: jnp.dot(tri_bf16, y, preferred_element_type=F32)
    return d(g1) + (d(g2) + d(g3))


def _adaln_kernel(c_ref, w_ref, b_ref, o_ref):
    c = c_ref[...]
    o_ref[...] = _bdot(_silu(c), w_ref[...]) + b_ref[...]


def _adaln(c_all, w_ada, b_ada):
    n, d = c_all.shape
    width = w_ada.shape[1]
    tn = 1024
    return pl.pallas_call(
        _adaln_kernel,
        name="adaln",
        grid=(width // tn,),
        in_specs=[pl.BlockSpec((n, d), lambda j: (0, 0)),
                  pl.BlockSpec((d, tn), lambda j: (0, j)),
                  pl.BlockSpec((1, tn), lambda j: (0, j))],
        out_specs=pl.BlockSpec((n, tn), lambda j: (0, j)),
        out_shape=jax.ShapeDtypeStruct((n, width), F32),
        compiler_params=_cparams(1),
    )(c_all, w_ada, b_ada.reshape(1, width))


class _Group:
    def __init__(self, batch, seq, row_off, mod_off):
        self.batch, self.seq, self.row_off, self.mod_off = batch, seq, row_off, mod_off
        if seq >= TM:
            assert seq % TM == 0
            self.nb, self.tt = 1, TM
            self.tiles_per_batch = seq // TM
            self.n_tiles = batch * self.tiles_per_batch
        else:
            assert TM % seq == 0 and batch % (TM // seq) == 0
            self.nb, self.tt = TM // seq, seq
            self.tiles_per_batch = 1
            self.n_tiles = batch // self.nb
        self.rows = batch * seq
        self.tile_off = row_off // TM

    def x_map(self):
        if self.nb == 1:
            tpb = self.tiles_per_batch
            return lambda i, *_: (i // tpb, i % tpb, 0)
        return lambda i, *_: (i, 0, 0)

    def mod_map(self):
        assert self.mod_off % self.nb == 0
        off = self.mod_off // self.nb
        if self.nb == 1:
            tpb = self.tiles_per_batch
            return lambda i, *_: (i // tpb + off, 0, 0)
        return lambda i, *_: (i + off, 0, 0)


def _mod_rows(mod_ref, idx):
    return mod_ref[:, idx:idx + 1, :]


def _rmsnorm_mod(x, gain, scale, shift):
    ms = jnp.mean(x * x, axis=-1, keepdims=True)
    y = x * lax.rsqrt(ms + EPS) * gain
    return y * (1.0 + scale) + shift


def _in_proj_kernel(x_ref, mod_ref, n1_ref, wq_ref, wg_ref, wu_ref, gup_ref, gb_ref,
                    qkvr_ref, la_ref, u_ref):
    x = x_ref[...]
    h = _rmsnorm_mod(x, n1_ref[...], _mod_rows(mod_ref, 1), _mod_rows(mod_ref, 0))
    hb = h.reshape(TM, D_MODEL).astype(BF16)
    qkvr_ref[...] = jnp.dot(hb, wq_ref[...], preferred_element_type=F32)
    u_ref[...] = jnp.dot(hb, wu_ref[...], preferred_element_type=F32)
    g_lr = jnp.dot(hb, wg_ref[...], preferred_element_type=F32)
    pre = jnp.dot(g_lr.astype(BF16), gup_ref[...], preferred_element_type=F32) + gb_ref[...]
    log_sig = jnp.minimum(pre, 0.0) - jnp.log1p(jnp.exp(-jnp.abs(pre)))
    la_ref[...] = log_sig / GATE_TEMP


def _in_proj(grp, x, mod, norm1, wq, wg, wu, gup, gb):
    const = lambda i: (0, 0)
    row = lambda i: (i, 0)
    return pl.pallas_call(
        _in_proj_kernel,
        name="in_proj",
        grid=(grp.n_tiles,),
        in_specs=[pl.BlockSpec((grp.nb, grp.tt, D_MODEL), grp.x_map()),
                  pl.BlockSpec((grp.nb, 6, D_MODEL), grp.mod_map()),
                  pl.BlockSpec((1, 1, D_MODEL), lambda i: (0, 0, 0)),
                  pl.BlockSpec(wq.shape, const, pipeline_mode=pl.Buffered(1)),
                  pl.BlockSpec(wg.shape, const, pipeline_mode=pl.Buffered(1)),
                  pl.BlockSpec(wu.shape, const, pipeline_mode=pl.Buffered(1)),
                  pl.BlockSpec(gup.shape, const, pipeline_mode=pl.Buffered(1)),
                  pl.BlockSpec(gb.shape, const, pipeline_mode=pl.Buffered(1))],
        out_specs=[pl.BlockSpec((TM, QKVR_WIDTH), row),
                   pl.BlockSpec((TM, GLA_KEY_WIDTH), row),
                   pl.BlockSpec((TM, POOL_WIDTH), row)],
        out_shape=[jax.ShapeDtypeStruct((grp.rows, QKVR_WIDTH), F32),
                   jax.ShapeDtypeStruct((grp.rows, GLA_KEY_WIDTH), F32),
                   jax.ShapeDtypeStruct((grp.rows, POOL_WIDTH), F32)],
        compiler_params=_cparams(1),
    )(x, mod, norm1.reshape(1, 1, D_MODEL), wq, wg, wu, gup, gb)


def _gla_select_matrix(chunk, sub):
    r = jnp.arange(sub * GLA_DK, dtype=jnp.int32)[:, None] // GLA_DK
    l = jnp.arange(LANES, dtype=jnp.int32)[None, :]
    return ((l % sub == r) & (l < chunk)).astype(BF16)


def _gla_chunk(q4, k4, v4, g4, states, chunk, sub, wsel, t_ref):
    n_sub = chunk // sub
    rows = lax.broadcasted_iota(jnp.int32, (chunk, chunk), 0)
    cols = lax.broadcasted_iota(jnp.int32, (chunk, chunk), 1)
    tri = (rows >= cols).astype(BF16)
    b4 = _dot_exact_lhs(tri, g4)
    q4 = q4 * (GLA_DK ** -0.5)
    nt = (((1,), (1,)), ((), ()))
    tn = (((0,), (0,)), ((), ()))
    key_row = lax.broadcasted_iota(jnp.int32, (chunk, LANES), 0)

    for h in range(GLA_HEADS):
        hs = slice(h * GLA_DK, (h + 1) * GLA_DK)
        q, k, b = q4[:, hs], k4[:, hs], b4[:, hs]
        for s in range(n_sub):
            lo = s * sub
            r0 = (h * n_sub + s) * sub
            q_s, b_s = q[lo:lo + sub, :], b[lo:lo + sub, :]
            for jl in range(sub):
                j = lo + jl
                decay = jnp.exp(jnp.minimum(b_s - b[j:j + 1, :], 0.0))
                t_ref[r0:r0 + sub, jl * GLA_DK:(jl + 1) * GLA_DK] = (q_s * k[j:j + 1, :] * decay).astype(t_ref.dtype)
    p_diag = jnp.dot(t_ref[...].astype(BF16), wsel, preferred_element_type=F32)

    lane = lax.broadcasted_iota(jnp.int32, (sub, chunk), 1)
    row = lax.broadcasted_iota(jnp.int32, (sub, chunk), 0)
    outs, new_states = [], []
    for h in range(GLA_HEADS):
        hs = slice(h * GLA_DK, (h + 1) * GLA_DK)
        vs = slice(h * GLA_DV, (h + 1) * GLA_DV)
        q, k, b, v, state = q4[:, hs], k4[:, hs], b4[:, hs], v4[:, vs], states[h]
        o = _bdot(q * jnp.exp(b), state)

        b_last = b[chunk - 1:chunk, :]
        k_dec = k * jnp.exp(b_last - b)
        decay_col = jnp.exp(b[chunk - 8:chunk, :]).T[:, 7:8]
        new_states.append(decay_col * state + lax.dot_general(
            k_dec.astype(BF16), v.astype(BF16), tn, preferred_element_type=F32))

        p_blocks = []
        for s in range(n_sub):
            lo = s * sub
            r0 = (h * n_sub + s) * sub
            in_block = (lane >= lo) & (lane - lo <= row)
            p = jnp.where(in_block, p_diag[r0:r0 + sub, :chunk], 0.0)
            if s > 0:
                ref_row = b[lo - 1:lo, :]
                q_rel = q[lo:lo + sub, :] * jnp.exp(b[lo:lo + sub, :] - ref_row)
                k_rel = k * jnp.exp(jnp.where(key_row < lo, ref_row - b, NEG))
                p = p + lax.dot_general(q_rel.astype(BF16), k_rel.astype(BF16), nt, preferred_element_type=F32)
            p_blocks.append(p)
        p_full = p_blocks[0] if n_sub == 1 else jnp.concatenate(p_blocks, axis=0)
        outs.append(o + _bdot(p_full, v))
    return jnp.concatenate(outs, axis=1), new_states


def _gla_prompt_kernel(q_ref, k_ref, v_ref, la_ref, wsel_ref, o_ref, s_ref, t_ref):
    @pl.when(pl.program_id(1) == 0)
    def _():
        s_ref[...] = jnp.zeros_like(s_ref)

    def body(c, carry):
        r0 = pl.multiple_of(c * GLA_CHUNK, GLA_CHUNK)
        sl = pl.ds(r0, GLA_CHUNK)
        states = [s_ref[0, h] for h in range(GLA_HEADS)]
        o, new_states = _gla_chunk(q_ref[sl, :], k_ref[sl, :], v_ref[sl, :], la_ref[sl, :], states,
                                   GLA_CHUNK, GLA_SUB, wsel_ref[...], t_ref)
        o_ref[sl, :] = o
        for h in range(GLA_HEADS):
            s_ref[0, h] = new_states[h]
        return carry

    lax.fori_loop(0, GLA_STEP // GLA_CHUNK, body, 0)


def _gla_prompt(batch, seq, qkvr, log_a):
    steps = seq // GLA_STEP
    row = lambda b, s: b * steps + s
    wsel = _gla_select_matrix(GLA_CHUNK, GLA_SUB)
    return pl.pallas_call(
        _gla_prompt_kernel,
        name="gla_prompt",
        grid=(batch, steps),
        in_specs=[pl.BlockSpec((GLA_STEP, GLA_KEY_WIDTH), lambda b, s: (row(b, s), 0)),
                  pl.BlockSpec((GLA_STEP, GLA_KEY_WIDTH), lambda b, s: (row(b, s), 1)),
                  pl.BlockSpec((GLA_STEP, GLA_WIDTH), lambda b, s: (row(b, s), 1)),
                  pl.BlockSpec((GLA_STEP, GLA_KEY_WIDTH), lambda b, s: (row(b, s), 0)),
                  pl.BlockSpec(wsel.shape, lambda b, s: (0, 0))],
        out_specs=[pl.BlockSpec((GLA_STEP, GLA_WIDTH), lambda b, s: (row(b, s), 0)),
                   pl.BlockSpec((1, GLA_HEADS, GLA_DK, GLA_DV), lambda b, s: (b, 0, 0, 0))],
        out_shape=[jax.ShapeDtypeStruct((batch * seq, GLA_WIDTH), F32),
                   jax.ShapeDtypeStruct((batch, GLA_HEADS, GLA_DK, GLA_DV), F32)],
        scratch_shapes=[pltpu.VMEM((GLA_HEADS * GLA_CHUNK, GLA_SUB * GLA_DK), BF16)],
        compiler_params=_cparams(2),
    )(qkvr, qkvr, qkvr, log_a, wsel)


GLA_DEC_BB = 8


def _gla_decode_kernel(seq, q_ref, k_ref, v_ref, la_ref, wsel_ref, s0_ref, o_ref, s_ref, t_ref):
    def body(i, carry):
        r0 = pl.multiple_of(i * seq, seq)
        sl = pl.ds(r0, seq)
        states = [s0_ref[i, h] for h in range(GLA_HEADS)]
        o, new_states = _gla_chunk(q_ref[sl, :], k_ref[sl, :], v_ref[sl, :], la_ref[sl, :], states,
                                   seq, seq, wsel_ref[...], t_ref)
        o_ref[sl, :] = o
        for h in range(GLA_HEADS):
            s_ref[i, h] = new_states[h]
        return carry

    lax.fori_loop(0, GLA_DEC_BB, body, 0)


def _gla_decode(batch, seq, qkvr, log_a, state):
    rows = GLA_DEC_BB * seq
    wsel = _gla_select_matrix(seq, seq)
    state_spec = pl.BlockSpec((GLA_DEC_BB, GLA_HEADS, GLA_DK, GLA_DV), lambda i: (i, 0, 0, 0))
    return pl.pallas_call(
        functools.partial(_gla_decode_kernel, seq),
        name="gla_decode",
        grid=(batch // GLA_DEC_BB,),
        in_specs=[pl.BlockSpec((rows, GLA_KEY_WIDTH), lambda i: (i, 0)),
                  pl.BlockSpec((rows, GLA_KEY_WIDTH), lambda i: (i, 1)),
                  pl.BlockSpec((rows, GLA_WIDTH), lambda i: (i, 1)),
                  pl.BlockSpec((rows, GLA_KEY_WIDTH), lambda i: (i, 0)),
                  pl.BlockSpec(wsel.shape, lambda i: (0, 0)),
                  state_spec],
        out_specs=[pl.BlockSpec((rows, GLA_WIDTH), lambda i: (i, 0)), state_spec],
        out_shape=[jax.ShapeDtypeStruct((batch * seq, GLA_WIDTH), F32),
                   jax.ShapeDtypeStruct((batch, GLA_HEADS, GLA_DK, GLA_DV), F32)],
        scratch_shapes=[pltpu.VMEM((GLA_HEADS * seq, seq * GLA_DK), F32)],
        compiler_params=_cparams(1),
    )(qkvr, qkvr, qkvr, log_a, wsel, state)


def _route(logits):
    lt = logits.T
    n = lt.shape[1]
    big = jnp.int32(10 ** 6)
    row8 = lax.broadcasted_iota(jnp.int32, (EXPERTS_PER_GROUP, n), 0)
    lg = jnp.where(row8 < N_GROUPS, lt[0:EXPERTS_PER_GROUP], NEG)
    mg = jnp.max(lg, axis=0, keepdims=True)
    g_idx = jnp.min(jnp.where(lg == mg, row8, big), axis=0, keepdims=True)
    p_sel = 1.0 / jnp.sum(jnp.exp(lg - mg), axis=0, keepdims=True)
    le = jnp.zeros((EXPERTS_PER_GROUP, n), F32)
    for g in range(N_GROUPS):
        lo = EXPERTS_PER_GROUP * (g + 1)
        le = jnp.where(g_idx == g, lt[lo:lo + EXPERTS_PER_GROUP], le)
    m1 = jnp.max(le, axis=0, keepdims=True)
    i1 = jnp.min(jnp.where(le == m1, row8, big), axis=0, keepdims=True)
    rest = row8 != i1
    m2 = jnp.max(jnp.where(rest, le, NEG), axis=0, keepdims=True)
    i2 = jnp.min(jnp.where(rest & (le == m2), row8, big), axis=0, keepdims=True)
    e2 = jnp.exp(m2 - m1)
    w1 = p_sel / (1.0 + e2)
    w2 = p_sel * e2 / (1.0 + e2)
    ex1 = (g_idx * EXPERTS_PER_GROUP + i1).astype(F32)
    ex2 = (g_idx * EXPERTS_PER_GROUP + i2).astype(F32)
    packed = jnp.where(row8 == 0, ex1, jnp.where(row8 == 1, ex2,
                       jnp.where(row8 == 2, w1, jnp.where(row8 == 3, w2, 0.0))))
    full = jnp.concatenate([packed, jnp.zeros((LANES - EXPERTS_PER_GROUP, n), F32)], axis=0)
    return full.T


def _mix_out_kernel(grp, pos0, zero_first_halo, n_alias,
                    o_ref, r_ref, u_ref, halo_ref, x_ref, mod_ref, n2_ref, gn_ref, pw_ref, ps_ref,
                    wo_ref, wr_ref, br_ref, *rest):
    x1_ref, h2_ref, rt_ref, ext_ref, lvl_a, lvl_b, ymix_ref = rest[n_alias:]
    i = pl.program_id(0)

    @pl.when(i < grp.n_tiles)
    def _():
        _mix_out_tile(grp, pos0, zero_first_halo, i, o_ref, r_ref, u_ref, halo_ref, x_ref, mod_ref, n2_ref,
                      gn_ref, pw_ref, ps_ref, wo_ref, wr_ref, br_ref, x1_ref, h2_ref, rt_ref, ext_ref,
                      (lvl_a, lvl_b), ymix_ref)

    @pl.when(i >= grp.n_tiles)
    def _():
        h2_ref[...] = jnp.zeros_like(h2_ref)
        rt_ref[...] = jnp.zeros_like(rt_ref)


def _mix_out_tile(grp, pos0, zero_first_halo, i, o_ref, r_ref, u_ref, halo_ref, x_ref, mod_ref, n2_ref,
                  gn_ref, pw_ref, ps_ref, wo_ref, wr_ref, br_ref, x1_ref, h2_ref, rt_ref, ext_ref, lvl_refs,
                  ymix_ref):
    nb, tt = grp.nb, grp.tt
    hist = halo_ref.shape[-2]

    for h in range(GLA_HEADS):
        cs = slice(h * GLA_DV, (h + 1) * GLA_DV)
        oh = o_ref[:, cs]
        ms = jnp.mean(oh * oh, axis=-1, keepdims=True)
        yh = oh * lax.rsqrt(ms + EPS) * gn_ref[:, cs] * _silu(r_ref[:, cs])
        ymix_ref[:, cs] = yh.astype(BF16)

    halo = halo_ref[...]
    if zero_first_halo:
        halo = jnp.where(i % grp.tiles_per_batch == 0, 0.0, halo)
    n_ext = EXT_TOK0 + tt
    ext_ref[:, 0:EXT_TOK0 - hist, :] = jnp.zeros((nb, EXT_TOK0 - hist, POOL_WIDTH), F32)
    ext_ref[:, EXT_TOK0 - hist:EXT_TOK0, :] = halo.reshape(nb, hist, POOL_WIDTH)
    u = u_ref[...].reshape(nb, tt, POOL_WIDTH)
    ext_ref[:, EXT_TOK0:n_ext, :] = u
    for lvl_ref in lvl_refs:
        lvl_ref[:, 0:EXT_PAD, :] = jnp.zeros((nb, EXT_PAD, POOL_GW), F32)
    t_idx = lax.broadcasted_iota(jnp.int32, (nb, tt, POOL_GW), 1)
    if grp.nb == 1:
        pos = (i % grp.tiles_per_batch) * TM + t_idx + pos0
    else:
        pos = t_idx + pos0
    for gi, w in enumerate(POOL_WINDOWS):
        cs = slice(gi * POOL_GW, (gi + 1) * POOL_GW)
        cur = lambda lo, hi: ext_ref[:, lo:hi, cs]
        d, level = 1, 0
        while d < w:
            nxt = lvl_refs[level % 2]
            nxt[:, EXT_PAD:n_ext, :] = cur(EXT_PAD, n_ext) + cur(EXT_PAD - d, n_ext - d)
            cur = lambda lo, hi, ref=nxt: ref[:, lo:hi, :]
            d, level = 2 * d, level + 1
        acc = cur(EXT_TOK0, n_ext)
        cnt = jnp.minimum(pos + 1, w).astype(F32)
        pooled = acc / cnt - u[:, :, cs]
        yp = _bdot(pooled.reshape(TM, POOL_GW), pw_ref[gi]) * ps_ref[:, cs]
        ymix_ref[:, GLA_WIDTH + gi * POOL_GW:GLA_WIDTH + (gi + 1) * POOL_GW] = yp.astype(BF16)

    y = jnp.dot(ymix_ref[...], wo_ref[...], preferred_element_type=F32)
    x1 = x_ref[...] + _mod_rows(mod_ref, 2) * y.reshape(nb, tt, D_MODEL)
    x1_ref[...] = x1
    h2 = _rmsnorm_mod(x1, n2_ref[...], _mod_rows(mod_ref, 4), _mod_rows(mod_ref, 3)).reshape(TM, D_MODEL)
    h2_ref[...] = h2
    logits = _dot_3pass(h2, wr_ref[0], wr_ref[1]) + br_ref[...]
    rt_ref[...] = _route(logits)


def _mix_out(grp, n_tok, pos0, zero_first_halo, o, qkvr, u, halo_src, halo_block, halo_map, x, mod, norm2,
             gla_norm, pool_w, pool_scale, w_out, w_router, b_router, shared=()):
    n_alias = len(shared)
    n = grp.n_tiles
    n_fill = 0 if shared else n_tok // TM - n
    assert n_fill == 0 or grp.tile_off == 0
    clamp = lambda f: (lambda i: f(jnp.minimum(i, n - 1)))
    const2 = lambda i: (0, 0)
    row = clamp(lambda i: (i, 0))
    off = grp.tile_off
    kern = functools.partial(_mix_out_kernel, grp, pos0, zero_first_halo, n_alias)
    return pl.pallas_call(
        kern,
        name="mix_out",
        grid=(n + n_fill,),
        in_specs=[pl.BlockSpec((TM, GLA_WIDTH), row),
                  pl.BlockSpec((TM, GLA_WIDTH), clamp(lambda i: (i, 2))),
                  pl.BlockSpec((TM, POOL_WIDTH), row),
                  pl.BlockSpec(halo_block, clamp(halo_map)),
                  pl.BlockSpec((grp.nb, grp.tt, D_MODEL), clamp(grp.x_map())),
                  pl.BlockSpec((grp.nb, 6, D_MODEL), clamp(grp.mod_map())),
                  pl.BlockSpec((1, 1, D_MODEL), lambda i: (0, 0, 0)),
                  pl.BlockSpec((1, GLA_WIDTH), const2),
                  pl.BlockSpec(pool_w.shape, lambda i: (0, 0, 0), pipeline_mode=pl.Buffered(1)),
                  pl.BlockSpec((1, POOL_WIDTH), const2),
                  pl.BlockSpec(w_out.shape, const2, pipeline_mode=pl.Buffered(1)),
                  pl.BlockSpec(w_router.shape, lambda i: (0, 0, 0), pipeline_mode=pl.Buffered(1)),
                  pl.BlockSpec((1, LANES), const2)]
                 + [pl.BlockSpec(memory_space=pl.ANY)] * n_alias,
        out_specs=[pl.BlockSpec((grp.nb, grp.tt, D_MODEL), clamp(grp.x_map())),
                   pl.BlockSpec((TM, D_MODEL), lambda i: (i + off, 0)),
                   pl.BlockSpec((TM, LANES), lambda i: (i + off, 0))],
        out_shape=[jax.ShapeDtypeStruct(x.shape, F32),
                   jax.ShapeDtypeStruct((n_tok, D_MODEL), F32),
                   jax.ShapeDtypeStruct((n_tok, LANES), F32)],
        scratch_shapes=[pltpu.VMEM((grp.nb, EXT_TOK0 + grp.tt, POOL_WIDTH), F32),
                        pltpu.VMEM((grp.nb, EXT_TOK0 + grp.tt, POOL_GW), F32),
                        pltpu.VMEM((grp.nb, EXT_TOK0 + grp.tt, POOL_GW), F32),
                        pltpu.VMEM((TM, D_MODEL), BF16)],
        input_output_aliases={13 + k: 1 + k for k in range(n_alias)},
        compiler_params=_cparams(1),
    )(o, qkvr, u, halo_src, x, mod, norm2.reshape(1, 1, D_MODEL), gla_norm.reshape(1, GLA_WIDTH),
      pool_w, pool_scale.reshape(1, POOL_WIDTH), w_out, w_router, b_router, *shared)


def _row_copy(src_hbm, src_row, dst, dst_row, sem):
    return pltpu.make_async_copy(src_hbm.at[pl.ds(src_row, 1), :], dst.at[pl.ds(dst_row, 1), :], sem)


def _tile_wait(src_hbm, dst, sem):
    pltpu.make_async_copy(src_hbm.at[pl.ds(0, dst.shape[0]), :], dst, sem).wait()


def _dispatch_kernel(src_ref, nu_ref, h_hbm, hs_ref, buf, sem):
    i = pl.program_id(0)
    n_used = (nu_ref[0] * TME + TD - 1) // TD
    slot = i % 2

    def issue(tile, slot_):
        base = tile * TD

        def body(r, carry):
            _row_copy(h_hbm, src_ref[base + 2 * r], buf.at[slot_], 2 * r, sem.at[slot_]).start(priority=0)
            _row_copy(h_hbm, src_ref[base + 2 * r + 1], buf.at[slot_], 2 * r + 1, sem.at[slot_]).start(priority=1)
            return carry

        lax.fori_loop(0, TD // 2, body, 0, unroll=4)

    @pl.when(i == 0)
    def _():
        issue(0, 0)

    @pl.when(i + 1 < n_used)
    def _():
        issue(i + 1, 1 - slot)

    @pl.when(i < n_used)
    def _():
        _tile_wait(h_hbm, buf.at[slot], sem.at[slot])
        hs_ref[...] = buf[slot].astype(BF16)

    @pl.when(i >= n_used)
    def _():
        hs_ref[...] = jnp.zeros_like(hs_ref)


def _dispatch(plan, h2_all):
    n_sorted = plan["src_row"].shape[0]
    assert n_sorted % TD == 0
    grid_spec = pltpu.PrefetchScalarGridSpec(
        num_scalar_prefetch=2,
        grid=(n_sorted // TD,),
        in_specs=[pl.BlockSpec(memory_space=pl.ANY)],
        out_specs=pl.BlockSpec((TD, D_MODEL), lambda i, *_: (i, 0)),
        scratch_shapes=[pltpu.VMEM((2, TD, D_MODEL), F32), pltpu.SemaphoreType.DMA((2,))],
    )
    return pl.pallas_call(
        _dispatch_kernel,
        name="dispatch",
        grid_spec=grid_spec,
        out_shape=jax.ShapeDtypeStruct((n_sorted, D_MODEL), BF16),
        compiler_params=_cparams(1),
    )(plan["src_row"], plan["n_used"], h2_all)


def _moe_kernel(te_ref, nu_ref, seg_ref, nxt_ref, hs_ref, w1_hbm, w3_hbm, w2_hbm, y_ref,
                wf1, wf3, wf2, wsem, w1b, w3b, w2b):
    i = pl.program_id(0)
    n_used = nu_ref[0]

    def weight_copies(expert, wslot):
        return (pltpu.make_async_copy(w1_hbm.at[expert], wf1.at[wslot], wsem.at[wslot]),
                pltpu.make_async_copy(w3_hbm.at[expert], wf3.at[wslot], wsem.at[wslot]),
                pltpu.make_async_copy(w2_hbm.at[expert], wf2.at[wslot], wsem.at[wslot]))

    @pl.when(i == 0)
    def _():
        for c in weight_copies(te_ref[0], 0):
            c.start()

    prev = jnp.maximum(i - 1, 0)

    @pl.when((i < n_used) & ((i == 0) | (te_ref[i] != te_ref[prev])))
    def _():
        wslot = seg_ref[i] % 2
        for c in weight_copies(te_ref[i], wslot):
            c.wait()
        w1b[...] = wf1[wslot].astype(BF16)
        w3b[...] = wf3[wslot].astype(BF16)
        w2b[...] = wf2[wslot].astype(BF16)

        @pl.when(nxt_ref[i] >= 0)
        def _():
            for c in weight_copies(nxt_ref[i], 1 - wslot):
                c.start()

    @pl.when(i < n_used)
    def _():
        x = hs_ref[...]
        a = jnp.dot(x, w1b[...], preferred_element_type=F32)
        b = jnp.dot(x, w3b[...], preferred_element_type=F32)
        hid = _silu(a) * b
        y_ref[...] = jnp.dot(hid.astype(BF16), w2b[...], preferred_element_type=F32)

    @pl.when(i >= n_used)
    def _():
        y_ref[...] = jnp.zeros_like(y_ref)


def _moe(plan, h_sorted, w1, w3, w2):
    n_sorted = h_sorted.shape[0]
    n_tiles = n_sorted // TME
    used = lambda i, te, nu, seg, nxt: (jnp.minimum(i, nu[0] - 1), 0)
    grid_spec = pltpu.PrefetchScalarGridSpec(
        num_scalar_prefetch=4,
        grid=(n_tiles,),
        in_specs=[pl.BlockSpec((TME, D_MODEL), used)] + [pl.BlockSpec(memory_space=pl.ANY)] * 3,
        out_specs=pl.BlockSpec((TME, D_MODEL), lambda i, *_: (i, 0)),
        scratch_shapes=[pltpu.VMEM((2, D_MODEL, EXPERT_FF), F32), pltpu.VMEM((2, D_MODEL, EXPERT_FF), F32),
                        pltpu.VMEM((2, EXPERT_FF, D_MODEL), F32),
                        pltpu.SemaphoreType.DMA((2,)),
                        pltpu.VMEM((D_MODEL, EXPERT_FF), BF16), pltpu.VMEM((D_MODEL, EXPERT_FF), BF16),
                        pltpu.VMEM((EXPERT_FF, D_MODEL), BF16)],
    )
    return pl.pallas_call(
        _moe_kernel,
        name="moe",
        grid_spec=grid_spec,
        out_shape=jax.ShapeDtypeStruct((n_sorted, D_MODEL), F32),
        compiler_params=_cparams(1),
    )(plan["tile_expert"], plan["n_used"], plan["segment"], plan["next_expert"], h_sorted, w1, w3, w2)


def _finish_kernel(grp, pos_ref, x1_ref, mod_ref, rt_ref, nf_ref, y_hbm, out_ref, buf_a, buf_b, sem):
    i = pl.program_id(0)
    n_steps = pl.num_programs(0)
    n_slots = FINISH_AHEAD + 1
    slot = i % n_slots

    def issue(tile, slot_):
        base = (tile * TM + grp.row_off) * 2

        def body(r, carry):
            _row_copy(y_hbm, pos_ref[base + 2 * r], buf_a.at[slot_], r, sem.at[slot_]).start(priority=0)
            _row_copy(y_hbm, pos_ref[base + 2 * r + 1], buf_b.at[slot_], r, sem.at[slot_]).start(priority=1)
            return carry

        lax.fori_loop(0, TM, body, 0, unroll=8)

    @pl.when(i == 0)
    def _():
        for t in range(FINISH_AHEAD):
            @pl.when(t < n_steps)
            def _():
                issue(t, t)

    @pl.when(i + FINISH_AHEAD < n_steps)
    def _():
        issue(i + FINISH_AHEAD, (i + FINISH_AHEAD) % n_slots)

    _tile_wait(y_hbm, buf_a.at[slot], sem.at[slot])
    _tile_wait(y_hbm, buf_b.at[slot], sem.at[slot])
    rt = rt_ref[...]
    moe = rt[:, 2:3] * buf_a[slot] + rt[:, 3:4] * buf_b[slot]
    x2 = x1_ref[...] + _mod_rows(mod_ref, 5) * moe.reshape(grp.nb, grp.tt, D_MODEL)
    ms = jnp.mean(x2 * x2, axis=-1, keepdims=True)
    out_ref[...] = x2 * lax.rsqrt(ms + EPS) * nf_ref[...]


def _finish(grp, pos, x1, mod, route_all, norm_f, y_sorted):
    off = grp.tile_off
    grid_spec = pltpu.PrefetchScalarGridSpec(
        num_scalar_prefetch=1,
        grid=(grp.n_tiles,),
        in_specs=[pl.BlockSpec((grp.nb, grp.tt, D_MODEL), grp.x_map()),
                  pl.BlockSpec((grp.nb, 6, D_MODEL), grp.mod_map()),
                  pl.BlockSpec((TM, LANES), lambda i, p: (i + off, 0)),
                  pl.BlockSpec((1, 1, D_MODEL), lambda i, p: (0, 0, 0)),
                  pl.BlockSpec(memory_space=pl.ANY)],
        out_specs=pl.BlockSpec((grp.nb, grp.tt, D_MODEL), grp.x_map()),
        scratch_shapes=[pltpu.VMEM((FINISH_AHEAD + 1, TM, D_MODEL), F32),
                        pltpu.VMEM((FINISH_AHEAD + 1, TM, D_MODEL), F32),
                        pltpu.SemaphoreType.DMA((FINISH_AHEAD + 1,))],
    )
    return pl.pallas_call(
        functools.partial(_finish_kernel, grp),
        name="finish",
        grid_spec=grid_spec,
        out_shape=jax.ShapeDtypeStruct(x1.shape, F32),
        compiler_params=_cparams(1),
    )(pos, x1, mod, route_all, norm_f.reshape(1, 1, D_MODEL), y_sorted)


def _sort_plan(route_all):
    n_tok = route_all.shape[0]
    n_pairs = 2 * n_tok
    n_sorted = n_pairs + N_EXPERTS * TME
    flat_e = route_all[:, 0:2].astype(jnp.int32).reshape(n_pairs)
    onehot = (flat_e[:, None] == jnp.arange(N_EXPERTS, dtype=jnp.int32)[None, :]).astype(jnp.int32)
    csum = jnp.cumsum(onehot, axis=0)
    rank = jnp.sum(onehot * csum, axis=1) - 1
    counts = csum[-1]
    padded = ((counts + TME - 1) // TME) * TME
    ends = jnp.cumsum(padded)
    starts = ends - padded
    pos = starts[flat_e] + rank
    token = jnp.arange(n_pairs, dtype=jnp.int32) // 2
    src_row = (jnp.arange(n_sorted, dtype=jnp.int32) % n_tok).at[pos].set(token)
    tile_start = jnp.arange(n_sorted // TME, dtype=jnp.int32) * TME
    tile_expert = jnp.sum((tile_start[:, None] >= ends[None, :]).astype(jnp.int32), axis=1)
    tile_expert = jnp.minimum(tile_expert, N_EXPERTS - 1)
    n_used = ends[-1] // TME
    is_first = jnp.concatenate([jnp.ones((1,), jnp.int32),
                                (tile_expert[1:] != tile_expert[:-1]).astype(jnp.int32)])
    segment = jnp.cumsum(is_first) - 1
    next_tile = ends[tile_expert] // TME
    next_expert = jnp.where(next_tile < n_used, tile_expert[jnp.minimum(next_tile, n_sorted // TME - 1)], -1)
    return dict(pos=pos.astype(jnp.int32), src_row=src_row, tile_expert=tile_expert.astype(jnp.int32),
                n_used=n_used.astype(jnp.int32).reshape(1), segment=segment.astype(jnp.int32),
                next_expert=next_expert.astype(jnp.int32))


def kernel(x_prompt, x_sample, c_prompt, c_sample, state_gla, state_pool, w_ada, b_ada, norm1, norm2, w_in,
           gate_up, gate_bias, gla_norm, pool_w, pool_scale, w_out, w_group, b_group, w_expert, b_expert,
           w1, w3, w2, norm_f):
    assert w_ada.shape[0] == 1, "single-layer step"
    bp, tp, _ = x_prompt.shape
    bs, ts, _ = x_sample.shape
    grp_p = _Group(bp, tp, 0, bs)
    grp_s = _Group(bs, ts, bp * tp, 0)
    n_tok = bp * tp + bs * ts

    n_c = bp + bs
    n_c_pad = -(-n_c // 8) * 8
    c_all = jnp.concatenate([c_sample, c_prompt, jnp.zeros((n_c_pad - n_c, D_MODEL), F32)], axis=0)
    mod = _adaln(c_all, w_ada[0], b_ada[0]).reshape(n_c_pad, 6, D_MODEL)
    mod_p = mod_s = mod

    wi = w_in[0]
    wq = wi[:, :QKVR_WIDTH].astype(BF16)
    wg = jnp.pad(wi[:, QKVR_WIDTH:QKVR_WIDTH + GATE_RANK], ((0, 0), (0, LANES - GATE_RANK))).astype(BF16)
    wu = wi[:, QKVR_WIDTH + GATE_RANK:].astype(BF16)
    gup = jnp.pad(gate_up[0], ((0, LANES - GATE_RANK), (0, 0))).astype(BF16)
    gb = gate_bias[0].reshape(1, GLA_KEY_WIDTH)
    pw = pool_w[0].astype(BF16)
    wo = w_out[0].astype(BF16)
    gap = EXPERTS_PER_GROUP - N_GROUPS
    tail = LANES - EXPERTS_PER_GROUP - N_EXPERTS
    w_router = jnp.concatenate(
        [w_group[0], jnp.zeros((D_MODEL, gap), F32),
         jnp.transpose(w_expert[0], (1, 0, 2)).reshape(D_MODEL, N_EXPERTS),
         jnp.zeros((D_MODEL, tail), F32)], axis=1)
    w_router_hi = w_router.astype(BF16)
    w_router = jnp.stack([w_router_hi, (w_router - w_router_hi.astype(F32)).astype(BF16)])
    b_router = jnp.concatenate([b_group[0], jnp.zeros((gap,), F32), b_expert[0].reshape(N_EXPERTS),
                                jnp.zeros((tail,), F32)]).reshape(1, LANES)

    qkvr_p, la_p, u_p = _in_proj(grp_p, x_prompt, mod_p, norm1[0], wq, wg, wu, gup, gb)
    qkvr_s, la_s, u_s = _in_proj(grp_s, x_sample, mod_s, norm1[0], wq, wg, wu, gup, gb)

    o_p, gla_p = _gla_prompt(bp, tp, qkvr_p, la_p)
    o_s, gla_s = _gla_decode(bs, ts, qkvr_s, la_s, state_gla.reshape(state_gla.shape[1:]))

    halo_per_tile = TM // HALO
    halo_map_p = lambda i: (jnp.maximum(i * halo_per_tile - 1, 0), 0)
    x1_p, h2_all, route_all = _mix_out(grp_p, n_tok, 0, True, o_p, qkvr_p, u_p, u_p, (HALO, POOL_WIDTH),
                                       halo_map_p, x_prompt, mod_p, norm2[0], gla_norm[0], pw, pool_scale[0],
                                       wo, w_router, b_router)
    x1_s, h2_all, route_all = _mix_out(grp_s, n_tok, PAST_LEN, False, o_s, qkvr_s, u_s,
                                       state_pool.reshape(bs, POOL_BUF, POOL_WIDTH),
                                       (grp_s.nb, POOL_BUF, POOL_WIDTH), lambda i: (i, 0, 0), x_sample, mod_s,
                                       norm2[0], gla_norm[0], pw, pool_scale[0], wo, w_router, b_router,
                                       shared=(h2_all, route_all))

    plan = _sort_plan(route_all)
    y_sorted = _moe(plan, _dispatch(plan, h2_all), w1.reshape(w1.shape[1:]), w3.reshape(w3.shape[1:]),
                    w2.reshape(w2.shape[1:]))

    y_p = _finish(grp_p, plan["pos"], x1_p, mod_p, route_all, norm_f, y_sorted)
    y_s = _finish(grp_s, plan["pos"], x1_s, mod_s, route_all, norm_f, y_sorted)

    u_p3 = u_p.reshape(bp, tp, POOL_WIDTH)
    u_s3 = u_s.reshape(bs, ts, POOL_WIDTH)
    assert tp >= POOL_BUF > ts
    pool_p = u_p3[:, tp - POOL_BUF:]
    pool_s = jnp.concatenate([state_pool.reshape(bs, POOL_BUF, POOL_WIDTH)[:, ts:], u_s3], axis=1)
    lead = lambda a: a.reshape((1,) + a.shape)
    return (y_p, y_s, lead(gla_p), lead(pool_p), lead(gla_s), lead(pool_s))
```

```python
import functools

import jax
import jax.numpy as jnp
from jax import lax
from jax.experimental import pallas as pl
from jax.experimental.pallas import tpu as pltpu

D_MODEL = 2048
GLA_HEADS = 4
GLA_DK = 128
GLA_DV = 256
GLA_KEY_WIDTH = GLA_HEADS * GLA_DK
GLA_WIDTH = GLA_HEADS * GLA_DV
POOL_WIDTH = 1024
POOL_WINDOWS = (2, 4, 8, 16)
POOL_GW = 256
POOL_BUF = 15
HALO = 16
EXT_PAD = 8
EXT_TOK0 = EXT_PAD + HALO
GATE_RANK = 16
GATE_TEMP = 16.0
N_GROUPS = 4
EXPERTS_PER_GROUP = 8
N_EXPERTS = 32
EXPERT_FF = 512
EPS = 1e-6
PAST_LEN = 16384
QKVR_WIDTH = 2 * GLA_KEY_WIDTH + 2 * GLA_WIDTH

LANES = 128
TM = 256
TME = 256
TD = 1024
FINISH_AHEAD = 3
GLA_CHUNK = 64
GLA_SUB = 16
GLA_STEP = 512
VMEM_LIMIT = 56 * 1024 * 1024

BF16 = jnp.bfloat16
F32 = jnp.float32
NEG = -1e30


def _cparams(n_axes):
    return pltpu.CompilerParams(dimension_semantics=("arbitrary",) * n_axes,
                                vmem_limit_bytes=VMEM_LIMIT)


def _silu(x):
    return x / (1.0 + jnp.exp(-x))


def _bdot(a, b):
    return jnp.dot(a.astype(BF16), b.astype(BF16), preferred_element_type=F32)


def _split3(a):
    a1 = a.astype(BF16)
    r1 = a - a1.astype(F32)
    a2 = r1.astype(BF16)
    a3 = (r1 - a2.astype(F32)).astype(BF16)
    return a1, a2, a3


def _split2(a):
    hi = a.astype(BF16)
    lo = (a - hi.astype(F32)).astype(BF16)
    return hi, lo


def _dot_3pass(a, b_hi, b_lo):
    a_hi, a_lo = _split2(a)
    d = lambda x, y: jnp.dot(x, y, preferred_element_type=F32)
    return d(a_hi, b_hi) + (d(a_hi, b_lo) + d(a_lo, b_hi))


def _dot_exact_lhs(tri_bf16, g):
    g1, g2, g3 = _split3(g)
    d = lambda y: jnp.dot(tri_bf16, y, preferred_element_type=F32)
    return d(g1) + (d(g2) + d(g3))


def _adaln_kernel(c_ref, w_ref, b_ref, o_ref):
    c = c_ref[...]
    o_ref[...] = _bdot(_silu(c), w_ref[...]) + b_ref[...]


def _adaln(c_all, w_ada, b_ada):
    n, d = c_all.shape
    width = w_ada.shape[1]
    tn = 1024
    return pl.pallas_call(
        _adaln_kernel,
        name="adaln",
        grid=(width // tn,),
        in_specs=[pl.BlockSpec((n, d), lambda j: (0, 0)),
                  pl.BlockSpec((d, tn), lambda j: (0, j)),
                  pl.BlockSpec((1, tn), lambda j: (0, j))],
        out_specs=pl.BlockSpec((n, tn), lambda j: (0, j)),
        out_shape=jax.ShapeDtypeStruct((n, width), F32),
        compiler_params=_cparams(1),
    )(c_all, w_ada, b_ada.reshape(1, width))


class _Group:
    def __init__(self, batch, seq, row_off, mod_off):
        self.batch, self.seq, self.row_off, self.mod_off = batch, seq, row_off, mod_off
        if seq >= TM:
            assert seq % TM == 0
            self.nb, self.tt = 1, TM
            self.tiles_per_batch = seq // TM
            self.n_tiles = batch * self.tiles_per_batch
        else:
            assert TM % seq == 0 and batch % (TM // seq) == 0
            self.nb, self.tt = TM // seq, seq
            self.tiles_per_batch = 1
            self.n_tiles = batch // self.nb
        self.rows = batch * seq
        self.tile_off = row_off // TM

    def x_map(self):
        if self.nb == 1:
            tpb = self.tiles_per_batch
            return lambda i, *_: (i // tpb, i % tpb, 0)
        return lambda i, *_: (i, 0, 0)

    def mod_map(self):
        assert self.mod_off % self.nb == 0
        off = self.mod_off // self.nb
        if self.nb == 1:
            tpb = self.tiles_per_batch
            return lambda i, *_: (i // tpb + off, 0, 0)
        return lambda i, *_: (i + off, 0, 0)


def _mod_rows(mod_ref, idx):
    return mod_ref[:, idx:idx + 1, :]


def _rmsnorm_mod(x, gain, scale, shift):
    ms = jnp.mean(x * x, axis=-1, keepdims=True)
    y = x * lax.rsqrt(ms + EPS) * gain
    return y * (1.0 + scale) + shift


def _in_proj_kernel(x_ref, mod_ref, n1_ref, wq_ref, wg_ref, wu_ref, gup_ref, gb_ref,
                    qkvr_ref, la_ref, u_ref):
    x = x_ref[...]
    h = _rmsnorm_mod(x, n1_ref[...], _mod_rows(mod_ref, 1), _mod_rows(mod_ref, 0))
    hb = h.reshape(TM, D_MODEL).astype(BF16)
    qkvr_ref[...] = jnp.dot(hb, wq_ref[...], preferred_element_type=F32)
    u_ref[...] = jnp.dot(hb, wu_ref[...], preferred_element_type=F32)
    g_lr = jnp.dot(hb, wg_ref[...], preferred_element_type=F32)
    pre = jnp.dot(g_lr.astype(BF16), gup_ref[...], preferred_element_type=F32) + gb_ref[...]
    log_sig = jnp.minimum(pre, 0.0) - jnp.log1p(jnp.exp(-jnp.abs(pre)))
    la_ref[...] = log_sig / GATE_TEMP


def _in_proj(grp, x, mod, norm1, wq, wg, wu, gup, gb):
    const = lambda i: (0, 0)
    row = lambda i: (i, 0)
    return pl.pallas_call(
        _in_proj_kernel,
        name="in_proj",
        grid=(grp.n_tiles,),
        in_specs=[pl.BlockSpec((grp.nb, grp.tt, D_MODEL), grp.x_map()),
                  pl.BlockSpec((grp.nb, 6, D_MODEL), grp.mod_map()),
                  pl.BlockSpec((1, 1, D_MODEL), lambda i: (0, 0, 0)),
                  pl.BlockSpec(wq.shape, const, pipeline_mode=pl.Buffered(1)),
                  pl.BlockSpec(wg.shape, const, pipeline_mode=pl.Buffered(1)),
                  pl.BlockSpec(wu.shape, const, pipeline_mode=pl.Buffered(1)),
                  pl.BlockSpec(gup.shape, const, pipeline_mode=pl.Buffered(1)),
                  pl.BlockSpec(gb.shape, const, pipeline_mode=pl.Buffered(1))],
        out_specs=[pl.BlockSpec((TM, QKVR_WIDTH), row),
                   pl.BlockSpec((TM, GLA_KEY_WIDTH), row),
                   pl.BlockSpec((TM, POOL_WIDTH), row)],
        out_shape=[jax.ShapeDtypeStruct((grp.rows, QKVR_WIDTH), F32),
                   jax.ShapeDtypeStruct((grp.rows, GLA_KEY_WIDTH), F32),
                   jax.ShapeDtypeStruct((grp.rows, POOL_WIDTH), F32)],
        compiler_params=_cparams(1),
    )(x, mod, norm1.reshape(1, 1, D_MODEL), wq, wg, wu, gup, gb)


def _gla_select_matrix(chunk, sub):
    r = jnp.arange(sub * GLA_DK, dtype=jnp.int32)[:, None] // GLA_DK
    l = jnp.arange(LANES, dtype=jnp.int32)[None, :]
    return ((l % sub == r) & (l < chunk)).astype(BF16)


def _gla_chunk(q4, k4, v4, g4, states, chunk, sub, wsel, t_ref):
    n_sub = chunk // sub
    rows = lax.broadcasted_iota(jnp.int32, (chunk, chunk), 0)
    cols = lax.broadcasted_iota(jnp.int32, (chunk, chunk), 1)
    tri = (rows >= cols).astype(BF16)
    b4 = _dot_exact_lhs(tri, g4)
    q4 = q4 * (GLA_DK ** -0.5)
    nt = (((1,), (1,)), ((), ()))
    tn = (((0,), (0,)), ((), ()))
    key_row = lax.broadcasted_iota(jnp.int32, (chunk, LANES), 0)

    for h in range(GLA_HEADS):
        hs = slice(h * GLA_DK, (h + 1) * GLA_DK)
        q, k, b = q4[:, hs], k4[:, hs], b4[:, hs]
        for s in range(n_sub):
            lo = s * sub
            r0 = (h * n_sub + s) * sub
            q_s, b_s = q[lo:lo + sub, :], b[lo:lo + sub, :]
            for jl in range(sub):
                j = lo + jl
                decay = jnp.exp(jnp.minimum(b_s - b[j:j + 1, :], 0.0))
                t_ref[r0:r0 + sub, jl * GLA_DK:(jl + 1) * GLA_DK] = (q_s * k[j:j + 1, :] * decay).astype(t_ref.dtype)
    p_diag = jnp.dot(t_ref[...].astype(BF16), wsel, preferred_element_type=F32)

    lane = lax.broadcasted_iota(jnp.int32, (sub, chunk), 1)
    row = lax.broadcasted_iota(jnp.int32, (sub, chunk), 0)
    outs, new_states = [], []
    for h in range(GLA_HEADS):
        hs = slice(h * GLA_DK, (h + 1) * GLA_DK)
        vs = slice(h * GLA_DV, (h + 1) * GLA_DV)
        q, k, b, v, state = q4[:, hs], k4[:, hs], b4[:, hs], v4[:, vs], states[h]
        o = _bdot(q * jnp.exp(b), state)

        b_last = b[chunk - 1:chunk, :]
        k_dec = k * jnp.exp(b_last - b)
        decay_col = jnp.exp(b[chunk - 8:chunk, :]).T[:, 7:8]
        new_states.append(decay_col * state + lax.dot_general(
            k_dec.astype(BF16), v.astype(BF16), tn, preferred_element_type=F32))

        p_blocks = []
        for s in range(n_sub):
            lo = s * sub
            r0 = (h * n_sub + s) * sub
            in_block = (lane >= lo) & (lane - lo <= row)
            p = jnp.where(in_block, p_diag[r0:r0 + sub, :chunk], 0.0)
            if s > 0:
                ref_row = b[lo - 1:lo, :]
                q_rel = q[lo:lo + sub, :] * jnp.exp(b[lo:lo + sub, :] - ref_row)
                k_rel = k * jnp.exp(jnp.where(key_row < lo, ref_row - b, NEG))
                p = p + lax.dot_general(q_rel.astype(BF16), k_rel.astype(BF16), nt, preferred_element_type=F32)
            p_blocks.append(p)
        p_full = p_blocks[0] if n_sub == 1 else jnp.concatenate(p_blocks, axis=0)
        outs.append(o + _bdot(p_full, v))
    return jnp.concatenate(outs, axis=1), new_states


def _gla_prompt_kernel(q_ref, k_ref, v_ref, la_ref, wsel_ref, o_ref, s_ref, t_ref):
    @pl.when(pl.program_id(1) == 0)
    def _():
        s_ref[...] = jnp.zeros_like(s_ref)

    def body(c, carry):
        r0 = pl.multiple_of(c * GLA_CHUNK, GLA_CHUNK)
        sl = pl.ds(r0, GLA_CHUNK)
        states = [s_ref[0, h] for h in range(GLA_HEADS)]
        o, new_states = _gla_chunk(q_ref[sl, :], k_ref[sl, :], v_ref[sl, :], la_ref[sl, :], states,
                                   GLA_CHUNK, GLA_SUB, wsel_ref[...], t_ref)
        o_ref[sl, :] = o
        for h in range(GLA_HEADS):
            s_ref[0, h] = new_states[h]
        return carry

    lax.fori_loop(0, GLA_STEP // GLA_CHUNK, body, 0)


def _gla_prompt(batch, seq, qkvr, log_a):
    steps = seq // GLA_STEP
    row = lambda b, s: b * steps + s
    wsel = _gla_select_matrix(GLA_CHUNK, GLA_SUB)
    return pl.pallas_call(
        _gla_prompt_kernel,
        name="gla_prompt",
        grid=(batch, steps),
        in_specs=[pl.BlockSpec((GLA_STEP, GLA_KEY_WIDTH), lambda b, s: (row(b, s), 0)),
                  pl.BlockSpec((GLA_STEP, GLA_KEY_WIDTH), lambda b, s: (row(b, s), 1)),
                  pl.BlockSpec((GLA_STEP, GLA_WIDTH), lambda b, s: (row(b, s), 1)),
                  pl.BlockSpec((GLA_STEP, GLA_KEY_WIDTH), lambda b, s: (row(b, s), 0)),
                  pl.BlockSpec(wsel.shape, lambda b, s: (0, 0))],
        out_specs=[pl.BlockSpec((GLA_STEP, GLA_WIDTH), lambda b, s: (row(b, s), 0)),
                   pl.BlockSpec((1, GLA_HEADS, GLA_DK, GLA_DV), lambda b, s: (b, 0, 0, 0))],
        out_shape=[jax.ShapeDtypeStruct((batch * seq, GLA_WIDTH), F32),
                   jax.ShapeDtypeStruct((batch, GLA_HEADS, GLA_DK, GLA_DV), F32)],
        scratch_shapes=[pltpu.VMEM((GLA_HEADS * GLA_CHUNK, GLA_SUB * GLA_DK), BF16)],
        compiler_params=_cparams(2),
    )(qkvr, qkvr, qkvr, log_a, wsel)


GLA_DEC_BB = 8


def _gla_decode_kernel(seq, q_ref, k_ref, v_ref, la_ref, wsel_ref, s0_ref, o_ref, s_ref, t_ref):
    def body(i, carry):
        r0 = pl.multiple_of(i * seq, seq)
        sl = pl.ds(r0, seq)
        states = [s0_ref[i, h] for h in range(GLA_HEADS)]
        o, new_states = _gla_chunk(q_ref[sl, :], k_ref[sl, :], v_ref[sl, :], la_ref[sl, :], states,
                                   seq, seq, wsel_ref[...], t_ref)
        o_ref[sl, :] = o
        for h in range(GLA_HEADS):
            s_ref[i, h] = new_states[h]
        return carry

    lax.fori_loop(0, GLA_DEC_BB, body, 0)


def _gla_decode(batch, seq, qkvr, log_a, state):
    rows = GLA_DEC_BB * seq
    wsel = _gla_select_matrix(seq, seq)
    state_spec = pl.BlockSpec((GLA_DEC_BB, GLA_HEADS, GLA_DK, GLA_DV), lambda i: (i, 0, 0, 0))
    return pl.pallas_call(
        functools.partial(_gla_decode_kernel, seq),
        name="gla_decode",
        grid=(batch // GLA_DEC_BB,),
        in_specs=[pl.BlockSpec((rows, GLA_KEY_WIDTH), lambda i: (i, 0)),
                  pl.BlockSpec((rows, GLA_KEY_WIDTH), lambda i: (i, 1)),
                  pl.BlockSpec((rows, GLA_WIDTH), lambda i: (i, 1)),
                  pl.BlockSpec((rows, GLA_KEY_WIDTH), lambda i: (i, 0)),
                  pl.BlockSpec(wsel.shape, lambda i: (0, 0)),
                  state_spec],
        out_specs=[pl.BlockSpec((rows, GLA_WIDTH), lambda i: (i, 0)), state_spec],
        out_shape=[jax.ShapeDtypeStruct((batch * seq, GLA_WIDTH), F32),
                   jax.ShapeDtypeStruct((batch, GLA_HEADS, GLA_DK, GLA_DV), F32)],
        scratch_shapes=[pltpu.VMEM((GLA_HEADS * seq, seq * GLA_DK), F32)],
        compiler_params=_cparams(1),
    )(qkvr, qkvr, qkvr, log_a, wsel, state)


def _route(logits):
    lt = logits.T
    n = lt.shape[1]
    big = jnp.int32(10 ** 6)
    row8 = lax.broadcasted_iota(jnp.int32, (EXPERTS_PER_GROUP, n), 0)
    lg = jnp.where(row8 < N_GROUPS, lt[0:EXPERTS_PER_GROUP], NEG)
    mg = jnp.max(lg, axis=0, keepdims=True)
    g_idx = jnp.min(jnp.where(lg == mg, row8, big), axis=0, keepdims=True)
    p_sel = 1.0 / jnp.sum(jnp.exp(lg - mg), axis=0, keepdims=True)
    le = jnp.zeros((EXPERTS_PER_GROUP, n), F32)
    for g in range(N_GROUPS):
        lo = EXPERTS_PER_GROUP * (g + 1)
        le = jnp.where(g_idx == g, lt[lo:lo + EXPERTS_PER_GROUP], le)
    m1 = jnp.max(le, axis=0, keepdims=True)
    i1 = jnp.min(jnp.where(le == m1, row8, big), axis=0, keepdims=True)
    rest = row8 != i1
    m2 = jnp.max(jnp.where(rest, le, NEG), axis=0, keepdims=True)
    i2 = jnp.min(jnp.where(rest & (le == m2), row8, big), axis=0, keepdims=True)
    e2 = jnp.exp(m2 - m1)
    w1 = p_sel / (1.0 + e2)
    w2 = p_sel * e2 / (1.0 + e2)
    ex1 = (g_idx * EXPERTS_PER_GROUP + i1).astype(F32)
    ex2 = (g_idx * EXPERTS_PER_GROUP + i2).astype(F32)
    packed = jnp.where(row8 == 0, ex1, jnp.where(row8 == 1, ex2,
                       jnp.where(row8 == 2, w1, jnp.where(row8 == 3, w2, 0.0))))
    full = jnp.concatenate([packed, jnp.zeros((LANES - EXPERTS_PER_GROUP, n), F32)], axis=0)
    return full.T


def _mix_out_kernel(grp, pos0, zero_first_halo, n_alias,
                    o_ref, r_ref, u_ref, halo_ref, x_ref, mod_ref, n2_ref, gn_ref, pw_ref, ps_ref,
                    wo_ref, wr_ref, br_ref, *rest):
    x1_ref, h2_ref, rt_ref, ext_ref, lvl_a, lvl_b, ymix_ref = rest[n_alias:]
    i = pl.program_id(0)

    @pl.when(i < grp.n_tiles)
    def _():
        _mix_out_tile(grp, pos0, zero_first_halo, i, o_ref, r_ref, u_ref, halo_ref, x_ref, mod_ref, n2_ref,
                      gn_ref, pw_ref, ps_ref, wo_ref, wr_ref, br_ref, x1_ref, h2_ref, rt_ref, ext_ref,
                      (lvl_a, lvl_b), ymix_ref)

    @pl.when(i >= grp.n_tiles)
    def _():
        h2_ref[...] = jnp.zeros_like(h2_ref)
        rt_ref[...] = jnp.zeros_like(rt_ref)


def _mix_out_tile(grp, pos0, zero_first_halo, i, o_ref, r_ref, u_ref, halo_ref, x_ref, mod_ref, n2_ref,
                  gn_ref, pw_ref, ps_ref, wo_ref, wr_ref, br_ref, x1_ref, h2_ref, rt_ref, ext_ref, lvl_refs,
                  ymix_ref):
    nb, tt = grp.nb, grp.tt
    hist = halo_ref.shape[-2]

    for h in range(GLA_HEADS):
        cs = slice(h * GLA_DV, (h + 1) * GLA_DV)
        oh = o_ref[:, cs]
        ms = jnp.mean(oh * oh, axis=-1, keepdims=True)
        yh = oh * lax.rsqrt(ms + EPS) * gn_ref[:, cs] * _silu(r_ref[:, cs])
        ymix_ref[:, cs] = yh.astype(BF16)

    halo = halo_ref[...]
    if zero_first_halo:
        halo = jnp.where(i % grp.tiles_per_batch == 0, 0.0, halo)
    n_ext = EXT_TOK0 + tt
    ext_ref[:, 0:EXT_TOK0 - hist, :] = jnp.zeros((nb, EXT_TOK0 - hist, POOL_WIDTH), F32)
    ext_ref[:, EXT_TOK0 - hist:EXT_TOK0, :] = halo.reshape(nb, hist, POOL_WIDTH)
    u = u_ref[...].reshape(nb, tt, POOL_WIDTH)
    ext_ref[:, EXT_TOK0:n_ext, :] = u
    for lvl_ref in lvl_refs:
        lvl_ref[:, 0:EXT_PAD, :] = jnp.zeros((nb, EXT_PAD, POOL_GW), F32)
    t_idx = lax.broadcasted_iota(jnp.int32, (nb, tt, POOL_GW), 1)
    if grp.nb == 1:
        pos = (i % grp.tiles_per_batch) * TM + t_idx + pos0
    else:
        pos = t_idx + pos0
    for gi, w in enumerate(POOL_WINDOWS):
        cs = slice(gi * POOL_GW, (gi + 1) * POOL_GW)
        cur = lambda lo, hi: ext_ref[:, lo:hi, cs]
        d, level = 1, 0
        while d < w:
            nxt = lvl_refs[level % 2]
            nxt[:, EXT_PAD:n_ext, :] = cur(EXT_PAD, n_ext) + cur(EXT_PAD - d, n_ext - d)
            cur = lambda lo, hi, ref=nxt: ref[:, lo:hi, :]
            d, level = 2 * d, level + 1
        acc = cur(EXT_TOK0, n_ext)
        cnt = jnp.minimum(pos + 1, w).astype(F32)
        pooled = acc / cnt - u[:, :, cs]
        yp = _bdot(pooled.reshape(TM, POOL_GW), pw_ref[gi]) * ps_ref[:, cs]
        ymix_ref[:, GLA_WIDTH + gi * POOL_GW:GLA_WIDTH + (gi + 1) * POOL_GW] = yp.astype(BF16)

    y = jnp.dot(ymix_ref[...], wo_ref[...], preferred_element_type=F32)
    x1 = x_ref[...] + _mod_rows(mod_ref, 2) * y.reshape(nb, tt, D_MODEL)
    x1_ref[...] = x1
    h2 = _rmsnorm_mod(x1, n2_ref[...], _mod_rows(mod_ref, 4), _mod_rows(mod_ref, 3)).reshape(TM, D_MODEL)
    h2_ref[...] = h2
    logits = _dot_3pass(h2, wr_ref[0], wr_ref[1]) + br_ref[...]
    rt_ref[...] = _route(logits)


def _mix_out(grp, n_tok, pos0, zero_first_halo, o, qkvr, u, halo_src, halo_block, halo_map, x, mod, norm2,
             gla_norm, pool_w, pool_scale, w_out, w_router, b_router, shared=()):
    n_alias = len(shared)
    n = grp.n_tiles
    n_fill = 0 if shared else n_tok // TM - n
    assert n_fill == 0 or grp.tile_off == 0
    clamp = lambda f: (lambda i: f(jnp.minimum(i, n - 1)))
    const2 = lambda i: (0, 0)
    row = clamp(lambda i: (i, 0))
    off = grp.tile_off
    kern = functools.partial(_mix_out_kernel, grp, pos0, zero_first_halo, n_alias)
    return pl.pallas_call(
        kern,
        name="mix_out",
        grid=(n + n_fill,),
        in_specs=[pl.BlockSpec((TM, GLA_WIDTH), row),
                  pl.BlockSpec((TM, GLA_WIDTH), clamp(lambda i: (i, 2))),
                  pl.BlockSpec((TM, POOL_WIDTH), row),
                  pl.BlockSpec(halo_block, clamp(halo_map)),
                  pl.BlockSpec((grp.nb, grp.tt, D_MODEL), clamp(grp.x_map())),
                  pl.BlockSpec((grp.nb, 6, D_MODEL), clamp(grp.mod_map())),
                  pl.BlockSpec((1, 1, D_MODEL), lambda i: (0, 0, 0)),
                  pl.BlockSpec((1, GLA_WIDTH), const2),
                  pl.BlockSpec(pool_w.shape, lambda i: (0, 0, 0), pipeline_mode=pl.Buffered(1)),
                  pl.BlockSpec((1, POOL_WIDTH), const2),
                  pl.BlockSpec(w_out.shape, const2, pipeline_mode=pl.Buffered(1)),
                  pl.BlockSpec(w_router.shape, lambda i: (0, 0, 0), pipeline_mode=pl.Buffered(1)),
                  pl.BlockSpec((1, LANES), const2)]
                 + [pl.BlockSpec(memory_space=pl.ANY)] * n_alias,
        out_specs=[pl.BlockSpec((grp.nb, grp.tt, D_MODEL), clamp(grp.x_map())),
                   pl.BlockSpec((TM, D_MODEL), lambda i: (i + off, 0)),
                   pl.BlockSpec((TM, LANES), lambda i: (i + off, 0))],
        out_shape=[jax.ShapeDtypeStruct(x.shape, F32),
                   jax.ShapeDtypeStruct((n_tok, D_MODEL), F32),
                   jax.ShapeDtypeStruct((n_tok, LANES), F32)],
        scratch_shapes=[pltpu.VMEM((grp.nb, EXT_TOK0 + grp.tt, POOL_WIDTH), F32),
                        pltpu.VMEM((grp.nb, EXT_TOK0 + grp.tt, POOL_GW), F32),
                        pltpu.VMEM((grp.nb, EXT_TOK0 + grp.tt, POOL_GW), F32),
                        pltpu.VMEM((TM, D_MODEL), BF16)],
        input_output_aliases={13 + k: 1 + k for k in range(n_alias)},
        compiler_params=_cparams(1),
    )(o, qkvr, u, halo_src, x, mod, norm2.reshape(1, 1, D_MODEL), gla_norm.reshape(1, GLA_WIDTH),
      pool_w, pool_scale.reshape(1, POOL_WIDTH), w_out, w_router, b_router, *shared)


def _row_copy(src_hbm, src_row, dst, dst_row, sem):
    return pltpu.make_async_copy(src_hbm.at[pl.ds(src_row, 1), :], dst.at[pl.ds(dst_row, 1), :], sem)


def _tile_wait(src_hbm, dst, sem):
    pltpu.make_async_copy(src_hbm.at[pl.ds(0, dst.shape[0]), :], dst, sem).wait()


def _dispatch_kernel(src_ref, nu_ref, h_hbm, hs_ref, buf, sem):
    i = pl.program_id(0)
    n_used = (nu_ref[0] * TME + TD - 1) // TD
    slot = i % 2

    def issue(tile, slot_):
        base = tile * TD

        def body(r, carry):
            _row_copy(h_hbm, src_ref[base + 2 * r], buf.at[slot_], 2 * r, sem.at[slot_]).start(priority=0)
            _row_copy(h_hbm, src_ref[base + 2 * r + 1], buf.at[slot_], 2 * r + 1, sem.at[slot_]).start(priority=1)
            return carry

        lax.fori_loop(0, TD // 2, body, 0, unroll=4)

    @pl.when(i == 0)
    def _():
        issue(0, 0)

    @pl.when(i + 1 < n_used)
    def _():
        issue(i + 1, 1 - slot)

    @pl.when(i < n_used)
    def _():
        _tile_wait(h_hbm, buf.at[slot], sem.at[slot])
        hs_ref[...] = buf[slot].astype(BF16)

    @pl.when(i >= n_used)
    def _():
        hs_ref[...] = jnp.zeros_like(hs_ref)


def _dispatch(plan, h2_all):
    n_sorted = plan["src_row"].shape[0]
    assert n_sorted % TD == 0
    grid_spec = pltpu.PrefetchScalarGridSpec(
        num_scalar_prefetch=2,
        grid=(n_sorted // TD,),
        in_specs=[pl.BlockSpec(memory_space=pl.ANY)],
        out_specs=pl.BlockSpec((TD, D_MODEL), lambda i, *_: (i, 0)),
        scratch_shapes=[pltpu.VMEM((2, TD, D_MODEL), F32), pltpu.SemaphoreType.DMA((2,))],
    )
    return pl.pallas_call(
        _dispatch_kernel,
        name="dispatch",
        grid_spec=grid_spec,
        out_shape=jax.ShapeDtypeStruct((n_sorted, D_MODEL), BF16),
        compiler_params=_cparams(1),
    )(plan["src_row"], plan["n_used"], h2_all)


def _moe_kernel(te_ref, nu_ref, seg_ref, nxt_ref, hs_ref, w1_hbm, w3_hbm, w2_hbm, y_ref,
                wf1, wf3, wf2, wsem, w1b, w3b, w2b):
    i = pl.program_id(0)
    n_used = nu_ref[0]

    def weight_copies(expert, wslot):
        return (pltpu.make_async_copy(w1_hbm.at[expert], wf1.at[wslot], wsem.at[wslot]),
                pltpu.make_async_copy(w3_hbm.at[expert], wf3.at[wslot], wsem.at[wslot]),
                pltpu.make_async_copy(w2_hbm.at[expert], wf2.at[wslot], wsem.at[wslot]))

    @pl.when(i == 0)
    def _():
        for c in weight_copies(te_ref[0], 0):
            c.start()

    prev = jnp.maximum(i - 1, 0)

    @pl.when((i < n_used) & ((i == 0) | (te_ref[i] != te_ref[prev])))
    def _():
        wslot = seg_ref[i] % 2
        for c in weight_copies(te_ref[i], wslot):
            c.wait()
        w1b[...] = wf1[wslot].astype(BF16)
        w3b[...] = wf3[wslot].astype(BF16)
        w2b[...] = wf2[wslot].astype(BF16)

        @pl.when(nxt_ref[i] >= 0)
        def _():
            for c in weight_copies(nxt_ref[i], 1 - wslot):
                c.start()

    @pl.when(i < n_used)
    def _():
        x = hs_ref[...]
        a = jnp.dot(x, w1b[...], preferred_element_type=F32)
        b = jnp.dot(x, w3b[...], preferred_element_type=F32)
        hid = _silu(a) * b
        y_ref[...] = jnp.dot(hid.astype(BF16), w2b[...], preferred_element_type=F32)

    @pl.when(i >= n_used)
    def _():
        y_ref[...] = jnp.zeros_like(y_ref)


MOE_AHEAD = 2


def _moe_gather_kernel(te_ref, src_ref, nu_ref, seg_ref, nxt_ref, h_hbm, w1_hbm, w3_hbm, w2_hbm, y_ref,
                       buf0, buf1, buf2, sem, wf1, wf3, wf2, wsem, w1b, w3b, w2b):
    i = pl.program_id(0)
    n_used = nu_ref[0]
    bufs = (buf0, buf1, buf2)
    n_slots = MOE_AHEAD + 1
    n_parts = 4

    def issue(tile, slot, part):
        base = tile * TME
        per = TME // n_parts
        for r in range(part * per, (part + 1) * per):
            _row_copy(h_hbm, src_ref[base + r], bufs[slot], r, sem.at[slot]).start(priority=r % 2)

    def weight_copies(expert, wslot):
        return (pltpu.make_async_copy(w1_hbm.at[expert], wf1.at[wslot], wsem.at[wslot]),
                pltpu.make_async_copy(w3_hbm.at[expert], wf3.at[wslot], wsem.at[wslot]),
                pltpu.make_async_copy(w2_hbm.at[expert], wf2.at[wslot], wsem.at[wslot]))

    def compute(slot, ahead):
        nxt = (lambda part: issue(i + MOE_AHEAD, (slot + MOE_AHEAD) % n_slots, part)) if ahead else (lambda part: None)
        _tile_wait(h_hbm, bufs[slot], sem.at[slot])
        nxt(0)
        x = bufs[slot][...].astype(BF16)
        nxt(1)
        a = jnp.dot(x, w1b[...], preferred_element_type=F32)
        nxt(2)
        b = jnp.dot(x, w3b[...], preferred_element_type=F32)
        nxt(3)
        hid = _silu(a) * b
        y_ref[...] = jnp.dot(hid.astype(BF16), w2b[...], preferred_element_type=F32)

    @pl.when(i == 0)
    def _():
        for c in weight_copies(te_ref[0], 0):
            c.start()
        for t in range(MOE_AHEAD):
            @pl.when(t < n_used)
            def _():
                for part in range(n_parts):
                    issue(t, t, part)

    prev = jnp.maximum(i - 1, 0)

    @pl.when((i < n_used) & ((i == 0) | (te_ref[i] != te_ref[prev])))
    def _():
        wslot = seg_ref[i] % 2
        for c in weight_copies(te_ref[i], wslot):
            c.wait()
        w1b[...] = wf1[wslot].astype(BF16)
        w3b[...] = wf3[wslot].astype(BF16)
        w2b[...] = wf2[wslot].astype(BF16)

        @pl.when(nxt_ref[i] >= 0)
        def _():
            for c in weight_copies(nxt_ref[i], 1 - wslot):
                c.start()

    for slot in range(n_slots):
        @pl.when((i + MOE_AHEAD < n_used) & (i % n_slots == slot))
        def _():
            compute(slot, True)

        @pl.when((i < n_used) & (i + MOE_AHEAD >= n_used) & (i % n_slots == slot))
        def _():
            compute(slot, False)

    @pl.when(i >= n_used)
    def _():
        y_ref[...] = jnp.zeros_like(y_ref)


def _moe_gather(plan, h2_all, w1, w3, w2):
    n_sorted = plan["src_row"].shape[0]
    grid_spec = pltpu.PrefetchScalarGridSpec(
        num_scalar_prefetch=5,
        grid=(n_sorted // TME,),
        in_specs=[pl.BlockSpec(memory_space=pl.ANY)] * 4,
        out_specs=pl.BlockSpec((TME, D_MODEL), lambda i, *_: (i, 0)),
        scratch_shapes=[pltpu.VMEM((TME, D_MODEL), F32)] * (MOE_AHEAD + 1)
                       + [pltpu.SemaphoreType.DMA((MOE_AHEAD + 1,)),
                          pltpu.VMEM((2, D_MODEL, EXPERT_FF), F32), pltpu.VMEM((2, D_MODEL, EXPERT_FF), F32),
                          pltpu.VMEM((2, EXPERT_FF, D_MODEL), F32),
                          pltpu.SemaphoreType.DMA((2,)),
                          pltpu.VMEM((D_MODEL, EXPERT_FF), BF16), pltpu.VMEM((D_MODEL, EXPERT_FF), BF16),
                          pltpu.VMEM((EXPERT_FF, D_MODEL), BF16)],
    )
    return pl.pallas_call(
        _moe_gather_kernel,
        name="moe",
        grid_spec=grid_spec,
        out_shape=jax.ShapeDtypeStruct((n_sorted, D_MODEL), F32),
        compiler_params=_cparams(1),
    )(plan["tile_expert"], plan["src_row"], plan["n_used"], plan["segment"], plan["next_expert"],
      h2_all, w1, w3, w2)


def _moe(plan, h_sorted, w1, w3, w2):
    n_sorted = h_sorted.shape[0]
    n_tiles = n_sorted // TME
    used = lambda i, te, nu, seg, nxt: (jnp.minimum(i, nu[0] - 1), 0)
    grid_spec = pltpu.PrefetchScalarGridSpec(
        num_scalar_prefetch=4,
        grid=(n_tiles,),
        in_specs=[pl.BlockSpec((TME, D_MODEL), used)] + [pl.BlockSpec(memory_space=pl.ANY)] * 3,
        out_specs=pl.BlockSpec((TME, D_MODEL), lambda i, *_: (i, 0)),
        scratch_shapes=[pltpu.VMEM((2, D_MODEL, EXPERT_FF), F32), pltpu.VMEM((2, D_MODEL, EXPERT_FF), F32),
                        pltpu.VMEM((2, EXPERT_FF, D_MODEL), F32),
                        pltpu.SemaphoreType.DMA((2,)),
                        pltpu.VMEM((D_MODEL, EXPERT_FF), BF16), pltpu.VMEM((D_MODEL, EXPERT_FF), BF16),
                        pltpu.VMEM((EXPERT_FF, D_MODEL), BF16)],
    )
    return pl.pallas_call(
        _moe_kernel,
        name="moe",
        grid_spec=grid_spec,
        out_shape=jax.ShapeDtypeStruct((n_sorted, D_MODEL), F32),
        compiler_params=_cparams(1),
    )(plan["tile_expert"], plan["n_used"], plan["segment"], plan["next_expert"], h_sorted, w1, w3, w2)


def _finish_kernel(grp, pos_ref, x1_ref, mod_ref, rt_ref, nf_ref, y_hbm, out_ref, buf_a, buf_b, sem):
    i = pl.program_id(0)
    n_steps = pl.num_programs(0)
    n_slots = FINISH_AHEAD + 1
    slot = i % n_slots

    def issue(tile, slot_):
        base = (tile * TM + grp.row_off) * 2

        def body(r, carry):
            _row_copy(y_hbm, pos_ref[base + 2 * r], buf_a.at[slot_], r, sem.at[slot_]).start(priority=0)
            _row_copy(y_hbm, pos_ref[base + 2 * r + 1], buf_b.at[slot_], r, sem.at[slot_]).start(priority=1)
            return carry

        lax.fori_loop(0, TM, body, 0, unroll=8)

    @pl.when(i == 0)
    def _():
        for t in range(FINISH_AHEAD):
            @pl.when(t < n_steps)
            def _():
                issue(t, t)

    @pl.when(i + FINISH_AHEAD < n_steps)
    def _():
        issue(i + FINISH_AHEAD, (i + FINISH_AHEAD) % n_slots)

    _tile_wait(y_hbm, buf_a.at[slot], sem.at[slot])
    _tile_wait(y_hbm, buf_b.at[slot], sem.at[slot])
    rt = rt_ref[...]
    moe = rt[:, 2:3] * buf_a[slot] + rt[:, 3:4] * buf_b[slot]
    x2 = x1_ref[...] + _mod_rows(mod_ref, 5) * moe.reshape(grp.nb, grp.tt, D_MODEL)
    ms = jnp.mean(x2 * x2, axis=-1, keepdims=True)
    out_ref[...] = x2 * lax.rsqrt(ms + EPS) * nf_ref[...]


def _finish(grp, pos, x1, mod, route_all, norm_f, y_sorted):
    off = grp.tile_off
    grid_spec = pltpu.PrefetchScalarGridSpec(
        num_scalar_prefetch=1,
        grid=(grp.n_tiles,),
        in_specs=[pl.BlockSpec((grp.nb, grp.tt, D_MODEL), grp.x_map()),
                  pl.BlockSpec((grp.nb, 6, D_MODEL), grp.mod_map()),
                  pl.BlockSpec((TM, LANES), lambda i, p: (i + off, 0)),
                  pl.BlockSpec((1, 1, D_MODEL), lambda i, p: (0, 0, 0)),
                  pl.BlockSpec(memory_space=pl.ANY)],
        out_specs=pl.BlockSpec((grp.nb, grp.tt, D_MODEL), grp.x_map()),
        scratch_shapes=[pltpu.VMEM((FINISH_AHEAD + 1, TM, D_MODEL), F32),
                        pltpu.VMEM((FINISH_AHEAD + 1, TM, D_MODEL), F32),
                        pltpu.SemaphoreType.DMA((FINISH_AHEAD + 1,))],
    )
    return pl.pallas_call(
        functools.partial(_finish_kernel, grp),
        name="finish",
        grid_spec=grid_spec,
        out_shape=jax.ShapeDtypeStruct(x1.shape, F32),
        compiler_params=_cparams(1),
    )(pos, x1, mod, route_all, norm_f.reshape(1, 1, D_MODEL), y_sorted)


def _sort_plan(route_all):
    n_tok = route_all.shape[0]
    n_pairs = 2 * n_tok
    n_sorted = n_pairs + N_EXPERTS * TME
    flat_e = route_all[:, 0:2].astype(jnp.int32).reshape(n_pairs)
    onehot = (flat_e[:, None] == jnp.arange(N_EXPERTS, dtype=jnp.int32)[None, :]).astype(jnp.int32)
    csum = jnp.cumsum(onehot, axis=0)
    rank = jnp.sum(onehot * csum, axis=1) - 1
    counts = csum[-1]
    padded = ((counts + TME - 1) // TME) * TME
    ends = jnp.cumsum(padded)
    starts = ends - padded
    pos = starts[flat_e] + rank
    token = jnp.arange(n_pairs, dtype=jnp.int32) // 2
    src_row = (jnp.arange(n_sorted, dtype=jnp.int32) % n_tok).at[pos].set(token)
    tile_start = jnp.arange(n_sorted // TME, dtype=jnp.int32) * TME
    tile_expert = jnp.sum((tile_start[:, None] >= ends[None, :]).astype(jnp.int32), axis=1)
    tile_expert = jnp.minimum(tile_expert, N_EXPERTS - 1)
    n_used = ends[-1] // TME
    is_first = jnp.concatenate([jnp.ones((1,), jnp.int32),
                                (tile_expert[1:] != tile_expert[:-1]).astype(jnp.int32)])
    segment = jnp.cumsum(is_first) - 1
    next_tile = ends[tile_expert] // TME
    next_expert = jnp.where(next_tile < n_used, tile_expert[jnp.minimum(next_tile, n_sorted // TME - 1)], -1)
    return dict(pos=pos.astype(jnp.int32), src_row=src_row, tile_expert=tile_expert.astype(jnp.int32),
                n_used=n_used.astype(jnp.int32).reshape(1), segment=segment.astype(jnp.int32),
                next_expert=next_expert.astype(jnp.int32))


def kernel(x_prompt, x_sample, c_prompt, c_sample, state_gla, state_pool, w_ada, b_ada, norm1, norm2, w_in,
           gate_up, gate_bias, gla_norm, pool_w, pool_scale, w_out, w_group, b_group, w_expert, b_expert,
           w1, w3, w2, norm_f):
    assert w_ada.shape[0] == 1, "single-layer step"
    bp, tp, _ = x_prompt.shape
    bs, ts, _ = x_sample.shape
    grp_p = _Group(bp, tp, 0, bs)
    grp_s = _Group(bs, ts, bp * tp, 0)
    n_tok = bp * tp + bs * ts

    n_c = bp + bs
    n_c_pad = -(-n_c // 8) * 8
    c_all = jnp.concatenate([c_sample, c_prompt, jnp.zeros((n_c_pad - n_c, D_MODEL), F32)], axis=0)
    mod = _adaln(c_all, w_ada[0], b_ada[0]).reshape(n_c_pad, 6, D_MODEL)
    mod_p = mod_s = mod

    wi = w_in[0]
    wq = wi[:, :QKVR_WIDTH].astype(BF16)
    wg = jnp.pad(wi[:, QKVR_WIDTH:QKVR_WIDTH + GATE_RANK], ((0, 0), (0, LANES - GATE_RANK))).astype(BF16)
    wu = wi[:, QKVR_WIDTH + GATE_RANK:].astype(BF16)
    gup = jnp.pad(gate_up[0], ((0, LANES - GATE_RANK), (0, 0))).astype(BF16)
    gb = gate_bias[0].reshape(1, GLA_KEY_WIDTH)
    pw = pool_w[0].astype(BF16)
    wo = w_out[0].astype(BF16)
    gap = EXPERTS_PER_GROUP - N_GROUPS
    tail = LANES - EXPERTS_PER_GROUP - N_EXPERTS
    w_router = jnp.concatenate(
        [w_group[0], jnp.zeros((D_MODEL, gap), F32),
         jnp.transpose(w_expert[0], (1, 0, 2)).reshape(D_MODEL, N_EXPERTS),
         jnp.zeros((D_MODEL, tail), F32)], axis=1)
    w_router_hi = w_router.astype(BF16)
    w_router = jnp.stack([w_router_hi, (w_router - w_router_hi.astype(F32)).astype(BF16)])
    b_router = jnp.concatenate([b_group[0], jnp.zeros((gap,), F32), b_expert[0].reshape(N_EXPERTS),
                                jnp.zeros((tail,), F32)]).reshape(1, LANES)

    qkvr_p, la_p, u_p = _in_proj(grp_p, x_prompt, mod_p, norm1[0], wq, wg, wu, gup, gb)
    qkvr_s, la_s, u_s = _in_proj(grp_s, x_sample, mod_s, norm1[0], wq, wg, wu, gup, gb)

    o_p, gla_p = _gla_prompt(bp, tp, qkvr_p, la_p)
    o_s, gla_s = _gla_decode(bs, ts, qkvr_s, la_s, state_gla.reshape(state_gla.shape[1:]))

    halo_per_tile = TM // HALO
    halo_map_p = lambda i: (jnp.maximum(i * halo_per_tile - 1, 0), 0)
    x1_p, h2_all, route_all = _mix_out(grp_p, n_tok, 0, True, o_p, qkvr_p, u_p, u_p, (HALO, POOL_WIDTH),
                                       halo_map_p, x_prompt, mod_p, norm2[0], gla_norm[0], pw, pool_scale[0],
                                       wo, w_router, b_router)
    x1_s, h2_all, route_all = _mix_out(grp_s, n_tok, PAST_LEN, False, o_s, qkvr_s, u_s,
                                       state_pool.reshape(bs, POOL_BUF, POOL_WIDTH),
                                       (grp_s.nb, POOL_BUF, POOL_WIDTH), lambda i: (i, 0, 0), x_sample, mod_s,
                                       norm2[0], gla_norm[0], pw, pool_scale[0], wo, w_router, b_router,
                                       shared=(h2_all, route_all))

    plan = _sort_plan(route_all)
    y_sorted = _moe_gather(plan, h2_all, w1.reshape(w1.shape[1:]), w3.reshape(w3.shape[1:]),
                           w2.reshape(w2.shape[1:]))

    y_p = _finish(grp_p, plan["pos"], x1_p, mod_p, route_all, norm_f, y_sorted)
    y_s = _finish(grp_s, plan["pos"], x1_s, mod_s, route_all, norm_f, y_sorted)

    u_p3 = u_p.reshape(bp, tp, POOL_WIDTH)
    u_s3 = u_s.reshape(bs, ts, POOL_WIDTH)
    assert tp >= POOL_BUF > ts
    pool_p = u_p3[:, tp - POOL_BUF:]
    pool_s = jnp.concatenate([state_pool.reshape(bs, POOL_BUF, POOL_WIDTH)[:, ts:], u_s3], axis=1)
    lead = lambda a: a.reshape((1,) + a.shape)
    return (y_p, y_s, lead(gla_p), lead(pool_p), lead(gla_s), lead(pool_s))
```

```python
import functools

import jax
import jax.numpy as jnp
from jax import lax
from jax.experimental import pallas as pl
from jax.experimental.pallas import tpu as pltpu

D_MODEL = 2048
GLA_HEADS = 4
GLA_DK = 128
GLA_DV = 256
GLA_KEY_WIDTH = GLA_HEADS * GLA_DK
GLA_WIDTH = GLA_HEADS * GLA_DV
POOL_WIDTH = 1024
POOL_WINDOWS = (2, 4, 8, 16)
POOL_GW = 256
POOL_BUF = 15
HALO = 16
EXT_PAD = 8
EXT_TOK0 = EXT_PAD + HALO
GATE_RANK = 16
GATE_TEMP = 16.0
N_GROUPS = 4
EXPERTS_PER_GROUP = 8
N_EXPERTS = 32
EXPERT_FF = 512
EPS = 1e-6
PAST_LEN = 16384
QKVR_WIDTH = 2 * GLA_KEY_WIDTH + 2 * GLA_WIDTH

LANES = 128
TM = 256
TME = 256
TD = 1024
FINISH_AHEAD = 3
GLA_CHUNK = 64
GLA_SUB = 8
GLA_STEP = 512
GLA_TRIP = 2
VMEM_LIMIT = 56 * 1024 * 1024

BF16 = jnp.bfloat16
F32 = jnp.float32
NEG = -1e30
LOG2E = 1.4426950408889634


def _cparams(n_axes):
    return pltpu.CompilerParams(dimension_semantics=("arbitrary",) * n_axes,
                                vmem_limit_bytes=VMEM_LIMIT)


def _silu(x):
    return x / (1.0 + jnp.exp(-x))


def _bdot(a, b):
    return jnp.dot(a.astype(BF16), b.astype(BF16), preferred_element_type=F32)


def _split3(a):
    a1 = a.astype(BF16)
    r1 = a - a1.astype(F32)
    a2 = r1.astype(BF16)
    a3 = (r1 - a2.astype(F32)).astype(BF16)
    return a1, a2, a3


def _split2(a):
    hi = a.astype(BF16)
    lo = (a - hi.astype(F32)).astype(BF16)
    return hi, lo


def _dot_3pass(a, b_hi, b_lo):
    a_hi, a_lo = _split2(a)
    d = lambda x, y: jnp.dot(x, y, preferred_element_type=F32)
    return d(a_hi, b_hi) + (d(a_hi, b_lo) + d(a_lo, b_hi))


def _dot_exact_lhs(tri_bf16, g):
    g1, g2, g3 = _split3(g)
    d = lambda y: jnp.dot(tri_bf16, y, preferred_element_type=F32)
    return d(g1) + (d(g2) + d(g3))


def _adaln_kernel(c_ref, w_ref, b_ref, o_ref):
    c = c_ref[...]
    o_ref[...] = _bdot(_silu(c), w_ref[...]) + b_ref[...]


def _adaln(c_all, w_ada, b_ada):
    n, d = c_all.shape
    width = w_ada.shape[1]
    tn = 1024
    return pl.pallas_call(
        _adaln_kernel,
        name="adaln",
        grid=(width // tn,),
        in_specs=[pl.BlockSpec((n, d), lambda j: (0, 0)),
                  pl.BlockSpec((d, tn), lambda j: (0, j)),
                  pl.BlockSpec((1, tn), lambda j: (0, j))],
        out_specs=pl.BlockSpec((n, tn), lambda j: (0, j)),
        out_shape=jax.ShapeDtypeStruct((n, width), F32),
        compiler_params=_cparams(1),
    )(c_all, w_ada, b_ada.reshape(1, width))


class _Group:
    def __init__(self, batch, seq, row_off, mod_off):
        self.batch, self.seq, self.row_off, self.mod_off = batch, seq, row_off, mod_off
        if seq >= TM:
            assert seq % TM == 0
            self.nb, self.tt = 1, TM
            self.tiles_per_batch = seq // TM
            self.n_tiles = batch * self.tiles_per_batch
        else:
            assert TM % seq == 0 and batch % (TM // seq) == 0
            self.nb, self.tt = TM // seq, seq
            self.tiles_per_batch = 1
            self.n_tiles = batch // self.nb
        self.rows = batch * seq
        self.tile_off = row_off // TM

    def x_map(self):
        if self.nb == 1:
            tpb = self.tiles_per_batch
            return lambda i, *_: (i // tpb, i % tpb, 0)
        return lambda i, *_: (i, 0, 0)

    def mod_map(self):
        assert self.mod_off % self.nb == 0
        off = self.mod_off // self.nb
        if self.nb == 1:
            tpb = self.tiles_per_batch
            return lambda i, *_: (i // tpb + off, 0, 0)
        return lambda i, *_: (i + off, 0, 0)


def _mod_rows(mod_ref, idx):
    return mod_ref[:, idx:idx + 1, :]


def _rmsnorm_mod(x, gain, scale, shift):
    ms = jnp.mean(x * x, axis=-1, keepdims=True)
    y = x * lax.rsqrt(ms + EPS) * gain
    return y * (1.0 + scale) + shift


def _in_proj_kernel(x_ref, mod_ref, n1_ref, wq_ref, wg_ref, wu_ref, gup_ref, gb_ref,
                    qkvr_ref, la_ref, u_ref):
    x = x_ref[...]
    h = _rmsnorm_mod(x, n1_ref[...], _mod_rows(mod_ref, 1), _mod_rows(mod_ref, 0))
    hb = h.reshape(TM, D_MODEL).astype(BF16)
    qkvr_ref[...] = jnp.dot(hb, wq_ref[...], preferred_element_type=F32)
    u_ref[...] = jnp.dot(hb, wu_ref[...], preferred_element_type=F32)
    g_lr = jnp.dot(hb, wg_ref[...], preferred_element_type=F32)
    pre = jnp.dot(g_lr.astype(BF16), gup_ref[...], preferred_element_type=F32) + gb_ref[...]
    log_sig = jnp.minimum(pre, 0.0) - jnp.log1p(jnp.exp(-jnp.abs(pre)))
    la_ref[...] = log_sig / GATE_TEMP


def _in_proj(grp, x, mod, norm1, wq, wg, wu, gup, gb):
    const = lambda i: (0, 0)
    row = lambda i: (i, 0)
    return pl.pallas_call(
        _in_proj_kernel,
        name="in_proj",
        grid=(grp.n_tiles,),
        in_specs=[pl.BlockSpec((grp.nb, grp.tt, D_MODEL), grp.x_map()),
                  pl.BlockSpec((grp.nb, 6, D_MODEL), grp.mod_map()),
                  pl.BlockSpec((1, 1, D_MODEL), lambda i: (0, 0, 0)),
                  pl.BlockSpec(wq.shape, const, pipeline_mode=pl.Buffered(1)),
                  pl.BlockSpec(wg.shape, const, pipeline_mode=pl.Buffered(1)),
                  pl.BlockSpec(wu.shape, const, pipeline_mode=pl.Buffered(1)),
                  pl.BlockSpec(gup.shape, const, pipeline_mode=pl.Buffered(1)),
                  pl.BlockSpec(gb.shape, const, pipeline_mode=pl.Buffered(1))],
        out_specs=[pl.BlockSpec((TM, QKVR_WIDTH), row),
                   pl.BlockSpec((TM, GLA_KEY_WIDTH), row),
                   pl.BlockSpec((TM, POOL_WIDTH), row)],
        out_shape=[jax.ShapeDtypeStruct((grp.rows, QKVR_WIDTH), F32),
                   jax.ShapeDtypeStruct((grp.rows, GLA_KEY_WIDTH), F32),
                   jax.ShapeDtypeStruct((grp.rows, POOL_WIDTH), F32)],
        compiler_params=_cparams(1),
    )(x, mod, norm1.reshape(1, 1, D_MODEL), wq, wg, wu, gup, gb)


def _gla_select_matrix(chunk, sub):
    r = jnp.arange(sub * GLA_DK, dtype=jnp.int32)[:, None] // GLA_DK
    l = jnp.arange(LANES, dtype=jnp.int32)[None, :]
    return ((l % sub == r) & (l < chunk)).astype(BF16)


def _gla_chunks(chunks, states, chunk, sub, wsel, t_refs, kb_refs):
    n_sub = chunk // sub
    rows = lax.broadcasted_iota(jnp.int32, (chunk, chunk), 0)
    cols = lax.broadcasted_iota(jnp.int32, (chunk, chunk), 1)
    tri = (rows >= cols).astype(BF16)
    nt = (((1,), (1,)), ((), ()))
    tn = (((0,), (0,)), ((), ()))
    key_row = lax.broadcasted_iota(jnp.int32, (chunk, LANES), 0)
    lane = lax.broadcasted_iota(jnp.int32, (sub, chunk), 1)
    row = lax.broadcasted_iota(jnp.int32, (sub, chunk), 0)
    head = lambda a, h, w: a[:, h * w:(h + 1) * w]
    n = len(chunks)

    b4s = [_dot_exact_lhs(tri, g4) for (_, _, _, g4) in chunks]
    q4s = [q4 * (GLA_DK ** -0.5) for (q4, _, _, _) in chunks]

    for c in range(n):
        kb_ref, t_ref, k4 = kb_refs[c], t_refs[c], chunks[c][1]
        for h in range(GLA_HEADS):
            kb_ref[h] = head(k4, h, GLA_DK)
            kb_ref[GLA_HEADS + h] = head(b4s[c], h, GLA_DK) * LOG2E
        for h in range(GLA_HEADS):
            for s in range(n_sub):
                lo = s * sub
                r0 = (h * n_sub + s) * sub
                q_s, b_s = head(q4s[c], h, GLA_DK)[lo:lo + sub, :], kb_ref[GLA_HEADS + h, lo:lo + sub, :]
                for jl in range(sub):
                    k_j = jnp.broadcast_to(kb_ref[h, lo + jl:lo + jl + 1, :], (sub, GLA_DK))
                    b_j = jnp.broadcast_to(kb_ref[GLA_HEADS + h, lo + jl:lo + jl + 1, :], (sub, GLA_DK))
                    decay = jnp.exp2(jnp.minimum(b_s - b_j, 0.0))
                    t_ref[r0:r0 + sub, jl * GLA_DK:(jl + 1) * GLA_DK] = (q_s * k_j * decay).astype(t_ref.dtype)
    p_diags = [jnp.dot(t_refs[c][...].astype(BF16), wsel, preferred_element_type=F32) for c in range(n)]

    intra = []
    for c in range(n):
        per_head = []
        for h in range(GLA_HEADS):
            q, k, b = head(q4s[c], h, GLA_DK), head(chunks[c][1], h, GLA_DK), head(b4s[c], h, GLA_DK)
            p_blocks = []
            for s in range(n_sub):
                lo = s * sub
                r0 = (h * n_sub + s) * sub
                in_block = (lane >= lo) & (lane - lo <= row)
                p = jnp.where(in_block, p_diags[c][r0:r0 + sub, :chunk], 0.0)
                if s > 0:
                    ref_row = b[lo - 1:lo, :]
                    q_rel = q[lo:lo + sub, :] * jnp.exp(b[lo:lo + sub, :] - ref_row)
                    k_rel = k * jnp.exp(jnp.where(key_row < lo, ref_row - b, NEG))
                    p = p + lax.dot_general(q_rel.astype(BF16), k_rel.astype(BF16), nt,
                                            preferred_element_type=F32)
                p_blocks.append(p)
            p_full = p_blocks[0] if n_sub == 1 else jnp.concatenate(p_blocks, axis=0)
            per_head.append(_bdot(p_full, head(chunks[c][2], h, GLA_DV)))
        intra.append(per_head)

    outs = []
    for c in range(n):
        o_heads, new_states = [], []
        for h in range(GLA_HEADS):
            q, k, b = head(q4s[c], h, GLA_DK), head(chunks[c][1], h, GLA_DK), head(b4s[c], h, GLA_DK)
            v = head(chunks[c][2], h, GLA_DV)
            o_heads.append(intra[c][h] + _bdot(q * jnp.exp(b), states[h]))
            b_last = b[chunk - 1:chunk, :]
            k_dec = k * jnp.exp(b_last - b)
            decay_col = jnp.exp(b[chunk - 8:chunk, :]).T[:, 7:8]
            new_states.append(decay_col * states[h] + lax.dot_general(
                k_dec.astype(BF16), v.astype(BF16), tn, preferred_element_type=F32))
        states = new_states
        outs.append(jnp.concatenate(o_heads, axis=1))
    return outs, states


def _gla_prompt_kernel(q_ref, k_ref, v_ref, la_ref, wsel_ref, o_ref, s_ref, t_ref, kb_ref):
    @pl.when(pl.program_id(1) == 0)
    def _():
        s_ref[...] = jnp.zeros_like(s_ref)

    def body(pair, carry):
        states = [s_ref[0, h] for h in range(GLA_HEADS)]
        slices = [pl.ds(pl.multiple_of((GLA_TRIP * pair + c) * GLA_CHUNK, GLA_CHUNK), GLA_CHUNK)
                  for c in range(GLA_TRIP)]
        chunks = [(q_ref[sl, :], k_ref[sl, :], v_ref[sl, :], la_ref[sl, :]) for sl in slices]
        outs, states = _gla_chunks(chunks, states, GLA_CHUNK, GLA_SUB, wsel_ref[...],
                                   [t_ref.at[c] for c in range(GLA_TRIP)],
                                   [kb_ref.at[c] for c in range(GLA_TRIP)])
        for sl, o in zip(slices, outs):
            o_ref[sl, :] = o
        for h in range(GLA_HEADS):
            s_ref[0, h] = states[h]
        return carry

    lax.fori_loop(0, GLA_STEP // (GLA_TRIP * GLA_CHUNK), body, 0)


def _gla_prompt(batch, seq, qkvr, log_a):
    steps = seq // GLA_STEP
    row = lambda b, s: b * steps + s
    wsel = _gla_select_matrix(GLA_CHUNK, GLA_SUB)
    return pl.pallas_call(
        _gla_prompt_kernel,
        name="gla_prompt",
        grid=(batch, steps),
        in_specs=[pl.BlockSpec((GLA_STEP, GLA_KEY_WIDTH), lambda b, s: (row(b, s), 0)),
                  pl.BlockSpec((GLA_STEP, GLA_KEY_WIDTH), lambda b, s: (row(b, s), 1)),
                  pl.BlockSpec((GLA_STEP, GLA_WIDTH), lambda b, s: (row(b, s), 1)),
                  pl.BlockSpec((GLA_STEP, GLA_KEY_WIDTH), lambda b, s: (row(b, s), 0)),
                  pl.BlockSpec(wsel.shape, lambda b, s: (0, 0))],
        out_specs=[pl.BlockSpec((GLA_STEP, GLA_WIDTH), lambda b, s: (row(b, s), 0)),
                   pl.BlockSpec((1, GLA_HEADS, GLA_DK, GLA_DV), lambda b, s: (b, 0, 0, 0))],
        out_shape=[jax.ShapeDtypeStruct((batch * seq, GLA_WIDTH), F32),
                   jax.ShapeDtypeStruct((batch, GLA_HEADS, GLA_DK, GLA_DV), F32)],
        scratch_shapes=[pltpu.VMEM((GLA_TRIP, GLA_HEADS * GLA_CHUNK, GLA_SUB * GLA_DK), BF16),
                        pltpu.VMEM((GLA_TRIP, 2 * GLA_HEADS, GLA_CHUNK, GLA_DK), F32)],
        compiler_params=_cparams(2),
    )(qkvr, qkvr, qkvr, log_a, wsel)


GLA_DEC_BB = 8


def _gla_decode_kernel(seq, q_ref, k_ref, v_ref, la_ref, wsel_ref, s0_ref, o_ref, s_ref, t_ref, kb_ref):
    def body(i, carry):
        r0 = pl.multiple_of(i * seq, seq)
        sl = pl.ds(r0, seq)
        states = [s0_ref[i, h] for h in range(GLA_HEADS)]
        (o,), new_states = _gla_chunks([(q_ref[sl, :], k_ref[sl, :], v_ref[sl, :], la_ref[sl, :])], states,
                                       seq, seq, wsel_ref[...], [t_ref], [kb_ref])
        o_ref[sl, :] = o
        for h in range(GLA_HEADS):
            s_ref[i, h] = new_states[h]
        return carry

    lax.fori_loop(0, GLA_DEC_BB, body, 0)


def _gla_decode(batch, seq, qkvr, log_a, state):
    rows = GLA_DEC_BB * seq
    wsel = _gla_select_matrix(seq, seq)
    state_spec = pl.BlockSpec((GLA_DEC_BB, GLA_HEADS, GLA_DK, GLA_DV), lambda i: (i, 0, 0, 0))
    return pl.pallas_call(
        functools.partial(_gla_decode_kernel, seq),
        name="gla_decode",
        grid=(batch // GLA_DEC_BB,),
        in_specs=[pl.BlockSpec((rows, GLA_KEY_WIDTH), lambda i: (i, 0)),
                  pl.BlockSpec((rows, GLA_KEY_WIDTH), lambda i: (i, 1)),
                  pl.BlockSpec((rows, GLA_WIDTH), lambda i: (i, 1)),
                  pl.BlockSpec((rows, GLA_KEY_WIDTH), lambda i: (i, 0)),
                  pl.BlockSpec(wsel.shape, lambda i: (0, 0)),
                  state_spec],
        out_specs=[pl.BlockSpec((rows, GLA_WIDTH), lambda i: (i, 0)), state_spec],
        out_shape=[jax.ShapeDtypeStruct((batch * seq, GLA_WIDTH), F32),
                   jax.ShapeDtypeStruct((batch, GLA_HEADS, GLA_DK, GLA_DV), F32)],
        scratch_shapes=[pltpu.VMEM((GLA_HEADS * seq, seq * GLA_DK), F32),
                        pltpu.VMEM((2 * GLA_HEADS, seq, GLA_DK), F32)],
        compiler_params=_cparams(1),
    )(qkvr, qkvr, qkvr, log_a, wsel, state)


def _route(logits):
    lt = logits.T
    n = lt.shape[1]
    big = jnp.int32(10 ** 6)
    row8 = lax.broadcasted_iota(jnp.int32, (EXPERTS_PER_GROUP, n), 0)
    lg = jnp.where(row8 < N_GROUPS, lt[0:EXPERTS_PER_GROUP], NEG)
    mg = jnp.max(lg, axis=0, keepdims=True)
    g_idx = jnp.min(jnp.where(lg == mg, row8, big), axis=0, keepdims=True)
    p_sel = 1.0 / jnp.sum(jnp.exp(lg - mg), axis=0, keepdims=True)
    le = jnp.zeros((EXPERTS_PER_GROUP, n), F32)
    for g in range(N_GROUPS):
        lo = EXPERTS_PER_GROUP * (g + 1)
        le = jnp.where(g_idx == g, lt[lo:lo + EXPERTS_PER_GROUP], le)
    m1 = jnp.max(le, axis=0, keepdims=True)
    i1 = jnp.min(jnp.where(le == m1, row8, big), axis=0, keepdims=True)
    rest = row8 != i1
    m2 = jnp.max(jnp.where(rest, le, NEG), axis=0, keepdims=True)
    i2 = jnp.min(jnp.where(rest & (le == m2), row8, big), axis=0, keepdims=True)
    e2 = jnp.exp(m2 - m1)
    w1 = p_sel / (1.0 + e2)
    w2 = p_sel * e2 / (1.0 + e2)
    ex1 = (g_idx * EXPERTS_PER_GROUP + i1).astype(F32)
    ex2 = (g_idx * EXPERTS_PER_GROUP + i2).astype(F32)
    packed = jnp.where(row8 == 0, ex1, jnp.where(row8 == 1, ex2,
                       jnp.where(row8 == 2, w1, jnp.where(row8 == 3, w2, 0.0))))
    full = jnp.concatenate([packed, jnp.zeros((LANES - EXPERTS_PER_GROUP, n), F32)], axis=0)
    return full.T


def _mix_out_kernel(grp, pos0, zero_first_halo, n_alias,
                    o_ref, r_ref, u_ref, halo_ref, x_ref, mod_ref, n2_ref, gn_ref, pw_ref, ps_ref,
                    wo_ref, wr_ref, br_ref, *rest):
    x1_ref, h2_ref, rt_ref, ext_ref, lvl_a, lvl_b, ymix_ref = rest[n_alias:]
    i = pl.program_id(0)

    @pl.when(i < grp.n_tiles)
    def _():
        _mix_out_tile(grp, pos0, zero_first_halo, i, o_ref, r_ref, u_ref, halo_ref, x_ref, mod_ref, n2_ref,
                      gn_ref, pw_ref, ps_ref, wo_ref, wr_ref, br_ref, x1_ref, h2_ref, rt_ref, ext_ref,
                      (lvl_a, lvl_b), ymix_ref)

    @pl.when(i >= grp.n_tiles)
    def _():
        h2_ref[...] = jnp.zeros_like(h2_ref)
        rt_ref[...] = jnp.zeros_like(rt_ref)


def _mix_out_tile(grp, pos0, zero_first_halo, i, o_ref, r_ref, u_ref, halo_ref, x_ref, mod_ref, n2_ref,
                  gn_ref, pw_ref, ps_ref, wo_ref, wr_ref, br_ref, x1_ref, h2_ref, rt_ref, ext_ref, lvl_refs,
                  ymix_ref):
    nb, tt = grp.nb, grp.tt
    hist = halo_ref.shape[-2]

    for h in range(GLA_HEADS):
        cs = slice(h * GLA_DV, (h + 1) * GLA_DV)
        oh = o_ref[:, cs]
        ms = jnp.mean(oh * oh, axis=-1, keepdims=True)
        yh = oh * lax.rsqrt(ms + EPS) * gn_ref[:, cs] * _silu(r_ref[:, cs])
        ymix_ref[:, cs] = yh.astype(BF16)

    halo = halo_ref[...]
    if zero_first_halo:
        halo = jnp.where(i % grp.tiles_per_batch == 0, 0.0, halo)
    n_ext = EXT_TOK0 + tt
    ext_ref[:, 0:EXT_TOK0 - hist, :] = jnp.zeros((nb, EXT_TOK0 - hist, POOL_WIDTH), F32)
    ext_ref[:, EXT_TOK0 - hist:EXT_TOK0, :] = halo.reshape(nb, hist, POOL_WIDTH)
    u = u_ref[...].reshape(nb, tt, POOL_WIDTH)
    ext_ref[:, EXT_TOK0:n_ext, :] = u
    for lvl_ref in lvl_refs:
        lvl_ref[:, 0:EXT_PAD, :] = jnp.zeros((nb, EXT_PAD, POOL_GW), F32)
    t_idx = lax.broadcasted_iota(jnp.int32, (nb, tt, POOL_GW), 1)
    if grp.nb == 1:
        pos = (i % grp.tiles_per_batch) * TM + t_idx + pos0
    else:
        pos = t_idx + pos0
    for gi, w in enumerate(POOL_WINDOWS):
        cs = slice(gi * POOL_GW, (gi + 1) * POOL_GW)
        cur = lambda lo, hi: ext_ref[:, lo:hi, cs]
        d, level = 1, 0
        while d < w:
            nxt = lvl_refs[level % 2]
            nxt[:, EXT_PAD:n_ext, :] = cur(EXT_PAD, n_ext) + cur(EXT_PAD - d, n_ext - d)
            cur = lambda lo, hi, ref=nxt: ref[:, lo:hi, :]
            d, level = 2 * d, level + 1
        acc = cur(EXT_TOK0, n_ext)
        cnt = jnp.minimum(pos + 1, w).astype(F32)
        pooled = acc / cnt - u[:, :, cs]
        yp = _bdot(pooled.reshape(TM, POOL_GW), pw_ref[gi]) * ps_ref[:, cs]
        ymix_ref[:, GLA_WIDTH + gi * POOL_GW:GLA_WIDTH + (gi + 1) * POOL_GW] = yp.astype(BF16)

    y = jnp.dot(ymix_ref[...], wo_ref[...], preferred_element_type=F32)
    x1 = x_ref[...] + _mod_rows(mod_ref, 2) * y.reshape(nb, tt, D_MODEL)
    x1_ref[...] = x1
    h2 = _rmsnorm_mod(x1, n2_ref[...], _mod_rows(mod_ref, 4), _mod_rows(mod_ref, 3)).reshape(TM, D_MODEL)
    h2_ref[...] = h2
    logits = _dot_3pass(h2, wr_ref[0], wr_ref[1]) + br_ref[...]
    rt_ref[...] = _route(logits)


def _mix_out(grp, n_tok, pos0, zero_first_halo, o, qkvr, u, halo_src, halo_block, halo_map, x, mod, norm2,
             gla_norm, pool_w, pool_scale, w_out, w_router, b_router, shared=()):
    n_alias = len(shared)
    n = grp.n_tiles
    n_fill = 0 if shared else n_tok // TM - n
    assert n_fill == 0 or grp.tile_off == 0
    clamp = lambda f: (lambda i: f(jnp.minimum(i, n - 1)))
    const2 = lambda i: (0, 0)
    row = clamp(lambda i: (i, 0))
    off = grp.tile_off
    kern = functools.partial(_mix_out_kernel, grp, pos0, zero_first_halo, n_alias)
    return pl.pallas_call(
        kern,
        name="mix_out",
        grid=(n + n_fill,),
        in_specs=[pl.BlockSpec((TM, GLA_WIDTH), row),
                  pl.BlockSpec((TM, GLA_WIDTH), clamp(lambda i: (i, 2))),
                  pl.BlockSpec((TM, POOL_WIDTH), row),
                  pl.BlockSpec(halo_block, clamp(halo_map)),
                  pl.BlockSpec((grp.nb, grp.tt, D_MODEL), clamp(grp.x_map())),
                  pl.BlockSpec((grp.nb, 6, D_MODEL), clamp(grp.mod_map())),
                  pl.BlockSpec((1, 1, D_MODEL), lambda i: (0, 0, 0)),
                  pl.BlockSpec((1, GLA_WIDTH), const2),
                  pl.BlockSpec(pool_w.shape, lambda i: (0, 0, 0), pipeline_mode=pl.Buffered(1)),
                  pl.BlockSpec((1, POOL_WIDTH), const2),
                  pl.BlockSpec(w_out.shape, const2, pipeline_mode=pl.Buffered(1)),
                  pl.BlockSpec(w_router.shape, lambda i: (0, 0, 0), pipeline_mode=pl.Buffered(1)),
                  pl.BlockSpec((1, LANES), const2)]
                 + [pl.BlockSpec(memory_space=pl.ANY)] * n_alias,
        out_specs=[pl.BlockSpec((grp.nb, grp.tt, D_MODEL), clamp(grp.x_map())),
                   pl.BlockSpec((TM, D_MODEL), lambda i: (i + off, 0)),
                   pl.BlockSpec((TM, LANES), lambda i: (i + off, 0))],
        out_shape=[jax.ShapeDtypeStruct(x.shape, F32),
                   jax.ShapeDtypeStruct((n_tok, D_MODEL), F32),
                   jax.ShapeDtypeStruct((n_tok, LANES), F32)],
        scratch_shapes=[pltpu.VMEM((grp.nb, EXT_TOK0 + grp.tt, POOL_WIDTH), F32),
                        pltpu.VMEM((grp.nb, EXT_TOK0 + grp.tt, POOL_GW), F32),
                        pltpu.VMEM((grp.nb, EXT_TOK0 + grp.tt, POOL_GW), F32),
                        pltpu.VMEM((TM, D_MODEL), BF16)],
        input_output_aliases={13 + k: 1 + k for k in range(n_alias)},
        compiler_params=_cparams(1),
    )(o, qkvr, u, halo_src, x, mod, norm2.reshape(1, 1, D_MODEL), gla_norm.reshape(1, GLA_WIDTH),
      pool_w, pool_scale.reshape(1, POOL_WIDTH), w_out, w_router, b_router, *shared)


def _row_copy(src_hbm, src_row, dst, dst_row, sem):
    return pltpu.make_async_copy(src_hbm.at[pl.ds(src_row, 1), :], dst.at[pl.ds(dst_row, 1), :], sem)


def _tile_wait(src_hbm, dst, sem):
    pltpu.make_async_copy(src_hbm.at[pl.ds(0, dst.shape[0]), :], dst, sem).wait()


def _dispatch_kernel(src_ref, nu_ref, h_hbm, hs_ref, buf, sem):
    i = pl.program_id(0)
    n_used = (nu_ref[0] * TME + TD - 1) // TD
    slot = i % 2

    def issue(tile, slot_):
        base = tile * TD

        def body(r, carry):
            _row_copy(h_hbm, src_ref[base + 2 * r], buf.at[slot_], 2 * r, sem.at[slot_]).start(priority=0)
            _row_copy(h_hbm, src_ref[base + 2 * r + 1], buf.at[slot_], 2 * r + 1, sem.at[slot_]).start(priority=1)
            return carry

        lax.fori_loop(0, TD // 2, body, 0, unroll=4)

    @pl.when(i == 0)
    def _():
        issue(0, 0)

    @pl.when(i + 1 < n_used)
    def _():
        issue(i + 1, 1 - slot)

    @pl.when(i < n_used)
    def _():
        _tile_wait(h_hbm, buf.at[slot], sem.at[slot])
        hs_ref[...] = buf[slot].astype(BF16)

    @pl.when(i >= n_used)
    def _():
        hs_ref[...] = jnp.zeros_like(hs_ref)


def _dispatch(plan, h2_all):
    n_sorted = plan["src_row"].shape[0]
    assert n_sorted % TD == 0
    grid_spec = pltpu.PrefetchScalarGridSpec(
        num_scalar_prefetch=2,
        grid=(n_sorted // TD,),
        in_specs=[pl.BlockSpec(memory_space=pl.ANY)],
        out_specs=pl.BlockSpec((TD, D_MODEL), lambda i, *_: (i, 0)),
        scratch_shapes=[pltpu.VMEM((2, TD, D_MODEL), F32), pltpu.SemaphoreType.DMA((2,))],
    )
    return pl.pallas_call(
        _dispatch_kernel,
        name="dispatch",
        grid_spec=grid_spec,
        out_shape=jax.ShapeDtypeStruct((n_sorted, D_MODEL), BF16),
        compiler_params=_cparams(1),
    )(plan["src_row"], plan["n_used"], h2_all)


def _moe_kernel(te_ref, nu_ref, seg_ref, nxt_ref, hs_ref, w1_hbm, w3_hbm, w2_hbm, y_ref,
                wf1, wf3, wf2, wsem, w1b, w3b, w2b):
    i = pl.program_id(0)
    n_used = nu_ref[0]

    def weight_copies(expert, wslot):
        return (pltpu.make_async_copy(w1_hbm.at[expert], wf1.at[wslot], wsem.at[wslot]),
                pltpu.make_async_copy(w3_hbm.at[expert], wf3.at[wslot], wsem.at[wslot]),
                pltpu.make_async_copy(w2_hbm.at[expert], wf2.at[wslot], wsem.at[wslot]))

    @pl.when(i == 0)
    def _():
        for c in weight_copies(te_ref[0], 0):
            c.start()

    prev = jnp.maximum(i - 1, 0)

    @pl.when((i < n_used) & ((i == 0) | (te_ref[i] != te_ref[prev])))
    def _():
        wslot = seg_ref[i] % 2
        for c in weight_copies(te_ref[i], wslot):
            c.wait()
        w1b[...] = wf1[wslot].astype(BF16)
        w3b[...] = wf3[wslot].astype(BF16)
        w2b[...] = wf2[wslot].astype(BF16)

        @pl.when(nxt_ref[i] >= 0)
        def _():
            for c in weight_copies(nxt_ref[i], 1 - wslot):
                c.start()

    @pl.when(i < n_used)
    def _():
        x = hs_ref[...]
        a = jnp.dot(x, w1b[...], preferred_element_type=F32)
        b = jnp.dot(x, w3b[...], preferred_element_type=F32)
        hid = _silu(a) * b
        y_ref[...] = jnp.dot(hid.astype(BF16), w2b[...], preferred_element_type=F32)

    @pl.when(i >= n_used)
    def _():
        y_ref[...] = jnp.zeros_like(y_ref)


MOE_AHEAD = 2


def _moe_gather_kernel(te_ref, src_ref, nu_ref, seg_ref, nxt_ref, h_hbm, w1_hbm, w3_hbm, w2_hbm, y_ref,
                       buf0, buf1, buf2, sem, wf1, wf3, wf2, wsem, w1b, w3b, w2b):
    i = pl.program_id(0)
    n_used = nu_ref[0]
    bufs = (buf0, buf1, buf2)
    n_slots = MOE_AHEAD + 1
    n_parts = 4

    def issue(tile, slot, part):
        base = tile * TME
        per = TME // n_parts
        for r in range(part * per, (part + 1) * per):
            _row_copy(h_hbm, src_ref[base + r], bufs[slot], r, sem.at[slot]).start(priority=r % 2)

    def weight_copies(expert, wslot):
        return (pltpu.make_async_copy(w1_hbm.at[expert], wf1.at[wslot], wsem.at[wslot]),
                pltpu.make_async_copy(w3_hbm.at[expert], wf3.at[wslot], wsem.at[wslot]),
                pltpu.make_async_copy(w2_hbm.at[expert], wf2.at[wslot], wsem.at[wslot]))

    def compute(slot, ahead):
        nxt = (lambda part: issue(i + MOE_AHEAD, (slot + MOE_AHEAD) % n_slots, part)) if ahead else (lambda part: None)
        _tile_wait(h_hbm, bufs[slot], sem.at[slot])
        nxt(0)
        x = bufs[slot][...].astype(BF16)
        nxt(1)
        a = jnp.dot(x, w1b[...], preferred_element_type=F32)
        nxt(2)
        b = jnp.dot(x, w3b[...], preferred_element_type=F32)
        nxt(3)
        hid = _silu(a) * b
        y_ref[...] = jnp.dot(hid.astype(BF16), w2b[...], preferred_element_type=F32)

    @pl.when(i == 0)
    def _():
        for c in weight_copies(te_ref[0], 0):
            c.start()
        for t in range(MOE_AHEAD):
            @pl.when(t < n_used)
            def _():
                for part in range(n_parts):
                    issue(t, t, part)

    prev = jnp.maximum(i - 1, 0)

    @pl.when((i < n_used) & ((i == 0) | (te_ref[i] != te_ref[prev])))
    def _():
        wslot = seg_ref[i] % 2
        for c in weight_copies(te_ref[i], wslot):
            c.wait()
        w1b[...] = wf1[wslot].astype(BF16)
        w3b[...] = wf3[wslot].astype(BF16)
        w2b[...] = wf2[wslot].astype(BF16)

        @pl.when(nxt_ref[i] >= 0)
        def _():
            for c in weight_copies(nxt_ref[i], 1 - wslot):
                c.start()

    for slot in range(n_slots):
        @pl.when((i + MOE_AHEAD < n_used) & (i % n_slots == slot))
        def _():
            compute(slot, True)

        @pl.when((i < n_used) & (i + MOE_AHEAD >= n_used) & (i % n_slots == slot))
        def _():
            compute(slot, False)

    @pl.when(i >= n_used)
    def _():
        y_ref[...] = jnp.zeros_like(y_ref)


def _moe_gather(plan, h2_all, w1, w3, w2):
    n_sorted = plan["src_row"].shape[0]
    grid_spec = pltpu.PrefetchScalarGridSpec(
        num_scalar_prefetch=5,
        grid=(n_sorted // TME,),
        in_specs=[pl.BlockSpec(memory_space=pl.ANY)] * 4,
        out_specs=pl.BlockSpec((TME, D_MODEL), lambda i, *_: (i, 0)),
        scratch_shapes=[pltpu.VMEM((TME, D_MODEL), F32)] * (MOE_AHEAD + 1)
                       + [pltpu.SemaphoreType.DMA((MOE_AHEAD + 1,)),
                          pltpu.VMEM((2, D_MODEL, EXPERT_FF), F32), pltpu.VMEM((2, D_MODEL, EXPERT_FF), F32),
                          pltpu.VMEM((2, EXPERT_FF, D_MODEL), F32),
                          pltpu.SemaphoreType.DMA((2,)),
                          pltpu.VMEM((D_MODEL, EXPERT_FF), BF16), pltpu.VMEM((D_MODEL, EXPERT_FF), BF16),
                          pltpu.VMEM((EXPERT_FF, D_MODEL), BF16)],
    )
    return pl.pallas_call(
        _moe_gather_kernel,
        name="moe",
        grid_spec=grid_spec,
        out_shape=jax.ShapeDtypeStruct((n_sorted, D_MODEL), F32),
        compiler_params=_cparams(1),
    )(plan["tile_expert"], plan["src_row"], plan["n_used"], plan["segment"], plan["next_expert"],
      h2_all, w1, w3, w2)


def _moe(plan, h_sorted, w1, w3, w2):
    n_sorted = h_sorted.shape[0]
    n_tiles = n_sorted // TME
    used = lambda i, te, nu, seg, nxt: (jnp.minimum(i, nu[0] - 1), 0)
    grid_spec = pltpu.PrefetchScalarGridSpec(
        num_scalar_prefetch=4,
        grid=(n_tiles,),
        in_specs=[pl.BlockSpec((TME, D_MODEL), used)] + [pl.BlockSpec(memory_space=pl.ANY)] * 3,
        out_specs=pl.BlockSpec((TME, D_MODEL), lambda i, *_: (i, 0)),
        scratch_shapes=[pltpu.VMEM((2, D_MODEL, EXPERT_FF), F32), pltpu.VMEM((2, D_MODEL, EXPERT_FF), F32),
                        pltpu.VMEM((2, EXPERT_FF, D_MODEL), F32),
                        pltpu.SemaphoreType.DMA((2,)),
                        pltpu.VMEM((D_MODEL, EXPERT_FF), BF16), pltpu.VMEM((D_MODEL, EXPERT_FF), BF16),
                        pltpu.VMEM((EXPERT_FF, D_MODEL), BF16)],
    )
    return pl.pallas_call(
        _moe_kernel,
        name="moe",
        grid_spec=grid_spec,
        out_shape=jax.ShapeDtypeStruct((n_sorted, D_MODEL), F32),
        compiler_params=_cparams(1),
    )(plan["tile_expert"], plan["n_used"], plan["segment"], plan["next_expert"], h_sorted, w1, w3, w2)


def _finish_kernel(grp, pos_ref, x1_ref, mod_ref, rt_ref, nf_ref, y_hbm, out_ref, buf_a, buf_b, sem):
    i = pl.program_id(0)
    n_steps = pl.num_programs(0)
    n_slots = FINISH_AHEAD + 1
    slot = i % n_slots

    def issue(tile, slot_):
        base = (tile * TM + grp.row_off) * 2

        def body(r, carry):
            _row_copy(y_hbm, pos_ref[base + 2 * r], buf_a.at[slot_], r, sem.at[slot_]).start(priority=0)
            _row_copy(y_hbm, pos_ref[base + 2 * r + 1], buf_b.at[slot_], r, sem.at[slot_]).start(priority=1)
            return carry

        lax.fori_loop(0, TM, body, 0, unroll=8)

    @pl.when(i == 0)
    def _():
        for t in range(FINISH_AHEAD):
            @pl.when(t < n_steps)
            def _():
                issue(t, t)

    @pl.when(i + FINISH_AHEAD < n_steps)
    def _():
        issue(i + FINISH_AHEAD, (i + FINISH_AHEAD) % n_slots)

    _tile_wait(y_hbm, buf_a.at[slot], sem.at[slot])
    _tile_wait(y_hbm, buf_b.at[slot], sem.at[slot])
    rt = rt_ref[...]
    moe = rt[:, 2:3] * buf_a[slot] + rt[:, 3:4] * buf_b[slot]
    x2 = x1_ref[...] + _mod_rows(mod_ref, 5) * moe.reshape(grp.nb, grp.tt, D_MODEL)
    ms = jnp.mean(x2 * x2, axis=-1, keepdims=True)
    out_ref[...] = x2 * lax.rsqrt(ms + EPS) * nf_ref[...]


def _finish(grp, pos, x1, mod, route_all, norm_f, y_sorted):
    off = grp.tile_off
    grid_spec = pltpu.PrefetchScalarGridSpec(
        num_scalar_prefetch=1,
        grid=(grp.n_tiles,),
        in_specs=[pl.BlockSpec((grp.nb, grp.tt, D_MODEL), grp.x_map()),
                  pl.BlockSpec((grp.nb, 6, D_MODEL), grp.mod_map()),
                  pl.BlockSpec((TM, LANES), lambda i, p: (i + off, 0)),
                  pl.BlockSpec((1, 1, D_MODEL), lambda i, p: (0, 0, 0)),
                  pl.BlockSpec(memory_space=pl.ANY)],
        out_specs=pl.BlockSpec((grp.nb, grp.tt, D_MODEL), grp.x_map()),
        scratch_shapes=[pltpu.VMEM((FINISH_AHEAD + 1, TM, D_MODEL), F32),
                        pltpu.VMEM((FINISH_AHEAD + 1, TM, D_MODEL), F32),
                        pltpu.SemaphoreType.DMA((FINISH_AHEAD + 1,))],
    )
    return pl.pallas_call(
        functools.partial(_finish_kernel, grp),
        name="finish",
        grid_spec=grid_spec,
        out_shape=jax.ShapeDtypeStruct(x1.shape, F32),
        compiler_params=_cparams(1),
    )(pos, x1, mod, route_all, norm_f.reshape(1, 1, D_MODEL), y_sorted)


def _sort_plan(route_all):
    n_tok = route_all.shape[0]
    n_pairs = 2 * n_tok
    n_sorted = n_pairs + N_EXPERTS * TME
    flat_e = route_all[:, 0:2].astype(jnp.int32).reshape(n_pairs)
    onehot = (flat_e[:, None] == jnp.arange(N_EXPERTS, dtype=jnp.int32)[None, :]).astype(jnp.int32)
    csum = jnp.cumsum(onehot, axis=0)
    rank = jnp.sum(onehot * csum, axis=1) - 1
    counts = csum[-1]
    padded = ((counts + TME - 1) // TME) * TME
    ends = jnp.cumsum(padded)
    starts = ends - padded
    pos = starts[flat_e] + rank
    token = jnp.arange(n_pairs, dtype=jnp.int32) // 2
    src_row = (jnp.arange(n_sorted, dtype=jnp.int32) % n_tok).at[pos].set(token)
    tile_start = jnp.arange(n_sorted // TME, dtype=jnp.int32) * TME
    tile_expert = jnp.sum((tile_start[:, None] >= ends[None, :]).astype(jnp.int32), axis=1)
    tile_expert = jnp.minimum(tile_expert, N_EXPERTS - 1)
    n_used = ends[-1] // TME
    is_first = jnp.concatenate([jnp.ones((1,), jnp.int32),
                                (tile_expert[1:] != tile_expert[:-1]).astype(jnp.int32)])
    segment = jnp.cumsum(is_first) - 1
    next_tile = ends[tile_expert] // TME
    next_expert = jnp.where(next_tile < n_used, tile_expert[jnp.minimum(next_tile, n_sorted // TME - 1)], -1)
    return dict(pos=pos.astype(jnp.int32), src_row=src_row, tile_expert=tile_expert.astype(jnp.int32),
                n_used=n_used.astype(jnp.int32).reshape(1), segment=segment.astype(jnp.int32),
                next_expert=next_expert.astype(jnp.int32))


def kernel(x_prompt, x_sample, c_prompt, c_sample, state_gla, state_pool, w_ada, b_ada, norm1, norm2, w_in,
           gate_up, gate_bias, gla_norm, pool_w, pool_scale, w_out, w_group, b_group, w_expert, b_expert,
           w1, w3, w2, norm_f):
    assert w_ada.shape[0] == 1, "single-layer step"
    bp, tp, _ = x_prompt.shape
    bs, ts, _ = x_sample.shape
    grp_p = _Group(bp, tp, 0, bs)
    grp_s = _Group(bs, ts, bp * tp, 0)
    n_tok = bp * tp + bs * ts

    n_c = bp + bs
    n_c_pad = -(-n_c // 8) * 8
    c_all = jnp.concatenate([c_sample, c_prompt, jnp.zeros((n_c_pad - n_c, D_MODEL), F32)], axis=0)
    mod = _adaln(c_all, w_ada[0], b_ada[0]).reshape(n_c_pad, 6, D_MODEL)
    mod_p = mod_s = mod

    wi = w_in[0]
    wq = wi[:, :QKVR_WIDTH].astype(BF16)
    wg = jnp.pad(wi[:, QKVR_WIDTH:QKVR_WIDTH + GATE_RANK], ((0, 0), (0, LANES - GATE_RANK))).astype(BF16)
    wu = wi[:, QKVR_WIDTH + GATE_RANK:].astype(BF16)
    gup = jnp.pad(gate_up[0], ((0, LANES - GATE_RANK), (0, 0))).astype(BF16)
    gb = gate_bias[0].reshape(1, GLA_KEY_WIDTH)
    pw = pool_w[0].astype(BF16)
    wo = w_out[0].astype(BF16)
    gap = EXPERTS_PER_GROUP - N_GROUPS
    tail = LANES - EXPERTS_PER_GROUP - N_EXPERTS
    w_router = jnp.concatenate(
        [w_group[0], jnp.zeros((D_MODEL, gap), F32),
         jnp.transpose(w_expert[0], (1, 0, 2)).reshape(D_MODEL, N_EXPERTS),
         jnp.zeros((D_MODEL, tail), F32)], axis=1)
    w_router_hi = w_router.astype(BF16)
    w_router = jnp.stack([w_router_hi, (w_router - w_router_hi.astype(F32)).astype(BF16)])
    b_router = jnp.concatenate([b_group[0], jnp.zeros((gap,), F32), b_expert[0].reshape(N_EXPERTS),
                                jnp.zeros((tail,), F32)]).reshape(1, LANES)

    qkvr_p, la_p, u_p = _in_proj(grp_p, x_prompt, mod_p, norm1[0], wq, wg, wu, gup, gb)
    qkvr_s, la_s, u_s = _in_proj(grp_s, x_sample, mod_s, norm1[0], wq, wg, wu, gup, gb)

    o_p, gla_p = _gla_prompt(bp, tp, qkvr_p, la_p)
    o_s, gla_s = _gla_decode(bs, ts, qkvr_s, la_s, state_gla.reshape(state_gla.shape[1:]))

    halo_per_tile = TM // HALO
    halo_map_p = lambda i: (jnp.maximum(i * halo_per_tile - 1, 0), 0)
    x1_p, h2_all, route_all = _mix_out(grp_p, n_tok, 0, True, o_p, qkvr_p, u_p, u_p, (HALO, POOL_WIDTH),
                                       halo_map_p, x_prompt, mod_p, norm2[0], gla_norm[0], pw, pool_scale[0],
                                       wo, w_router, b_router)
    x1_s, h2_all, route_all = _mix_out(grp_s, n_tok, PAST_LEN, False, o_s, qkvr_s, u_s,
                                       state_pool.reshape(bs, POOL_BUF, POOL_WIDTH),
                                       (grp_s.nb, POOL_BUF, POOL_WIDTH), lambda i: (i, 0, 0), x_sample, mod_s,
                                       norm2[0], gla_norm[0], pw, pool_scale[0], wo, w_router, b_router,
                                       shared=(h2_all, route_all))

    plan = _sort_plan(route_all)
    y_sorted = _moe_gather(plan, h2_all, w1.reshape(w1.shape[1:]), w3.reshape(w3.shape[1:]),
                           w2.reshape(w2.shape[1:]))

    y_p = _finish(grp_p, plan["pos"], x1_p, mod_p, route_all, norm_f, y_sorted)
    y_s = _finish(grp_s, plan["pos"], x1_s, mod_s, route_all, norm_f, y_sorted)

    u_p3 = u_p.reshape(bp, tp, POOL_WIDTH)
    u_s3 = u_s.reshape(bs, ts, POOL_WIDTH)
    assert tp >= POOL_BUF > ts
    pool_p = u_p3[:, tp - POOL_BUF:]
    pool_s = jnp.concatenate([state_pool.reshape(bs, POOL_BUF, POOL_WIDTH)[:, ts:], u_s3], axis=1)
    lead = lambda a: a.reshape((1,) + a.shape)
    return (y_p, y_s, lead(gla_p), lead(pool_p), lead(gla_s), lead(pool_s))
```

```python
import functools

import jax
import jax.numpy as jnp
from jax import lax
from jax.experimental import pallas as pl
from jax.experimental.pallas import tpu as pltpu

D_MODEL = 2048
GLA_HEADS = 4
GLA_DK = 128
GLA_DV = 256
GLA_KEY_WIDTH = GLA_HEADS * GLA_DK
GLA_WIDTH = GLA_HEADS * GLA_DV
POOL_WIDTH = 1024
POOL_WINDOWS = (2, 4, 8, 16)
POOL_GW = 256
POOL_BUF = 15
HALO = 16
EXT_PAD = 8
EXT_TOK0 = EXT_PAD + HALO
GATE_RANK = 16
GATE_TEMP = 16.0
N_GROUPS = 4
EXPERTS_PER_GROUP = 8
N_EXPERTS = 32
EXPERT_FF = 512
EPS = 1e-6
PAST_LEN = 16384
QKVR_WIDTH = 2 * GLA_KEY_WIDTH + 2 * GLA_WIDTH

LANES = 128
TM = 256
TME = 256
TD = 1024
FINISH_AHEAD = 3
GLA_CHUNK = 64
GLA_SUB = 8
GLA_STEP = 512
GLA_TRIP = 4
VMEM_LIMIT = 56 * 1024 * 1024

BF16 = jnp.bfloat16
F32 = jnp.float32
NEG = -1e30
LOG2E = 1.4426950408889634


def _cparams(n_axes):
    return pltpu.CompilerParams(dimension_semantics=("arbitrary",) * n_axes,
                                vmem_limit_bytes=VMEM_LIMIT)


def _silu(x):
    return x / (1.0 + jnp.exp(-x))


def _bdot(a, b):
    return jnp.dot(a.astype(BF16), b.astype(BF16), preferred_element_type=F32)


def _split3(a):
    a1 = a.astype(BF16)
    r1 = a - a1.astype(F32)
    a2 = r1.astype(BF16)
    a3 = (r1 - a2.astype(F32)).astype(BF16)
    return a1, a2, a3


def _split2(a):
    hi = a.astype(BF16)
    lo = (a - hi.astype(F32)).astype(BF16)
    return hi, lo


def _dot_3pass(a, b_hi, b_lo):
    a_hi, a_lo = _split2(a)
    d = lambda x, y: jnp.dot(x, y, preferred_element_type=F32)
    return d(a_hi, b_hi) + (d(a_hi, b_lo) + d(a_lo, b_hi))


def _dot_exact_lhs(tri_bf16, g):
    g1, g2, g3 = _split3(g)
    d = lambda y: jnp.dot(tri_bf16, y, preferred_element_type=F32)
    return d(g1) + (d(g2) + d(g3))


def _adaln_kernel(c_ref, w_ref, b_ref, o_ref):
    c = c_ref[...]
    o_ref[...] = _bdot(_silu(c), w_ref[...]) + b_ref[...]


def _adaln(c_all, w_ada, b_ada):
    n, d = c_all.shape
    width = w_ada.shape[1]
    tn = 1024
    return pl.pallas_call(
        _adaln_kernel,
        name="adaln",
        grid=(width // tn,),
        in_specs=[pl.BlockSpec((n, d), lambda j: (0, 0)),
                  pl.BlockSpec((d, tn), lambda j: (0, j)),
                  pl.BlockSpec((1, tn), lambda j: (0, j))],
        out_specs=pl.BlockSpec((n, tn), lambda j: (0, j)),
        out_shape=jax.ShapeDtypeStruct((n, width), F32),
        compiler_params=_cparams(1),
    )(c_all, w_ada, b_ada.reshape(1, width))


class _Group:
    def __init__(self, batch, seq, row_off, mod_off):
        self.batch, self.seq, self.row_off, self.mod_off = batch, seq, row_off, mod_off
        if seq >= TM:
            assert seq % TM == 0
            self.nb, self.tt = 1, TM
            self.tiles_per_batch = seq // TM
            self.n_tiles = batch * self.tiles_per_batch
        else:
            assert TM % seq == 0 and batch % (TM // seq) == 0
            self.nb, self.tt = TM // seq, seq
            self.tiles_per_batch = 1
            self.n_tiles = batch // self.nb
        self.rows = batch * seq
        self.tile_off = row_off // TM

    def x_map(self):
        if self.nb == 1:
            tpb = self.tiles_per_batch
            return lambda i, *_: (i // tpb, i % tpb, 0)
        return lambda i, *_: (i, 0, 0)

    def mod_map(self):
        assert self.mod_off % self.nb == 0
        off = self.mod_off // self.nb
        if self.nb == 1:
            tpb = self.tiles_per_batch
            return lambda i, *_: (i // tpb + off, 0, 0)
        return lambda i, *_: (i + off, 0, 0)


def _mod_rows(mod_ref, idx):
    return mod_ref[:, idx:idx + 1, :]


def _rmsnorm_mod(x, gain, scale, shift):
    ms = jnp.mean(x * x, axis=-1, keepdims=True)
    y = x * lax.rsqrt(ms + EPS) * gain
    return y * (1.0 + scale) + shift


def _in_proj_kernel(x_ref, mod_ref, n1_ref, wq_ref, wg_ref, wu_ref, gup_ref, gb_ref,
                    qkvr_ref, la_ref, u_ref):
    x = x_ref[...]
    h = _rmsnorm_mod(x, n1_ref[...], _mod_rows(mod_ref, 1), _mod_rows(mod_ref, 0))
    hb = h.reshape(TM, D_MODEL).astype(BF16)
    qkvr_ref[...] = jnp.dot(hb, wq_ref[...], preferred_element_type=F32)
    u_ref[...] = jnp.dot(hb, wu_ref[...], preferred_element_type=F32)
    g_lr = jnp.dot(hb, wg_ref[...], preferred_element_type=F32)
    pre = jnp.dot(g_lr.astype(BF16), gup_ref[...], preferred_element_type=F32) + gb_ref[...]
    log_sig = jnp.minimum(pre, 0.0) - jnp.log1p(jnp.exp(-jnp.abs(pre)))
    la_ref[...] = log_sig / GATE_TEMP


def _in_proj(grp, x, mod, norm1, wq, wg, wu, gup, gb):
    const = lambda i: (0, 0)
    row = lambda i: (i, 0)
    return pl.pallas_call(
        _in_proj_kernel,
        name="in_proj",
        grid=(grp.n_tiles,),
        in_specs=[pl.BlockSpec((grp.nb, grp.tt, D_MODEL), grp.x_map()),
                  pl.BlockSpec((grp.nb, 6, D_MODEL), grp.mod_map()),
                  pl.BlockSpec((1, 1, D_MODEL), lambda i: (0, 0, 0)),
                  pl.BlockSpec(wq.shape, const, pipeline_mode=pl.Buffered(1)),
                  pl.BlockSpec(wg.shape, const, pipeline_mode=pl.Buffered(1)),
                  pl.BlockSpec(wu.shape, const, pipeline_mode=pl.Buffered(1)),
                  pl.BlockSpec(gup.shape, const, pipeline_mode=pl.Buffered(1)),
                  pl.BlockSpec(gb.shape, const, pipeline_mode=pl.Buffered(1))],
        out_specs=[pl.BlockSpec((TM, QKVR_WIDTH), row),
                   pl.BlockSpec((TM, GLA_KEY_WIDTH), row),
                   pl.BlockSpec((TM, POOL_WIDTH), row)],
        out_shape=[jax.ShapeDtypeStruct((grp.rows, QKVR_WIDTH), F32),
                   jax.ShapeDtypeStruct((grp.rows, GLA_KEY_WIDTH), F32),
                   jax.ShapeDtypeStruct((grp.rows, POOL_WIDTH), F32)],
        compiler_params=_cparams(1),
    )(x, mod, norm1.reshape(1, 1, D_MODEL), wq, wg, wu, gup, gb)


def _gla_select_matrix(chunk, sub):
    r = jnp.arange(sub * GLA_DK, dtype=jnp.int32)[:, None] // GLA_DK
    l = jnp.arange(LANES, dtype=jnp.int32)[None, :]
    return ((l % sub == r) & (l < chunk)).astype(BF16)


def _gla_chunks(chunks, states, chunk, sub, wsel, t_refs, kb_refs, chained=True):
    n_sub = chunk // sub
    rows = lax.broadcasted_iota(jnp.int32, (chunk, chunk), 0)
    cols = lax.broadcasted_iota(jnp.int32, (chunk, chunk), 1)
    tri = (rows >= cols).astype(BF16)
    nt = (((1,), (1,)), ((), ()))
    tn = (((0,), (0,)), ((), ()))
    key_row = lax.broadcasted_iota(jnp.int32, (chunk, LANES), 0)
    lane = lax.broadcasted_iota(jnp.int32, (sub, chunk), 1)
    row = lax.broadcasted_iota(jnp.int32, (sub, chunk), 0)
    head = lambda a, h, w: a[:, h * w:(h + 1) * w]
    n = len(chunks)

    b4s = [_dot_exact_lhs(tri, g4) for (_, _, _, g4) in chunks]
    q4s = [q4 * (GLA_DK ** -0.5) for (q4, _, _, _) in chunks]

    for c in range(n):
        kb_ref, t_ref, k4 = kb_refs[c], t_refs[c], chunks[c][1]
        for h in range(GLA_HEADS):
            kb_ref[h] = head(k4, h, GLA_DK)
            kb_ref[GLA_HEADS + h] = head(b4s[c], h, GLA_DK) * LOG2E
        for h in range(GLA_HEADS):
            for s in range(n_sub):
                lo = s * sub
                r0 = (h * n_sub + s) * sub
                q_s, b_s = head(q4s[c], h, GLA_DK)[lo:lo + sub, :], kb_ref[GLA_HEADS + h, lo:lo + sub, :]
                for jl in range(sub):
                    k_j = jnp.broadcast_to(kb_ref[h, lo + jl:lo + jl + 1, :], (sub, GLA_DK))
                    b_j = jnp.broadcast_to(kb_ref[GLA_HEADS + h, lo + jl:lo + jl + 1, :], (sub, GLA_DK))
                    decay = jnp.exp2(jnp.minimum(b_s - b_j, 0.0))
                    t_ref[r0:r0 + sub, jl * GLA_DK:(jl + 1) * GLA_DK] = (q_s * k_j * decay).astype(t_ref.dtype)
    p_diags = [jnp.dot(t_refs[c][...].astype(BF16), wsel, preferred_element_type=F32) for c in range(n)]

    intra = []
    for c in range(n):
        per_head = []
        for h in range(GLA_HEADS):
            q, k, b = head(q4s[c], h, GLA_DK), head(chunks[c][1], h, GLA_DK), head(b4s[c], h, GLA_DK)
            p_blocks = []
            for s in range(n_sub):
                lo = s * sub
                r0 = (h * n_sub + s) * sub
                in_block = (lane >= lo) & (lane - lo <= row)
                p = jnp.where(in_block, p_diags[c][r0:r0 + sub, :chunk], 0.0)
                if s > 0:
                    ref_row = b[lo - 1:lo, :]
                    q_rel = q[lo:lo + sub, :] * jnp.exp(b[lo:lo + sub, :] - ref_row)
                    k_rel = k * jnp.exp(jnp.where(key_row < lo, ref_row - b, NEG))
                    p = p + lax.dot_general(q_rel.astype(BF16), k_rel.astype(BF16), nt,
                                            preferred_element_type=F32)
                p_blocks.append(p)
            p_full = p_blocks[0] if n_sub == 1 else jnp.concatenate(p_blocks, axis=0)
            per_head.append(_bdot(p_full, head(chunks[c][2], h, GLA_DV)))
        intra.append(per_head)

    outs, end_states = [], []
    for c in range(n):
        cur = states if chained else states[c]
        o_heads, new_states = [], []
        for h in range(GLA_HEADS):
            q, k, b = head(q4s[c], h, GLA_DK), head(chunks[c][1], h, GLA_DK), head(b4s[c], h, GLA_DK)
            v = head(chunks[c][2], h, GLA_DV)
            o_heads.append(intra[c][h] + _bdot(q * jnp.exp(b), cur[h]))
            b_last = b[chunk - 1:chunk, :]
            k_dec = k * jnp.exp(b_last - b)
            decay_col = jnp.exp(b[chunk - 8:chunk, :]).T[:, 7:8]
            new_states.append(decay_col * cur[h] + lax.dot_general(
                k_dec.astype(BF16), v.astype(BF16), tn, preferred_element_type=F32))
        states = new_states if chained else states
        end_states.append(new_states)
        outs.append(jnp.concatenate(o_heads, axis=1))
    return outs, (end_states[-1] if chained else end_states)


def _gla_prompt_kernel(q_ref, k_ref, v_ref, la_ref, wsel_ref, o_ref, s_ref, t_ref, kb_ref):
    @pl.when(pl.program_id(1) == 0)
    def _():
        s_ref[...] = jnp.zeros_like(s_ref)

    def body(pair, carry):
        states = [s_ref[0, h] for h in range(GLA_HEADS)]
        slices = [pl.ds(pl.multiple_of((GLA_TRIP * pair + c) * GLA_CHUNK, GLA_CHUNK), GLA_CHUNK)
                  for c in range(GLA_TRIP)]
        chunks = [(q_ref[sl, :], k_ref[sl, :], v_ref[sl, :], la_ref[sl, :]) for sl in slices]
        outs, states = _gla_chunks(chunks, states, GLA_CHUNK, GLA_SUB, wsel_ref[...],
                                   [t_ref.at[c] for c in range(GLA_TRIP)],
                                   [kb_ref.at[c] for c in range(GLA_TRIP)])
        for sl, o in zip(slices, outs):
            o_ref[sl, :] = o
        for h in range(GLA_HEADS):
            s_ref[0, h] = states[h]
        return carry

    lax.fori_loop(0, GLA_STEP // (GLA_TRIP * GLA_CHUNK), body, 0)


def _gla_prompt(batch, seq, qkvr, log_a):
    steps = seq // GLA_STEP
    row = lambda b, s: b * steps + s
    wsel = _gla_select_matrix(GLA_CHUNK, GLA_SUB)
    return pl.pallas_call(
        _gla_prompt_kernel,
        name="gla_prompt",
        grid=(batch, steps),
        in_specs=[pl.BlockSpec((GLA_STEP, GLA_KEY_WIDTH), lambda b, s: (row(b, s), 0)),
                  pl.BlockSpec((GLA_STEP, GLA_KEY_WIDTH), lambda b, s: (row(b, s), 1)),
                  pl.BlockSpec((GLA_STEP, GLA_WIDTH), lambda b, s: (row(b, s), 1)),
                  pl.BlockSpec((GLA_STEP, GLA_KEY_WIDTH), lambda b, s: (row(b, s), 0)),
                  pl.BlockSpec(wsel.shape, lambda b, s: (0, 0))],
        out_specs=[pl.BlockSpec((GLA_STEP, GLA_WIDTH), lambda b, s: (row(b, s), 0)),
                   pl.BlockSpec((1, GLA_HEADS, GLA_DK, GLA_DV), lambda b, s: (b, 0, 0, 0))],
        out_shape=[jax.ShapeDtypeStruct((batch * seq, GLA_WIDTH), F32),
                   jax.ShapeDtypeStruct((batch, GLA_HEADS, GLA_DK, GLA_DV), F32)],
        scratch_shapes=[pltpu.VMEM((GLA_TRIP, GLA_HEADS * GLA_CHUNK, GLA_SUB * GLA_DK), BF16),
                        pltpu.VMEM((GLA_TRIP, 2 * GLA_HEADS, GLA_CHUNK, GLA_DK), F32)],
        compiler_params=_cparams(2),
    )(qkvr, qkvr, qkvr, log_a, wsel)


GLA_DEC_BB = 8
GLA_DEC_TRIP = 4


def _gla_decode_kernel(seq, q_ref, k_ref, v_ref, la_ref, wsel_ref, s0_ref, o_ref, s_ref, t_ref, kb_ref):
    def body(trip, carry):
        elems = [GLA_DEC_TRIP * trip + c for c in range(GLA_DEC_TRIP)]
        slices = [pl.ds(pl.multiple_of(e * seq, seq), seq) for e in elems]
        chunks = [(q_ref[sl, :], k_ref[sl, :], v_ref[sl, :], la_ref[sl, :]) for sl in slices]
        states = [[s0_ref[e, h] for h in range(GLA_HEADS)] for e in elems]
        outs, new_states = _gla_chunks(chunks, states, seq, seq, wsel_ref[...],
                                       [t_ref.at[c] for c in range(GLA_DEC_TRIP)],
                                       [kb_ref.at[c] for c in range(GLA_DEC_TRIP)], chained=False)
        for e, sl, o, ns in zip(elems, slices, outs, new_states):
            o_ref[sl, :] = o
            for h in range(GLA_HEADS):
                s_ref[e, h] = ns[h]
        return carry

    lax.fori_loop(0, GLA_DEC_BB // GLA_DEC_TRIP, body, 0)


def _gla_decode(batch, seq, qkvr, log_a, state):
    rows = GLA_DEC_BB * seq
    wsel = _gla_select_matrix(seq, seq)
    state_spec = pl.BlockSpec((GLA_DEC_BB, GLA_HEADS, GLA_DK, GLA_DV), lambda i: (i, 0, 0, 0))
    return pl.pallas_call(
        functools.partial(_gla_decode_kernel, seq),
        name="gla_decode",
        grid=(batch // GLA_DEC_BB,),
        in_specs=[pl.BlockSpec((rows, GLA_KEY_WIDTH), lambda i: (i, 0)),
                  pl.BlockSpec((rows, GLA_KEY_WIDTH), lambda i: (i, 1)),
                  pl.BlockSpec((rows, GLA_WIDTH), lambda i: (i, 1)),
                  pl.BlockSpec((rows, GLA_KEY_WIDTH), lambda i: (i, 0)),
                  pl.BlockSpec(wsel.shape, lambda i: (0, 0)),
                  state_spec],
        out_specs=[pl.BlockSpec((rows, GLA_WIDTH), lambda i: (i, 0)), state_spec],
        out_shape=[jax.ShapeDtypeStruct((batch * seq, GLA_WIDTH), F32),
                   jax.ShapeDtypeStruct((batch, GLA_HEADS, GLA_DK, GLA_DV), F32)],
        scratch_shapes=[pltpu.VMEM((GLA_DEC_TRIP, GLA_HEADS * seq, seq * GLA_DK), F32),
                        pltpu.VMEM((GLA_DEC_TRIP, 2 * GLA_HEADS, seq, GLA_DK), F32)],
        compiler_params=_cparams(1),
    )(qkvr, qkvr, qkvr, log_a, wsel, state)


def _route(logits):
    lt = logits.T
    n = lt.shape[1]
    big = jnp.int32(10 ** 6)
    row8 = lax.broadcasted_iota(jnp.int32, (EXPERTS_PER_GROUP, n), 0)
    lg = jnp.where(row8 < N_GROUPS, lt[0:EXPERTS_PER_GROUP], NEG)
    mg = jnp.max(lg, axis=0, keepdims=True)
    g_idx = jnp.min(jnp.where(lg == mg, row8, big), axis=0, keepdims=True)
    p_sel = 1.0 / jnp.sum(jnp.exp(lg - mg), axis=0, keepdims=True)
    le = jnp.zeros((EXPERTS_PER_GROUP, n), F32)
    for g in range(N_GROUPS):
        lo = EXPERTS_PER_GROUP * (g + 1)
        le = jnp.where(g_idx == g, lt[lo:lo + EXPERTS_PER_GROUP], le)
    m1 = jnp.max(le, axis=0, keepdims=True)
    i1 = jnp.min(jnp.where(le == m1, row8, big), axis=0, keepdims=True)
    rest = row8 != i1
    m2 = jnp.max(jnp.where(rest, le, NEG), axis=0, keepdims=True)
    i2 = jnp.min(jnp.where(rest & (le == m2), row8, big), axis=0, keepdims=True)
    e2 = jnp.exp(m2 - m1)
    w1 = p_sel / (1.0 + e2)
    w2 = p_sel * e2 / (1.0 + e2)
    ex1 = (g_idx * EXPERTS_PER_GROUP + i1).astype(F32)
    ex2 = (g_idx * EXPERTS_PER_GROUP + i2).astype(F32)
    packed = jnp.where(row8 == 0, ex1, jnp.where(row8 == 1, ex2,
                       jnp.where(row8 == 2, w1, jnp.where(row8 == 3, w2, 0.0))))
    full = jnp.concatenate([packed, jnp.zeros((LANES - EXPERTS_PER_GROUP, n), F32)], axis=0)
    return full.T


def _mix_out_kernel(grp, pos0, zero_first_halo, n_alias,
                    o_ref, r_ref, u_ref, halo_ref, x_ref, mod_ref, n2_ref, gn_ref, pw_ref, ps_ref,
                    wo_ref, wr_ref, br_ref, *rest):
    x1_ref, h2_ref, rt_ref, ext_ref, lvl_a, lvl_b, ymix_ref = rest[n_alias:]
    i = pl.program_id(0)

    @pl.when(i < grp.n_tiles)
    def _():
        _mix_out_tile(grp, pos0, zero_first_halo, i, o_ref, r_ref, u_ref, halo_ref, x_ref, mod_ref, n2_ref,
                      gn_ref, pw_ref, ps_ref, wo_ref, wr_ref, br_ref, x1_ref, h2_ref, rt_ref, ext_ref,
                      (lvl_a, lvl_b), ymix_ref)

    @pl.when(i >= grp.n_tiles)
    def _():
        h2_ref[...] = jnp.zeros_like(h2_ref)
        rt_ref[...] = jnp.zeros_like(rt_ref)


def _mix_out_tile(grp, pos0, zero_first_halo, i, o_ref, r_ref, u_ref, halo_ref, x_ref, mod_ref, n2_ref,
                  gn_ref, pw_ref, ps_ref, wo_ref, wr_ref, br_ref, x1_ref, h2_ref, rt_ref, ext_ref, lvl_refs,
                  ymix_ref):
    nb, tt = grp.nb, grp.tt
    hist = halo_ref.shape[-2]

    for h in range(GLA_HEADS):
        cs = slice(h * GLA_DV, (h + 1) * GLA_DV)
        oh = o_ref[:, cs]
        ms = jnp.mean(oh * oh, axis=-1, keepdims=True)
        yh = oh * lax.rsqrt(ms + EPS) * gn_ref[:, cs] * _silu(r_ref[:, cs])
        ymix_ref[:, cs] = yh.astype(BF16)

    halo = halo_ref[...]
    if zero_first_halo:
        halo = jnp.where(i % grp.tiles_per_batch == 0, 0.0, halo)
    n_ext = EXT_TOK0 + tt
    ext_ref[:, 0:EXT_TOK0 - hist, :] = jnp.zeros((nb, EXT_TOK0 - hist, POOL_WIDTH), F32)
    ext_ref[:, EXT_TOK0 - hist:EXT_TOK0, :] = halo.reshape(nb, hist, POOL_WIDTH)
    u = u_ref[...].reshape(nb, tt, POOL_WIDTH)
    ext_ref[:, EXT_TOK0:n_ext, :] = u
    for lvl_ref in lvl_refs:
        lvl_ref[:, 0:EXT_PAD, :] = jnp.zeros((nb, EXT_PAD, POOL_GW), F32)
    t_idx = lax.broadcasted_iota(jnp.int32, (nb, tt, POOL_GW), 1)
    if grp.nb == 1:
        pos = (i % grp.tiles_per_batch) * TM + t_idx + pos0
    else:
        pos = t_idx + pos0
    for gi, w in enumerate(POOL_WINDOWS):
        cs = slice(gi * POOL_GW, (gi + 1) * POOL_GW)
        cur = lambda lo, hi: ext_ref[:, lo:hi, cs]
        d, level = 1, 0
        while d < w:
            nxt = lvl_refs[level % 2]
            nxt[:, EXT_PAD:n_ext, :] = cur(EXT_PAD, n_ext) + cur(EXT_PAD - d, n_ext - d)
            cur = lambda lo, hi, ref=nxt: ref[:, lo:hi, :]
            d, level = 2 * d, level + 1
        acc = cur(EXT_TOK0, n_ext)
        cnt = jnp.minimum(pos + 1, w).astype(F32)
        pooled = acc / cnt - u[:, :, cs]
        yp = _bdot(pooled.reshape(TM, POOL_GW), pw_ref[gi]) * ps_ref[:, cs]
        ymix_ref[:, GLA_WIDTH + gi * POOL_GW:GLA_WIDTH + (gi + 1) * POOL_GW] = yp.astype(BF16)

    y = jnp.dot(ymix_ref[...], wo_ref[...], preferred_element_type=F32)
    x1 = x_ref[...] + _mod_rows(mod_ref, 2) * y.reshape(nb, tt, D_MODEL)
    x1_ref[...] = x1
    h2 = _rmsnorm_mod(x1, n2_ref[...], _mod_rows(mod_ref, 4), _mod_rows(mod_ref, 3)).reshape(TM, D_MODEL)
    h2_ref[...] = h2
    logits = _dot_3pass(h2, wr_ref[0], wr_ref[1]) + br_ref[...]
    rt_ref[...] = _route(logits)


def _mix_out(grp, n_tok, pos0, zero_first_halo, o, qkvr, u, halo_src, halo_block, halo_map, x, mod, norm2,
             gla_norm, pool_w, pool_scale, w_out, w_router, b_router, shared=()):
    n_alias = len(shared)
    n = grp.n_tiles
    n_fill = 0 if shared else n_tok // TM - n
    assert n_fill == 0 or grp.tile_off == 0
    clamp = lambda f: (lambda i: f(jnp.minimum(i, n - 1)))
    const2 = lambda i: (0, 0)
    row = clamp(lambda i: (i, 0))
    off = grp.tile_off
    kern = functools.partial(_mix_out_kernel, grp, pos0, zero_first_halo, n_alias)
    return pl.pallas_call(
        kern,
        name="mix_out",
        grid=(n + n_fill,),
        in_specs=[pl.BlockSpec((TM, GLA_WIDTH), row),
                  pl.BlockSpec((TM, GLA_WIDTH), clamp(lambda i: (i, 2))),
                  pl.BlockSpec((TM, POOL_WIDTH), row),
                  pl.BlockSpec(halo_block, clamp(halo_map)),
                  pl.BlockSpec((grp.nb, grp.tt, D_MODEL), clamp(grp.x_map())),
                  pl.BlockSpec((grp.nb, 6, D_MODEL), clamp(grp.mod_map())),
                  pl.BlockSpec((1, 1, D_MODEL), lambda i: (0, 0, 0)),
                  pl.BlockSpec((1, GLA_WIDTH), const2),
                  pl.BlockSpec(pool_w.shape, lambda i: (0, 0, 0), pipeline_mode=pl.Buffered(1)),
                  pl.BlockSpec((1, POOL_WIDTH), const2),
                  pl.BlockSpec(w_out.shape, const2, pipeline_mode=pl.Buffered(1)),
                  pl.BlockSpec(w_router.shape, lambda i: (0, 0, 0), pipeline_mode=pl.Buffered(1)),
                  pl.BlockSpec((1, LANES), const2)]
                 + [pl.BlockSpec(memory_space=pl.ANY)] * n_alias,
        out_specs=[pl.BlockSpec((grp.nb, grp.tt, D_MODEL), clamp(grp.x_map())),
                   pl.BlockSpec((TM, D_MODEL), lambda i: (i + off, 0)),
                   pl.BlockSpec((TM, LANES), lambda i: (i + off, 0))],
        out_shape=[jax.ShapeDtypeStruct(x.shape, F32),
                   jax.ShapeDtypeStruct((n_tok, D_MODEL), F32),
                   jax.ShapeDtypeStruct((n_tok, LANES), F32)],
        scratch_shapes=[pltpu.VMEM((grp.nb, EXT_TOK0 + grp.tt, POOL_WIDTH), F32),
                        pltpu.VMEM((grp.nb, EXT_TOK0 + grp.tt, POOL_GW), F32),
                        pltpu.VMEM((grp.nb, EXT_TOK0 + grp.tt, POOL_GW), F32),
                        pltpu.VMEM((TM, D_MODEL), BF16)],
        input_output_aliases={13 + k: 1 + k for k in range(n_alias)},
        compiler_params=_cparams(1),
    )(o, qkvr, u, halo_src, x, mod, norm2.reshape(1, 1, D_MODEL), gla_norm.reshape(1, GLA_WIDTH),
      pool_w, pool_scale.reshape(1, POOL_WIDTH), w_out, w_router, b_router, *shared)


def _row_copy(src_hbm, src_row, dst, dst_row, sem):
    return pltpu.make_async_copy(src_hbm.at[pl.ds(src_row, 1), :], dst.at[pl.ds(dst_row, 1), :], sem)


def _tile_wait(src_hbm, dst, sem):
    pltpu.make_async_copy(src_hbm.at[pl.ds(0, dst.shape[0]), :], dst, sem).wait()


def _dispatch_kernel(src_ref, nu_ref, h_hbm, hs_ref, buf, sem):
    i = pl.program_id(0)
    n_used = (nu_ref[0] * TME + TD - 1) // TD
    slot = i % 2

    def issue(tile, slot_):
        base = tile * TD

        def body(r, carry):
            _row_copy(h_hbm, src_ref[base + 2 * r], buf.at[slot_], 2 * r, sem.at[slot_]).start(priority=0)
            _row_copy(h_hbm, src_ref[base + 2 * r + 1], buf.at[slot_], 2 * r + 1, sem.at[slot_]).start(priority=1)
            return carry

        lax.fori_loop(0, TD // 2, body, 0, unroll=4)

    @pl.when(i == 0)
    def _():
        issue(0, 0)

    @pl.when(i + 1 < n_used)
    def _():
        issue(i + 1, 1 - slot)

    @pl.when(i < n_used)
    def _():
        _tile_wait(h_hbm, buf.at[slot], sem.at[slot])
        hs_ref[...] = buf[slot].astype(BF16)

    @pl.when(i >= n_used)
    def _():
        hs_ref[...] = jnp.zeros_like(hs_ref)


def _dispatch(plan, h2_all):
    n_sorted = plan["src_row"].shape[0]
    assert n_sorted % TD == 0
    grid_spec = pltpu.PrefetchScalarGridSpec(
        num_scalar_prefetch=2,
        grid=(n_sorted // TD,),
        in_specs=[pl.BlockSpec(memory_space=pl.ANY)],
        out_specs=pl.BlockSpec((TD, D_MODEL), lambda i, *_: (i, 0)),
        scratch_shapes=[pltpu.VMEM((2, TD, D_MODEL), F32), pltpu.SemaphoreType.DMA((2,))],
    )
    return pl.pallas_call(
        _dispatch_kernel,
        name="dispatch",
        grid_spec=grid_spec,
        out_shape=jax.ShapeDtypeStruct((n_sorted, D_MODEL), BF16),
        compiler_params=_cparams(1),
    )(plan["src_row"], plan["n_used"], h2_all)


def _moe_kernel(te_ref, nu_ref, seg_ref, nxt_ref, hs_ref, w1_hbm, w3_hbm, w2_hbm, y_ref,
                wf1, wf3, wf2, wsem, w1b, w3b, w2b):
    i = pl.program_id(0)
    n_used = nu_ref[0]

    def weight_copies(expert, wslot):
        return (pltpu.make_async_copy(w1_hbm.at[expert], wf1.at[wslot], wsem.at[wslot]),
                pltpu.make_async_copy(w3_hbm.at[expert], wf3.at[wslot], wsem.at[wslot]),
                pltpu.make_async_copy(w2_hbm.at[expert], wf2.at[wslot], wsem.at[wslot]))

    @pl.when(i == 0)
    def _():
        for c in weight_copies(te_ref[0], 0):
            c.start()

    prev = jnp.maximum(i - 1, 0)

    @pl.when((i < n_used) & ((i == 0) | (te_ref[i] != te_ref[prev])))
    def _():
        wslot = seg_ref[i] % 2
        for c in weight_copies(te_ref[i], wslot):
            c.wait()
        w1b[...] = wf1[wslot].astype(BF16)
        w3b[...] = wf3[wslot].astype(BF16)
        w2b[...] = wf2[wslot].astype(BF16)

        @pl.when(nxt_ref[i] >= 0)
        def _():
            for c in weight_copies(nxt_ref[i], 1 - wslot):
                c.start()

    @pl.when(i < n_used)
    def _():
        x = hs_ref[...]
        a = jnp.dot(x, w1b[...], preferred_element_type=F32)
        b = jnp.dot(x, w3b[...], preferred_element_type=F32)
        hid = _silu(a) * b
        y_ref[...] = jnp.dot(hid.astype(BF16), w2b[...], preferred_element_type=F32)

    @pl.when(i >= n_used)
    def _():
        y_ref[...] = jnp.zeros_like(y_ref)


MOE_AHEAD = 2


def _moe_gather_kernel(te_ref, src_ref, nu_ref, seg_ref, nxt_ref, h_hbm, w1_hbm, w3_hbm, w2_hbm, y_ref,
                       buf0, buf1, buf2, sem, wf1, wf3, wf2, wsem, w1b, w3b, w2b):
    i = pl.program_id(0)
    n_used = nu_ref[0]
    bufs = (buf0, buf1, buf2)
    n_slots = MOE_AHEAD + 1
    n_parts = 4

    def issue(tile, slot, part):
        base = tile * TME
        per = TME // n_parts
        for r in range(part * per, (part + 1) * per):
            _row_copy(h_hbm, src_ref[base + r], bufs[slot], r, sem.at[slot]).start(priority=r % 2)

    def weight_copies(expert, wslot):
        return (pltpu.make_async_copy(w1_hbm.at[expert], wf1.at[wslot], wsem.at[wslot]),
                pltpu.make_async_copy(w3_hbm.at[expert], wf3.at[wslot], wsem.at[wslot]),
                pltpu.make_async_copy(w2_hbm.at[expert], wf2.at[wslot], wsem.at[wslot]))

    def compute(slot, ahead):
        nxt = (lambda part: issue(i + MOE_AHEAD, (slot + MOE_AHEAD) % n_slots, part)) if ahead else (lambda part: None)
        _tile_wait(h_hbm, bufs[slot], sem.at[slot])
        nxt(0)
        x = bufs[slot][...].astype(BF16)
        nxt(1)
        a = jnp.dot(x, w1b[...], preferred_element_type=F32)
        nxt(2)
        b = jnp.dot(x, w3b[...], preferred_element_type=F32)
        nxt(3)
        hid = _silu(a) * b
        y_ref[...] = jnp.dot(hid.astype(BF16), w2b[...], preferred_element_type=F32)

    @pl.when(i == 0)
    def _():
        for c in weight_copies(te_ref[0], 0):
            c.start()
        for t in range(MOE_AHEAD):
            @pl.when(t < n_used)
            def _():
                for part in range(n_parts):
                    issue(t, t, part)

    prev = jnp.maximum(i - 1, 0)

    @pl.when((i < n_used) & ((i == 0) | (te_ref[i] != te_ref[prev])))
    def _():
        wslot = seg_ref[i] % 2
        for c in weight_copies(te_ref[i], wslot):
            c.wait()
        w1b[...] = wf1[wslot].astype(BF16)
        w3b[...] = wf3[wslot].astype(BF16)
        w2b[...] = wf2[wslot].astype(BF16)

        @pl.when(nxt_ref[i] >= 0)
        def _():
            for c in weight_copies(nxt_ref[i], 1 - wslot):
                c.start()

    for slot in range(n_slots):
        @pl.when((i + MOE_AHEAD < n_used) & (i % n_slots == slot))
        def _():
            compute(slot, True)

        @pl.when((i < n_used) & (i + MOE_AHEAD >= n_used) & (i % n_slots == slot))
        def _():
            compute(slot, False)

    @pl.when(i >= n_used)
    def _():
        y_ref[...] = jnp.zeros_like(y_ref)


def _moe_gather(plan, h2_all, w1, w3, w2):
    n_sorted = plan["src_row"].shape[0]
    grid_spec = pltpu.PrefetchScalarGridSpec(
        num_scalar_prefetch=5,
        grid=(n_sorted // TME,),
        in_specs=[pl.BlockSpec(memory_space=pl.ANY)] * 4,
        out_specs=pl.BlockSpec((TME, D_MODEL), lambda i, *_: (i, 0)),
        scratch_shapes=[pltpu.VMEM((TME, D_MODEL), F32)] * (MOE_AHEAD + 1)
                       + [pltpu.SemaphoreType.DMA((MOE_AHEAD + 1,)),
                          pltpu.VMEM((2, D_MODEL, EXPERT_FF), F32), pltpu.VMEM((2, D_MODEL, EXPERT_FF), F32),
                          pltpu.VMEM((2, EXPERT_FF, D_MODEL), F32),
                          pltpu.SemaphoreType.DMA((2,)),
                          pltpu.VMEM((D_MODEL, EXPERT_FF), BF16), pltpu.VMEM((D_MODEL, EXPERT_FF), BF16),
                          pltpu.VMEM((EXPERT_FF, D_MODEL), BF16)],
    )
    return pl.pallas_call(
        _moe_gather_kernel,
        name="moe",
        grid_spec=grid_spec,
        out_shape=jax.ShapeDtypeStruct((n_sorted, D_MODEL), F32),
        compiler_params=_cparams(1),
    )(plan["tile_expert"], plan["src_row"], plan["n_used"], plan["segment"], plan["next_expert"],
      h2_all, w1, w3, w2)


def _moe(plan, h_sorted, w1, w3, w2):
    n_sorted = h_sorted.shape[0]
    n_tiles = n_sorted // TME
    used = lambda i, te, nu, seg, nxt: (jnp.minimum(i, nu[0] - 1), 0)
    grid_spec = pltpu.PrefetchScalarGridSpec(
        num_scalar_prefetch=4,
        grid=(n_tiles,),
        in_specs=[pl.BlockSpec((TME, D_MODEL), used)] + [pl.BlockSpec(memory_space=pl.ANY)] * 3,
        out_specs=pl.BlockSpec((TME, D_MODEL), lambda i, *_: (i, 0)),
        scratch_shapes=[pltpu.VMEM((2, D_MODEL, EXPERT_FF), F32), pltpu.VMEM((2, D_MODEL, EXPERT_FF), F32),
                        pltpu.VMEM((2, EXPERT_FF, D_MODEL), F32),
                        pltpu.SemaphoreType.DMA((2,)),
                        pltpu.VMEM((D_MODEL, EXPERT_FF), BF16), pltpu.VMEM((D_MODEL, EXPERT_FF), BF16),
                        pltpu.VMEM((EXPERT_FF, D_MODEL), BF16)],
    )
    return pl.pallas_call(
        _moe_kernel,
        name="moe",
        grid_spec=grid_spec,
        out_shape=jax.ShapeDtypeStruct((n_sorted, D_MODEL), F32),
        compiler_params=_cparams(1),
    )(plan["tile_expert"], plan["n_used"], plan["segment"], plan["next_expert"], h_sorted, w1, w3, w2)


def _finish_kernel(grp, pos_ref, x1_ref, mod_ref, rt_ref, nf_ref, y_hbm, out_ref, buf_a, buf_b, sem):
    i = pl.program_id(0)
    n_steps = pl.num_programs(0)
    n_slots = FINISH_AHEAD + 1
    slot = i % n_slots

    def issue(tile, slot_):
        base = (tile * TM + grp.row_off) * 2

        def body(r, carry):
            _row_copy(y_hbm, pos_ref[base + 2 * r], buf_a.at[slot_], r, sem.at[slot_]).start(priority=0)
            _row_copy(y_hbm, pos_ref[base + 2 * r + 1], buf_b.at[slot_], r, sem.at[slot_]).start(priority=1)
            return carry

        lax.fori_loop(0, TM, body, 0, unroll=8)

    @pl.when(i == 0)
    def _():
        for t in range(FINISH_AHEAD):
            @pl.when(t < n_steps)
            def _():
                issue(t, t)

    @pl.when(i + FINISH_AHEAD < n_steps)
    def _():
        issue(i + FINISH_AHEAD, (i + FINISH_AHEAD) % n_slots)

    _tile_wait(y_hbm, buf_a.at[slot], sem.at[slot])
    _tile_wait(y_hbm, buf_b.at[slot], sem.at[slot])
    rt = rt_ref[...]
    moe = rt[:, 2:3] * buf_a[slot] + rt[:, 3:4] * buf_b[slot]
    x2 = x1_ref[...] + _mod_rows(mod_ref, 5) * moe.reshape(grp.nb, grp.tt, D_MODEL)
    ms = jnp.mean(x2 * x2, axis=-1, keepdims=True)
    out_ref[...] = x2 * lax.rsqrt(ms + EPS) * nf_ref[...]


def _finish(grp, pos, x1, mod, route_all, norm_f, y_sorted):
    off = grp.tile_off
    grid_spec = pltpu.PrefetchScalarGridSpec(
        num_scalar_prefetch=1,
        grid=(grp.n_tiles,),
        in_specs=[pl.BlockSpec((grp.nb, grp.tt, D_MODEL), grp.x_map()),
                  pl.BlockSpec((grp.nb, 6, D_MODEL), grp.mod_map()),
                  pl.BlockSpec((TM, LANES), lambda i, p: (i + off, 0)),
                  pl.BlockSpec((1, 1, D_MODEL), lambda i, p: (0, 0, 0)),
                  pl.BlockSpec(memory_space=pl.ANY)],
        out_specs=pl.BlockSpec((grp.nb, grp.tt, D_MODEL), grp.x_map()),
        scratch_shapes=[pltpu.VMEM((FINISH_AHEAD + 1, TM, D_MODEL), F32),
                        pltpu.VMEM((FINISH_AHEAD + 1, TM, D_MODEL), F32),
                        pltpu.SemaphoreType.DMA((FINISH_AHEAD + 1,))],
    )
    return pl.pallas_call(
        functools.partial(_finish_kernel, grp),
        name="finish",
        grid_spec=grid_spec,
        out_shape=jax.ShapeDtypeStruct(x1.shape, F32),
        compiler_params=_cparams(1),
    )(pos, x1, mod, route_all, norm_f.reshape(1, 1, D_MODEL), y_sorted)


def _sort_plan(route_all):
    n_tok = route_all.shape[0]
    n_pairs = 2 * n_tok
    n_sorted = n_pairs + N_EXPERTS * TME
    flat_e = route_all[:, 0:2].astype(jnp.int32).reshape(n_pairs)
    onehot = (flat_e[:, None] == jnp.arange(N_EXPERTS, dtype=jnp.int32)[None, :]).astype(jnp.int32)
    csum = jnp.cumsum(onehot, axis=0)
    rank = jnp.sum(onehot * csum, axis=1) - 1
    counts = csum[-1]
    padded = ((counts + TME - 1) // TME) * TME
    ends = jnp.cumsum(padded)
    starts = ends - padded
    pos = starts[flat_e] + rank
    token = jnp.arange(n_pairs, dtype=jnp.int32) // 2
    src_row = (jnp.arange(n_sorted, dtype=jnp.int32) % n_tok).at[pos].set(token)
    tile_start = jnp.arange(n_sorted // TME, dtype=jnp.int32) * TME
    tile_expert = jnp.sum((tile_start[:, None] >= ends[None, :]).astype(jnp.int32), axis=1)
    tile_expert = jnp.minimum(tile_expert, N_EXPERTS - 1)
    n_used = ends[-1] // TME
    is_first = jnp.concatenate([jnp.ones((1,), jnp.int32),
                                (tile_expert[1:] != tile_expert[:-1]).astype(jnp.int32)])
    segment = jnp.cumsum(is_first) - 1
    next_tile = ends[tile_expert] // TME
    next_expert = jnp.where(next_tile < n_used, tile_expert[jnp.minimum(next_tile, n_sorted // TME - 1)], -1)
    return dict(pos=pos.astype(jnp.int32), src_row=src_row, tile_expert=tile_expert.astype(jnp.int32),
                n_used=n_used.astype(jnp.int32).reshape(1), segment=segment.astype(jnp.int32),
                next_expert=next_expert.astype(jnp.int32))


def kernel(x_prompt, x_sample, c_prompt, c_sample, state_gla, state_pool, w_ada, b_ada, norm1, norm2, w_in,
           gate_up, gate_bias, gla_norm, pool_w, pool_scale, w_out, w_group, b_group, w_expert, b_expert,
           w1, w3, w2, norm_f):
    assert w_ada.shape[0] == 1, "single-layer step"
    bp, tp, _ = x_prompt.shape
    bs, ts, _ = x_sample.shape
    grp_p = _Group(bp, tp, 0, bs)
    grp_s = _Group(bs, ts, bp * tp, 0)
    n_tok = bp * tp + bs * ts

    n_c = bp + bs
    n_c_pad = -(-n_c // 8) * 8
    c_all = jnp.concatenate([c_sample, c_prompt, jnp.zeros((n_c_pad - n_c, D_MODEL), F32)], axis=0)
    mod = _adaln(c_all, w_ada[0], b_ada[0]).reshape(n_c_pad, 6, D_MODEL)
    mod_p = mod_s = mod

    wi = w_in[0]
    wq = wi[:, :QKVR_WIDTH].astype(BF16)
    wg = jnp.pad(wi[:, QKVR_WIDTH:QKVR_WIDTH + GATE_RANK], ((0, 0), (0, LANES - GATE_RANK))).astype(BF16)
    wu = wi[:, QKVR_WIDTH + GATE_RANK:].astype(BF16)
    gup = jnp.pad(gate_up[0], ((0, LANES - GATE_RANK), (0, 0))).astype(BF16)
    gb = gate_bias[0].reshape(1, GLA_KEY_WIDTH)
    pw = pool_w[0].astype(BF16)
    wo = w_out[0].astype(BF16)
    gap = EXPERTS_PER_GROUP - N_GROUPS
    tail = LANES - EXPERTS_PER_GROUP - N_EXPERTS
    w_router = jnp.concatenate(
        [w_group[0], jnp.zeros((D_MODEL, gap), F32),
         jnp.transpose(w_expert[0], (1, 0, 2)).reshape(D_MODEL, N_EXPERTS),
         jnp.zeros((D_MODEL, tail), F32)], axis=1)
    w_router_hi = w_router.astype(BF16)
    w_router = jnp.stack([w_router_hi, (w_router - w_router_hi.astype(F32)).astype(BF16)])
    b_router = jnp.concatenate([b_group[0], jnp.zeros((gap,), F32), b_expert[0].reshape(N_EXPERTS),
                                jnp.zeros((tail,), F32)]).reshape(1, LANES)

    qkvr_p, la_p, u_p = _in_proj(grp_p, x_prompt, mod_p, norm1[0], wq, wg, wu, gup, gb)
    qkvr_s, la_s, u_s = _in_proj(grp_s, x_sample, mod_s, norm1[0], wq, wg, wu, gup, gb)

    o_p, gla_p = _gla_prompt(bp, tp, qkvr_p, la_p)
    o_s, gla_s = _gla_decode(bs, ts, qkvr_s, la_s, state_gla.reshape(state_gla.shape[1:]))

    halo_per_tile = TM // HALO
    halo_map_p = lambda i: (jnp.maximum(i * halo_per_tile - 1, 0), 0)
    x1_p, h2_all, route_all = _mix_out(grp_p, n_tok, 0, True, o_p, qkvr_p, u_p, u_p, (HALO, POOL_WIDTH),
                                       halo_map_p, x_prompt, mod_p, norm2[0], gla_norm[0], pw, pool_scale[0],
                                       wo, w_router, b_router)
    x1_s, h2_all, route_all = _mix_out(grp_s, n_tok, PAST_LEN, False, o_s, qkvr_s, u_s,
                                       state_pool.reshape(bs, POOL_BUF, POOL_WIDTH),
                                       (grp_s.nb, POOL_BUF, POOL_WIDTH), lambda i: (i, 0, 0), x_sample, mod_s,
                                       norm2[0], gla_norm[0], pw, pool_scale[0], wo, w_router, b_router,
                                       shared=(h2_all, route_all))

    plan = _sort_plan(route_all)
    y_sorted = _moe_gather(plan, h2_all, w1.reshape(w1.shape[1:]), w3.reshape(w3.shape[1:]),
                           w2.reshape(w2.shape[1:]))

    y_p = _finish(grp_p, plan["pos"], x1_p, mod_p, route_all, norm_f, y_sorted)
    y_s = _finish(grp_s, plan["pos"], x1_s, mod_s, route_all, norm_f, y_sorted)

    u_p3 = u_p.reshape(bp, tp, POOL_WIDTH)
    u_s3 = u_s.reshape(bs, ts, POOL_WIDTH)
    assert tp >= POOL_BUF > ts
    pool_p = u_p3[:, tp - POOL_BUF:]
    pool_s = jnp.concatenate([state_pool.reshape(bs, POOL_BUF, POOL_WIDTH)[:, ts:], u_s3], axis=1)
    lead = lambda a: a.reshape((1,) + a.shape)
    return (y_p, y_s, lead(gla_p), lead(pool_p), lead(gla_s), lead(pool_s))
```

```python
import functools

import jax
import jax.numpy as jnp
from jax import lax
from jax.experimental import pallas as pl
from jax.experimental.pallas import tpu as pltpu

D_MODEL = 2048
GLA_HEADS = 4
GLA_DK = 128
GLA_DV = 256
GLA_KEY_WIDTH = GLA_HEADS * GLA_DK
GLA_WIDTH = GLA_HEADS * GLA_DV
POOL_WIDTH = 1024
POOL_WINDOWS = (2, 4, 8, 16)
POOL_GW = 256
POOL_BUF = 15
HALO = 16
EXT_PAD = 8
EXT_TOK0 = EXT_PAD + HALO
GATE_RANK = 16
GATE_TEMP = 16.0
N_GROUPS = 4
EXPERTS_PER_GROUP = 8
N_EXPERTS = 32
EXPERT_FF = 512
EPS = 1e-6
PAST_LEN = 16384
QKVR_WIDTH = 2 * GLA_KEY_WIDTH + 2 * GLA_WIDTH

LANES = 128
TM = 256
TME = 256
TD = 1024
FINISH_AHEAD = 3
GLA_CHUNK = 64
GLA_SUB = 8
GLA_STEP = 512
GLA_TRIP = 4
VMEM_LIMIT = 56 * 1024 * 1024

BF16 = jnp.bfloat16
F32 = jnp.float32
NEG = -1e30
LOG2E = 1.4426950408889634


def _cparams(n_axes):
    return pltpu.CompilerParams(dimension_semantics=("arbitrary",) * n_axes,
                                vmem_limit_bytes=VMEM_LIMIT)


def _silu(x):
    return x / (1.0 + jnp.exp(-x))


def _bdot(a, b):
    return jnp.dot(a.astype(BF16), b.astype(BF16), preferred_element_type=F32)


def _split3(a):
    a1 = a.astype(BF16)
    r1 = a - a1.astype(F32)
    a2 = r1.astype(BF16)
    a3 = (r1 - a2.astype(F32)).astype(BF16)
    return a1, a2, a3


def _split2(a):
    hi = a.astype(BF16)
    lo = (a - hi.astype(F32)).astype(BF16)
    return hi, lo


def _dot_3pass(a, b_hi, b_lo):
    a_hi, a_lo = _split2(a)
    d = lambda x, y: jnp.dot(x, y, preferred_element_type=F32)
    return d(a_hi, b_hi) + (d(a_hi, b_lo) + d(a_lo, b_hi))


def _dot_exact_lhs(tri_bf16, g):
    g1, g2, g3 = _split3(g)
    d = lambda y: jnp.dot(tri_bf16, y, preferred_element_type=F32)
    return d(g1) + (d(g2) + d(g3))


def _adaln_kernel(c_ref, w_ref, b_ref, o_ref):
    c = c_ref[...]
    o_ref[...] = _bdot(_silu(c), w_ref[...]) + b_ref[...]


def _adaln(c_all, w_ada, b_ada):
    n, d = c_all.shape
    width = w_ada.shape[1]
    tn = 1024
    return pl.pallas_call(
        _adaln_kernel,
        name="adaln",
        grid=(width // tn,),
        in_specs=[pl.BlockSpec((n, d), lambda j: (0, 0)),
                  pl.BlockSpec((d, tn), lambda j: (0, j)),
                  pl.BlockSpec((1, tn), lambda j: (0, j))],
        out_specs=pl.BlockSpec((n, tn), lambda j: (0, j)),
        out_shape=jax.ShapeDtypeStruct((n, width), F32),
        compiler_params=_cparams(1),
    )(c_all, w_ada, b_ada.reshape(1, width))


class _Group:
    def __init__(self, batch, seq, row_off, mod_off):
        self.batch, self.seq, self.row_off, self.mod_off = batch, seq, row_off, mod_off
        if seq >= TM:
            assert seq % TM == 0
            self.nb, self.tt = 1, TM
            self.tiles_per_batch = seq // TM
            self.n_tiles = batch * self.tiles_per_batch
        else:
            assert TM % seq == 0 and batch % (TM // seq) == 0
            self.nb, self.tt = TM // seq, seq
            self.tiles_per_batch = 1
            self.n_tiles = batch // self.nb
        self.rows = batch * seq
        self.tile_off = row_off // TM

    def x_map(self):
        if self.nb == 1:
            tpb = self.tiles_per_batch
            return lambda i, *_: (i // tpb, i % tpb, 0)
        return lambda i, *_: (i, 0, 0)

    def mod_map(self):
        assert self.mod_off % self.nb == 0
        off = self.mod_off // self.nb
        if self.nb == 1:
            tpb = self.tiles_per_batch
            return lambda i, *_: (i // tpb + off, 0, 0)
        return lambda i, *_: (i + off, 0, 0)


def _mod_rows(mod_ref, idx):
    return mod_ref[:, idx:idx + 1, :]


def _rmsnorm_mod(x, gain, scale, shift):
    ms = jnp.mean(x * x, axis=-1, keepdims=True)
    y = x * lax.rsqrt(ms + EPS) * gain
    return y * (1.0 + scale) + shift


def _in_proj_kernel(x_ref, mod_ref, n1_ref, wq_ref, wg_ref, wu_ref, gup_ref, gb_ref,
                    qkvr_ref, la_ref, u_ref):
    x = x_ref[...]
    h = _rmsnorm_mod(x, n1_ref[...], _mod_rows(mod_ref, 1), _mod_rows(mod_ref, 0))
    hb = h.reshape(TM, D_MODEL).astype(BF16)
    nt = (((1,), (1,)), ((), ()))
    qkvr_ref[...] = lax.dot_general(hb, wq_ref[...], nt, preferred_element_type=F32)
    u_ref[...] = lax.dot_general(hb, wu_ref[...], nt, preferred_element_type=F32)
    g_lr = lax.dot_general(hb, wg_ref[...], nt, preferred_element_type=F32)
    pre = jnp.dot(g_lr.astype(BF16), gup_ref[...], preferred_element_type=F32) + gb_ref[...]
    log_sig = jnp.minimum(pre, 0.0) - jnp.log1p(jnp.exp(-jnp.abs(pre)))
    la_ref[...] = log_sig / GATE_TEMP


def _in_proj(grp, x, mod, norm1, w_in_t, gup, gb):
    const = lambda i: (0, 0)
    u_row0 = QKVR_WIDTH + GATE_RANK
    row = lambda i: (i, 0)
    return pl.pallas_call(
        _in_proj_kernel,
        name="in_proj",
        grid=(grp.n_tiles,),
        in_specs=[pl.BlockSpec((grp.nb, grp.tt, D_MODEL), grp.x_map()),
                  pl.BlockSpec((grp.nb, 6, D_MODEL), grp.mod_map()),
                  pl.BlockSpec((1, 1, D_MODEL), lambda i: (0, 0, 0)),
                  pl.BlockSpec((QKVR_WIDTH, D_MODEL), const, pipeline_mode=pl.Buffered(1)),
                  pl.BlockSpec((GATE_RANK, D_MODEL), lambda i: (QKVR_WIDTH // GATE_RANK, 0),
                               pipeline_mode=pl.Buffered(1)),
                  pl.BlockSpec((pl.Element(POOL_WIDTH), pl.Element(D_MODEL)), lambda i: (u_row0, 0),
                               pipeline_mode=pl.Buffered(1)),
                  pl.BlockSpec(gup.shape, const, pipeline_mode=pl.Buffered(1)),
                  pl.BlockSpec(gb.shape, const, pipeline_mode=pl.Buffered(1))],
        out_specs=[pl.BlockSpec((TM, QKVR_WIDTH), row),
                   pl.BlockSpec((TM, GLA_KEY_WIDTH), row),
                   pl.BlockSpec((TM, POOL_WIDTH), row)],
        out_shape=[jax.ShapeDtypeStruct((grp.rows, QKVR_WIDTH), F32),
                   jax.ShapeDtypeStruct((grp.rows, GLA_KEY_WIDTH), F32),
                   jax.ShapeDtypeStruct((grp.rows, POOL_WIDTH), F32)],
        compiler_params=_cparams(1),
    )(x, mod, norm1.reshape(1, 1, D_MODEL), w_in_t, w_in_t, w_in_t, gup, gb)


def _gla_select_matrix(chunk, sub):
    r = jnp.arange(sub * GLA_DK, dtype=jnp.int32)[:, None] // GLA_DK
    l = jnp.arange(LANES, dtype=jnp.int32)[None, :]
    return ((l % sub == r) & (l < chunk)).astype(BF16)


def _gla_chunks(chunks, states, chunk, sub, wsel, t_refs, kb_refs, chained=True):
    n_sub = chunk // sub
    rows = lax.broadcasted_iota(jnp.int32, (chunk, chunk), 0)
    cols = lax.broadcasted_iota(jnp.int32, (chunk, chunk), 1)
    tri = (rows >= cols).astype(BF16)
    nt = (((1,), (1,)), ((), ()))
    tn = (((0,), (0,)), ((), ()))
    key_row = lax.broadcasted_iota(jnp.int32, (chunk, LANES), 0)
    lane = lax.broadcasted_iota(jnp.int32, (sub, chunk), 1)
    row = lax.broadcasted_iota(jnp.int32, (sub, chunk), 0)
    head = lambda a, h, w: a[:, h * w:(h + 1) * w]
    n = len(chunks)

    b4s = [_dot_exact_lhs(tri, g4) for (_, _, _, g4) in chunks]
    q4s = [q4 * (GLA_DK ** -0.5) for (q4, _, _, _) in chunks]

    for c in range(n):
        kb_ref, t_ref, k4 = kb_refs[c], t_refs[c], chunks[c][1]
        for h in range(GLA_HEADS):
            kb_ref[h] = head(k4, h, GLA_DK)
            kb_ref[GLA_HEADS + h] = head(b4s[c], h, GLA_DK) * LOG2E
        for h in range(GLA_HEADS):
            for s in range(n_sub):
                lo = s * sub
                r0 = (h * n_sub + s) * sub
                q_s, b_s = head(q4s[c], h, GLA_DK)[lo:lo + sub, :], kb_ref[GLA_HEADS + h, lo:lo + sub, :]
                for jl in range(sub):
                    k_j = jnp.broadcast_to(kb_ref[h, lo + jl:lo + jl + 1, :], (sub, GLA_DK))
                    b_j = jnp.broadcast_to(kb_ref[GLA_HEADS + h, lo + jl:lo + jl + 1, :], (sub, GLA_DK))
                    decay = jnp.exp2(jnp.minimum(b_s - b_j, 0.0))
                    t_ref[r0:r0 + sub, jl * GLA_DK:(jl + 1) * GLA_DK] = (q_s * k_j * decay).astype(t_ref.dtype)
    p_diags = [jnp.dot(t_refs[c][...].astype(BF16), wsel, preferred_element_type=F32) for c in range(n)]

    intra = []
    for c in range(n):
        per_head = []
        for h in range(GLA_HEADS):
            q, k, b = head(q4s[c], h, GLA_DK), head(chunks[c][1], h, GLA_DK), head(b4s[c], h, GLA_DK)
            p_blocks = []
            for s in range(n_sub):
                lo = s * sub
                r0 = (h * n_sub + s) * sub
                in_block = (lane >= lo) & (lane - lo <= row)
                p = jnp.where(in_block, p_diags[c][r0:r0 + sub, :chunk], 0.0)
                if s > 0:
                    ref_row = b[lo - 1:lo, :]
                    q_rel = q[lo:lo + sub, :] * jnp.exp(b[lo:lo + sub, :] - ref_row)
                    k_rel = k * jnp.exp(jnp.where(key_row < lo, ref_row - b, NEG))
                    p = p + lax.dot_general(q_rel.astype(BF16), k_rel.astype(BF16), nt,
                                            preferred_element_type=F32)
                p_blocks.append(p)
            p_full = p_blocks[0] if n_sub == 1 else jnp.concatenate(p_blocks, axis=0)
            per_head.append(_bdot(p_full, head(chunks[c][2], h, GLA_DV)))
        intra.append(per_head)

    outs, end_states = [], []
    for c in range(n):
        cur = states if chained else states[c]
        o_heads, new_states = [], []
        for h in range(GLA_HEADS):
            q, k, b = head(q4s[c], h, GLA_DK), head(chunks[c][1], h, GLA_DK), head(b4s[c], h, GLA_DK)
            v = head(chunks[c][2], h, GLA_DV)
            o_heads.append(intra[c][h] + _bdot(q * jnp.exp(b), cur[h]))
            b_last = b[chunk - 1:chunk, :]
            k_dec = k * jnp.exp(b_last - b)
            decay_col = jnp.exp(b[chunk - 8:chunk, :]).T[:, 7:8]
            new_states.append(decay_col * cur[h] + lax.dot_general(
                k_dec.astype(BF16), v.astype(BF16), tn, preferred_element_type=F32))
        states = new_states if chained else states
        end_states.append(new_states)
        outs.append(jnp.concatenate(o_heads, axis=1))
    return outs, (end_states[-1] if chained else end_states)


def _gla_prompt_kernel(q_ref, k_ref, v_ref, la_ref, wsel_ref, o_ref, s_ref, t_ref, kb_ref):
    @pl.when(pl.program_id(1) == 0)
    def _():
        s_ref[...] = jnp.zeros_like(s_ref)

    def body(pair, carry):
        states = [s_ref[0, h] for h in range(GLA_HEADS)]
        slices = [pl.ds(pl.multiple_of((GLA_TRIP * pair + c) * GLA_CHUNK, GLA_CHUNK), GLA_CHUNK)
                  for c in range(GLA_TRIP)]
        chunks = [(q_ref[sl, :], k_ref[sl, :], v_ref[sl, :], la_ref[sl, :]) for sl in slices]
        outs, states = _gla_chunks(chunks, states, GLA_CHUNK, GLA_SUB, wsel_ref[...],
                                   [t_ref.at[c] for c in range(GLA_TRIP)],
                                   [kb_ref.at[c] for c in range(GLA_TRIP)])
        for sl, o in zip(slices, outs):
            o_ref[sl, :] = o
        for h in range(GLA_HEADS):
            s_ref[0, h] = states[h]
        return carry

    lax.fori_loop(0, GLA_STEP // (GLA_TRIP * GLA_CHUNK), body, 0)


def _gla_prompt(batch, seq, qkvr, log_a):
    steps = seq // GLA_STEP
    row = lambda b, s: b * steps + s
    wsel = _gla_select_matrix(GLA_CHUNK, GLA_SUB)
    return pl.pallas_call(
        _gla_prompt_kernel,
        name="gla_prompt",
        grid=(batch, steps),
        in_specs=[pl.BlockSpec((GLA_STEP, GLA_KEY_WIDTH), lambda b, s: (row(b, s), 0)),
                  pl.BlockSpec((GLA_STEP, GLA_KEY_WIDTH), lambda b, s: (row(b, s), 1)),
                  pl.BlockSpec((GLA_STEP, GLA_WIDTH), lambda b, s: (row(b, s), 1)),
                  pl.BlockSpec((GLA_STEP, GLA_KEY_WIDTH), lambda b, s: (row(b, s), 0)),
                  pl.BlockSpec(wsel.shape, lambda b, s: (0, 0))],
        out_specs=[pl.BlockSpec((GLA_STEP, GLA_WIDTH), lambda b, s: (row(b, s), 0)),
                   pl.BlockSpec((1, GLA_HEADS, GLA_DK, GLA_DV), lambda b, s: (b, 0, 0, 0))],
        out_shape=[jax.ShapeDtypeStruct((batch * seq, GLA_WIDTH), F32),
                   jax.ShapeDtypeStruct((batch, GLA_HEADS, GLA_DK, GLA_DV), F32)],
        scratch_shapes=[pltpu.VMEM((GLA_TRIP, GLA_HEADS * GLA_CHUNK, GLA_SUB * GLA_DK), BF16),
                        pltpu.VMEM((GLA_TRIP, 2 * GLA_HEADS, GLA_CHUNK, GLA_DK), F32)],
        compiler_params=_cparams(2),
    )(qkvr, qkvr, qkvr, log_a, wsel)


GLA_DEC_BB = 8
GLA_DEC_TRIP = 4


def _gla_decode_kernel(seq, q_ref, k_ref, v_ref, la_ref, wsel_ref, s0_ref, o_ref, s_ref, t_ref, kb_ref):
    def body(trip, carry):
        elems = [GLA_DEC_TRIP * trip + c for c in range(GLA_DEC_TRIP)]
        slices = [pl.ds(pl.multiple_of(e * seq, seq), seq) for e in elems]
        chunks = [(q_ref[sl, :], k_ref[sl, :], v_ref[sl, :], la_ref[sl, :]) for sl in slices]
        states = [[s0_ref[e, h] for h in range(GLA_HEADS)] for e in elems]
        outs, new_states = _gla_chunks(chunks, states, seq, seq, wsel_ref[...],
                                       [t_ref.at[c] for c in range(GLA_DEC_TRIP)],
                                       [kb_ref.at[c] for c in range(GLA_DEC_TRIP)], chained=False)
        for e, sl, o, ns in zip(elems, slices, outs, new_states):
            o_ref[sl, :] = o
            for h in range(GLA_HEADS):
                s_ref[e, h] = ns[h]
        return carry

    lax.fori_loop(0, GLA_DEC_BB // GLA_DEC_TRIP, body, 0)


def _gla_decode(batch, seq, qkvr, log_a, state):
    rows = GLA_DEC_BB * seq
    wsel = _gla_select_matrix(seq, seq)
    state_spec = pl.BlockSpec((GLA_DEC_BB, GLA_HEADS, GLA_DK, GLA_DV), lambda i: (i, 0, 0, 0))
    return pl.pallas_call(
        functools.partial(_gla_decode_kernel, seq),
        name="gla_decode",
        grid=(batch // GLA_DEC_BB,),
        in_specs=[pl.BlockSpec((rows, GLA_KEY_WIDTH), lambda i: (i, 0)),
                  pl.BlockSpec((rows, GLA_KEY_WIDTH), lambda i: (i, 1)),
                  pl.BlockSpec((rows, GLA_WIDTH), lambda i: (i, 1)),
                  pl.BlockSpec((rows, GLA_KEY_WIDTH), lambda i: (i, 0)),
                  pl.BlockSpec(wsel.shape, lambda i: (0, 0)),
                  state_spec],
        out_specs=[pl.BlockSpec((rows, GLA_WIDTH), lambda i: (i, 0)), state_spec],
        out_shape=[jax.ShapeDtypeStruct((batch * seq, GLA_WIDTH), F32),
                   jax.ShapeDtypeStruct((batch, GLA_HEADS, GLA_DK, GLA_DV), F32)],
        scratch_shapes=[pltpu.VMEM((GLA_DEC_TRIP, GLA_HEADS * seq, seq * GLA_DK), F32),
                        pltpu.VMEM((GLA_DEC_TRIP, 2 * GLA_HEADS, seq, GLA_DK), F32)],
        compiler_params=_cparams(1),
    )(qkvr, qkvr, qkvr, log_a, wsel, state)


def _route(logits):
    lt = logits.T
    n = lt.shape[1]
    big = jnp.int32(10 ** 6)
    row8 = lax.broadcasted_iota(jnp.int32, (EXPERTS_PER_GROUP, n), 0)
    lg = jnp.where(row8 < N_GROUPS, lt[0:EXPERTS_PER_GROUP], NEG)
    mg = jnp.max(lg, axis=0, keepdims=True)
    g_idx = jnp.min(jnp.where(lg == mg, row8, big), axis=0, keepdims=True)
    p_sel = 1.0 / jnp.sum(jnp.exp(lg - mg), axis=0, keepdims=True)
    le = jnp.zeros((EXPERTS_PER_GROUP, n), F32)
    for g in range(N_GROUPS):
        lo = EXPERTS_PER_GROUP * (g + 1)
        le = jnp.where(g_idx == g, lt[lo:lo + EXPERTS_PER_GROUP], le)
    m1 = jnp.max(le, axis=0, keepdims=True)
    i1 = jnp.min(jnp.where(le == m1, row8, big), axis=0, keepdims=True)
    rest = row8 != i1
    m2 = jnp.max(jnp.where(rest, le, NEG), axis=0, keepdims=True)
    i2 = jnp.min(jnp.where(rest & (le == m2), row8, big), axis=0, keepdims=True)
    e2 = jnp.exp(m2 - m1)
    w1 = p_sel / (1.0 + e2)
    w2 = p_sel * e2 / (1.0 + e2)
    ex1 = (g_idx * EXPERTS_PER_GROUP + i1).astype(F32)
    ex2 = (g_idx * EXPERTS_PER_GROUP + i2).astype(F32)
    packed = jnp.where(row8 == 0, ex1, jnp.where(row8 == 1, ex2,
                       jnp.where(row8 == 2, w1, jnp.where(row8 == 3, w2, 0.0))))
    full = jnp.concatenate([packed, jnp.zeros((LANES - EXPERTS_PER_GROUP, n), F32)], axis=0)
    return full.T


def _mix_out_kernel(grp, pos0, zero_first_halo, n_alias,
                    o_ref, r_ref, u_ref, halo_ref, x_ref, mod_ref, n2_ref, gn_ref, pw_ref, ps_ref,
                    wo_ref, wr_ref, br_ref, *rest):
    x1_ref, h2_ref, rt_ref, ext_ref, lvl_a, lvl_b, ymix_ref = rest[n_alias:]
    i = pl.program_id(0)

    @pl.when(i < grp.n_tiles)
    def _():
        _mix_out_tile(grp, pos0, zero_first_halo, i, o_ref, r_ref, u_ref, halo_ref, x_ref, mod_ref, n2_ref,
                      gn_ref, pw_ref, ps_ref, wo_ref, wr_ref, br_ref, x1_ref, h2_ref, rt_ref, ext_ref,
                      (lvl_a, lvl_b), ymix_ref)

    @pl.when(i >= grp.n_tiles)
    def _():
        h2_ref[...] = jnp.zeros_like(h2_ref)
        rt_ref[...] = jnp.zeros_like(rt_ref)


def _mix_out_tile(grp, pos0, zero_first_halo, i, o_ref, r_ref, u_ref, halo_ref, x_ref, mod_ref, n2_ref,
                  gn_ref, pw_ref, ps_ref, wo_ref, wr_ref, br_ref, x1_ref, h2_ref, rt_ref, ext_ref, lvl_refs,
                  ymix_ref):
    nb, tt = grp.nb, grp.tt
    hist = halo_ref.shape[-2]

    for h in range(GLA_HEADS):
        cs = slice(h * GLA_DV, (h + 1) * GLA_DV)
        oh = o_ref[:, cs]
        ms = jnp.mean(oh * oh, axis=-1, keepdims=True)
        yh = oh * lax.rsqrt(ms + EPS) * gn_ref[:, cs] * _silu(r_ref[:, cs])
        ymix_ref[:, cs] = yh.astype(BF16)

    halo = halo_ref[...]
    if zero_first_halo:
        halo = jnp.where(i % grp.tiles_per_batch == 0, 0.0, halo)
    n_ext = EXT_TOK0 + tt
    ext_ref[:, 0:EXT_TOK0 - hist, :] = jnp.zeros((nb, EXT_TOK0 - hist, POOL_WIDTH), F32)
    ext_ref[:, EXT_TOK0 - hist:EXT_TOK0, :] = halo.reshape(nb, hist, POOL_WIDTH)
    u = u_ref[...].reshape(nb, tt, POOL_WIDTH)
    ext_ref[:, EXT_TOK0:n_ext, :] = u
    for lvl_ref in lvl_refs:
        lvl_ref[:, 0:EXT_PAD, :] = jnp.zeros((nb, EXT_PAD, POOL_GW), F32)
    t_idx = lax.broadcasted_iota(jnp.int32, (nb, tt, POOL_GW), 1)
    if grp.nb == 1:
        pos = (i % grp.tiles_per_batch) * TM + t_idx + pos0
    else:
        pos = t_idx + pos0
    for gi, w in enumerate(POOL_WINDOWS):
        cs = slice(gi * POOL_GW, (gi + 1) * POOL_GW)
        cur = lambda lo, hi: ext_ref[:, lo:hi, cs]
        d, level = 1, 0
        while d < w:
            nxt = lvl_refs[level % 2]
            nxt[:, EXT_PAD:n_ext, :] = cur(EXT_PAD, n_ext) + cur(EXT_PAD - d, n_ext - d)
            cur = lambda lo, hi, ref=nxt: ref[:, lo:hi, :]
            d, level = 2 * d, level + 1
        acc = cur(EXT_TOK0, n_ext)
        cnt = jnp.minimum(pos + 1, w).astype(F32)
        pooled = acc / cnt - u[:, :, cs]
        yp = _bdot(pooled.reshape(TM, POOL_GW), pw_ref[gi]) * ps_ref[:, cs]
        ymix_ref[:, GLA_WIDTH + gi * POOL_GW:GLA_WIDTH + (gi + 1) * POOL_GW] = yp.astype(BF16)

    y = jnp.dot(ymix_ref[...], wo_ref[...], preferred_element_type=F32)
    x1 = x_ref[...] + _mod_rows(mod_ref, 2) * y.reshape(nb, tt, D_MODEL)
    x1_ref[...] = x1
    h2 = _rmsnorm_mod(x1, n2_ref[...], _mod_rows(mod_ref, 4), _mod_rows(mod_ref, 3)).reshape(TM, D_MODEL)
    h2_ref[...] = h2
    logits = _dot_3pass(h2, wr_ref[0], wr_ref[1]) + br_ref[...]
    rt_ref[...] = _route(logits)


def _mix_out(grp, n_tok, pos0, zero_first_halo, o, qkvr, u, halo_src, halo_block, halo_map, x, mod, norm2,
             gla_norm, pool_w, pool_scale, w_out, w_router, b_router, shared=()):
    n_alias = len(shared)
    n = grp.n_tiles
    n_fill = 0 if shared else n_tok // TM - n
    assert n_fill == 0 or grp.tile_off == 0
    clamp = lambda f: (lambda i: f(jnp.minimum(i, n - 1)))
    const2 = lambda i: (0, 0)
    row = clamp(lambda i: (i, 0))
    off = grp.tile_off
    kern = functools.partial(_mix_out_kernel, grp, pos0, zero_first_halo, n_alias)
    return pl.pallas_call(
        kern,
        name="mix_out",
        grid=(n + n_fill,),
        in_specs=[pl.BlockSpec((TM, GLA_WIDTH), row),
                  pl.BlockSpec((TM, GLA_WIDTH), clamp(lambda i: (i, 2))),
                  pl.BlockSpec((TM, POOL_WIDTH), row),
                  pl.BlockSpec(halo_block, clamp(halo_map)),
                  pl.BlockSpec((grp.nb, grp.tt, D_MODEL), clamp(grp.x_map())),
                  pl.BlockSpec((grp.nb, 6, D_MODEL), clamp(grp.mod_map())),
                  pl.BlockSpec((1, 1, D_MODEL), lambda i: (0, 0, 0)),
                  pl.BlockSpec((1, GLA_WIDTH), const2),
                  pl.BlockSpec(pool_w.shape, lambda i: (0, 0, 0), pipeline_mode=pl.Buffered(1)),
                  pl.BlockSpec((1, POOL_WIDTH), const2),
                  pl.BlockSpec(w_out.shape, const2, pipeline_mode=pl.Buffered(1)),
                  pl.BlockSpec(w_router.shape, lambda i: (0, 0, 0), pipeline_mode=pl.Buffered(1)),
                  pl.BlockSpec((1, LANES), const2)]
                 + [pl.BlockSpec(memory_space=pl.ANY)] * n_alias,
        out_specs=[pl.BlockSpec((grp.nb, grp.tt, D_MODEL), clamp(grp.x_map())),
                   pl.BlockSpec((TM, D_MODEL), lambda i: (i + off, 0)),
                   pl.BlockSpec((TM, LANES), lambda i: (i + off, 0))],
        out_shape=[jax.ShapeDtypeStruct(x.shape, F32),
                   jax.ShapeDtypeStruct((n_tok, D_MODEL), F32),
                   jax.ShapeDtypeStruct((n_tok, LANES), F32)],
        scratch_shapes=[pltpu.VMEM((grp.nb, EXT_TOK0 + grp.tt, POOL_WIDTH), F32),
                        pltpu.VMEM((grp.nb, EXT_TOK0 + grp.tt, POOL_GW), F32),
                        pltpu.VMEM((grp.nb, EXT_TOK0 + grp.tt, POOL_GW), F32),
                        pltpu.VMEM((TM, D_MODEL), BF16)],
        input_output_aliases={13 + k: 1 + k for k in range(n_alias)},
        compiler_params=_cparams(1),
    )(o, qkvr, u, halo_src, x, mod, norm2.reshape(1, 1, D_MODEL), gla_norm.reshape(1, GLA_WIDTH),
      pool_w, pool_scale.reshape(1, POOL_WIDTH), w_out, w_router, b_router, *shared)


def _row_copy(src_hbm, src_row, dst, dst_row, sem):
    return pltpu.make_async_copy(src_hbm.at[pl.ds(src_row, 1), :], dst.at[pl.ds(dst_row, 1), :], sem)


def _tile_wait(src_hbm, dst, sem):
    pltpu.make_async_copy(src_hbm.at[pl.ds(0, dst.shape[0]), :], dst, sem).wait()


def _dispatch_kernel(src_ref, nu_ref, h_hbm, hs_ref, buf, sem):
    i = pl.program_id(0)
    n_used = (nu_ref[0] * TME + TD - 1) // TD
    slot = i % 2

    def issue(tile, slot_):
        base = tile * TD

        def body(r, carry):
            _row_copy(h_hbm, src_ref[base + 2 * r], buf.at[slot_], 2 * r, sem.at[slot_]).start(priority=0)
            _row_copy(h_hbm, src_ref[base + 2 * r + 1], buf.at[slot_], 2 * r + 1, sem.at[slot_]).start(priority=1)
            return carry

        lax.fori_loop(0, TD // 2, body, 0, unroll=4)

    @pl.when(i == 0)
    def _():
        issue(0, 0)

    @pl.when(i + 1 < n_used)
    def _():
        issue(i + 1, 1 - slot)

    @pl.when(i < n_used)
    def _():
        _tile_wait(h_hbm, buf.at[slot], sem.at[slot])
        hs_ref[...] = buf[slot].astype(BF16)

    @pl.when(i >= n_used)
    def _():
        hs_ref[...] = jnp.zeros_like(hs_ref)


def _dispatch(plan, h2_all):
    n_sorted = plan["src_row"].shape[0]
    assert n_sorted % TD == 0
    grid_spec = pltpu.PrefetchScalarGridSpec(
        num_scalar_prefetch=2,
        grid=(n_sorted // TD,),
        in_specs=[pl.BlockSpec(memory_space=pl.ANY)],
        out_specs=pl.BlockSpec((TD, D_MODEL), lambda i, *_: (i, 0)),
        scratch_shapes=[pltpu.VMEM((2, TD, D_MODEL), F32), pltpu.SemaphoreType.DMA((2,))],
    )
    return pl.pallas_call(
        _dispatch_kernel,
        name="dispatch",
        grid_spec=grid_spec,
        out_shape=jax.ShapeDtypeStruct((n_sorted, D_MODEL), BF16),
        compiler_params=_cparams(1),
    )(plan["src_row"], plan["n_used"], h2_all)


def _moe_kernel(te_ref, nu_ref, seg_ref, nxt_ref, hs_ref, w1_hbm, w3_hbm, w2_hbm, y_ref,
                wf1, wf3, wf2, wsem, w1b, w3b, w2b):
    i = pl.program_id(0)
    n_used = nu_ref[0]

    def weight_copies(expert, wslot):
        return (pltpu.make_async_copy(w1_hbm.at[expert], wf1.at[wslot], wsem.at[wslot]),
                pltpu.make_async_copy(w3_hbm.at[expert], wf3.at[wslot], wsem.at[wslot]),
                pltpu.make_async_copy(w2_hbm.at[expert], wf2.at[wslot], wsem.at[wslot]))

    @pl.when(i == 0)
    def _():
        for c in weight_copies(te_ref[0], 0):
            c.start()

    prev = jnp.maximum(i - 1, 0)

    @pl.when((i < n_used) & ((i == 0) | (te_ref[i] != te_ref[prev])))
    def _():
        wslot = seg_ref[i] % 2
        for c in weight_copies(te_ref[i], wslot):
            c.wait()
        w1b[...] = wf1[wslot].astype(BF16)
        w3b[...] = wf3[wslot].astype(BF16)
        w2b[...] = wf2[wslot].astype(BF16)

        @pl.when(nxt_ref[i] >= 0)
        def _():
            for c in weight_copies(nxt_ref[i], 1 - wslot):
                c.start()

    @pl.when(i < n_used)
    def _():
        x = hs_ref[...]
        a = jnp.dot(x, w1b[...], preferred_element_type=F32)
        b = jnp.dot(x, w3b[...], preferred_element_type=F32)
        hid = _silu(a) * b
        y_ref[...] = jnp.dot(hid.astype(BF16), w2b[...], preferred_element_type=F32)

    @pl.when(i >= n_used)
    def _():
        y_ref[...] = jnp.zeros_like(y_ref)


MOE_AHEAD = 2


def _moe_gather_kernel(te_ref, src_ref, nu_ref, seg_ref, nxt_ref, h_hbm, w1_hbm, w3_hbm, w2_hbm, y_ref,
                       buf0, buf1, buf2, sem, wf1, wf3, wf2, wsem, w1b, w3b, w2b):
    i = pl.program_id(0)
    n_used = nu_ref[0]
    bufs = (buf0, buf1, buf2)
    n_slots = MOE_AHEAD + 1
    n_parts = 4

    def issue(tile, slot, part):
        base = tile * TME
        per = TME // n_parts
        for r in range(part * per, (part + 1) * per):
            _row_copy(h_hbm, src_ref[base + r], bufs[slot], r, sem.at[slot]).start(priority=r % 2)

    def weight_copies(expert, wslot):
        return (pltpu.make_async_copy(w1_hbm.at[expert], wf1.at[wslot], wsem.at[wslot]),
                pltpu.make_async_copy(w3_hbm.at[expert], wf3.at[wslot], wsem.at[wslot]),
                pltpu.make_async_copy(w2_hbm.at[expert], wf2.at[wslot], wsem.at[wslot]))

    def compute(slot, ahead):
        nxt = (lambda part: issue(i + MOE_AHEAD, (slot + MOE_AHEAD) % n_slots, part)) if ahead else (lambda part: None)
        _tile_wait(h_hbm, bufs[slot], sem.at[slot])
        nxt(0)
        x = bufs[slot][...].astype(BF16)
        nxt(1)
        a = jnp.dot(x, w1b[...], preferred_element_type=F32)
        nxt(2)
        b = jnp.dot(x, w3b[...], preferred_element_type=F32)
        nxt(3)
        hid = _silu(a) * b
        y_ref[...] = jnp.dot(hid.astype(BF16), w2b[...], preferred_element_type=F32)

    @pl.when(i == 0)
    def _():
        for c in weight_copies(te_ref[0], 0):
            c.start()
        for t in range(MOE_AHEAD):
            @pl.when(t < n_used)
            def _():
                for part in range(n_parts):
                    issue(t, t, part)

    prev = jnp.maximum(i - 1, 0)

    @pl.when((i < n_used) & ((i == 0) | (te_ref[i] != te_ref[prev])))
    def _():
        wslot = seg_ref[i] % 2
        for c in weight_copies(te_ref[i], wslot):
            c.wait()
        w1b[...] = wf1[wslot].astype(BF16)
        w3b[...] = wf3[wslot].astype(BF16)
        w2b[...] = wf2[wslot].astype(BF16)

        @pl.when(nxt_ref[i] >= 0)
        def _():
            for c in weight_copies(nxt_ref[i], 1 - wslot):
                c.start()

    for slot in range(n_slots):
        @pl.when((i + MOE_AHEAD < n_used) & (i % n_slots == slot))
        def _():
            compute(slot, True)

        @pl.when((i < n_used) & (i + MOE_AHEAD >= n_used) & (i % n_slots == slot))
        def _():
            compute(slot, False)

    @pl.when(i >= n_used)
    def _():
        y_ref[...] = jnp.zeros_like(y_ref)


def _moe_gather(plan, h2_all, w1, w3, w2):
    n_sorted = plan["src_row"].shape[0]
    grid_spec = pltpu.PrefetchScalarGridSpec(
        num_scalar_prefetch=5,
        grid=(n_sorted // TME,),
        in_specs=[pl.BlockSpec(memory_space=pl.ANY)] * 4,
        out_specs=pl.BlockSpec((TME, D_MODEL), lambda i, *_: (i, 0)),
        scratch_shapes=[pltpu.VMEM((TME, D_MODEL), F32)] * (MOE_AHEAD + 1)
                       + [pltpu.SemaphoreType.DMA((MOE_AHEAD + 1,)),
                          pltpu.VMEM((2, D_MODEL, EXPERT_FF), F32), pltpu.VMEM((2, D_MODEL, EXPERT_FF), F32),
                          pltpu.VMEM((2, EXPERT_FF, D_MODEL), F32),
                          pltpu.SemaphoreType.DMA((2,)),
                          pltpu.VMEM((D_MODEL, EXPERT_FF), BF16), pltpu.VMEM((D_MODEL, EXPERT_FF), BF16),
                          pltpu.VMEM((EXPERT_FF, D_MODEL), BF16)],
    )
    return pl.pallas_call(
        _moe_gather_kernel,
        name="moe",
        grid_spec=grid_spec,
        out_shape=jax.ShapeDtypeStruct((n_sorted, D_MODEL), F32),
        compiler_params=_cparams(1),
    )(plan["tile_expert"], plan["src_row"], plan["n_used"], plan["segment"], plan["next_expert"],
      h2_all, w1, w3, w2)


def _moe(plan, h_sorted, w1, w3, w2):
    n_sorted = h_sorted.shape[0]
    n_tiles = n_sorted // TME
    used = lambda i, te, nu, seg, nxt: (jnp.minimum(i, nu[0] - 1), 0)
    grid_spec = pltpu.PrefetchScalarGridSpec(
        num_scalar_prefetch=4,
        grid=(n_tiles,),
        in_specs=[pl.BlockSpec((TME, D_MODEL), used)] + [pl.BlockSpec(memory_space=pl.ANY)] * 3,
        out_specs=pl.BlockSpec((TME, D_MODEL), lambda i, *_: (i, 0)),
        scratch_shapes=[pltpu.VMEM((2, D_MODEL, EXPERT_FF), F32), pltpu.VMEM((2, D_MODEL, EXPERT_FF), F32),
                        pltpu.VMEM((2, EXPERT_FF, D_MODEL), F32),
                        pltpu.SemaphoreType.DMA((2,)),
                        pltpu.VMEM((D_MODEL, EXPERT_FF), BF16), pltpu.VMEM((D_MODEL, EXPERT_FF), BF16),
                        pltpu.VMEM((EXPERT_FF, D_MODEL), BF16)],
    )
    return pl.pallas_call(
        _moe_kernel,
        name="moe",
        grid_spec=grid_spec,
        out_shape=jax.ShapeDtypeStruct((n_sorted, D_MODEL), F32),
        compiler_params=_cparams(1),
    )(plan["tile_expert"], plan["n_used"], plan["segment"], plan["next_expert"], h_sorted, w1, w3, w2)


def _finish_kernel(grp, pos_ref, x1_ref, mod_ref, rt_ref, nf_ref, y_hbm, out_ref, buf_a, buf_b, sem):
    i = pl.program_id(0)
    n_steps = pl.num_programs(0)
    n_slots = FINISH_AHEAD + 1
    slot = i % n_slots

    def issue(tile, slot_):
        base = (tile * TM + grp.row_off) * 2

        def body(r, carry):
            _row_copy(y_hbm, pos_ref[base + 2 * r], buf_a.at[slot_], r, sem.at[slot_]).start(priority=0)
            _row_copy(y_hbm, pos_ref[base + 2 * r + 1], buf_b.at[slot_], r, sem.at[slot_]).start(priority=1)
            return carry

        lax.fori_loop(0, TM, body, 0, unroll=8)

    @pl.when(i == 0)
    def _():
        for t in range(FINISH_AHEAD):
            @pl.when(t < n_steps)
            def _():
                issue(t, t)

    @pl.when(i + FINISH_AHEAD < n_steps)
    def _():
        issue(i + FINISH_AHEAD, (i + FINISH_AHEAD) % n_slots)

    _tile_wait(y_hbm, buf_a.at[slot], sem.at[slot])
    _tile_wait(y_hbm, buf_b.at[slot], sem.at[slot])
    rt = rt_ref[...]
    moe = rt[:, 2:3] * buf_a[slot] + rt[:, 3:4] * buf_b[slot]
    x2 = x1_ref[...] + _mod_rows(mod_ref, 5) * moe.reshape(grp.nb, grp.tt, D_MODEL)
    ms = jnp.mean(x2 * x2, axis=-1, keepdims=True)
    out_ref[...] = x2 * lax.rsqrt(ms + EPS) * nf_ref[...]


def _finish(grp, pos, x1, mod, route_all, norm_f, y_sorted):
    off = grp.tile_off
    grid_spec = pltpu.PrefetchScalarGridSpec(
        num_scalar_prefetch=1,
        grid=(grp.n_tiles,),
        in_specs=[pl.BlockSpec((grp.nb, grp.tt, D_MODEL), grp.x_map()),
                  pl.BlockSpec((grp.nb, 6, D_MODEL), grp.mod_map()),
                  pl.BlockSpec((TM, LANES), lambda i, p: (i + off, 0)),
                  pl.BlockSpec((1, 1, D_MODEL), lambda i, p: (0, 0, 0)),
                  pl.BlockSpec(memory_space=pl.ANY)],
        out_specs=pl.BlockSpec((grp.nb, grp.tt, D_MODEL), grp.x_map()),
        scratch_shapes=[pltpu.VMEM((FINISH_AHEAD + 1, TM, D_MODEL), F32),
                        pltpu.VMEM((FINISH_AHEAD + 1, TM, D_MODEL), F32),
                        pltpu.SemaphoreType.DMA((FINISH_AHEAD + 1,))],
    )
    return pl.pallas_call(
        functools.partial(_finish_kernel, grp),
        name="finish",
        grid_spec=grid_spec,
        out_shape=jax.ShapeDtypeStruct(x1.shape, F32),
        compiler_params=_cparams(1),
    )(pos, x1, mod, route_all, norm_f.reshape(1, 1, D_MODEL), y_sorted)


def _sort_plan(route_all):
    n_tok = route_all.shape[0]
    n_pairs = 2 * n_tok
    n_sorted = n_pairs + N_EXPERTS * TME
    flat_e = route_all[:, 0:2].astype(jnp.int32).reshape(n_pairs)
    onehot = (flat_e[:, None] == jnp.arange(N_EXPERTS, dtype=jnp.int32)[None, :]).astype(jnp.int32)
    csum = jnp.cumsum(onehot, axis=0)
    rank = jnp.sum(onehot * csum, axis=1) - 1
    counts = csum[-1]
    padded = ((counts + TME - 1) // TME) * TME
    ends = jnp.cumsum(padded)
    starts = ends - padded
    pos = starts[flat_e] + rank
    token = jnp.arange(n_pairs, dtype=jnp.int32) // 2
    src_row = (jnp.arange(n_sorted, dtype=jnp.int32) % n_tok).at[pos].set(token)
    tile_start = jnp.arange(n_sorted // TME, dtype=jnp.int32) * TME
    tile_expert = jnp.sum((tile_start[:, None] >= ends[None, :]).astype(jnp.int32), axis=1)
    tile_expert = jnp.minimum(tile_expert, N_EXPERTS - 1)
    n_used = ends[-1] // TME
    is_first = jnp.concatenate([jnp.ones((1,), jnp.int32),
                                (tile_expert[1:] != tile_expert[:-1]).astype(jnp.int32)])
    segment = jnp.cumsum(is_first) - 1
    next_tile = ends[tile_expert] // TME
    next_expert = jnp.where(next_tile < n_used, tile_expert[jnp.minimum(next_tile, n_sorted // TME - 1)], -1)
    return dict(pos=pos.astype(jnp.int32), src_row=src_row, tile_expert=tile_expert.astype(jnp.int32),
                n_used=n_used.astype(jnp.int32).reshape(1), segment=segment.astype(jnp.int32),
                next_expert=next_expert.astype(jnp.int32))


def kernel(x_prompt, x_sample, c_prompt, c_sample, state_gla, state_pool, w_ada, b_ada, norm1, norm2, w_in,
           gate_up, gate_bias, gla_norm, pool_w, pool_scale, w_out, w_group, b_group, w_expert, b_expert,
           w1, w3, w2, norm_f):
    assert w_ada.shape[0] == 1, "single-layer step"
    bp, tp, _ = x_prompt.shape
    bs, ts, _ = x_sample.shape
    grp_p = _Group(bp, tp, 0, bs)
    grp_s = _Group(bs, ts, bp * tp, 0)
    n_tok = bp * tp + bs * ts

    n_c = bp + bs
    n_c_pad = -(-n_c // 8) * 8
    c_all = jnp.concatenate([c_sample, c_prompt, jnp.zeros((n_c_pad - n_c, D_MODEL), F32)], axis=0)
    mod = _adaln(c_all, w_ada[0], b_ada[0]).reshape(n_c_pad, 6, D_MODEL)
    mod_p = mod_s = mod

    w_in_t = jnp.swapaxes(w_in.reshape(w_in.shape[1:]), 0, 1).astype(BF16)
    gup = gate_up[0].astype(BF16)
    gb = gate_bias[0].reshape(1, GLA_KEY_WIDTH)
    pw = pool_w[0].astype(BF16)
    wo = w_out[0].astype(BF16)
    gap = EXPERTS_PER_GROUP - N_GROUPS
    tail = LANES - EXPERTS_PER_GROUP - N_EXPERTS
    w_router = jnp.concatenate(
        [w_group[0], jnp.zeros((D_MODEL, gap), F32),
         jnp.transpose(w_expert[0], (1, 0, 2)).reshape(D_MODEL, N_EXPERTS),
         jnp.zeros((D_MODEL, tail), F32)], axis=1)
    w_router_hi = w_router.astype(BF16)
    w_router = jnp.stack([w_router_hi, (w_router - w_router_hi.astype(F32)).astype(BF16)])
    b_router = jnp.concatenate([b_group[0], jnp.zeros((gap,), F32), b_expert[0].reshape(N_EXPERTS),
                                jnp.zeros((tail,), F32)]).reshape(1, LANES)

    qkvr_p, la_p, u_p = _in_proj(grp_p, x_prompt, mod_p, norm1[0], w_in_t, gup, gb)
    qkvr_s, la_s, u_s = _in_proj(grp_s, x_sample, mod_s, norm1[0], w_in_t, gup, gb)

    o_p, gla_p = _gla_prompt(bp, tp, qkvr_p, la_p)
    o_s, gla_s = _gla_decode(bs, ts, qkvr_s, la_s, state_gla.reshape(state_gla.shape[1:]))

    halo_per_tile = TM // HALO
    halo_map_p = lambda i: (jnp.maximum(i * halo_per_tile - 1, 0), 0)
    x1_p, h2_all, route_all = _mix_out(grp_p, n_tok, 0, True, o_p, qkvr_p, u_p, u_p, (HALO, POOL_WIDTH),
                                       halo_map_p, x_prompt, mod_p, norm2[0], gla_norm[0], pw, pool_scale[0],
                                       wo, w_router, b_router)
    x1_s, h2_all, route_all = _mix_out(grp_s, n_tok, PAST_LEN, False, o_s, qkvr_s, u_s,
                                       state_pool.reshape(bs, POOL_BUF, POOL_WIDTH),
                                       (grp_s.nb, POOL_BUF, POOL_WIDTH), lambda i: (i, 0, 0), x_sample, mod_s,
                                       norm2[0], gla_norm[0], pw, pool_scale[0], wo, w_router, b_router,
                                       shared=(h2_all, route_all))

    plan = _sort_plan(route_all)
    y_sorted = _moe_gather(plan, h2_all, w1.reshape(w1.shape[1:]), w3.reshape(w3.shape[1:]),
                           w2.reshape(w2.shape[1:]))

    y_p = _finish(grp_p, plan["pos"], x1_p, mod_p, route_all, norm_f, y_sorted)
    y_s = _finish(grp_s, plan["pos"], x1_s, mod_s, route_all, norm_f, y_sorted)

    u_p3 = u_p.reshape(bp, tp, POOL_WIDTH)
    u_s3 = u_s.reshape(bs, ts, POOL_WIDTH)
    assert tp >= POOL_BUF > ts
    pool_p = u_p3[:, tp - POOL_BUF:]
    pool_s = jnp.concatenate([state_pool.reshape(bs, POOL_BUF, POOL_WIDTH)[:, ts:], u_s3], axis=1)
    lead = lambda a: a.reshape((1,) + a.shape)
    return (y_p, y_s, lead(gla_p), lead(pool_p), lead(gla_s), lead(pool_s))
```

```python
import functools

import jax
import jax.numpy as jnp
from jax import lax
from jax.experimental import pallas as pl
from jax.experimental.pallas import tpu as pltpu

D_MODEL = 2048
GLA_HEADS = 4
GLA_DK = 128
GLA_DV = 256
GLA_KEY_WIDTH = GLA_HEADS * GLA_DK
GLA_WIDTH = GLA_HEADS * GLA_DV
POOL_WIDTH = 1024
POOL_WINDOWS = (2, 4, 8, 16)
POOL_GW = 256
POOL_BUF = 15
HALO = 16
EXT_PAD = 8
EXT_TOK0 = EXT_PAD + HALO
GATE_RANK = 16
GATE_TEMP = 16.0
N_GROUPS = 4
EXPERTS_PER_GROUP = 8
N_EXPERTS = 32
EXPERT_FF = 512
EPS = 1e-6
PAST_LEN = 16384
QKVR_WIDTH = 2 * GLA_KEY_WIDTH + 2 * GLA_WIDTH

LANES = 128
ADALN_COLS = 1024
TM = 256
TME = 256
FINISH_AHEAD = 2
GLA_CHUNK = 64
GLA_SUB = 8
GLA_STEP = 512
GLA_TRIP = 4
VMEM_LIMIT = 56 * 1024 * 1024

BF16 = jnp.bfloat16
F32 = jnp.float32
NEG = -1e30
LOG2E = 1.4426950408889634


def _cparams(n_axes):
    return pltpu.CompilerParams(dimension_semantics=("arbitrary",) * n_axes,
                                vmem_limit_bytes=VMEM_LIMIT)


def _silu(x):
    return x / (1.0 + jnp.exp(-x))


def _bdot(a, b):
    return jnp.dot(a.astype(BF16), b.astype(BF16), preferred_element_type=F32)


def _split3(a):
    a1 = a.astype(BF16)
    r1 = a - a1.astype(F32)
    a2 = r1.astype(BF16)
    a3 = (r1 - a2.astype(F32)).astype(BF16)
    return a1, a2, a3


def _split2(a):
    hi = a.astype(BF16)
    lo = (a - hi.astype(F32)).astype(BF16)
    return hi, lo


def _dot_3pass(a, b_hi, b_lo):
    a_hi, a_lo = _split2(a)
    d = lambda x, y: jnp.dot(x, y, preferred_element_type=F32)
    return d(a_hi, b_hi) + (d(a_hi, b_lo) + d(a_lo, b_hi))


def _dot_exact_lhs(tri_bf16, g):
    g1, g2, g3 = _split3(g)
    d = lambda y: jnp.dot(tri_bf16, y, preferred_element_type=F32)
    return d(g1) + (d(g2) + d(g3))


def _adaln_kernel(c_ref, w_ref, b_ref, o_ref):
    c = c_ref[...]
    o_ref[...] = _bdot(_silu(c), w_ref[...]) + b_ref[...]


def _adaln(c_all, w_ada, b_ada):
    n, d = c_all.shape
    width = w_ada.shape[1]
    tn = ADALN_COLS
    return pl.pallas_call(
        _adaln_kernel,
        name="adaln",
        grid=(width // tn,),
        in_specs=[pl.BlockSpec((n, d), lambda j: (0, 0)),
                  pl.BlockSpec((d, tn), lambda j: (0, j)),
                  pl.BlockSpec((1, tn), lambda j: (0, j))],
        out_specs=pl.BlockSpec((n, tn), lambda j: (0, j)),
        out_shape=jax.ShapeDtypeStruct((n, width), F32),
        compiler_params=_cparams(1),
    )(c_all, w_ada, b_ada.reshape(1, width))


class _Group:
    def __init__(self, batch, seq, row_off, mod_off):
        self.batch, self.seq, self.row_off, self.mod_off = batch, seq, row_off, mod_off
        if seq >= TM:
            assert seq % TM == 0
            self.nb, self.tt = 1, TM
            self.tiles_per_batch = seq // TM
            self.n_tiles = batch * self.tiles_per_batch
        else:
            assert TM % seq == 0 and batch % (TM // seq) == 0
            self.nb, self.tt = TM // seq, seq
            self.tiles_per_batch = 1
            self.n_tiles = batch // self.nb
        self.rows = batch * seq
        self.tile_off = row_off // TM

    def x_map(self):
        if self.nb == 1:
            tpb = self.tiles_per_batch
            return lambda i, *_: (i // tpb, i % tpb, 0)
        return lambda i, *_: (i, 0, 0)

    def mod_map(self):
        assert self.mod_off % self.nb == 0
        off = self.mod_off // self.nb
        if self.nb == 1:
            tpb = self.tiles_per_batch
            return lambda i, *_: (i // tpb + off, 0, 0)
        return lambda i, *_: (i + off, 0, 0)


def _mod_rows(mod_ref, idx):
    return mod_ref[:, idx:idx + 1, :]


def _rmsnorm_mod(x, gain, scale, shift):
    ms = jnp.mean(x * x, axis=-1, keepdims=True)
    y = x * lax.rsqrt(ms + EPS) * gain
    return y * (1.0 + scale) + shift


def _in_proj_kernel(x_ref, mod_ref, n1_ref, wq_ref, wg_ref, wu_ref, gup_ref, gb_ref,
                    qkvr_ref, la_ref, u_ref):
    x = x_ref[...]
    h = _rmsnorm_mod(x, n1_ref[...], _mod_rows(mod_ref, 1), _mod_rows(mod_ref, 0))
    hb = h.reshape(TM, D_MODEL).astype(BF16)
    nt = (((1,), (1,)), ((), ()))
    qkvr_ref[...] = lax.dot_general(hb, wq_ref[...], nt, preferred_element_type=F32)
    u_ref[...] = lax.dot_general(hb, wu_ref[...], nt, preferred_element_type=F32)
    g_lr = lax.dot_general(hb, wg_ref[...], nt, preferred_element_type=F32)
    pre = jnp.dot(g_lr.astype(BF16), gup_ref[...], preferred_element_type=F32) + gb_ref[...]
    log_sig = jnp.minimum(pre, 0.0) - jnp.log1p(jnp.exp(-jnp.abs(pre)))
    la_ref[...] = log_sig / GATE_TEMP


def _in_proj(grp, x, mod, norm1, w_in_t, gup, gb):
    const = lambda i: (0, 0)
    u_row0 = QKVR_WIDTH + GATE_RANK
    row = lambda i: (i, 0)
    return pl.pallas_call(
        _in_proj_kernel,
        name="in_proj",
        grid=(grp.n_tiles,),
        in_specs=[pl.BlockSpec((grp.nb, grp.tt, D_MODEL), grp.x_map()),
                  pl.BlockSpec((grp.nb, 6, D_MODEL), grp.mod_map()),
                  pl.BlockSpec((1, 1, D_MODEL), lambda i: (0, 0, 0)),
                  pl.BlockSpec((QKVR_WIDTH, D_MODEL), const, pipeline_mode=pl.Buffered(1)),
                  pl.BlockSpec((GATE_RANK, D_MODEL), lambda i: (QKVR_WIDTH // GATE_RANK, 0),
                               pipeline_mode=pl.Buffered(1)),
                  pl.BlockSpec((pl.Element(POOL_WIDTH), pl.Element(D_MODEL)), lambda i: (u_row0, 0),
                               pipeline_mode=pl.Buffered(1)),
                  pl.BlockSpec(gup.shape, const, pipeline_mode=pl.Buffered(1)),
                  pl.BlockSpec(gb.shape, const, pipeline_mode=pl.Buffered(1))],
        out_specs=[pl.BlockSpec((TM, QKVR_WIDTH), row),
                   pl.BlockSpec((TM, GLA_KEY_WIDTH), row),
                   pl.BlockSpec((TM, POOL_WIDTH), row)],
        out_shape=[jax.ShapeDtypeStruct((grp.rows, QKVR_WIDTH), F32),
                   jax.ShapeDtypeStruct((grp.rows, GLA_KEY_WIDTH), F32),
                   jax.ShapeDtypeStruct((grp.rows, POOL_WIDTH), F32)],
        compiler_params=_cparams(1),
    )(x, mod, norm1.reshape(1, 1, D_MODEL), w_in_t, w_in_t, w_in_t, gup, gb)


def _gla_select_matrix(chunk, sub):
    r = jnp.arange(sub * GLA_DK, dtype=jnp.int32)[:, None] // GLA_DK
    l = jnp.arange(LANES, dtype=jnp.int32)[None, :]
    return ((l % sub == r) & (l < chunk)).astype(BF16)


def _gla_chunks(chunks, states, chunk, sub, wsel, t_refs, kb_refs, chained=True):
    n_sub = chunk // sub
    rows = lax.broadcasted_iota(jnp.int32, (chunk, chunk), 0)
    cols = lax.broadcasted_iota(jnp.int32, (chunk, chunk), 1)
    tri = (rows >= cols).astype(BF16)
    nt = (((1,), (1,)), ((), ()))
    tn = (((0,), (0,)), ((), ()))
    key_row = lax.broadcasted_iota(jnp.int32, (chunk, LANES), 0)
    lane = lax.broadcasted_iota(jnp.int32, (sub, chunk), 1)
    row = lax.broadcasted_iota(jnp.int32, (sub, chunk), 0)
    head = lambda a, h, w: a[:, h * w:(h + 1) * w]
    n = len(chunks)

    b4s = [_dot_exact_lhs(tri, g4) for (_, _, _, g4) in chunks]
    q4s = [q4 * (GLA_DK ** -0.5) for (q4, _, _, _) in chunks]

    for c in range(n):
        kb_ref, t_ref, k4 = kb_refs[c], t_refs[c], chunks[c][1]
        for h in range(GLA_HEADS):
            kb_ref[h] = head(k4, h, GLA_DK)
            kb_ref[GLA_HEADS + h] = head(b4s[c], h, GLA_DK) * LOG2E
        for h in range(GLA_HEADS):
            for s in range(n_sub):
                lo = s * sub
                r0 = (h * n_sub + s) * sub
                q_s, b_s = head(q4s[c], h, GLA_DK)[lo:lo + sub, :], kb_ref[GLA_HEADS + h, lo:lo + sub, :]
                for jl in range(sub):
                    k_j = jnp.broadcast_to(kb_ref[h, lo + jl:lo + jl + 1, :], (sub, GLA_DK))
                    b_j = jnp.broadcast_to(kb_ref[GLA_HEADS + h, lo + jl:lo + jl + 1, :], (sub, GLA_DK))
                    decay = jnp.exp2(jnp.minimum(b_s - b_j, 0.0))
                    t_ref[r0:r0 + sub, jl * GLA_DK:(jl + 1) * GLA_DK] = (q_s * k_j * decay).astype(t_ref.dtype)
    p_diags = [jnp.dot(t_refs[c][...].astype(BF16), wsel, preferred_element_type=F32) for c in range(n)]

    intra = []
    for c in range(n):
        per_head = []
        for h in range(GLA_HEADS):
            q, k, b = head(q4s[c], h, GLA_DK), head(chunks[c][1], h, GLA_DK), head(b4s[c], h, GLA_DK)
            p_blocks = []
            for s in range(n_sub):
                lo = s * sub
                r0 = (h * n_sub + s) * sub
                in_block = (lane >= lo) & (lane - lo <= row)
                p = jnp.where(in_block, p_diags[c][r0:r0 + sub, :chunk], 0.0)
                if s > 0:
                    ref_row = b[lo - 1:lo, :]
                    q_rel = q[lo:lo + sub, :] * jnp.exp(b[lo:lo + sub, :] - ref_row)
                    k_rel = k * jnp.exp(jnp.where(key_row < lo, ref_row - b, NEG))
                    p = p + lax.dot_general(q_rel.astype(BF16), k_rel.astype(BF16), nt,
                                            preferred_element_type=F32)
                p_blocks.append(p)
            p_full = p_blocks[0] if n_sub == 1 else jnp.concatenate(p_blocks, axis=0)
            per_head.append(_bdot(p_full, head(chunks[c][2], h, GLA_DV)))
        intra.append(per_head)

    outs, end_states = [], []
    for c in range(n):
        cur = states if chained else states[c]
        o_heads, new_states = [], []
        for h in range(GLA_HEADS):
            q, k, b = head(q4s[c], h, GLA_DK), head(chunks[c][1], h, GLA_DK), head(b4s[c], h, GLA_DK)
            v = head(chunks[c][2], h, GLA_DV)
            o_heads.append(intra[c][h] + _bdot(q * jnp.exp(b), cur[h]))
            b_last = b[chunk - 1:chunk, :]
            k_dec = k * jnp.exp(b_last - b)
            decay_col = jnp.exp(b[chunk - 8:chunk, :]).T[:, 7:8]
            new_states.append(decay_col * cur[h] + lax.dot_general(
                k_dec.astype(BF16), v.astype(BF16), tn, preferred_element_type=F32))
        states = new_states if chained else states
        end_states.append(new_states)
        outs.append(jnp.concatenate(o_heads, axis=1))
    return outs, (end_states[-1] if chained else end_states)


def _gla_prompt_kernel(q_ref, k_ref, v_ref, la_ref, wsel_ref, o_ref, s_ref, t_ref, kb_ref):
    @pl.when(pl.program_id(1) == 0)
    def _():
        s_ref[...] = jnp.zeros_like(s_ref)

    def body(trip, carry):
        states = [s_ref[0, h] for h in range(GLA_HEADS)]
        slices = [pl.ds(pl.multiple_of((GLA_TRIP * trip + c) * GLA_CHUNK, GLA_CHUNK), GLA_CHUNK)
                  for c in range(GLA_TRIP)]
        chunks = [(q_ref[sl, :], k_ref[sl, :], v_ref[sl, :], la_ref[sl, :]) for sl in slices]
        outs, states = _gla_chunks(chunks, states, GLA_CHUNK, GLA_SUB, wsel_ref[...],
                                   [t_ref.at[c] for c in range(GLA_TRIP)],
                                   [kb_ref.at[c] for c in range(GLA_TRIP)])
        for sl, o in zip(slices, outs):
            o_ref[sl, :] = o
        for h in range(GLA_HEADS):
            s_ref[0, h] = states[h]
        return carry

    lax.fori_loop(0, GLA_STEP // (GLA_TRIP * GLA_CHUNK), body, 0)


def _gla_prompt(batch, seq, qkvr, log_a):
    steps = seq // GLA_STEP
    row = lambda b, s: b * steps + s
    wsel = _gla_select_matrix(GLA_CHUNK, GLA_SUB)
    return pl.pallas_call(
        _gla_prompt_kernel,
        name="gla_prompt",
        grid=(batch, steps),
        in_specs=[pl.BlockSpec((GLA_STEP, GLA_KEY_WIDTH), lambda b, s: (row(b, s), 0)),
                  pl.BlockSpec((GLA_STEP, GLA_KEY_WIDTH), lambda b, s: (row(b, s), 1)),
                  pl.BlockSpec((GLA_STEP, GLA_WIDTH), lambda b, s: (row(b, s), 1)),
                  pl.BlockSpec((GLA_STEP, GLA_KEY_WIDTH), lambda b, s: (row(b, s), 0)),
                  pl.BlockSpec(wsel.shape, lambda b, s: (0, 0))],
        out_specs=[pl.BlockSpec((GLA_STEP, GLA_WIDTH), lambda b, s: (row(b, s), 0)),
                   pl.BlockSpec((1, GLA_HEADS, GLA_DK, GLA_DV), lambda b, s: (b, 0, 0, 0))],
        out_shape=[jax.ShapeDtypeStruct((batch * seq, GLA_WIDTH), F32),
                   jax.ShapeDtypeStruct((batch, GLA_HEADS, GLA_DK, GLA_DV), F32)],
        scratch_shapes=[pltpu.VMEM((GLA_TRIP, GLA_HEADS * GLA_CHUNK, GLA_SUB * GLA_DK), BF16),
                        pltpu.VMEM((GLA_TRIP, 2 * GLA_HEADS, GLA_CHUNK, GLA_DK), F32)],
        compiler_params=_cparams(2),
    )(qkvr, qkvr, qkvr, log_a, wsel)


GLA_DEC_BB = 8
GLA_DEC_TRIP = 4


def _gla_decode_kernel(seq, q_ref, k_ref, v_ref, la_ref, wsel_ref, s0_ref, o_ref, s_ref, t_ref, kb_ref):
    def body(trip, carry):
        elems = [GLA_DEC_TRIP * trip + c for c in range(GLA_DEC_TRIP)]
        slices = [pl.ds(pl.multiple_of(e * seq, seq), seq) for e in elems]
        chunks = [(q_ref[sl, :], k_ref[sl, :], v_ref[sl, :], la_ref[sl, :]) for sl in slices]
        states = [[s0_ref[e, h] for h in range(GLA_HEADS)] for e in elems]
        outs, new_states = _gla_chunks(chunks, states, seq, seq, wsel_ref[...],
                                       [t_ref.at[c] for c in range(GLA_DEC_TRIP)],
                                       [kb_ref.at[c] for c in range(GLA_DEC_TRIP)], chained=False)
        for e, sl, o, ns in zip(elems, slices, outs, new_states):
            o_ref[sl, :] = o
            for h in range(GLA_HEADS):
                s_ref[e, h] = ns[h]
        return carry

    lax.fori_loop(0, GLA_DEC_BB // GLA_DEC_TRIP, body, 0)


def _gla_decode(batch, seq, qkvr, log_a, state):
    rows = GLA_DEC_BB * seq
    wsel = _gla_select_matrix(seq, seq)
    state_spec = pl.BlockSpec((GLA_DEC_BB, GLA_HEADS, GLA_DK, GLA_DV), lambda i: (i, 0, 0, 0))
    return pl.pallas_call(
        functools.partial(_gla_decode_kernel, seq),
        name="gla_decode",
        grid=(batch // GLA_DEC_BB,),
        in_specs=[pl.BlockSpec((rows, GLA_KEY_WIDTH), lambda i: (i, 0)),
                  pl.BlockSpec((rows, GLA_KEY_WIDTH), lambda i: (i, 1)),
                  pl.BlockSpec((rows, GLA_WIDTH), lambda i: (i, 1)),
                  pl.BlockSpec((rows, GLA_KEY_WIDTH), lambda i: (i, 0)),
                  pl.BlockSpec(wsel.shape, lambda i: (0, 0)),
                  state_spec],
        out_specs=[pl.BlockSpec((rows, GLA_WIDTH), lambda i: (i, 0)), state_spec],
        out_shape=[jax.ShapeDtypeStruct((batch * seq, GLA_WIDTH), F32),
                   jax.ShapeDtypeStruct((batch, GLA_HEADS, GLA_DK, GLA_DV), F32)],
        scratch_shapes=[pltpu.VMEM((GLA_DEC_TRIP, GLA_HEADS * seq, seq * GLA_DK), F32),
                        pltpu.VMEM((GLA_DEC_TRIP, 2 * GLA_HEADS, seq, GLA_DK), F32)],
        compiler_params=_cparams(1),
    )(qkvr, qkvr, qkvr, log_a, wsel, state)


def _route(logits, cnt_ref):
    lt = logits.T
    n = lt.shape[1]
    big = jnp.int32(10 ** 6)
    row8 = lax.broadcasted_iota(jnp.int32, (EXPERTS_PER_GROUP, n), 0)
    lg = jnp.where(row8 < N_GROUPS, lt[0:EXPERTS_PER_GROUP], NEG)
    mg = jnp.max(lg, axis=0, keepdims=True)
    g_idx = jnp.min(jnp.where(lg == mg, row8, big), axis=0, keepdims=True)
    p_sel = 1.0 / jnp.sum(jnp.exp(lg - mg), axis=0, keepdims=True)
    le = jnp.zeros((EXPERTS_PER_GROUP, n), F32)
    for g in range(N_GROUPS):
        lo = EXPERTS_PER_GROUP * (g + 1)
        le = jnp.where(g_idx == g, lt[lo:lo + EXPERTS_PER_GROUP], le)
    m1 = jnp.max(le, axis=0, keepdims=True)
    i1 = jnp.min(jnp.where(le == m1, row8, big), axis=0, keepdims=True)
    rest = row8 != i1
    m2 = jnp.max(jnp.where(rest, le, NEG), axis=0, keepdims=True)
    i2 = jnp.min(jnp.where(rest & (le == m2), row8, big), axis=0, keepdims=True)
    e2 = jnp.exp(m2 - m1)
    w1 = p_sel / (1.0 + e2)
    w2 = p_sel * e2 / (1.0 + e2)
    ex1_i = g_idx * EXPERTS_PER_GROUP + i1
    ex2_i = g_idx * EXPERTS_PER_GROUP + i2
    ex1, ex2 = ex1_i.astype(F32), ex2_i.astype(F32)

    e_row = lax.broadcasted_iota(jnp.int32, (N_EXPERTS, n), 0)
    hit1, hit2 = e_row == ex1_i, e_row == ex2_i
    member = (hit1 | hit2).astype(BF16)
    before = (lax.broadcasted_iota(jnp.int32, (n, n), 0) < lax.broadcasted_iota(jnp.int32, (n, n), 1))
    prior = jnp.dot(member, before.astype(BF16), preferred_element_type=F32) + cnt_ref[:, 0:1]
    rank1 = jnp.sum(jnp.where(hit1, prior, 0.0), axis=0, keepdims=True)
    rank2 = jnp.sum(jnp.where(hit2, prior, 0.0), axis=0, keepdims=True)
    cnt_ref[...] = cnt_ref[...] + jnp.sum(member.astype(F32), axis=1, keepdims=True)

    packed = jnp.where(row8 == 0, ex1, jnp.where(row8 == 1, ex2,
                       jnp.where(row8 == 2, w1, jnp.where(row8 == 3, w2,
                                 jnp.where(row8 == 4, rank1, jnp.where(row8 == 5, rank2, 0.0))))))
    full = jnp.concatenate([packed, jnp.zeros((LANES - EXPERTS_PER_GROUP, n), F32)], axis=0)
    return full.T


def _mix_out_kernel(grp, pos0, zero_first_halo, n_alias,
                    o_ref, r_ref, u_ref, halo_ref, x_ref, mod_ref, n2_ref, gn_ref, pw_ref, ps_ref,
                    wo_ref, wr_ref, br_ref, cnt0_ref, *rest):
    x1_ref, h2_ref, rt_ref, cnt_ref, ext_ref, lvl_a, lvl_b, ymix_ref = rest[n_alias:]
    i = pl.program_id(0)

    @pl.when(i == 0)
    def _():
        cnt_ref[...] = cnt0_ref[...]

    @pl.when(i < grp.n_tiles)
    def _():
        _mix_out_tile(grp, pos0, zero_first_halo, i, o_ref, r_ref, u_ref, halo_ref, x_ref, mod_ref, n2_ref,
                      gn_ref, pw_ref, ps_ref, wo_ref, wr_ref, br_ref, x1_ref, h2_ref, rt_ref, cnt_ref, ext_ref,
                      (lvl_a, lvl_b), ymix_ref)

    @pl.when(i >= grp.n_tiles)
    def _():
        h2_ref[...] = jnp.zeros_like(h2_ref)
        rt_ref[...] = jnp.zeros_like(rt_ref)


def _mix_out_tile(grp, pos0, zero_first_halo, i, o_ref, r_ref, u_ref, halo_ref, x_ref, mod_ref, n2_ref,
                  gn_ref, pw_ref, ps_ref, wo_ref, wr_ref, br_ref, x1_ref, h2_ref, rt_ref, cnt_ref, ext_ref,
                  lvl_refs, ymix_ref):
    nb, tt = grp.nb, grp.tt
    hist = halo_ref.shape[-2]

    for h in range(GLA_HEADS):
        cs = slice(h * GLA_DV, (h + 1) * GLA_DV)
        oh = o_ref[:, cs]
        ms = jnp.mean(oh * oh, axis=-1, keepdims=True)
        yh = oh * lax.rsqrt(ms + EPS) * gn_ref[:, cs] * _silu(r_ref[:, cs])
        ymix_ref[:, cs] = yh.astype(BF16)

    halo = halo_ref[...]
    if zero_first_halo:
        halo = jnp.where(i % grp.tiles_per_batch == 0, 0.0, halo)
    n_ext = EXT_TOK0 + tt
    ext_ref[:, 0:EXT_TOK0 - hist, :] = jnp.zeros((nb, EXT_TOK0 - hist, POOL_WIDTH), F32)
    ext_ref[:, EXT_TOK0 - hist:EXT_TOK0, :] = halo.reshape(nb, hist, POOL_WIDTH)
    u = u_ref[...].reshape(nb, tt, POOL_WIDTH)
    ext_ref[:, EXT_TOK0:n_ext, :] = u
    for lvl_ref in lvl_refs:
        lvl_ref[:, 0:EXT_PAD, :] = jnp.zeros((nb, EXT_PAD, POOL_GW), F32)
    t_idx = lax.broadcasted_iota(jnp.int32, (nb, tt, POOL_GW), 1)
    if grp.nb == 1:
        pos = (i % grp.tiles_per_batch) * TM + t_idx + pos0
    else:
        pos = t_idx + pos0
    y_gla = []
    part = D_MODEL // len(POOL_WINDOWS)
    for gi, w in enumerate(POOL_WINDOWS):
        y_gla.append(jnp.dot(ymix_ref[:, :GLA_WIDTH], wo_ref[:GLA_WIDTH, gi * part:(gi + 1) * part],
                             preferred_element_type=F32))
        cs = slice(gi * POOL_GW, (gi + 1) * POOL_GW)
        cur = lambda lo, hi: ext_ref[:, lo:hi, cs]
        d, level = 1, 0
        while d < w:
            nxt = lvl_refs[level % 2]
            nxt[:, EXT_PAD:n_ext, :] = cur(EXT_PAD, n_ext) + cur(EXT_PAD - d, n_ext - d)
            cur = lambda lo, hi, ref=nxt: ref[:, lo:hi, :]
            d, level = 2 * d, level + 1
        acc = cur(EXT_TOK0, n_ext)
        cnt = jnp.minimum(pos + 1, w).astype(F32)
        pooled = acc / cnt - u[:, :, cs]
        yp = _bdot(pooled.reshape(TM, POOL_GW), pw_ref[gi]) * ps_ref[:, cs]
        ymix_ref[:, GLA_WIDTH + gi * POOL_GW:GLA_WIDTH + (gi + 1) * POOL_GW] = yp.astype(BF16)

    y = jnp.concatenate(y_gla, axis=1) + jnp.dot(ymix_ref[:, GLA_WIDTH:], wo_ref[GLA_WIDTH:, :],
                                                 preferred_element_type=F32)
    x1 = x_ref[...] + _mod_rows(mod_ref, 2) * y.reshape(nb, tt, D_MODEL)
    x1_ref[...] = x1
    h2 = _rmsnorm_mod(x1, n2_ref[...], _mod_rows(mod_ref, 4), _mod_rows(mod_ref, 3)).reshape(TM, D_MODEL)
    h2_ref[...] = h2
    logits = _dot_3pass(h2, wr_ref[0], wr_ref[1]) + br_ref[...]
    rt_ref[...] = _route(logits, cnt_ref)


def _mix_out(grp, n_tok, pos0, zero_first_halo, o, qkvr, u, halo_src, halo_block, halo_map, x, mod, norm2,
             gla_norm, pool_w, pool_scale, w_out, w_router, b_router, counts, shared=()):
    n_alias = len(shared)
    n = grp.n_tiles
    n_fill = 0 if shared else n_tok // TM - n
    assert n_fill == 0 or grp.tile_off == 0
    clamp = lambda f: (lambda i: f(jnp.minimum(i, n - 1)))
    const2 = lambda i: (0, 0)
    row = clamp(lambda i: (i, 0))
    off = grp.tile_off
    kern = functools.partial(_mix_out_kernel, grp, pos0, zero_first_halo, n_alias)
    return pl.pallas_call(
        kern,
        name="mix_out",
        grid=(n + n_fill,),
        in_specs=[pl.BlockSpec((TM, GLA_WIDTH), row),
                  pl.BlockSpec((TM, GLA_WIDTH), clamp(lambda i: (i, 2))),
                  pl.BlockSpec((TM, POOL_WIDTH), row),
                  pl.BlockSpec(halo_block, clamp(halo_map)),
                  pl.BlockSpec((grp.nb, grp.tt, D_MODEL), clamp(grp.x_map())),
                  pl.BlockSpec((grp.nb, 6, D_MODEL), clamp(grp.mod_map())),
                  pl.BlockSpec((1, 1, D_MODEL), lambda i: (0, 0, 0)),
                  pl.BlockSpec((1, GLA_WIDTH), const2),
                  pl.BlockSpec(pool_w.shape, lambda i: (0, 0, 0), pipeline_mode=pl.Buffered(1)),
                  pl.BlockSpec((1, POOL_WIDTH), const2),
                  pl.BlockSpec(w_out.shape, const2, pipeline_mode=pl.Buffered(1)),
                  pl.BlockSpec(w_router.shape, lambda i: (0, 0, 0), pipeline_mode=pl.Buffered(1)),
                  pl.BlockSpec((1, LANES), const2),
                  pl.BlockSpec((N_EXPERTS, LANES), const2)]
                 + [pl.BlockSpec(memory_space=pl.ANY)] * n_alias,
        out_specs=[pl.BlockSpec((grp.nb, grp.tt, D_MODEL), clamp(grp.x_map())),
                   pl.BlockSpec((TM, D_MODEL), lambda i: (i + off, 0)),
                   pl.BlockSpec((TM, LANES), lambda i: (i + off, 0)),
                   pl.BlockSpec((N_EXPERTS, LANES), const2)],
        out_shape=[jax.ShapeDtypeStruct(x.shape, F32),
                   jax.ShapeDtypeStruct((n_tok, D_MODEL), F32),
                   jax.ShapeDtypeStruct((n_tok, LANES), F32),
                   jax.ShapeDtypeStruct((N_EXPERTS, LANES), F32)],
        scratch_shapes=[pltpu.VMEM((grp.nb, EXT_TOK0 + grp.tt, POOL_WIDTH), F32),
                        pltpu.VMEM((grp.nb, EXT_TOK0 + grp.tt, POOL_GW), F32),
                        pltpu.VMEM((grp.nb, EXT_TOK0 + grp.tt, POOL_GW), F32),
                        pltpu.VMEM((TM, D_MODEL), BF16)],
        input_output_aliases={14 + k: 1 + k for k in range(n_alias)},
        compiler_params=_cparams(1),
    )(o, qkvr, u, halo_src, x, mod, norm2.reshape(1, 1, D_MODEL), gla_norm.reshape(1, GLA_WIDTH),
      pool_w, pool_scale.reshape(1, POOL_WIDTH), w_out, w_router, b_router, counts, *shared)


def _row_copy(src_hbm, src_row, dst, dst_row, sem):
    return pltpu.make_async_copy(src_hbm.at[pl.ds(src_row, 1), :], dst.at[pl.ds(dst_row, 1), :], sem)


def _tile_wait(src_hbm, dst, sem):
    pltpu.make_async_copy(src_hbm.at[pl.ds(0, dst.shape[0]), :], dst, sem).wait()


MOE_AHEAD = 2


def _moe_gather_kernel(te_ref, src_ref, nu_ref, seg_ref, nxt_ref, h_hbm, w1_hbm, w3_hbm, w2_hbm, y_ref,
                       buf0, buf1, buf2, sem, wf1, wf3, wf2, wsem, w1b, w3b, w2b):
    i = pl.program_id(0)
    n_used = nu_ref[0]
    bufs = (buf0, buf1, buf2)
    n_slots = MOE_AHEAD + 1
    n_parts = 4

    def issue(tile, slot, part):
        base = tile * TME
        per = TME // n_parts
        for r in range(part * per, (part + 1) * per):
            _row_copy(h_hbm, src_ref[base + r], bufs[slot], r, sem.at[slot]).start(priority=r % 2)

    def weight_copies(expert, wslot):
        return (pltpu.make_async_copy(w1_hbm.at[expert], wf1.at[wslot], wsem.at[wslot]),
                pltpu.make_async_copy(w3_hbm.at[expert], wf3.at[wslot], wsem.at[wslot]),
                pltpu.make_async_copy(w2_hbm.at[expert], wf2.at[wslot], wsem.at[wslot]))

    def compute(slot, ahead):
        nxt = (lambda part: issue(i + MOE_AHEAD, (slot + MOE_AHEAD) % n_slots, part)) if ahead else (lambda part: None)
        _tile_wait(h_hbm, bufs[slot], sem.at[slot])
        nxt(0)
        x = bufs[slot][...].astype(BF16)
        nxt(1)
        a = jnp.dot(x, w1b[...], preferred_element_type=F32)
        nxt(2)
        b = jnp.dot(x, w3b[...], preferred_element_type=F32)
        nxt(3)
        hid = _silu(a) * b
        y_ref[...] = jnp.dot(hid.astype(BF16), w2b[...], preferred_element_type=F32)

    @pl.when(i == 0)
    def _():
        for c in weight_copies(te_ref[0], 0):
            c.start()
        for t in range(MOE_AHEAD):
            @pl.when(t < n_used)
            def _():
                for part in range(n_parts):
                    issue(t, t, part)

    prev = jnp.maximum(i - 1, 0)

    @pl.when((i < n_used) & ((i == 0) | (te_ref[i] != te_ref[prev])))
    def _():
        wslot = seg_ref[i] % 2
        for c in weight_copies(te_ref[i], wslot):
            c.wait()
        w1b[...] = wf1[wslot].astype(BF16)
        w3b[...] = wf3[wslot].astype(BF16)
        w2b[...] = wf2[wslot].astype(BF16)

        @pl.when(nxt_ref[i] >= 0)
        def _():
            for c in weight_copies(nxt_ref[i], 1 - wslot):
                c.start()

    for slot in range(n_slots):
        @pl.when((i + MOE_AHEAD < n_used) & (i % n_slots == slot))
        def _():
            compute(slot, True)

        @pl.when((i < n_used) & (i + MOE_AHEAD >= n_used) & (i % n_slots == slot))
        def _():
            compute(slot, False)

    @pl.when(i >= n_used)
    def _():
        y_ref[...] = jnp.zeros_like(y_ref)


def _moe_gather(plan, h2_all, w1, w3, w2):
    n_sorted = plan["src_row"].shape[0]
    grid_spec = pltpu.PrefetchScalarGridSpec(
        num_scalar_prefetch=5,
        grid=(n_sorted // TME,),
        in_specs=[pl.BlockSpec(memory_space=pl.ANY)] * 4,
        out_specs=pl.BlockSpec((TME, D_MODEL), lambda i, *_: (i, 0)),
        scratch_shapes=[pltpu.VMEM((TME, D_MODEL), F32)] * (MOE_AHEAD + 1)
                       + [pltpu.SemaphoreType.DMA((MOE_AHEAD + 1,)),
                          pltpu.VMEM((2, D_MODEL, EXPERT_FF), F32), pltpu.VMEM((2, D_MODEL, EXPERT_FF), F32),
                          pltpu.VMEM((2, EXPERT_FF, D_MODEL), F32),
                          pltpu.SemaphoreType.DMA((2,)),
                          pltpu.VMEM((D_MODEL, EXPERT_FF), BF16), pltpu.VMEM((D_MODEL, EXPERT_FF), BF16),
                          pltpu.VMEM((EXPERT_FF, D_MODEL), BF16)],
    )
    return pl.pallas_call(
        _moe_gather_kernel,
        name="moe",
        grid_spec=grid_spec,
        out_shape=jax.ShapeDtypeStruct((n_sorted, D_MODEL), F32),
        compiler_params=_cparams(1),
    )(plan["tile_expert"], plan["src_row"], plan["n_used"], plan["segment"], plan["next_expert"],
      h2_all, w1, w3, w2)


def _finish_kernel(grp, pos_ref, x1_ref, mod_ref, rt_ref, nf_ref, y_hbm, out_ref, *scratch):
    n_slots = FINISH_AHEAD + 1
    bufs_a, bufs_b, sem = scratch[:n_slots], scratch[n_slots:2 * n_slots], scratch[2 * n_slots]
    i = pl.program_id(0)
    n_steps = pl.num_programs(0)
    n_parts = 4

    def issue(tile, slot, part):
        base = (tile * TM + grp.row_off) * 2
        per = TM // n_parts
        for r in range(part * per, (part + 1) * per):
            _row_copy(y_hbm, pos_ref[base + 2 * r], bufs_a[slot], r, sem.at[slot]).start(priority=0)
            _row_copy(y_hbm, pos_ref[base + 2 * r + 1], bufs_b[slot], r, sem.at[slot]).start(priority=1)

    def compute(slot, ahead):
        nxt = ((lambda part: issue(i + FINISH_AHEAD, (slot + FINISH_AHEAD) % n_slots, part)) if ahead
               else (lambda part: None))
        _tile_wait(y_hbm, bufs_a[slot], sem.at[slot])
        _tile_wait(y_hbm, bufs_b[slot], sem.at[slot])
        nxt(0)
        rt = rt_ref[...]
        moe = rt[:, 2:3] * bufs_a[slot][...] + rt[:, 3:4] * bufs_b[slot][...]
        nxt(1)
        x2 = x1_ref[...] + _mod_rows(mod_ref, 5) * moe.reshape(grp.nb, grp.tt, D_MODEL)
        nxt(2)
        ms = jnp.mean(x2 * x2, axis=-1, keepdims=True)
        nxt(3)
        out_ref[...] = x2 * lax.rsqrt(ms + EPS) * nf_ref[...]

    @pl.when(i == 0)
    def _():
        for t in range(FINISH_AHEAD):
            @pl.when(t < n_steps)
            def _():
                for part in range(n_parts):
                    issue(t, t, part)

    for slot in range(n_slots):
        @pl.when((i + FINISH_AHEAD < n_steps) & (i % n_slots == slot))
        def _():
            compute(slot, True)

        @pl.when((i + FINISH_AHEAD >= n_steps) & (i % n_slots == slot))
        def _():
            compute(slot, False)


def _finish(grp, pos, x1, mod, route_all, norm_f, y_sorted):
    off = grp.tile_off
    grid_spec = pltpu.PrefetchScalarGridSpec(
        num_scalar_prefetch=1,
        grid=(grp.n_tiles,),
        in_specs=[pl.BlockSpec((grp.nb, grp.tt, D_MODEL), grp.x_map()),
                  pl.BlockSpec((grp.nb, 6, D_MODEL), grp.mod_map()),
                  pl.BlockSpec((TM, LANES), lambda i, p: (i + off, 0)),
                  pl.BlockSpec((1, 1, D_MODEL), lambda i, p: (0, 0, 0)),
                  pl.BlockSpec(memory_space=pl.ANY)],
        out_specs=pl.BlockSpec((grp.nb, grp.tt, D_MODEL), grp.x_map()),
        scratch_shapes=[pltpu.VMEM((TM, D_MODEL), F32)] * (2 * (FINISH_AHEAD + 1))
                       + [pltpu.SemaphoreType.DMA((FINISH_AHEAD + 1,))],
    )
    return pl.pallas_call(
        functools.partial(_finish_kernel, grp),
        name="finish",
        grid_spec=grid_spec,
        out_shape=jax.ShapeDtypeStruct(x1.shape, F32),
        compiler_params=_cparams(1),
    )(pos, x1, mod, route_all, norm_f.reshape(1, 1, D_MODEL), y_sorted)


def _sort_plan(route_all, counts):
    n_tok = route_all.shape[0]
    n_pairs = 2 * n_tok
    n_sorted = n_pairs + N_EXPERTS * TME
    flat_e = route_all[:, 0:2].astype(jnp.int32).reshape(n_pairs)
    rank = route_all[:, 4:6].astype(jnp.int32).reshape(n_pairs)
    counts = counts[:, 0].astype(jnp.int32)
    padded = ((counts + TME - 1) // TME) * TME
    ends = jnp.cumsum(padded)
    starts = ends - padded
    pos = starts[flat_e] + rank
    token = jnp.arange(n_pairs, dtype=jnp.int32) // 2
    src_row = (jnp.arange(n_sorted, dtype=jnp.int32) % n_tok).at[pos].set(
        token, unique_indices=True, mode="promise_in_bounds")
    tile_start = jnp.arange(n_sorted // TME, dtype=jnp.int32) * TME
    tile_expert = jnp.sum((tile_start[:, None] >= ends[None, :]).astype(jnp.int32), axis=1)
    tile_expert = jnp.minimum(tile_expert, N_EXPERTS - 1)
    n_used = ends[-1] // TME
    is_first = jnp.concatenate([jnp.ones((1,), jnp.int32),
                                (tile_expert[1:] != tile_expert[:-1]).astype(jnp.int32)])
    segment = jnp.cumsum(is_first) - 1
    next_tile = ends[tile_expert] // TME
    next_expert = jnp.where(next_tile < n_used, tile_expert[jnp.minimum(next_tile, n_sorted // TME - 1)], -1)
    return dict(pos=pos.astype(jnp.int32), src_row=src_row, tile_expert=tile_expert.astype(jnp.int32),
                n_used=n_used.astype(jnp.int32).reshape(1), segment=segment.astype(jnp.int32),
                next_expert=next_expert.astype(jnp.int32))


def kernel(x_prompt, x_sample, c_prompt, c_sample, state_gla, state_pool, w_ada, b_ada, norm1, norm2, w_in,
           gate_up, gate_bias, gla_norm, pool_w, pool_scale, w_out, w_group, b_group, w_expert, b_expert,
           w1, w3, w2, norm_f):
    assert w_ada.shape[0] == 1, "single-layer step"
    bp, tp, _ = x_prompt.shape
    bs, ts, _ = x_sample.shape
    grp_p = _Group(bp, tp, 0, bs)
    grp_s = _Group(bs, ts, bp * tp, 0)
    n_tok = bp * tp + bs * ts

    n_c = bp + bs
    n_c_pad = -(-n_c // 8) * 8
    c_all = jnp.concatenate([c_sample, c_prompt, jnp.zeros((n_c_pad - n_c, D_MODEL), F32)], axis=0)
    mod = _adaln(c_all, w_ada[0], b_ada[0]).reshape(n_c_pad, 6, D_MODEL)
    mod_p = mod_s = mod

    w_in_t = jnp.swapaxes(w_in.reshape(w_in.shape[1:]), 0, 1).astype(BF16)
    gup = gate_up[0].astype(BF16)
    gb = gate_bias[0].reshape(1, GLA_KEY_WIDTH)
    pw = pool_w[0].astype(BF16)
    wo = w_out[0].astype(BF16)
    gap = EXPERTS_PER_GROUP - N_GROUPS
    tail = LANES - EXPERTS_PER_GROUP - N_EXPERTS
    w_router = jnp.concatenate(
        [w_group[0], jnp.zeros((D_MODEL, gap), F32),
         jnp.transpose(w_expert[0], (1, 0, 2)).reshape(D_MODEL, N_EXPERTS),
         jnp.zeros((D_MODEL, tail), F32)], axis=1)
    w_router_hi = w_router.astype(BF16)
    w_router = jnp.stack([w_router_hi, (w_router - w_router_hi.astype(F32)).astype(BF16)])
    b_router = jnp.concatenate([b_group[0], jnp.zeros((gap,), F32), b_expert[0].reshape(N_EXPERTS),
                                jnp.zeros((tail,), F32)]).reshape(1, LANES)

    qkvr_p, la_p, u_p = _in_proj(grp_p, x_prompt, mod_p, norm1[0], w_in_t, gup, gb)
    qkvr_s, la_s, u_s = _in_proj(grp_s, x_sample, mod_s, norm1[0], w_in_t, gup, gb)

    o_p, gla_p = _gla_prompt(bp, tp, qkvr_p, la_p)
    o_s, gla_s = _gla_decode(bs, ts, qkvr_s, la_s, state_gla.reshape(state_gla.shape[1:]))

    halo_per_tile = TM // HALO
    halo_map_p = lambda i: (jnp.maximum(i * halo_per_tile - 1, 0), 0)
    counts = jnp.zeros((N_EXPERTS, LANES), F32)
    x1_p, h2_all, route_all, counts = _mix_out(
        grp_p, n_tok, 0, True, o_p, qkvr_p, u_p, u_p, (HALO, POOL_WIDTH), halo_map_p, x_prompt, mod_p,
        norm2[0], gla_norm[0], pw, pool_scale[0], wo, w_router, b_router, counts)
    x1_s, h2_all, route_all, counts = _mix_out(
        grp_s, n_tok, PAST_LEN, False, o_s, qkvr_s, u_s, state_pool.reshape(bs, POOL_BUF, POOL_WIDTH),
        (grp_s.nb, POOL_BUF, POOL_WIDTH), lambda i: (i, 0, 0), x_sample, mod_s, norm2[0], gla_norm[0], pw,
        pool_scale[0], wo, w_router, b_router, counts, shared=(h2_all, route_all))

    plan = _sort_plan(route_all, counts)
    y_sorted = _moe_gather(plan, h2_all, w1.reshape(w1.shape[1:]), w3.reshape(w3.shape[1:]),
                           w2.reshape(w2.shape[1:]))

    y_p = _finish(grp_p, plan["pos"], x1_p, mod_p, route_all, norm_f, y_sorted)
    y_s = _finish(grp_s, plan["pos"], x1_s, mod_s, route_all, norm_f, y_sorted)

    u_p3 = u_p.reshape(bp, tp, POOL_WIDTH)
    u_s3 = u_s.reshape(bs, ts, POOL_WIDTH)
    assert tp >= POOL_BUF > ts
    pool_p = u_p3[:, tp - POOL_BUF:]
    pool_s = jnp.concatenate([state_pool.reshape(bs, POOL_BUF, POOL_WIDTH)[:, ts:], u_s3], axis=1)
    lead = lambda a: a.reshape((1,) + a.shape)
    return (y_p, y_s, lead(gla_p), lead(pool_p), lead(gla_s), lead(pool_s))
```

```python
import functools

import jax
import jax.numpy as jnp
from jax import lax
from jax.experimental import pallas as pl
from jax.experimental.pallas import tpu as pltpu

D_MODEL = 2048
GLA_HEADS = 4
GLA_DK = 128
GLA_DV = 256
GLA_KEY_WIDTH = GLA_HEADS * GLA_DK
GLA_WIDTH = GLA_HEADS * GLA_DV
POOL_WIDTH = 1024
POOL_WINDOWS = (2, 4, 8, 16)
POOL_GW = 256
POOL_BUF = 15
HALO = 16
EXT_PAD = 8
EXT_TOK0 = EXT_PAD + HALO
GATE_RANK = 16
GATE_TEMP = 16.0
N_GROUPS = 4
EXPERTS_PER_GROUP = 8
N_EXPERTS = 32
EXPERT_FF = 512
EPS = 1e-6
PAST_LEN = 16384
QKVR_WIDTH = 2 * GLA_KEY_WIDTH + 2 * GLA_WIDTH

LANES = 128
ADALN_COLS = 1024
TM = 256
TME = 256
FINISH_AHEAD = 2
GLA_CHUNK = 64
GLA_SUB = 8
GLA_STEP = 512
GLA_TRIP = 4
VMEM_LIMIT = 56 * 1024 * 1024

BF16 = jnp.bfloat16
F32 = jnp.float32
NEG = -1e30
LOG2E = 1.4426950408889634


def _cparams(n_axes):
    return pltpu.CompilerParams(dimension_semantics=("arbitrary",) * n_axes,
                                vmem_limit_bytes=VMEM_LIMIT)


def _silu(x):
    return x / (1.0 + jnp.exp(-x))


def _bdot(a, b):
    return jnp.dot(a.astype(BF16), b.astype(BF16), preferred_element_type=F32)


def _split3(a):
    a1 = a.astype(BF16)
    r1 = a - a1.astype(F32)
    a2 = r1.astype(BF16)
    a3 = (r1 - a2.astype(F32)).astype(BF16)
    return a1, a2, a3


def _split2(a):
    hi = a.astype(BF16)
    lo = (a - hi.astype(F32)).astype(BF16)
    return hi, lo


def _dot_3pass(a, b_hi, b_lo):
    a_hi, a_lo = _split2(a)
    d = lambda x, y: jnp.dot(x, y, preferred_element_type=F32)
    return d(a_hi, b_hi) + (d(a_hi, b_lo) + d(a_lo, b_hi))


def _dot_exact_lhs(tri_bf16, g):
    g1, g2, g3 = _split3(g)
    d = lambda y: jnp.dot(tri_bf16, y, preferred_element_type=F32)
    return d(g1) + (d(g2) + d(g3))


def _adaln_kernel(c_ref, w_ref, b_ref, o_ref):
    c = c_ref[...]
    o_ref[...] = _bdot(_silu(c), w_ref[...]) + b_ref[...]


def _adaln(c_all, w_ada, b_ada):
    n, d = c_all.shape
    width = w_ada.shape[1]
    tn = ADALN_COLS
    return pl.pallas_call(
        _adaln_kernel,
        name="adaln",
        grid=(width // tn,),
        in_specs=[pl.BlockSpec((n, d), lambda j: (0, 0)),
                  pl.BlockSpec((d, tn), lambda j: (0, j)),
                  pl.BlockSpec((1, tn), lambda j: (0, j))],
        out_specs=pl.BlockSpec((n, tn), lambda j: (0, j)),
        out_shape=jax.ShapeDtypeStruct((n, width), F32),
        compiler_params=_cparams(1),
    )(c_all, w_ada, b_ada.reshape(1, width))


class _Group:
    def __init__(self, batch, seq, row_off, mod_off):
        self.batch, self.seq, self.row_off, self.mod_off = batch, seq, row_off, mod_off
        if seq >= TM:
            assert seq % TM == 0
            self.nb, self.tt = 1, TM
            self.tiles_per_batch = seq // TM
            self.n_tiles = batch * self.tiles_per_batch
        else:
            assert TM % seq == 0 and batch % (TM // seq) == 0
            self.nb, self.tt = TM // seq, seq
            self.tiles_per_batch = 1
            self.n_tiles = batch // self.nb
        self.rows = batch * seq
        self.tile_off = row_off // TM

    def x_map(self):
        if self.nb == 1:
            tpb = self.tiles_per_batch
            return lambda i, *_: (i // tpb, i % tpb, 0)
        return lambda i, *_: (i, 0, 0)

    def mod_map(self):
        assert self.mod_off % self.nb == 0
        off = self.mod_off // self.nb
        if self.nb == 1:
            tpb = self.tiles_per_batch
            return lambda i, *_: (i // tpb + off, 0, 0)
        return lambda i, *_: (i + off, 0, 0)


def _mod_rows(mod_ref, idx):
    return mod_ref[:, idx:idx + 1, :]


def _rmsnorm_mod(x, gain, scale, shift):
    ms = jnp.mean(x * x, axis=-1, keepdims=True)
    y = x * lax.rsqrt(ms + EPS) * gain
    return y * (1.0 + scale) + shift


def _in_proj_kernel(x_ref, mod_ref, n1_ref, wq_ref, wg_ref, wu_ref, gup_ref, gb_ref,
                    qkvr_ref, la_ref, u_ref):
    x = x_ref[...]
    h = _rmsnorm_mod(x, n1_ref[...], _mod_rows(mod_ref, 1), _mod_rows(mod_ref, 0))
    hb = h.reshape(TM, D_MODEL).astype(BF16)
    nt = (((1,), (1,)), ((), ()))
    qkvr_ref[...] = lax.dot_general(hb, wq_ref[...], nt, preferred_element_type=F32)
    u_ref[...] = lax.dot_general(hb, wu_ref[...], nt, preferred_element_type=F32)
    g_lr = lax.dot_general(hb, wg_ref[...], nt, preferred_element_type=F32)
    pre = jnp.dot(g_lr.astype(BF16), gup_ref[...], preferred_element_type=F32) + gb_ref[...]
    log_sig = jnp.minimum(pre, 0.0) - jnp.log1p(jnp.exp(-jnp.abs(pre)))
    la_ref[...] = log_sig / GATE_TEMP


def _in_proj(grp, x, mod, norm1, w_in_t, gup, gb):
    const = lambda i: (0, 0)
    u_row0 = QKVR_WIDTH + GATE_RANK
    row = lambda i: (i, 0)
    return pl.pallas_call(
        _in_proj_kernel,
        name="in_proj",
        grid=(grp.n_tiles,),
        in_specs=[pl.BlockSpec((grp.nb, grp.tt, D_MODEL), grp.x_map()),
                  pl.BlockSpec((grp.nb, 6, D_MODEL), grp.mod_map()),
                  pl.BlockSpec((1, 1, D_MODEL), lambda i: (0, 0, 0)),
                  pl.BlockSpec((QKVR_WIDTH, D_MODEL), const, pipeline_mode=pl.Buffered(1)),
                  pl.BlockSpec((GATE_RANK, D_MODEL), lambda i: (QKVR_WIDTH // GATE_RANK, 0),
                               pipeline_mode=pl.Buffered(1)),
                  pl.BlockSpec((pl.Element(POOL_WIDTH), pl.Element(D_MODEL)), lambda i: (u_row0, 0),
                               pipeline_mode=pl.Buffered(1)),
                  pl.BlockSpec(gup.shape, const, pipeline_mode=pl.Buffered(1)),
                  pl.BlockSpec(gb.shape, const, pipeline_mode=pl.Buffered(1))],
        out_specs=[pl.BlockSpec((TM, QKVR_WIDTH), row),
                   pl.BlockSpec((TM, GLA_KEY_WIDTH), row),
                   pl.BlockSpec((TM, POOL_WIDTH), row)],
        out_shape=[jax.ShapeDtypeStruct((grp.rows, QKVR_WIDTH), F32),
                   jax.ShapeDtypeStruct((grp.rows, GLA_KEY_WIDTH), F32),
                   jax.ShapeDtypeStruct((grp.rows, POOL_WIDTH), F32)],
        compiler_params=_cparams(1),
    )(x, mod, norm1.reshape(1, 1, D_MODEL), w_in_t, w_in_t, w_in_t, gup, gb)


def _gla_select_matrix(chunk, sub):
    r = jnp.arange(sub * GLA_DK, dtype=jnp.int32)[:, None] // GLA_DK
    l = jnp.arange(LANES, dtype=jnp.int32)[None, :]
    return ((l % sub == r) & (l < chunk)).astype(BF16)


def _gla_chunks(chunks, states, chunk, sub, wsel, t_refs, kb_refs, chained=True):
    n_sub = chunk // sub
    rows = lax.broadcasted_iota(jnp.int32, (chunk, chunk), 0)
    cols = lax.broadcasted_iota(jnp.int32, (chunk, chunk), 1)
    tri = (rows >= cols).astype(BF16)
    nt = (((1,), (1,)), ((), ()))
    tn = (((0,), (0,)), ((), ()))
    key_row = lax.broadcasted_iota(jnp.int32, (chunk, LANES), 0)
    lane = lax.broadcasted_iota(jnp.int32, (sub, chunk), 1)
    row = lax.broadcasted_iota(jnp.int32, (sub, chunk), 0)
    head = lambda a, h, w: a[:, h * w:(h + 1) * w]
    n = len(chunks)

    b4s = [_dot_exact_lhs(tri, g4) for (_, _, _, g4) in chunks]
    q4s = [q4 * (GLA_DK ** -0.5) for (q4, _, _, _) in chunks]

    for c in range(n):
        kb_ref, t_ref, k4 = kb_refs[c], t_refs[c], chunks[c][1]
        for h in range(GLA_HEADS):
            kb_ref[h] = head(k4, h, GLA_DK)
            kb_ref[GLA_HEADS + h] = head(b4s[c], h, GLA_DK) * LOG2E
        for h in range(GLA_HEADS):
            for s in range(n_sub):
                lo = s * sub
                r0 = (h * n_sub + s) * sub
                q_s, b_s = head(q4s[c], h, GLA_DK)[lo:lo + sub, :], kb_ref[GLA_HEADS + h, lo:lo + sub, :]
                for jl in range(sub):
                    k_j = jnp.broadcast_to(kb_ref[h, lo + jl:lo + jl + 1, :], (sub, GLA_DK))
                    b_j = jnp.broadcast_to(kb_ref[GLA_HEADS + h, lo + jl:lo + jl + 1, :], (sub, GLA_DK))
                    decay = jnp.exp2(jnp.minimum(b_s - b_j, 0.0))
                    t_ref[r0:r0 + sub, jl * GLA_DK:(jl + 1) * GLA_DK] = (q_s * k_j * decay).astype(t_ref.dtype)
    p_diags = [jnp.dot(t_refs[c][...].astype(BF16), wsel, preferred_element_type=F32) for c in range(n)]

    intra = []
    for c in range(n):
        per_head = []
        for h in range(GLA_HEADS):
            q, k, b = head(q4s[c], h, GLA_DK), head(chunks[c][1], h, GLA_DK), head(b4s[c], h, GLA_DK)
            p_blocks = []
            for s in range(n_sub):
                lo = s * sub
                r0 = (h * n_sub + s) * sub
                in_block = (lane >= lo) & (lane - lo <= row)
                p = jnp.where(in_block, p_diags[c][r0:r0 + sub, :chunk], 0.0)
                if s > 0:
                    ref_row = b[lo - 1:lo, :]
                    q_rel = q[lo:lo + sub, :] * jnp.exp(b[lo:lo + sub, :] - ref_row)
                    k_rel = k * jnp.exp(jnp.where(key_row < lo, ref_row - b, NEG))
                    p = p + lax.dot_general(q_rel.astype(BF16), k_rel.astype(BF16), nt,
                                            preferred_element_type=F32)
                p_blocks.append(p)
            p_full = p_blocks[0] if n_sub == 1 else jnp.concatenate(p_blocks, axis=0)
            per_head.append(_bdot(p_full, head(chunks[c][2], h, GLA_DV)))
        intra.append(per_head)

    outs, end_states = [], []
    for c in range(n):
        cur = states if chained else states[c]
        o_heads, new_states = [], []
        for h in range(GLA_HEADS):
            q, k, b = head(q4s[c], h, GLA_DK), head(chunks[c][1], h, GLA_DK), head(b4s[c], h, GLA_DK)
            v = head(chunks[c][2], h, GLA_DV)
            o_heads.append(intra[c][h] + _bdot(q * jnp.exp(b), cur[h]))
            b_last = b[chunk - 1:chunk, :]
            k_dec = k * jnp.exp(b_last - b)
            decay_col = jnp.exp(b[chunk - 8:chunk, :]).T[:, 7:8]
            new_states.append(decay_col * cur[h] + lax.dot_general(
                k_dec.astype(BF16), v.astype(BF16), tn, preferred_element_type=F32))
        states = new_states if chained else states
        end_states.append(new_states)
        outs.append(jnp.concatenate(o_heads, axis=1))
    return outs, (end_states[-1] if chained else end_states)


def _gla_prompt_kernel(q_ref, k_ref, v_ref, la_ref, wsel_ref, o_ref, s_ref, t_ref, kb_ref):
    @pl.when(pl.program_id(1) == 0)
    def _():
        s_ref[...] = jnp.zeros_like(s_ref)

    def body(trip, carry):
        states = [s_ref[0, h] for h in range(GLA_HEADS)]
        slices = [pl.ds(pl.multiple_of((GLA_TRIP * trip + c) * GLA_CHUNK, GLA_CHUNK), GLA_CHUNK)
                  for c in range(GLA_TRIP)]
        chunks = [(q_ref[sl, :], k_ref[sl, :], v_ref[sl, :], la_ref[sl, :]) for sl in slices]
        outs, states = _gla_chunks(chunks, states, GLA_CHUNK, GLA_SUB, wsel_ref[...],
                                   [t_ref.at[c] for c in range(GLA_TRIP)],
                                   [kb_ref.at[c] for c in range(GLA_TRIP)])
        for sl, o in zip(slices, outs):
            o_ref[sl, :] = o
        for h in range(GLA_HEADS):
            s_ref[0, h] = states[h]
        return carry

    lax.fori_loop(0, GLA_STEP // (GLA_TRIP * GLA_CHUNK), body, 0)


def _gla_prompt(batch, seq, qkvr, log_a):
    steps = seq // GLA_STEP
    row = lambda b, s: b * steps + s
    wsel = _gla_select_matrix(GLA_CHUNK, GLA_SUB)
    return pl.pallas_call(
        _gla_prompt_kernel,
        name="gla_prompt",
        grid=(batch, steps),
        in_specs=[pl.BlockSpec((GLA_STEP, GLA_KEY_WIDTH), lambda b, s: (row(b, s), 0)),
                  pl.BlockSpec((GLA_STEP, GLA_KEY_WIDTH), lambda b, s: (row(b, s), 1)),
                  pl.BlockSpec((GLA_STEP, GLA_WIDTH), lambda b, s: (row(b, s), 1)),
                  pl.BlockSpec((GLA_STEP, GLA_KEY_WIDTH), lambda b, s: (row(b, s), 0)),
                  pl.BlockSpec(wsel.shape, lambda b, s: (0, 0))],
        out_specs=[pl.BlockSpec((GLA_STEP, GLA_WIDTH), lambda b, s: (row(b, s), 0)),
                   pl.BlockSpec((1, GLA_HEADS, GLA_DK, GLA_DV), lambda b, s: (b, 0, 0, 0))],
        out_shape=[jax.ShapeDtypeStruct((batch * seq, GLA_WIDTH), F32),
                   jax.ShapeDtypeStruct((batch, GLA_HEADS, GLA_DK, GLA_DV), F32)],
        scratch_shapes=[pltpu.VMEM((GLA_TRIP, GLA_HEADS * GLA_CHUNK, GLA_SUB * GLA_DK), BF16),
                        pltpu.VMEM((GLA_TRIP, 2 * GLA_HEADS, GLA_CHUNK, GLA_DK), F32)],
        compiler_params=_cparams(2),
    )(qkvr, qkvr, qkvr, log_a, wsel)


GLA_DEC_BB = 8
GLA_DEC_TRIP = 4


def _gla_decode_kernel(seq, q_ref, k_ref, v_ref, la_ref, wsel_ref, s0_ref, o_ref, s_ref, t_ref, kb_ref):
    def body(trip, carry):
        elems = [GLA_DEC_TRIP * trip + c for c in range(GLA_DEC_TRIP)]
        slices = [pl.ds(pl.multiple_of(e * seq, seq), seq) for e in elems]
        chunks = [(q_ref[sl, :], k_ref[sl, :], v_ref[sl, :], la_ref[sl, :]) for sl in slices]
        states = [[s0_ref[e, h] for h in range(GLA_HEADS)] for e in elems]
        outs, new_states = _gla_chunks(chunks, states, seq, seq, wsel_ref[...],
                                       [t_ref.at[c] for c in range(GLA_DEC_TRIP)],
                                       [kb_ref.at[c] for c in range(GLA_DEC_TRIP)], chained=False)
        for e, sl, o, ns in zip(elems, slices, outs, new_states):
            o_ref[sl, :] = o
            for h in range(GLA_HEADS):
                s_ref[e, h] = ns[h]
        return carry

    lax.fori_loop(0, GLA_DEC_BB // GLA_DEC_TRIP, body, 0)


def _gla_decode(batch, seq, qkvr, log_a, state):
    rows = GLA_DEC_BB * seq
    wsel = _gla_select_matrix(seq, seq)
    state_spec = pl.BlockSpec((GLA_DEC_BB, GLA_HEADS, GLA_DK, GLA_DV), lambda i: (i, 0, 0, 0))
    return pl.pallas_call(
        functools.partial(_gla_decode_kernel, seq),
        name="gla_decode",
        grid=(batch // GLA_DEC_BB,),
        in_specs=[pl.BlockSpec((rows, GLA_KEY_WIDTH), lambda i: (i, 0)),
                  pl.BlockSpec((rows, GLA_KEY_WIDTH), lambda i: (i, 1)),
                  pl.BlockSpec((rows, GLA_WIDTH), lambda i: (i, 1)),
                  pl.BlockSpec((rows, GLA_KEY_WIDTH), lambda i: (i, 0)),
                  pl.BlockSpec(wsel.shape, lambda i: (0, 0)),
                  state_spec],
        out_specs=[pl.BlockSpec((rows, GLA_WIDTH), lambda i: (i, 0)), state_spec],
        out_shape=[jax.ShapeDtypeStruct((batch * seq, GLA_WIDTH), F32),
                   jax.ShapeDtypeStruct((batch, GLA_HEADS, GLA_DK, GLA_DV), F32)],
        scratch_shapes=[pltpu.VMEM((GLA_DEC_TRIP, GLA_HEADS * seq, seq * GLA_DK), F32),
                        pltpu.VMEM((GLA_DEC_TRIP, 2 * GLA_HEADS, seq, GLA_DK), F32)],
        compiler_params=_cparams(1),
    )(qkvr, qkvr, qkvr, log_a, wsel, state)


def _route(logits, cnt_ref):
    lt = logits.T
    n = lt.shape[1]
    big = jnp.int32(10 ** 6)
    row8 = lax.broadcasted_iota(jnp.int32, (EXPERTS_PER_GROUP, n), 0)
    lg = jnp.where(row8 < N_GROUPS, lt[0:EXPERTS_PER_GROUP], NEG)
    mg = jnp.max(lg, axis=0, keepdims=True)
    g_idx = jnp.min(jnp.where(lg == mg, row8, big), axis=0, keepdims=True)
    p_sel = 1.0 / jnp.sum(jnp.exp(lg - mg), axis=0, keepdims=True)
    le = jnp.zeros((EXPERTS_PER_GROUP, n), F32)
    for g in range(N_GROUPS):
        lo = EXPERTS_PER_GROUP * (g + 1)
        le = jnp.where(g_idx == g, lt[lo:lo + EXPERTS_PER_GROUP], le)
    m1 = jnp.max(le, axis=0, keepdims=True)
    i1 = jnp.min(jnp.where(le == m1, row8, big), axis=0, keepdims=True)
    rest = row8 != i1
    m2 = jnp.max(jnp.where(rest, le, NEG), axis=0, keepdims=True)
    i2 = jnp.min(jnp.where(rest & (le == m2), row8, big), axis=0, keepdims=True)
    e2 = jnp.exp(m2 - m1)
    w1 = p_sel / (1.0 + e2)
    w2 = p_sel * e2 / (1.0 + e2)
    ex1_i = g_idx * EXPERTS_PER_GROUP + i1
    ex2_i = g_idx * EXPERTS_PER_GROUP + i2
    ex1, ex2 = ex1_i.astype(F32), ex2_i.astype(F32)

    e_row = lax.broadcasted_iota(jnp.int32, (N_EXPERTS, n), 0)
    hit1, hit2 = e_row == ex1_i, e_row == ex2_i
    member = (hit1 | hit2).astype(BF16)
    before = (lax.broadcasted_iota(jnp.int32, (n, n), 0) < lax.broadcasted_iota(jnp.int32, (n, n), 1))
    prior = jnp.dot(member, before.astype(BF16), preferred_element_type=F32) + cnt_ref[:, 0:1]
    rank1 = jnp.sum(jnp.where(hit1, prior, 0.0), axis=0, keepdims=True)
    rank2 = jnp.sum(jnp.where(hit2, prior, 0.0), axis=0, keepdims=True)
    cnt_ref[...] = cnt_ref[...] + jnp.sum(member.astype(F32), axis=1, keepdims=True)

    packed = jnp.where(row8 == 0, ex1, jnp.where(row8 == 1, ex2,
                       jnp.where(row8 == 2, w1, jnp.where(row8 == 3, w2,
                                 jnp.where(row8 == 4, rank1, jnp.where(row8 == 5, rank2, 0.0))))))
    full = jnp.concatenate([packed, jnp.zeros((LANES - EXPERTS_PER_GROUP, n), F32)], axis=0)
    return full.T, packed


def _mix_out_kernel(grp, pos0, zero_first_halo, n_alias,
                    o_ref, r_ref, u_ref, halo_ref, x_ref, mod_ref, n2_ref, gn_ref, pw_ref, ps_ref,
                    wo_ref, wr_ref, br_ref, cnt0_ref, *rest):
    x1_ref, h2_ref, rt_ref, rtt_ref, cnt_ref, ext_ref, lvl_a, lvl_b, ymix_ref = rest[n_alias:]
    i = pl.program_id(0)

    @pl.when(i == 0)
    def _():
        cnt_ref[...] = cnt0_ref[...]

    @pl.when(i < grp.n_tiles)
    def _():
        _mix_out_tile(grp, pos0, zero_first_halo, i, o_ref, r_ref, u_ref, halo_ref, x_ref, mod_ref, n2_ref,
                      gn_ref, pw_ref, ps_ref, wo_ref, wr_ref, br_ref, x1_ref, h2_ref, rt_ref, rtt_ref, cnt_ref,
                      ext_ref, (lvl_a, lvl_b), ymix_ref)

    @pl.when(i >= grp.n_tiles)
    def _():
        h2_ref[...] = jnp.zeros_like(h2_ref)
        rt_ref[...] = jnp.zeros_like(rt_ref)
        rtt_ref[...] = jnp.zeros_like(rtt_ref)


def _mix_out_tile(grp, pos0, zero_first_halo, i, o_ref, r_ref, u_ref, halo_ref, x_ref, mod_ref, n2_ref,
                  gn_ref, pw_ref, ps_ref, wo_ref, wr_ref, br_ref, x1_ref, h2_ref, rt_ref, rtt_ref, cnt_ref,
                  ext_ref, lvl_refs, ymix_ref):
    nb, tt = grp.nb, grp.tt
    hist = halo_ref.shape[-2]

    for h in range(GLA_HEADS):
        cs = slice(h * GLA_DV, (h + 1) * GLA_DV)
        oh = o_ref[:, cs]
        ms = jnp.mean(oh * oh, axis=-1, keepdims=True)
        yh = oh * lax.rsqrt(ms + EPS) * gn_ref[:, cs] * _silu(r_ref[:, cs])
        ymix_ref[:, cs] = yh.astype(BF16)

    halo = halo_ref[...]
    if zero_first_halo:
        halo = jnp.where(i % grp.tiles_per_batch == 0, 0.0, halo)
    n_ext = EXT_TOK0 + tt
    ext_ref[:, 0:EXT_TOK0 - hist, :] = jnp.zeros((nb, EXT_TOK0 - hist, POOL_WIDTH), F32)
    ext_ref[:, EXT_TOK0 - hist:EXT_TOK0, :] = halo.reshape(nb, hist, POOL_WIDTH)
    u = u_ref[...].reshape(nb, tt, POOL_WIDTH)
    ext_ref[:, EXT_TOK0:n_ext, :] = u
    for lvl_ref in lvl_refs:
        lvl_ref[:, 0:EXT_PAD, :] = jnp.zeros((nb, EXT_PAD, POOL_GW), F32)
    t_idx = lax.broadcasted_iota(jnp.int32, (nb, tt, POOL_GW), 1)
    if grp.nb == 1:
        pos = (i % grp.tiles_per_batch) * TM + t_idx + pos0
    else:
        pos = t_idx + pos0
    y_gla = []
    part = D_MODEL // len(POOL_WINDOWS)
    for gi, w in enumerate(POOL_WINDOWS):
        y_gla.append(jnp.dot(ymix_ref[:, :GLA_WIDTH], wo_ref[:GLA_WIDTH, gi * part:(gi + 1) * part],
                             preferred_element_type=F32))
        cs = slice(gi * POOL_GW, (gi + 1) * POOL_GW)
        cur = lambda lo, hi: ext_ref[:, lo:hi, cs]
        d, level = 1, 0
        while d < w:
            nxt = lvl_refs[level % 2]
            nxt[:, EXT_PAD:n_ext, :] = cur(EXT_PAD, n_ext) + cur(EXT_PAD - d, n_ext - d)
            cur = lambda lo, hi, ref=nxt: ref[:, lo:hi, :]
            d, level = 2 * d, level + 1
        acc = cur(EXT_TOK0, n_ext)
        cnt = jnp.minimum(pos + 1, w).astype(F32)
        pooled = acc / cnt - u[:, :, cs]
        yp = _bdot(pooled.reshape(TM, POOL_GW), pw_ref[gi]) * ps_ref[:, cs]
        ymix_ref[:, GLA_WIDTH + gi * POOL_GW:GLA_WIDTH + (gi + 1) * POOL_GW] = yp.astype(BF16)

    y = jnp.concatenate(y_gla, axis=1) + jnp.dot(ymix_ref[:, GLA_WIDTH:], wo_ref[GLA_WIDTH:, :],
                                                 preferred_element_type=F32)
    x1 = x_ref[...] + _mod_rows(mod_ref, 2) * y.reshape(nb, tt, D_MODEL)
    x1_ref[...] = x1
    h2 = _rmsnorm_mod(x1, n2_ref[...], _mod_rows(mod_ref, 4), _mod_rows(mod_ref, 3)).reshape(TM, D_MODEL)
    h2_ref[...] = h2
    logits = _dot_3pass(h2, wr_ref[0], wr_ref[1]) + br_ref[...]
    rt_ref[...], rtt_ref[...] = _route(logits, cnt_ref)


def _mix_out(grp, n_tok, pos0, zero_first_halo, o, qkvr, u, halo_src, halo_block, halo_map, x, mod, norm2,
             gla_norm, pool_w, pool_scale, w_out, w_router, b_router, counts, shared=()):
    n_alias = len(shared)
    n = grp.n_tiles
    n_fill = 0 if shared else n_tok // TM - n
    assert n_fill == 0 or grp.tile_off == 0
    clamp = lambda f: (lambda i: f(jnp.minimum(i, n - 1)))
    const2 = lambda i: (0, 0)
    row = clamp(lambda i: (i, 0))
    off = grp.tile_off
    kern = functools.partial(_mix_out_kernel, grp, pos0, zero_first_halo, n_alias)
    return pl.pallas_call(
        kern,
        name="mix_out",
        grid=(n + n_fill,),
        in_specs=[pl.BlockSpec((TM, GLA_WIDTH), row),
                  pl.BlockSpec((TM, GLA_WIDTH), clamp(lambda i: (i, 2))),
                  pl.BlockSpec((TM, POOL_WIDTH), row),
                  pl.BlockSpec(halo_block, clamp(halo_map)),
                  pl.BlockSpec((grp.nb, grp.tt, D_MODEL), clamp(grp.x_map())),
                  pl.BlockSpec((grp.nb, 6, D_MODEL), clamp(grp.mod_map())),
                  pl.BlockSpec((1, 1, D_MODEL), lambda i: (0, 0, 0)),
                  pl.BlockSpec((1, GLA_WIDTH), const2),
                  pl.BlockSpec(pool_w.shape, lambda i: (0, 0, 0), pipeline_mode=pl.Buffered(1)),
                  pl.BlockSpec((1, POOL_WIDTH), const2),
                  pl.BlockSpec(w_out.shape, const2, pipeline_mode=pl.Buffered(1)),
                  pl.BlockSpec(w_router.shape, lambda i: (0, 0, 0), pipeline_mode=pl.Buffered(1)),
                  pl.BlockSpec((1, LANES), const2),
                  pl.BlockSpec((N_EXPERTS, LANES), const2)]
                 + [pl.BlockSpec(memory_space=pl.ANY)] * n_alias,
        out_specs=[pl.BlockSpec((grp.nb, grp.tt, D_MODEL), clamp(grp.x_map())),
                   pl.BlockSpec((TM, D_MODEL), lambda i: (i + off, 0)),
                   pl.BlockSpec((TM, LANES), lambda i: (i + off, 0)),
                   pl.BlockSpec((EXPERTS_PER_GROUP, TM), lambda i: (0, i + off)),
                   pl.BlockSpec((N_EXPERTS, LANES), const2)],
        out_shape=[jax.ShapeDtypeStruct(x.shape, F32),
                   jax.ShapeDtypeStruct((n_tok, D_MODEL), F32),
                   jax.ShapeDtypeStruct((n_tok, LANES), F32),
                   jax.ShapeDtypeStruct((EXPERTS_PER_GROUP, n_tok), F32),
                   jax.ShapeDtypeStruct((N_EXPERTS, LANES), F32)],
        scratch_shapes=[pltpu.VMEM((grp.nb, EXT_TOK0 + grp.tt, POOL_WIDTH), F32),
                        pltpu.VMEM((grp.nb, EXT_TOK0 + grp.tt, POOL_GW), F32),
                        pltpu.VMEM((grp.nb, EXT_TOK0 + grp.tt, POOL_GW), F32),
                        pltpu.VMEM((TM, D_MODEL), BF16)],
        input_output_aliases={14 + k: 1 + k for k in range(n_alias)},
        compiler_params=_cparams(1),
    )(o, qkvr, u, halo_src, x, mod, norm2.reshape(1, 1, D_MODEL), gla_norm.reshape(1, GLA_WIDTH),
      pool_w, pool_scale.reshape(1, POOL_WIDTH), w_out, w_router, b_router, counts, *shared)


def _row_copy(src_hbm, src_row, dst, dst_row, sem):
    return pltpu.make_async_copy(src_hbm.at[pl.ds(src_row, 1), :], dst.at[pl.ds(dst_row, 1), :], sem)


def _tile_wait(src_hbm, dst, sem):
    pltpu.make_async_copy(src_hbm.at[pl.ds(0, dst.shape[0]), :], dst, sem).wait()


MOE_AHEAD = 2


def _moe_gather_kernel(te_ref, src_ref, nu_ref, seg_ref, nxt_ref, h_hbm, w1_hbm, w3_hbm, w2_hbm, y_ref,
                       buf0, buf1, buf2, sem, wf1, wf3, wf2, wsem, w1b, w3b, w2b):
    i = pl.program_id(0)
    n_used = nu_ref[0]
    bufs = (buf0, buf1, buf2)
    n_slots = MOE_AHEAD + 1
    n_parts = 4

    def issue(tile, slot, part):
        base = tile * TME
        per = TME // n_parts
        for r in range(part * per, (part + 1) * per):
            _row_copy(h_hbm, src_ref[base + r], bufs[slot], r, sem.at[slot]).start(priority=r % 2)

    def weight_copies(expert, wslot):
        return (pltpu.make_async_copy(w1_hbm.at[expert], wf1.at[wslot], wsem.at[wslot]),
                pltpu.make_async_copy(w3_hbm.at[expert], wf3.at[wslot], wsem.at[wslot]),
                pltpu.make_async_copy(w2_hbm.at[expert], wf2.at[wslot], wsem.at[wslot]))

    def compute(slot, ahead):
        nxt = (lambda part: issue(i + MOE_AHEAD, (slot + MOE_AHEAD) % n_slots, part)) if ahead else (lambda part: None)
        _tile_wait(h_hbm, bufs[slot], sem.at[slot])
        nxt(0)
        x = bufs[slot][...].astype(BF16)
        nxt(1)
        a = jnp.dot(x, w1b[...], preferred_element_type=F32)
        nxt(2)
        b = jnp.dot(x, w3b[...], preferred_element_type=F32)
        nxt(3)
        hid = _silu(a) * b
        y_ref[...] = jnp.dot(hid.astype(BF16), w2b[...], preferred_element_type=F32)

    @pl.when(i == 0)
    def _():
        for c in weight_copies(te_ref[0], 0):
            c.start()
        for t in range(MOE_AHEAD):
            @pl.when(t < n_used)
            def _():
                for part in range(n_parts):
                    issue(t, t, part)

    prev = jnp.maximum(i - 1, 0)

    @pl.when((i < n_used) & ((i == 0) | (te_ref[i] != te_ref[prev])))
    def _():
        wslot = seg_ref[i] % 2
        for c in weight_copies(te_ref[i], wslot):
            c.wait()
        w1b[...] = wf1[wslot].astype(BF16)
        w3b[...] = wf3[wslot].astype(BF16)
        w2b[...] = wf2[wslot].astype(BF16)

        @pl.when(nxt_ref[i] >= 0)
        def _():
            for c in weight_copies(nxt_ref[i], 1 - wslot):
                c.start()

    for slot in range(n_slots):
        @pl.when((i + MOE_AHEAD < n_used) & (i % n_slots == slot))
        def _():
            compute(slot, True)

        @pl.when((i < n_used) & (i + MOE_AHEAD >= n_used) & (i % n_slots == slot))
        def _():
            compute(slot, False)

    @pl.when(i >= n_used)
    def _():
        y_ref[...] = jnp.zeros_like(y_ref)


def _moe_gather(plan, h2_all, w1, w3, w2):
    n_sorted = plan["src_row"].shape[0]
    grid_spec = pltpu.PrefetchScalarGridSpec(
        num_scalar_prefetch=5,
        grid=(n_sorted // TME,),
        in_specs=[pl.BlockSpec(memory_space=pl.ANY)] * 4,
        out_specs=pl.BlockSpec((TME, D_MODEL), lambda i, *_: (i, 0)),
        scratch_shapes=[pltpu.VMEM((TME, D_MODEL), F32)] * (MOE_AHEAD + 1)
                       + [pltpu.SemaphoreType.DMA((MOE_AHEAD + 1,)),
                          pltpu.VMEM((2, D_MODEL, EXPERT_FF), F32), pltpu.VMEM((2, D_MODEL, EXPERT_FF), F32),
                          pltpu.VMEM((2, EXPERT_FF, D_MODEL), F32),
                          pltpu.SemaphoreType.DMA((2,)),
                          pltpu.VMEM((D_MODEL, EXPERT_FF), BF16), pltpu.VMEM((D_MODEL, EXPERT_FF), BF16),
                          pltpu.VMEM((EXPERT_FF, D_MODEL), BF16)],
    )
    return pl.pallas_call(
        _moe_gather_kernel,
        name="moe",
        grid_spec=grid_spec,
        out_shape=jax.ShapeDtypeStruct((n_sorted, D_MODEL), F32),
        compiler_params=_cparams(1),
    )(plan["tile_expert"], plan["src_row"], plan["n_used"], plan["segment"], plan["next_expert"],
      h2_all, w1, w3, w2)


def _finish_kernel(grp, pos_ref, x1_ref, mod_ref, rt_ref, nf_ref, y_hbm, out_ref, *scratch):
    n_slots = FINISH_AHEAD + 1
    bufs_a, bufs_b, sem = scratch[:n_slots], scratch[n_slots:2 * n_slots], scratch[2 * n_slots]
    i = pl.program_id(0)
    n_steps = pl.num_programs(0)
    n_parts = 4
    n_tok = pos_ref.shape[0] // 2

    def issue(tile, slot, part):
        base = tile * TM + grp.row_off
        per = TM // n_parts
        for r in range(part * per, (part + 1) * per):
            _row_copy(y_hbm, pos_ref[base + r], bufs_a[slot], r, sem.at[slot]).start(priority=0)
            _row_copy(y_hbm, pos_ref[n_tok + base + r], bufs_b[slot], r, sem.at[slot]).start(priority=1)

    def compute(slot, ahead):
        nxt = ((lambda part: issue(i + FINISH_AHEAD, (slot + FINISH_AHEAD) % n_slots, part)) if ahead
               else (lambda part: None))
        _tile_wait(y_hbm, bufs_a[slot], sem.at[slot])
        _tile_wait(y_hbm, bufs_b[slot], sem.at[slot])
        nxt(0)
        rt = rt_ref[...]
        moe = rt[:, 2:3] * bufs_a[slot][...] + rt[:, 3:4] * bufs_b[slot][...]
        nxt(1)
        x2 = x1_ref[...] + _mod_rows(mod_ref, 5) * moe.reshape(grp.nb, grp.tt, D_MODEL)
        nxt(2)
        ms = jnp.mean(x2 * x2, axis=-1, keepdims=True)
        nxt(3)
        out_ref[...] = x2 * lax.rsqrt(ms + EPS) * nf_ref[...]

    @pl.when(i == 0)
    def _():
        for t in range(FINISH_AHEAD):
            @pl.when(t < n_steps)
            def _():
                for part in range(n_parts):
                    issue(t, t, part)

    for slot in range(n_slots):
        @pl.when((i + FINISH_AHEAD < n_steps) & (i % n_slots == slot))
        def _():
            compute(slot, True)

        @pl.when((i + FINISH_AHEAD >= n_steps) & (i % n_slots == slot))
        def _():
            compute(slot, False)


def _finish(grp, pos, x1, mod, route_all, norm_f, y_sorted):
    off = grp.tile_off
    grid_spec = pltpu.PrefetchScalarGridSpec(
        num_scalar_prefetch=1,
        grid=(grp.n_tiles,),
        in_specs=[pl.BlockSpec((grp.nb, grp.tt, D_MODEL), grp.x_map()),
                  pl.BlockSpec((grp.nb, 6, D_MODEL), grp.mod_map()),
                  pl.BlockSpec((TM, LANES), lambda i, p: (i + off, 0)),
                  pl.BlockSpec((1, 1, D_MODEL), lambda i, p: (0, 0, 0)),
                  pl.BlockSpec(memory_space=pl.ANY)],
        out_specs=pl.BlockSpec((grp.nb, grp.tt, D_MODEL), grp.x_map()),
        scratch_shapes=[pltpu.VMEM((TM, D_MODEL), F32)] * (2 * (FINISH_AHEAD + 1))
                       + [pltpu.SemaphoreType.DMA((FINISH_AHEAD + 1,))],
    )
    return pl.pallas_call(
        functools.partial(_finish_kernel, grp),
        name="finish",
        grid_spec=grid_spec,
        out_shape=jax.ShapeDtypeStruct(x1.shape, F32),
        compiler_params=_cparams(1),
    )(pos, x1, mod, route_all, norm_f.reshape(1, 1, D_MODEL), y_sorted)


def _sort_plan(route_t, counts):
    n_tok = route_t.shape[1]
    n_pairs = 2 * n_tok
    n_sorted = n_pairs + N_EXPERTS * TME
    expert = route_t[0:2].astype(jnp.int32)
    rank = route_t[4:6].astype(jnp.int32)
    counts = counts[:, 0].astype(jnp.int32)
    padded = ((counts + TME - 1) // TME) * TME
    ends = jnp.cumsum(padded)
    starts = ends - padded
    start_of = jnp.zeros_like(expert)
    for e in range(N_EXPERTS):
        start_of = jnp.where(expert == e, starts[e], start_of)
    pos = (start_of + rank).reshape(n_pairs)
    token = jnp.tile(jnp.arange(n_tok, dtype=jnp.int32), 2)
    src_row = (jnp.arange(n_sorted, dtype=jnp.int32) % n_tok).at[pos].set(
        token, unique_indices=True, mode="promise_in_bounds")
    tile_start = jnp.arange(n_sorted // TME, dtype=jnp.int32) * TME
    tile_expert = jnp.sum((tile_start[:, None] >= ends[None, :]).astype(jnp.int32), axis=1)
    tile_expert = jnp.minimum(tile_expert, N_EXPERTS - 1)
    n_used = ends[-1] // TME
    is_first = jnp.concatenate([jnp.ones((1,), jnp.int32),
                                (tile_expert[1:] != tile_expert[:-1]).astype(jnp.int32)])
    segment = jnp.cumsum(is_first) - 1
    next_tile = ends[tile_expert] // TME
    next_expert = jnp.where(next_tile < n_used, tile_expert[jnp.minimum(next_tile, n_sorted // TME - 1)], -1)
    return dict(pos=pos.astype(jnp.int32), src_row=src_row, tile_expert=tile_expert.astype(jnp.int32),
                n_used=n_used.astype(jnp.int32).reshape(1), segment=segment.astype(jnp.int32),
                next_expert=next_expert.astype(jnp.int32))


def kernel(x_prompt, x_sample, c_prompt, c_sample, state_gla, state_pool, w_ada, b_ada, norm1, norm2, w_in,
           gate_up, gate_bias, gla_norm, pool_w, pool_scale, w_out, w_group, b_group, w_expert, b_expert,
           w1, w3, w2, norm_f):
    assert w_ada.shape[0] == 1, "single-layer step"
    bp, tp, _ = x_prompt.shape
    bs, ts, _ = x_sample.shape
    grp_p = _Group(bp, tp, 0, bs)
    grp_s = _Group(bs, ts, bp * tp, 0)
    n_tok = bp * tp + bs * ts

    n_c = bp + bs
    n_c_pad = -(-n_c // 8) * 8
    c_all = jnp.concatenate([c_sample, c_prompt, jnp.zeros((n_c_pad - n_c, D_MODEL), F32)], axis=0)
    mod = _adaln(c_all, w_ada[0], b_ada[0]).reshape(n_c_pad, 6, D_MODEL)
    mod_p = mod_s = mod

    w_in_t = jnp.swapaxes(w_in.reshape(w_in.shape[1:]), 0, 1).astype(BF16)
    gup = gate_up[0].astype(BF16)
    gb = gate_bias[0].reshape(1, GLA_KEY_WIDTH)
    pw = pool_w[0].astype(BF16)
    wo = w_out[0].astype(BF16)
    gap = EXPERTS_PER_GROUP - N_GROUPS
    tail = LANES - EXPERTS_PER_GROUP - N_EXPERTS
    w_router = jnp.concatenate(
        [w_group[0], jnp.zeros((D_MODEL, gap), F32),
         jnp.transpose(w_expert[0], (1, 0, 2)).reshape(D_MODEL, N_EXPERTS),
         jnp.zeros((D_MODEL, tail), F32)], axis=1)
    w_router_hi = w_router.astype(BF16)
    w_router = jnp.stack([w_router_hi, (w_router - w_router_hi.astype(F32)).astype(BF16)])
    b_router = jnp.concatenate([b_group[0], jnp.zeros((gap,), F32), b_expert[0].reshape(N_EXPERTS),
                                jnp.zeros((tail,), F32)]).reshape(1, LANES)

    qkvr_p, la_p, u_p = _in_proj(grp_p, x_prompt, mod_p, norm1[0], w_in_t, gup, gb)
    qkvr_s, la_s, u_s = _in_proj(grp_s, x_sample, mod_s, norm1[0], w_in_t, gup, gb)

    o_p, gla_p = _gla_prompt(bp, tp, qkvr_p, la_p)
    o_s, gla_s = _gla_decode(bs, ts, qkvr_s, la_s, state_gla.reshape(state_gla.shape[1:]))

    halo_per_tile = TM // HALO
    halo_map_p = lambda i: (jnp.maximum(i * halo_per_tile - 1, 0), 0)
    counts = jnp.zeros((N_EXPERTS, LANES), F32)
    x1_p, h2_all, route_all, route_t, counts = _mix_out(
        grp_p, n_tok, 0, True, o_p, qkvr_p, u_p, u_p, (HALO, POOL_WIDTH), halo_map_p, x_prompt, mod_p,
        norm2[0], gla_norm[0], pw, pool_scale[0], wo, w_router, b_router, counts)
    x1_s, h2_all, route_all, route_t, counts = _mix_out(
        grp_s, n_tok, PAST_LEN, False, o_s, qkvr_s, u_s, state_pool.reshape(bs, POOL_BUF, POOL_WIDTH),
        (grp_s.nb, POOL_BUF, POOL_WIDTH), lambda i: (i, 0, 0), x_sample, mod_s, norm2[0], gla_norm[0], pw,
        pool_scale[0], wo, w_router, b_router, counts, shared=(h2_all, route_all, route_t))

    plan = _sort_plan(route_t, counts)
    y_sorted = _moe_gather(plan, h2_all, w1.reshape(w1.shape[1:]), w3.reshape(w3.shape[1:]),
                           w2.reshape(w2.shape[1:]))

    y_p = _finish(grp_p, plan["pos"], x1_p, mod_p, route_all, norm_f, y_sorted)
    y_s = _finish(grp_s, plan["pos"], x1_s, mod_s, route_all, norm_f, y_sorted)

    u_p3 = u_p.reshape(bp, tp, POOL_WIDTH)
    u_s3 = u_s.reshape(bs, ts, POOL_WIDTH)
    assert tp >= POOL_BUF > ts
    pool_p = u_p3[:, tp - POOL_BUF:]
    pool_s = jnp.concatenate([state_pool.reshape(bs, POOL_BUF, POOL_WIDTH)[:, ts:], u_s3], axis=1)
    lead = lambda a: a.reshape((1,) + a.shape)
    return (y_p, y_s, lead(gla_p), lead(pool_p), lead(gla_s), lead(pool_s))
```

```python
import functools

import jax
import jax.numpy as jnp
from jax import lax
from jax.experimental import pallas as pl
from jax.experimental.pallas import tpu as pltpu

D_MODEL = 2048
GLA_HEADS = 4
GLA_DK = 128
GLA_DV = 256
GLA_KEY_WIDTH = GLA_HEADS * GLA_DK
GLA_WIDTH = GLA_HEADS * GLA_DV
POOL_WIDTH = 1024
POOL_WINDOWS = (2, 4, 8, 16)
POOL_GW = 256
POOL_BUF = 15
HALO = 16
EXT_PAD = 8
EXT_TOK0 = EXT_PAD + HALO
GATE_RANK = 16
GATE_TEMP = 16.0
N_GROUPS = 4
EXPERTS_PER_GROUP = 8
N_EXPERTS = 32
EXPERT_FF = 512
EPS = 1e-6
PAST_LEN = 16384
QKVR_WIDTH = 2 * GLA_KEY_WIDTH + 2 * GLA_WIDTH

LANES = 128
ADALN_COLS = 1024
TM = 256
TME = 256
FINISH_AHEAD = 2
GLA_CHUNK = 64
GLA_SUB = 8
GLA_STEP = 512
GLA_TRIP = 4
VMEM_LIMIT = 56 * 1024 * 1024

BF16 = jnp.bfloat16
F32 = jnp.float32
NEG = -1e30
LOG2E = 1.4426950408889634


def _cparams(n_axes):
    return pltpu.CompilerParams(dimension_semantics=("arbitrary",) * n_axes,
                                vmem_limit_bytes=VMEM_LIMIT)


def _silu(x):
    return x / (1.0 + jnp.exp(-x))


def _bdot(a, b):
    return jnp.dot(a.astype(BF16), b.astype(BF16), preferred_element_type=F32)


def _split3(a):
    a1 = a.astype(BF16)
    r1 = a - a1.astype(F32)
    a2 = r1.astype(BF16)
    a3 = (r1 - a2.astype(F32)).astype(BF16)
    return a1, a2, a3


def _split2(a):
    hi = a.astype(BF16)
    lo = (a - hi.astype(F32)).astype(BF16)
    return hi, lo


def _dot_3pass(a, b_hi, b_lo):
    a_hi, a_lo = _split2(a)
    d = lambda x, y: jnp.dot(x, y, preferred_element_type=F32)
    return d(a_hi, b_hi) + (d(a_hi, b_lo) + d(a_lo, b_hi))


def _dot_exact_lhs(tri_bf16, g):
    g1, g2, g3 = _split3(g)
    d = lambda y: jnp.dot(tri_bf16, y, preferred_element_type=F32)
    return d(g1) + (d(g2) + d(g3))


def _adaln_kernel(c_ref, w_ref, b_ref, o_ref):
    c = c_ref[...]
    o_ref[...] = _bdot(_silu(c), w_ref[...]) + b_ref[...]


def _adaln(c_all, w_ada, b_ada):
    n, d = c_all.shape
    width = w_ada.shape[1]
    tn = ADALN_COLS
    return pl.pallas_call(
        _adaln_kernel,
        name="adaln",
        grid=(width // tn,),
        in_specs=[pl.BlockSpec((n, d), lambda j: (0, 0)),
                  pl.BlockSpec((d, tn), lambda j: (0, j)),
                  pl.BlockSpec((1, tn), lambda j: (0, j))],
        out_specs=pl.BlockSpec((n, tn), lambda j: (0, j)),
        out_shape=jax.ShapeDtypeStruct((n, width), F32),
        compiler_params=_cparams(1),
    )(c_all, w_ada, b_ada.reshape(1, width))


class _Group:
    def __init__(self, batch, seq, row_off, mod_off):
        self.batch, self.seq, self.row_off, self.mod_off = batch, seq, row_off, mod_off
        if seq >= TM:
            assert seq % TM == 0
            self.nb, self.tt = 1, TM
            self.tiles_per_batch = seq // TM
            self.n_tiles = batch * self.tiles_per_batch
        else:
            assert TM % seq == 0 and batch % (TM // seq) == 0
            self.nb, self.tt = TM // seq, seq
            self.tiles_per_batch = 1
            self.n_tiles = batch // self.nb
        self.rows = batch * seq
        self.tile_off = row_off // TM

    def x_map(self):
        if self.nb == 1:
            tpb = self.tiles_per_batch
            return lambda i, *_: (i // tpb, i % tpb, 0)
        return lambda i, *_: (i, 0, 0)

    def mod_map(self):
        assert self.mod_off % self.nb == 0
        off = self.mod_off // self.nb
        if self.nb == 1:
            tpb = self.tiles_per_batch
            return lambda i, *_: (i // tpb + off, 0, 0)
        return lambda i, *_: (i + off, 0, 0)


def _mod_rows(mod_ref, idx):
    return mod_ref[:, idx:idx + 1, :]


def _rmsnorm_mod(x, gain, scale, shift):
    ms = jnp.mean(x * x, axis=-1, keepdims=True)
    y = x * lax.rsqrt(ms + EPS) * gain
    return y * (1.0 + scale) + shift


def _in_proj_kernel(x_ref, mod_ref, n1_ref, wq_ref, wg_ref, wu_ref, gup_ref, gb_ref,
                    qkvr_ref, la_ref, u_ref):
    x = x_ref[...]
    h = _rmsnorm_mod(x, n1_ref[...], _mod_rows(mod_ref, 1), _mod_rows(mod_ref, 0))
    hb = h.reshape(TM, D_MODEL).astype(BF16)
    nt = (((1,), (1,)), ((), ()))
    qkvr_ref[...] = lax.dot_general(hb, wq_ref[...], nt, preferred_element_type=F32)
    u_ref[...] = lax.dot_general(hb, wu_ref[...], nt, preferred_element_type=F32)
    g_lr = lax.dot_general(hb, wg_ref[...], nt, preferred_element_type=F32)
    pre = jnp.dot(g_lr.astype(BF16), gup_ref[...], preferred_element_type=F32) + gb_ref[...]
    log_sig = jnp.minimum(pre, 0.0) - jnp.log1p(jnp.exp(-jnp.abs(pre)))
    la_ref[...] = log_sig / GATE_TEMP


def _in_proj(grp, x, mod, norm1, w_in_t, gup, gb):
    const = lambda i: (0, 0)
    u_row0 = QKVR_WIDTH + GATE_RANK
    row = lambda i: (i, 0)
    return pl.pallas_call(
        _in_proj_kernel,
        name="in_proj",
        grid=(grp.n_tiles,),
        in_specs=[pl.BlockSpec((grp.nb, grp.tt, D_MODEL), grp.x_map()),
                  pl.BlockSpec((grp.nb, 6, D_MODEL), grp.mod_map()),
                  pl.BlockSpec((1, 1, D_MODEL), lambda i: (0, 0, 0)),
                  pl.BlockSpec((QKVR_WIDTH, D_MODEL), const, pipeline_mode=pl.Buffered(1)),
                  pl.BlockSpec((GATE_RANK, D_MODEL), lambda i: (QKVR_WIDTH // GATE_RANK, 0),
                               pipeline_mode=pl.Buffered(1)),
                  pl.BlockSpec((pl.Element(POOL_WIDTH), pl.Element(D_MODEL)), lambda i: (u_row0, 0),
                               pipeline_mode=pl.Buffered(1)),
                  pl.BlockSpec(gup.shape, const, pipeline_mode=pl.Buffered(1)),
                  pl.BlockSpec(gb.shape, const, pipeline_mode=pl.Buffered(1))],
        out_specs=[pl.BlockSpec((TM, QKVR_WIDTH), row),
                   pl.BlockSpec((TM, GLA_KEY_WIDTH), row),
                   pl.BlockSpec((TM, POOL_WIDTH), row)],
        out_shape=[jax.ShapeDtypeStruct((grp.rows, QKVR_WIDTH), F32),
                   jax.ShapeDtypeStruct((grp.rows, GLA_KEY_WIDTH), F32),
                   jax.ShapeDtypeStruct((grp.rows, POOL_WIDTH), F32)],
        compiler_params=_cparams(1),
    )(x, mod, norm1.reshape(1, 1, D_MODEL), w_in_t, w_in_t, w_in_t, gup, gb)


def _gla_select_matrix(chunk, sub):
    r = jnp.arange(sub * GLA_DK, dtype=jnp.int32)[:, None] // GLA_DK
    l = jnp.arange(LANES, dtype=jnp.int32)[None, :]
    return ((l % sub == r) & (l < chunk)).astype(BF16)


def _gla_chunks(chunks, states, chunk, sub, wsel, t_refs, kb_refs, chained=True):
    n_sub = chunk // sub
    rows = lax.broadcasted_iota(jnp.int32, (chunk, chunk), 0)
    cols = lax.broadcasted_iota(jnp.int32, (chunk, chunk), 1)
    tri = (rows >= cols).astype(BF16)
    nt = (((1,), (1,)), ((), ()))
    tn = (((0,), (0,)), ((), ()))
    key_row = lax.broadcasted_iota(jnp.int32, (chunk, LANES), 0)
    lane = lax.broadcasted_iota(jnp.int32, (sub, chunk), 1)
    row = lax.broadcasted_iota(jnp.int32, (sub, chunk), 0)
    head = lambda a, h, w: a[:, h * w:(h + 1) * w]
    n = len(chunks)

    b4s = [_dot_exact_lhs(tri, g4) for (_, _, _, g4) in chunks]
    q4s = [q4 * (GLA_DK ** -0.5) for (q4, _, _, _) in chunks]

    for c in range(n):
        kb_ref, t_ref, k4 = kb_refs[c], t_refs[c], chunks[c][1]
        for h in range(GLA_HEADS):
            kb_ref[h] = head(k4, h, GLA_DK)
            kb_ref[GLA_HEADS + h] = head(b4s[c], h, GLA_DK) * LOG2E
        for h in range(GLA_HEADS):
            for s in range(n_sub):
                lo = s * sub
                r0 = (h * n_sub + s) * sub
                q_s, b_s = head(q4s[c], h, GLA_DK)[lo:lo + sub, :], kb_ref[GLA_HEADS + h, lo:lo + sub, :]
                for jl in range(sub):
                    k_j = jnp.broadcast_to(kb_ref[h, lo + jl:lo + jl + 1, :], (sub, GLA_DK))
                    b_j = jnp.broadcast_to(kb_ref[GLA_HEADS + h, lo + jl:lo + jl + 1, :], (sub, GLA_DK))
                    decay = jnp.exp2(jnp.minimum(b_s - b_j, 0.0))
                    t_ref[r0:r0 + sub, jl * GLA_DK:(jl + 1) * GLA_DK] = (q_s * k_j * decay).astype(t_ref.dtype)
    p_diags = [jnp.dot(t_refs[c][...].astype(BF16), wsel, preferred_element_type=F32) for c in range(n)]

    intra = []
    for c in range(n):
        per_head = []
        for h in range(GLA_HEADS):
            q, k, b = head(q4s[c], h, GLA_DK), head(chunks[c][1], h, GLA_DK), head(b4s[c], h, GLA_DK)
            p_blocks = []
            for s in range(n_sub):
                lo = s * sub
                r0 = (h * n_sub + s) * sub
                in_block = (lane >= lo) & (lane - lo <= row)
                p = jnp.where(in_block, p_diags[c][r0:r0 + sub, :chunk], 0.0)
                if s > 0:
                    ref_row = b[lo - 1:lo, :]
                    q_rel = q[lo:lo + sub, :] * jnp.exp(b[lo:lo + sub, :] - ref_row)
                    k_rel = k * jnp.exp(jnp.where(key_row < lo, ref_row - b, NEG))
                    p = p + lax.dot_general(q_rel.astype(BF16), k_rel.astype(BF16), nt,
                                            preferred_element_type=F32)
                p_blocks.append(p)
            p_full = p_blocks[0] if n_sub == 1 else jnp.concatenate(p_blocks, axis=0)
            per_head.append(_bdot(p_full, head(chunks[c][2], h, GLA_DV)))
        intra.append(per_head)

    outs, end_states = [], []
    for c in range(n):
        cur = states if chained else states[c]
        o_heads, new_states = [], []
        for h in range(GLA_HEADS):
            q, k, b = head(q4s[c], h, GLA_DK), head(chunks[c][1], h, GLA_DK), head(b4s[c], h, GLA_DK)
            v = head(chunks[c][2], h, GLA_DV)
            o_heads.append(intra[c][h] + _bdot(q * jnp.exp(b), cur[h]))
            b_last = b[chunk - 1:chunk, :]
            k_dec = k * jnp.exp(b_last - b)
            decay_col = jnp.exp(b[chunk - 8:chunk, :]).T[:, 7:8]
            new_states.append(decay_col * cur[h] + lax.dot_general(
                k_dec.astype(BF16), v.astype(BF16), tn, preferred_element_type=F32))
        states = new_states if chained else states
        end_states.append(new_states)
        outs.append(jnp.concatenate(o_heads, axis=1))
    return outs, (end_states[-1] if chained else end_states)


def _gla_prompt_kernel(q_ref, k_ref, v_ref, la_ref, wsel_ref, o_ref, s_ref, t_ref, kb_ref):
    @pl.when(pl.program_id(1) == 0)
    def _():
        s_ref[...] = jnp.zeros_like(s_ref)

    def body(trip, carry):
        states = [s_ref[0, h] for h in range(GLA_HEADS)]
        slices = [pl.ds(pl.multiple_of((GLA_TRIP * trip + c) * GLA_CHUNK, GLA_CHUNK), GLA_CHUNK)
                  for c in range(GLA_TRIP)]
        chunks = [(q_ref[sl, :], k_ref[sl, :], v_ref[sl, :], la_ref[sl, :]) for sl in slices]
        outs, states = _gla_chunks(chunks, states, GLA_CHUNK, GLA_SUB, wsel_ref[...],
                                   [t_ref.at[c] for c in range(GLA_TRIP)],
                                   [kb_ref.at[c] for c in range(GLA_TRIP)])
        for sl, o in zip(slices, outs):
            o_ref[sl, :] = o
        for h in range(GLA_HEADS):
            s_ref[0, h] = states[h]
        return carry

    lax.fori_loop(0, GLA_STEP // (GLA_TRIP * GLA_CHUNK), body, 0)


def _gla_prompt(batch, seq, qkvr, log_a):
    steps = seq // GLA_STEP
    row = lambda b, s: b * steps + s
    wsel = _gla_select_matrix(GLA_CHUNK, GLA_SUB)
    return pl.pallas_call(
        _gla_prompt_kernel,
        name="gla_prompt",
        grid=(batch, steps),
        in_specs=[pl.BlockSpec((GLA_STEP, GLA_KEY_WIDTH), lambda b, s: (row(b, s), 0)),
                  pl.BlockSpec((GLA_STEP, GLA_KEY_WIDTH), lambda b, s: (row(b, s), 1)),
                  pl.BlockSpec((GLA_STEP, GLA_WIDTH), lambda b, s: (row(b, s), 1)),
                  pl.BlockSpec((GLA_STEP, GLA_KEY_WIDTH), lambda b, s: (row(b, s), 0)),
                  pl.BlockSpec(wsel.shape, lambda b, s: (0, 0))],
        out_specs=[pl.BlockSpec((GLA_STEP, GLA_WIDTH), lambda b, s: (row(b, s), 0)),
                   pl.BlockSpec((1, GLA_HEADS, GLA_DK, GLA_DV), lambda b, s: (b, 0, 0, 0))],
        out_shape=[jax.ShapeDtypeStruct((batch * seq, GLA_WIDTH), F32),
                   jax.ShapeDtypeStruct((batch, GLA_HEADS, GLA_DK, GLA_DV), F32)],
        scratch_shapes=[pltpu.VMEM((GLA_TRIP, GLA_HEADS * GLA_CHUNK, GLA_SUB * GLA_DK), BF16),
                        pltpu.VMEM((GLA_TRIP, 2 * GLA_HEADS, GLA_CHUNK, GLA_DK), F32)],
        compiler_params=_cparams(2),
    )(qkvr, qkvr, qkvr, log_a, wsel)


GLA_DEC_BB = 8
GLA_DEC_TRIP = 4


def _gla_decode_kernel(seq, q_ref, k_ref, v_ref, la_ref, wsel_ref, s0_ref, o_ref, s_ref, t_ref, kb_ref):
    def body(trip, carry):
        elems = [GLA_DEC_TRIP * trip + c for c in range(GLA_DEC_TRIP)]
        slices = [pl.ds(pl.multiple_of(e * seq, seq), seq) for e in elems]
        chunks = [(q_ref[sl, :], k_ref[sl, :], v_ref[sl, :], la_ref[sl, :]) for sl in slices]
        states = [[s0_ref[e, h] for h in range(GLA_HEADS)] for e in elems]
        outs, new_states = _gla_chunks(chunks, states, seq, seq, wsel_ref[...],
                                       [t_ref.at[c] for c in range(GLA_DEC_TRIP)],
                                       [kb_ref.at[c] for c in range(GLA_DEC_TRIP)], chained=False)
        for e, sl, o, ns in zip(elems, slices, outs, new_states):
            o_ref[sl, :] = o
            for h in range(GLA_HEADS):
                s_ref[e, h] = ns[h]
        return carry

    lax.fori_loop(0, GLA_DEC_BB // GLA_DEC_TRIP, body, 0)


def _gla_decode(batch, seq, qkvr, log_a, state):
    rows = GLA_DEC_BB * seq
    wsel = _gla_select_matrix(seq, seq)
    state_spec = pl.BlockSpec((GLA_DEC_BB, GLA_HEADS, GLA_DK, GLA_DV), lambda i: (i, 0, 0, 0))
    return pl.pallas_call(
        functools.partial(_gla_decode_kernel, seq),
        name="gla_decode",
        grid=(batch // GLA_DEC_BB,),
        in_specs=[pl.BlockSpec((rows, GLA_KEY_WIDTH), lambda i: (i, 0)),
                  pl.BlockSpec((rows, GLA_KEY_WIDTH), lambda i: (i, 1)),
                  pl.BlockSpec((rows, GLA_WIDTH), lambda i: (i, 1)),
                  pl.BlockSpec((rows, GLA_KEY_WIDTH), lambda i: (i, 0)),
                  pl.BlockSpec(wsel.shape, lambda i: (0, 0)),
                  state_spec],
        out_specs=[pl.BlockSpec((rows, GLA_WIDTH), lambda i: (i, 0)), state_spec],
        out_shape=[jax.ShapeDtypeStruct((batch * seq, GLA_WIDTH), F32),
                   jax.ShapeDtypeStruct((batch, GLA_HEADS, GLA_DK, GLA_DV), F32)],
        scratch_shapes=[pltpu.VMEM((GLA_DEC_TRIP, GLA_HEADS * seq, seq * GLA_DK), F32),
                        pltpu.VMEM((GLA_DEC_TRIP, 2 * GLA_HEADS, seq, GLA_DK), F32)],
        compiler_params=_cparams(1),
    )(qkvr, qkvr, qkvr, log_a, wsel, state)


def _route(logits, cnt_ref):
    lt = logits.T
    n = lt.shape[1]
    big = jnp.int32(10 ** 6)
    row8 = lax.broadcasted_iota(jnp.int32, (EXPERTS_PER_GROUP, n), 0)
    lg = jnp.where(row8 < N_GROUPS, lt[0:EXPERTS_PER_GROUP], NEG)
    mg = jnp.max(lg, axis=0, keepdims=True)
    g_idx = jnp.min(jnp.where(lg == mg, row8, big), axis=0, keepdims=True)
    p_sel = 1.0 / jnp.sum(jnp.exp(lg - mg), axis=0, keepdims=True)
    le = jnp.zeros((EXPERTS_PER_GROUP, n), F32)
    for g in range(N_GROUPS):
        lo = EXPERTS_PER_GROUP * (g + 1)
        le = jnp.where(g_idx == g, lt[lo:lo + EXPERTS_PER_GROUP], le)
    m1 = jnp.max(le, axis=0, keepdims=True)
    i1 = jnp.min(jnp.where(le == m1, row8, big), axis=0, keepdims=True)
    rest = row8 != i1
    m2 = jnp.max(jnp.where(rest, le, NEG), axis=0, keepdims=True)
    i2 = jnp.min(jnp.where(rest & (le == m2), row8, big), axis=0, keepdims=True)
    e2 = jnp.exp(m2 - m1)
    w1 = p_sel / (1.0 + e2)
    w2 = p_sel * e2 / (1.0 + e2)
    ex1_i = g_idx * EXPERTS_PER_GROUP + i1
    ex2_i = g_idx * EXPERTS_PER_GROUP + i2
    ex1, ex2 = ex1_i.astype(F32), ex2_i.astype(F32)

    e_row = lax.broadcasted_iota(jnp.int32, (N_EXPERTS, n), 0)
    hit1, hit2 = e_row == ex1_i, e_row == ex2_i
    member = (hit1 | hit2).astype(BF16)
    before = (lax.broadcasted_iota(jnp.int32, (n, n), 0) < lax.broadcasted_iota(jnp.int32, (n, n), 1))
    prior = jnp.dot(member, before.astype(BF16), preferred_element_type=F32) + cnt_ref[:, 0:1]
    rank1 = jnp.sum(jnp.where(hit1, prior, 0.0), axis=0, keepdims=True)
    rank2 = jnp.sum(jnp.where(hit2, prior, 0.0), axis=0, keepdims=True)
    cnt_ref[...] = cnt_ref[...] + jnp.sum(member.astype(F32), axis=1, keepdims=True)

    packed = jnp.where(row8 == 0, ex1, jnp.where(row8 == 1, ex2,
                       jnp.where(row8 == 2, w1, jnp.where(row8 == 3, w2,
                                 jnp.where(row8 == 4, rank1, jnp.where(row8 == 5, rank2, 0.0))))))
    full = jnp.concatenate([packed, jnp.zeros((LANES - EXPERTS_PER_GROUP, n), F32)], axis=0)
    return full.T, packed


def _mix_out_kernel(grp, pos0, zero_first_halo, n_alias,
                    o_ref, r_ref, u_ref, halo_ref, x_ref, mod_ref, n2_ref, gn_ref, pw_ref, ps_ref,
                    wo_ref, wr_ref, br_ref, cnt0_ref, *rest):
    x1_ref, h2_ref, rt_ref, rtt_ref, cnt_ref, ext_ref, lvl_a, lvl_b, ymix_ref = rest[n_alias:]
    i = pl.program_id(0)

    @pl.when(i == 0)
    def _():
        cnt_ref[...] = cnt0_ref[...]

    @pl.when(i < grp.n_tiles)
    def _():
        _mix_out_tile(grp, pos0, zero_first_halo, i, o_ref, r_ref, u_ref, halo_ref, x_ref, mod_ref, n2_ref,
                      gn_ref, pw_ref, ps_ref, wo_ref, wr_ref, br_ref, x1_ref, h2_ref, rt_ref, rtt_ref, cnt_ref,
                      ext_ref, (lvl_a, lvl_b), ymix_ref)

    @pl.when(i >= grp.n_tiles)
    def _():
        h2_ref[...] = jnp.zeros_like(h2_ref)
        rt_ref[...] = jnp.zeros_like(rt_ref)
        rtt_ref[...] = jnp.zeros_like(rtt_ref)


def _mix_out_tile(grp, pos0, zero_first_halo, i, o_ref, r_ref, u_ref, halo_ref, x_ref, mod_ref, n2_ref,
                  gn_ref, pw_ref, ps_ref, wo_ref, wr_ref, br_ref, x1_ref, h2_ref, rt_ref, rtt_ref, cnt_ref,
                  ext_ref, lvl_refs, ymix_ref):
    nb, tt = grp.nb, grp.tt
    hist = halo_ref.shape[-2]

    for h in range(GLA_HEADS):
        cs = slice(h * GLA_DV, (h + 1) * GLA_DV)
        oh = o_ref[:, cs]
        ms = jnp.mean(oh * oh, axis=-1, keepdims=True)
        yh = oh * lax.rsqrt(ms + EPS) * gn_ref[:, cs] * _silu(r_ref[:, cs])
        ymix_ref[:, cs] = yh.astype(BF16)

    halo = halo_ref[...]
    if zero_first_halo:
        halo = jnp.where(i % grp.tiles_per_batch == 0, 0.0, halo)
    n_ext = EXT_TOK0 + tt
    ext_ref[:, 0:EXT_TOK0 - hist, :] = jnp.zeros((nb, EXT_TOK0 - hist, POOL_WIDTH), F32)
    ext_ref[:, EXT_TOK0 - hist:EXT_TOK0, :] = halo.reshape(nb, hist, POOL_WIDTH)
    u = u_ref[...].reshape(nb, tt, POOL_WIDTH)
    ext_ref[:, EXT_TOK0:n_ext, :] = u
    for lvl_ref in lvl_refs:
        lvl_ref[:, 0:EXT_PAD, :] = jnp.zeros((nb, EXT_PAD, POOL_GW), F32)
    t_idx = lax.broadcasted_iota(jnp.int32, (nb, tt, POOL_GW), 1)
    if grp.nb == 1:
        pos = (i % grp.tiles_per_batch) * TM + t_idx + pos0
    else:
        pos = t_idx + pos0
    y_gla = []
    part = D_MODEL // len(POOL_WINDOWS)
    for gi, w in enumerate(POOL_WINDOWS):
        y_gla.append(jnp.dot(ymix_ref[:, :GLA_WIDTH], wo_ref[:GLA_WIDTH, gi * part:(gi + 1) * part],
                             preferred_element_type=F32))
        cs = slice(gi * POOL_GW, (gi + 1) * POOL_GW)
        cur = lambda lo, hi: ext_ref[:, lo:hi, cs]
        d, level = 1, 0
        while d < w:
            nxt = lvl_refs[level % 2]
            nxt[:, EXT_PAD:n_ext, :] = cur(EXT_PAD, n_ext) + cur(EXT_PAD - d, n_ext - d)
            cur = lambda lo, hi, ref=nxt: ref[:, lo:hi, :]
            d, level = 2 * d, level + 1
        acc = cur(EXT_TOK0, n_ext)
        cnt = jnp.minimum(pos + 1, w).astype(F32)
        pooled = acc / cnt - u[:, :, cs]
        yp = _bdot(pooled.reshape(TM, POOL_GW), pw_ref[gi]) * ps_ref[:, cs]
        ymix_ref[:, GLA_WIDTH + gi * POOL_GW:GLA_WIDTH + (gi + 1) * POOL_GW] = yp.astype(BF16)

    y = jnp.concatenate(y_gla, axis=1) + jnp.dot(ymix_ref[:, GLA_WIDTH:], wo_ref[GLA_WIDTH:, :],
                                                 preferred_element_type=F32)
    x1 = x_ref[...] + _mod_rows(mod_ref, 2) * y.reshape(nb, tt, D_MODEL)
    x1_ref[...] = x1
    h2 = _rmsnorm_mod(x1, n2_ref[...], _mod_rows(mod_ref, 4), _mod_rows(mod_ref, 3)).reshape(TM, D_MODEL)
    h2_ref[...] = h2
    logits = _dot_3pass(h2, wr_ref[0], wr_ref[1]) + br_ref[...]
    rt_ref[...], rtt_ref[...] = _route(logits, cnt_ref)


def _mix_out(grp, n_tok, pos0, zero_first_halo, o, qkvr, u, halo_src, halo_block, halo_map, x, mod, norm2,
             gla_norm, pool_w, pool_scale, w_out, w_router, b_router, counts, shared=()):
    n_alias = len(shared)
    n = grp.n_tiles
    n_fill = 0 if shared else n_tok // TM - n
    assert n_fill == 0 or grp.tile_off == 0
    clamp = lambda f: (lambda i: f(jnp.minimum(i, n - 1)))
    const2 = lambda i: (0, 0)
    row = clamp(lambda i: (i, 0))
    off = grp.tile_off
    kern = functools.partial(_mix_out_kernel, grp, pos0, zero_first_halo, n_alias)
    return pl.pallas_call(
        kern,
        name="mix_out",
        grid=(n + n_fill,),
        in_specs=[pl.BlockSpec((TM, GLA_WIDTH), row),
                  pl.BlockSpec((TM, GLA_WIDTH), clamp(lambda i: (i, 2))),
                  pl.BlockSpec((TM, POOL_WIDTH), row),
                  pl.BlockSpec(halo_block, clamp(halo_map)),
                  pl.BlockSpec((grp.nb, grp.tt, D_MODEL), clamp(grp.x_map())),
                  pl.BlockSpec((grp.nb, 6, D_MODEL), clamp(grp.mod_map())),
                  pl.BlockSpec((1, 1, D_MODEL), lambda i: (0, 0, 0)),
                  pl.BlockSpec((1, GLA_WIDTH), const2),
                  pl.BlockSpec(pool_w.shape, lambda i: (0, 0, 0), pipeline_mode=pl.Buffered(1)),
                  pl.BlockSpec((1, POOL_WIDTH), const2),
                  pl.BlockSpec(w_out.shape, const2, pipeline_mode=pl.Buffered(1)),
                  pl.BlockSpec(w_router.shape, lambda i: (0, 0, 0), pipeline_mode=pl.Buffered(1)),
                  pl.BlockSpec((1, LANES), const2),
                  pl.BlockSpec((N_EXPERTS, LANES), const2)]
                 + [pl.BlockSpec(memory_space=pl.ANY)] * n_alias,
        out_specs=[pl.BlockSpec((grp.nb, grp.tt, D_MODEL), clamp(grp.x_map())),
                   pl.BlockSpec((TM, D_MODEL), lambda i: (i + off, 0)),
                   pl.BlockSpec((TM, LANES), lambda i: (i + off, 0)),
                   pl.BlockSpec((EXPERTS_PER_GROUP, TM), lambda i: (0, i + off)),
                   pl.BlockSpec((N_EXPERTS, LANES), const2)],
        out_shape=[jax.ShapeDtypeStruct(x.shape, F32),
                   jax.ShapeDtypeStruct((n_tok, D_MODEL), F32),
                   jax.ShapeDtypeStruct((n_tok, LANES), F32),
                   jax.ShapeDtypeStruct((EXPERTS_PER_GROUP, n_tok), F32),
                   jax.ShapeDtypeStruct((N_EXPERTS, LANES), F32)],
        scratch_shapes=[pltpu.VMEM((grp.nb, EXT_TOK0 + grp.tt, POOL_WIDTH), F32),
                        pltpu.VMEM((grp.nb, EXT_TOK0 + grp.tt, POOL_GW), F32),
                        pltpu.VMEM((grp.nb, EXT_TOK0 + grp.tt, POOL_GW), F32),
                        pltpu.VMEM((TM, D_MODEL), BF16)],
        input_output_aliases={14 + k: 1 + k for k in range(n_alias)},
        compiler_params=_cparams(1),
    )(o, qkvr, u, halo_src, x, mod, norm2.reshape(1, 1, D_MODEL), gla_norm.reshape(1, GLA_WIDTH),
      pool_w, pool_scale.reshape(1, POOL_WIDTH), w_out, w_router, b_router, counts, *shared)


def _row_copy(src_hbm, src_row, dst, dst_row, sem):
    return pltpu.make_async_copy(src_hbm.at[pl.ds(src_row, 1), :], dst.at[pl.ds(dst_row, 1), :], sem)


def _tile_wait(src_hbm, dst, sem):
    pltpu.make_async_copy(src_hbm.at[pl.ds(0, dst.shape[0]), :], dst, sem).wait()


MOE_AHEAD = 2


def _moe_gather_kernel(te_ref, src_ref, nu_ref, seg_ref, nxt_ref, h_hbm, w1_hbm, w3_hbm, w2_hbm, y_ref,
                       buf0, buf1, buf2, sem, wf1, wf3, wf2, wsem, w1b, w3b, w2b):
    i = pl.program_id(0)
    n_used = nu_ref[0]
    bufs = (buf0, buf1, buf2)
    n_slots = MOE_AHEAD + 1
    n_parts = 4

    def issue(tile, slot, part):
        base = tile * TME
        per = TME // n_parts
        for r in range(part * per, (part + 1) * per):
            _row_copy(h_hbm, src_ref[base + r], bufs[slot], r, sem.at[slot]).start(priority=r % 2)

    def weight_copies(expert, wslot):
        return (pltpu.make_async_copy(w1_hbm.at[expert], wf1.at[wslot], wsem.at[wslot]),
                pltpu.make_async_copy(w3_hbm.at[expert], wf3.at[wslot], wsem.at[wslot]),
                pltpu.make_async_copy(w2_hbm.at[expert], wf2.at[wslot], wsem.at[wslot]))

    def compute(slot, ahead):
        nxt = (lambda part: issue(i + MOE_AHEAD, (slot + MOE_AHEAD) % n_slots, part)) if ahead else (lambda part: None)
        _tile_wait(h_hbm, bufs[slot], sem.at[slot])
        nxt(0)
        x = bufs[slot][...].astype(BF16)
        nxt(1)
        a = jnp.dot(x, w1b[...], preferred_element_type=F32)
        nxt(2)
        b = jnp.dot(x, w3b[...], preferred_element_type=F32)
        nxt(3)
        hid = _silu(a) * b
        y_ref[...] = jnp.dot(hid.astype(BF16), w2b[...], preferred_element_type=F32)

    @pl.when(i == 0)
    def _():
        for c in weight_copies(te_ref[0], 0):
            c.start()
        for t in range(MOE_AHEAD):
            @pl.when(t < n_used)
            def _():
                for part in range(n_parts):
                    issue(t, t, part)

    prev = jnp.maximum(i - 1, 0)

    @pl.when((i < n_used) & ((i == 0) | (te_ref[i] != te_ref[prev])))
    def _():
        wslot = seg_ref[i] % 2
        for c in weight_copies(te_ref[i], wslot):
            c.wait()
        w1b[...] = wf1[wslot].astype(BF16)
        w3b[...] = wf3[wslot].astype(BF16)
        w2b[...] = wf2[wslot].astype(BF16)

        @pl.when(nxt_ref[i] >= 0)
        def _():
            for c in weight_copies(nxt_ref[i], 1 - wslot):
                c.start()

    for slot in range(n_slots):
        @pl.when((i + MOE_AHEAD < n_used) & (i % n_slots == slot))
        def _():
            compute(slot, True)

        @pl.when((i < n_used) & (i + MOE_AHEAD >= n_used) & (i % n_slots == slot))
        def _():
            compute(slot, False)

    @pl.when(i >= n_used)
    def _():
        y_ref[...] = jnp.zeros_like(y_ref)


def _moe_gather(plan, h2_all, w1, w3, w2):
    n_sorted = plan["src_row"].shape[0]
    grid_spec = pltpu.PrefetchScalarGridSpec(
        num_scalar_prefetch=5,
        grid=(n_sorted // TME,),
        in_specs=[pl.BlockSpec(memory_space=pl.ANY)] * 4,
        out_specs=pl.BlockSpec((TME, D_MODEL), lambda i, *_: (i, 0)),
        scratch_shapes=[pltpu.VMEM((TME, D_MODEL), F32)] * (MOE_AHEAD + 1)
                       + [pltpu.SemaphoreType.DMA((MOE_AHEAD + 1,)),
                          pltpu.VMEM((2, D_MODEL, EXPERT_FF), F32), pltpu.VMEM((2, D_MODEL, EXPERT_FF), F32),
                          pltpu.VMEM((2, EXPERT_FF, D_MODEL), F32),
                          pltpu.SemaphoreType.DMA((2,)),
                          pltpu.VMEM((D_MODEL, EXPERT_FF), BF16), pltpu.VMEM((D_MODEL, EXPERT_FF), BF16),
                          pltpu.VMEM((EXPERT_FF, D_MODEL), BF16)],
    )
    return pl.pallas_call(
        _moe_gather_kernel,
        name="moe",
        grid_spec=grid_spec,
        out_shape=jax.ShapeDtypeStruct((n_sorted, D_MODEL), F32),
        compiler_params=_cparams(1),
    )(plan["tile_expert"], plan["src_row"], plan["n_used"], plan["segment"], plan["next_expert"],
      h2_all, w1, w3, w2)


def _finish_kernel(grp, pos_ref, x1_ref, mod_ref, rt_ref, nf_ref, y_hbm, out_ref, *scratch):
    n_slots = FINISH_AHEAD + 1
    bufs_a, bufs_b, sem = scratch[:n_slots], scratch[n_slots:2 * n_slots], scratch[2 * n_slots]
    i = pl.program_id(0)
    n_steps = pl.num_programs(0)
    n_parts = 4
    n_tok = pos_ref.shape[0] // 2

    def issue(tile, slot, part):
        base = tile * TM + grp.row_off
        per = TM // n_parts
        for r in range(part * per, (part + 1) * per):
            _row_copy(y_hbm, pos_ref[base + r], bufs_a[slot], r, sem.at[slot]).start(priority=0)
            _row_copy(y_hbm, pos_ref[n_tok + base + r], bufs_b[slot], r, sem.at[slot]).start(priority=1)

    def compute(slot, ahead):
        nxt = ((lambda part: issue(i + FINISH_AHEAD, (slot + FINISH_AHEAD) % n_slots, part)) if ahead
               else (lambda part: None))
        _tile_wait(y_hbm, bufs_a[slot], sem.at[slot])
        _tile_wait(y_hbm, bufs_b[slot], sem.at[slot])
        nxt(0)
        rt = rt_ref[...]
        moe = rt[:, 2:3] * bufs_a[slot][...] + rt[:, 3:4] * bufs_b[slot][...]
        nxt(1)
        x2 = x1_ref[...] + _mod_rows(mod_ref, 5) * moe.reshape(grp.nb, grp.tt, D_MODEL)
        nxt(2)
        ms = jnp.mean(x2 * x2, axis=-1, keepdims=True)
        nxt(3)
        out_ref[...] = x2 * lax.rsqrt(ms + EPS) * nf_ref[...]

    @pl.when(i == 0)
    def _():
        for t in range(FINISH_AHEAD):
            @pl.when(t < n_steps)
            def _():
                for part in range(n_parts):
                    issue(t, t, part)

    for slot in range(n_slots):
        @pl.when((i + FINISH_AHEAD < n_steps) & (i % n_slots == slot))
        def _():
            compute(slot, True)

        @pl.when((i + FINISH_AHEAD >= n_steps) & (i % n_slots == slot))
        def _():
            compute(slot, False)


def _finish(grp, pos, x1, mod, route_all, norm_f, y_sorted):
    off = grp.tile_off
    grid_spec = pltpu.PrefetchScalarGridSpec(
        num_scalar_prefetch=1,
        grid=(grp.n_tiles,),
        in_specs=[pl.BlockSpec((grp.nb, grp.tt, D_MODEL), grp.x_map()),
                  pl.BlockSpec((grp.nb, 6, D_MODEL), grp.mod_map()),
                  pl.BlockSpec((TM, LANES), lambda i, p: (i + off, 0)),
                  pl.BlockSpec((1, 1, D_MODEL), lambda i, p: (0, 0, 0)),
                  pl.BlockSpec(memory_space=pl.ANY)],
        out_specs=pl.BlockSpec((grp.nb, grp.tt, D_MODEL), grp.x_map()),
        scratch_shapes=[pltpu.VMEM((TM, D_MODEL), F32)] * (2 * (FINISH_AHEAD + 1))
                       + [pltpu.SemaphoreType.DMA((FINISH_AHEAD + 1,))],
    )
    return pl.pallas_call(
        functools.partial(_finish_kernel, grp),
        name="finish",
        grid_spec=grid_spec,
        out_shape=jax.ShapeDtypeStruct(x1.shape, F32),
        compiler_params=_cparams(1),
    )(pos, x1, mod, route_all, norm_f.reshape(1, 1, D_MODEL), y_sorted)


def _slot_kernel(rt_ref, cnt_ref, pos_ref):
    expert = rt_ref[0:2, :].astype(jnp.int32)
    rank = rt_ref[4:6, :].astype(jnp.int32)
    padded = jnp.ceil(cnt_ref[...] / TME) * TME
    r = lax.broadcasted_iota(jnp.int32, (N_EXPERTS, N_EXPERTS), 0)
    c = lax.broadcasted_iota(jnp.int32, (N_EXPERTS, N_EXPERTS), 1)
    starts = _dot_exact_lhs((c < r).astype(BF16), padded)
    start_of = jnp.zeros(expert.shape, F32)
    for e in range(N_EXPERTS):
        start_of = jnp.where(expert == e, starts[e:e + 1, 0:1], start_of)
    pos_ref[...] = start_of.astype(jnp.int32) + rank


def _slots(route_t, counts):
    n_tok = route_t.shape[1]
    return pl.pallas_call(
        _slot_kernel,
        name="slots",
        out_shape=jax.ShapeDtypeStruct((2, n_tok), jnp.int32),
        compiler_params=pltpu.CompilerParams(vmem_limit_bytes=VMEM_LIMIT),
    )(route_t, counts)


def _sort_plan(route_t, counts):
    n_tok = route_t.shape[1]
    n_pairs = 2 * n_tok
    n_sorted = n_pairs + N_EXPERTS * TME
    pos = _slots(route_t, counts).reshape(n_pairs)
    counts = counts[:, 0].astype(jnp.int32)
    padded = ((counts + TME - 1) // TME) * TME
    ends = jnp.cumsum(padded)
    token = jnp.tile(jnp.arange(n_tok, dtype=jnp.int32), 2)
    src_row = (jnp.arange(n_sorted, dtype=jnp.int32) % n_tok).at[pos].set(
        token, unique_indices=True, mode="promise_in_bounds")
    tile_start = jnp.arange(n_sorted // TME, dtype=jnp.int32) * TME
    tile_expert = jnp.sum((tile_start[:, None] >= ends[None, :]).astype(jnp.int32), axis=1)
    tile_expert = jnp.minimum(tile_expert, N_EXPERTS - 1)
    n_used = ends[-1] // TME
    is_first = jnp.concatenate([jnp.ones((1,), jnp.int32),
                                (tile_expert[1:] != tile_expert[:-1]).astype(jnp.int32)])
    segment = jnp.cumsum(is_first) - 1
    next_tile = ends[tile_expert] // TME
    next_expert = jnp.where(next_tile < n_used, tile_expert[jnp.minimum(next_tile, n_sorted // TME - 1)], -1)
    return dict(pos=pos.astype(jnp.int32), src_row=src_row, tile_expert=tile_expert.astype(jnp.int32),
                n_used=n_used.astype(jnp.int32).reshape(1), segment=segment.astype(jnp.int32),
                next_expert=next_expert.astype(jnp.int32))


def kernel(x_prompt, x_sample, c_prompt, c_sample, state_gla, state_pool, w_ada, b_ada, norm1, norm2, w_in,
           gate_up, gate_bias, gla_norm, pool_w, pool_scale, w_out, w_group, b_group, w_expert, b_expert,
           w1, w3, w2, norm_f):
    assert w_ada.shape[0] == 1, "single-layer step"
    bp, tp, _ = x_prompt.shape
    bs, ts, _ = x_sample.shape
    grp_p = _Group(bp, tp, 0, bs)
    grp_s = _Group(bs, ts, bp * tp, 0)
    n_tok = bp * tp + bs * ts

    n_c = bp + bs
    n_c_pad = -(-n_c // 8) * 8
    c_all = jnp.concatenate([c_sample, c_prompt, jnp.zeros((n_c_pad - n_c, D_MODEL), F32)], axis=0)
    mod = _adaln(c_all, w_ada[0], b_ada[0]).reshape(n_c_pad, 6, D_MODEL)
    mod_p = mod_s = mod

    w_in_t = jnp.swapaxes(w_in.reshape(w_in.shape[1:]), 0, 1).astype(BF16)
    gup = gate_up[0].astype(BF16)
    gb = gate_bias[0].reshape(1, GLA_KEY_WIDTH)
    pw = pool_w[0].astype(BF16)
    wo = w_out[0].astype(BF16)
    gap = EXPERTS_PER_GROUP - N_GROUPS
    tail = LANES - EXPERTS_PER_GROUP - N_EXPERTS
    w_router = jnp.concatenate(
        [w_group[0], jnp.zeros((D_MODEL, gap), F32),
         jnp.transpose(w_expert[0], (1, 0, 2)).reshape(D_MODEL, N_EXPERTS),
         jnp.zeros((D_MODEL, tail), F32)], axis=1)
    w_router_hi = w_router.astype(BF16)
    w_router = jnp.stack([w_router_hi, (w_router - w_router_hi.astype(F32)).astype(BF16)])
    b_router = jnp.concatenate([b_group[0], jnp.zeros((gap,), F32), b_expert[0].reshape(N_EXPERTS),
                                jnp.zeros((tail,), F32)]).reshape(1, LANES)

    qkvr_p, la_p, u_p = _in_proj(grp_p, x_prompt, mod_p, norm1[0], w_in_t, gup, gb)
    qkvr_s, la_s, u_s = _in_proj(grp_s, x_sample, mod_s, norm1[0], w_in_t, gup, gb)

    o_p, gla_p = _gla_prompt(bp, tp, qkvr_p, la_p)
    o_s, gla_s = _gla_decode(bs, ts, qkvr_s, la_s, state_gla.reshape(state_gla.shape[1:]))

    halo_per_tile = TM // HALO
    halo_map_p = lambda i: (jnp.maximum(i * halo_per_tile - 1, 0), 0)
    counts = jnp.zeros((N_EXPERTS, LANES), F32)
    x1_p, h2_all, route_all, route_t, counts = _mix_out(
        grp_p, n_tok, 0, True, o_p, qkvr_p, u_p, u_p, (HALO, POOL_WIDTH), halo_map_p, x_prompt, mod_p,
        norm2[0], gla_norm[0], pw, pool_scale[0], wo, w_router, b_router, counts)
    x1_s, h2_all, route_all, route_t, counts = _mix_out(
        grp_s, n_tok, PAST_LEN, False, o_s, qkvr_s, u_s, state_pool.reshape(bs, POOL_BUF, POOL_WIDTH),
        (grp_s.nb, POOL_BUF, POOL_WIDTH), lambda i: (i, 0, 0), x_sample, mod_s, norm2[0], gla_norm[0], pw,
        pool_scale[0], wo, w_router, b_router, counts, shared=(h2_all, route_all, route_t))

    plan = _sort_plan(route_t, counts)
    y_sorted = _moe_gather(plan, h2_all, w1.reshape(w1.shape[1:]), w3.reshape(w3.shape[1:]),
                           w2.reshape(w2.shape[1:]))

    y_p = _finish(grp_p, plan["pos"], x1_p, mod_p, route_all, norm_f, y_sorted)
    y_s = _finish(grp_s, plan["pos"], x1_s, mod_s, route_all, norm_f, y_sorted)

    u_p3 = u_p.reshape(bp, tp, POOL_WIDTH)
    u_s3 = u_s.reshape(bs, ts, POOL_WIDTH)
    assert tp >= POOL_BUF > ts
    pool_p = u_p3[:, tp - POOL_BUF:]
    pool_s = jnp.concatenate([state_pool.reshape(bs, POOL_BUF, POOL_WIDTH)[:, ts:], u_s3], axis=1)
    lead = lambda a: a.reshape((1,) + a.shape)
    return (y_p, y_s, lead(gla_p), lead(pool_p), lead(gla_s), lead(pool_s))
```

```python
import functools

import jax
import jax.numpy as jnp
from jax import lax
from jax.experimental import pallas as pl
from jax.experimental.pallas import tpu as pltpu

D_MODEL = 2048
GLA_HEADS = 4
GLA_DK = 128
GLA_DV = 256
GLA_KEY_WIDTH = GLA_HEADS * GLA_DK
GLA_WIDTH = GLA_HEADS * GLA_DV
POOL_WIDTH = 1024
POOL_WINDOWS = (2, 4, 8, 16)
POOL_GW = 256
POOL_BUF = 15
HALO = 16
EXT_PAD = 8
EXT_TOK0 = EXT_PAD + HALO
GATE_RANK = 16
GATE_TEMP = 16.0
N_GROUPS = 4
EXPERTS_PER_GROUP = 8
N_EXPERTS = 32
EXPERT_FF = 512
EPS = 1e-6
PAST_LEN = 16384
QKVR_WIDTH = 2 * GLA_KEY_WIDTH + 2 * GLA_WIDTH

LANES = 128
ADALN_COLS = 2048
TM = 256
TME = 256
FINISH_AHEAD = 2
GLA_CHUNK = 64
GLA_SUB = 8
GLA_STEP = 1024
GLA_TRIP = 4
VMEM_LIMIT = 56 * 1024 * 1024

BF16 = jnp.bfloat16
F32 = jnp.float32
NEG = -1e30
LOG2E = 1.4426950408889634


def _cparams(n_axes):
    return pltpu.CompilerParams(dimension_semantics=("arbitrary",) * n_axes,
                                vmem_limit_bytes=VMEM_LIMIT)


def _silu(x):
    return x / (1.0 + jnp.exp(-x))


def _bdot(a, b):
    return jnp.dot(a.astype(BF16), b.astype(BF16), preferred_element_type=F32)


def _split3(a):
    a1 = a.astype(BF16)
    r1 = a - a1.astype(F32)
    a2 = r1.astype(BF16)
    a3 = (r1 - a2.astype(F32)).astype(BF16)
    return a1, a2, a3


def _split2(a):
    hi = a.astype(BF16)
    lo = (a - hi.astype(F32)).astype(BF16)
    return hi, lo


def _dot_3pass(a, b_hi, b_lo):
    a_hi, a_lo = _split2(a)
    d = lambda x, y: jnp.dot(x, y, preferred_element_type=F32)
    return d(a_hi, b_hi) + (d(a_hi, b_lo) + d(a_lo, b_hi))


def _dot_exact_lhs(tri_bf16, g):
    g1, g2, g3 = _split3(g)
    d = lambda y: jnp.dot(tri_bf16, y, preferred_element_type=F32)
    return d(g1) + (d(g2) + d(g3))


def _adaln_kernel(c_ref, w_ref, b_ref, o_ref):
    c = c_ref[...]
    o_ref[...] = _bdot(_silu(c), w_ref[...]) + b_ref[...]


def _adaln(c_all, w_ada, b_ada):
    n, d = c_all.shape
    width = w_ada.shape[1]
    tn = ADALN_COLS
    return pl.pallas_call(
        _adaln_kernel,
        name="adaln",
        grid=(width // tn,),
        in_specs=[pl.BlockSpec((n, d), lambda j: (0, 0)),
                  pl.BlockSpec((d, tn), lambda j: (0, j)),
                  pl.BlockSpec((1, tn), lambda j: (0, j))],
        out_specs=pl.BlockSpec((n, tn), lambda j: (0, j)),
        out_shape=jax.ShapeDtypeStruct((n, width), F32),
        compiler_params=_cparams(1),
    )(c_all, w_ada, b_ada.reshape(1, width))


class _Group:
    def __init__(self, batch, seq, row_off, mod_off):
        self.batch, self.seq, self.row_off, self.mod_off = batch, seq, row_off, mod_off
        if seq >= TM:
            assert seq % TM == 0
            self.nb, self.tt = 1, TM
            self.tiles_per_batch = seq // TM
            self.n_tiles = batch * self.tiles_per_batch
        else:
            assert TM % seq == 0 and batch % (TM // seq) == 0
            self.nb, self.tt = TM // seq, seq
            self.tiles_per_batch = 1
            self.n_tiles = batch // self.nb
        self.rows = batch * seq
        self.tile_off = row_off // TM

    def x_map(self):
        if self.nb == 1:
            tpb = self.tiles_per_batch
            return lambda i, *_: (i // tpb, i % tpb, 0)
        return lambda i, *_: (i, 0, 0)

    def mod_map(self):
        assert self.mod_off % self.nb == 0
        off = self.mod_off // self.nb
        if self.nb == 1:
            tpb = self.tiles_per_batch
            return lambda i, *_: (i // tpb + off, 0, 0)
        return lambda i, *_: (i + off, 0, 0)


def _mod_rows(mod_ref, idx):
    return mod_ref[:, idx:idx + 1, :]


def _rmsnorm_mod(x, gain, scale, shift):
    ms = jnp.mean(x * x, axis=-1, keepdims=True)
    y = x * lax.rsqrt(ms + EPS) * gain
    return y * (1.0 + scale) + shift


def _in_proj_kernel(x_ref, mod_ref, n1_ref, wq_ref, wg_ref, wu_ref, gup_ref, gb_ref,
                    qkvr_ref, la_ref, u_ref):
    x = x_ref[...]
    h = _rmsnorm_mod(x, n1_ref[...], _mod_rows(mod_ref, 1), _mod_rows(mod_ref, 0))
    hb = h.reshape(TM, D_MODEL).astype(BF16)
    nt = (((1,), (1,)), ((), ()))
    qkvr_ref[...] = lax.dot_general(hb, wq_ref[...], nt, preferred_element_type=F32)
    u_ref[...] = lax.dot_general(hb, wu_ref[...], nt, preferred_element_type=F32)
    g_lr = lax.dot_general(hb, wg_ref[...], nt, preferred_element_type=F32)
    pre = jnp.dot(g_lr.astype(BF16), gup_ref[...], preferred_element_type=F32) + gb_ref[...]
    log_sig = jnp.minimum(pre, 0.0) - jnp.log1p(jnp.exp(-jnp.abs(pre)))
    la_ref[...] = log_sig / GATE_TEMP


def _in_proj(grp, x, mod, norm1, w_in_t, gup, gb):
    const = lambda i: (0, 0)
    u_row0 = QKVR_WIDTH + GATE_RANK
    row = lambda i: (i, 0)
    return pl.pallas_call(
        _in_proj_kernel,
        name="in_proj",
        grid=(grp.n_tiles,),
        in_specs=[pl.BlockSpec((grp.nb, grp.tt, D_MODEL), grp.x_map()),
                  pl.BlockSpec((grp.nb, 6, D_MODEL), grp.mod_map()),
                  pl.BlockSpec((1, 1, D_MODEL), lambda i: (0, 0, 0)),
                  pl.BlockSpec((QKVR_WIDTH, D_MODEL), const, pipeline_mode=pl.Buffered(1)),
                  pl.BlockSpec((GATE_RANK, D_MODEL), lambda i: (QKVR_WIDTH // GATE_RANK, 0),
                               pipeline_mode=pl.Buffered(1)),
                  pl.BlockSpec((pl.Element(POOL_WIDTH), pl.Element(D_MODEL)), lambda i: (u_row0, 0),
                               pipeline_mode=pl.Buffered(1)),
                  pl.BlockSpec(gup.shape, const, pipeline_mode=pl.Buffered(1)),
                  pl.BlockSpec(gb.shape, const, pipeline_mode=pl.Buffered(1))],
        out_specs=[pl.BlockSpec((TM, QKVR_WIDTH), row),
                   pl.BlockSpec((TM, GLA_KEY_WIDTH), row),
                   pl.BlockSpec((TM, POOL_WIDTH), row)],
        out_shape=[jax.ShapeDtypeStruct((grp.rows, QKVR_WIDTH), F32),
                   jax.ShapeDtypeStruct((grp.rows, GLA_KEY_WIDTH), F32),
                   jax.ShapeDtypeStruct((grp.rows, POOL_WIDTH), F32)],
        compiler_params=_cparams(1),
    )(x, mod, norm1.reshape(1, 1, D_MODEL), w_in_t, w_in_t, w_in_t, gup, gb)


def _gla_select_matrix(chunk, sub):
    r = jnp.arange(sub * GLA_DK, dtype=jnp.int32)[:, None] // GLA_DK
    l = jnp.arange(LANES, dtype=jnp.int32)[None, :]
    return ((l % sub == r) & (l < chunk)).astype(BF16)


def _gla_chunks(chunks, states, chunk, sub, wsel, t_refs, kb_refs, chained=True):
    n_sub = chunk // sub
    rows = lax.broadcasted_iota(jnp.int32, (chunk, chunk), 0)
    cols = lax.broadcasted_iota(jnp.int32, (chunk, chunk), 1)
    tri = (rows >= cols).astype(BF16)
    nt = (((1,), (1,)), ((), ()))
    tn = (((0,), (0,)), ((), ()))
    key_row = lax.broadcasted_iota(jnp.int32, (chunk, LANES), 0)
    lane = lax.broadcasted_iota(jnp.int32, (sub, chunk), 1)
    row = lax.broadcasted_iota(jnp.int32, (sub, chunk), 0)
    head = lambda a, h, w: a[:, h * w:(h + 1) * w]
    n = len(chunks)

    b4s = [_dot_exact_lhs(tri, g4) for (_, _, _, g4) in chunks]
    q4s = [q4 * (GLA_DK ** -0.5) for (q4, _, _, _) in chunks]

    for c in range(n):
        kb_ref, t_ref, k4 = kb_refs[c], t_refs[c], chunks[c][1]
        for h in range(GLA_HEADS):
            kb_ref[h] = head(k4, h, GLA_DK)
            kb_ref[GLA_HEADS + h] = head(b4s[c], h, GLA_DK) * LOG2E
        for h in range(GLA_HEADS):
            for s in range(n_sub):
                lo = s * sub
                r0 = (h * n_sub + s) * sub
                q_s, b_s = head(q4s[c], h, GLA_DK)[lo:lo + sub, :], kb_ref[GLA_HEADS + h, lo:lo + sub, :]
                for jl in range(sub):
                    k_j = jnp.broadcast_to(kb_ref[h, lo + jl:lo + jl + 1, :], (sub, GLA_DK))
                    b_j = jnp.broadcast_to(kb_ref[GLA_HEADS + h, lo + jl:lo + jl + 1, :], (sub, GLA_DK))
                    decay = jnp.exp2(jnp.minimum(b_s - b_j, 0.0))
                    t_ref[r0:r0 + sub, jl * GLA_DK:(jl + 1) * GLA_DK] = (q_s * k_j * decay).astype(t_ref.dtype)
    p_diags = [jnp.dot(t_refs[c][...].astype(BF16), wsel, preferred_element_type=F32) for c in range(n)]

    intra = []
    for c in range(n):
        per_head = []
        for h in range(GLA_HEADS):
            q, k, b = head(q4s[c], h, GLA_DK), head(chunks[c][1], h, GLA_DK), head(b4s[c], h, GLA_DK)
            p_blocks = []
            for s in range(n_sub):
                lo = s * sub
                r0 = (h * n_sub + s) * sub
                in_block = (lane >= lo) & (lane - lo <= row)
                p = jnp.where(in_block, p_diags[c][r0:r0 + sub, :chunk], 0.0)
                if s > 0:
                    ref_row = b[lo - 1:lo, :]
                    q_rel = q[lo:lo + sub, :] * jnp.exp(b[lo:lo + sub, :] - ref_row)
                    k_rel = k * jnp.exp(jnp.where(key_row < lo, ref_row - b, NEG))
                    p = p + lax.dot_general(q_rel.astype(BF16), k_rel.astype(BF16), nt,
                                            preferred_element_type=F32)
                p_blocks.append(p)
            p_full = p_blocks[0] if n_sub == 1 else jnp.concatenate(p_blocks, axis=0)
            per_head.append(_bdot(p_full, head(chunks[c][2], h, GLA_DV)))
        intra.append(per_head)

    outs, end_states = [], []
    for c in range(n):
        cur = states if chained else states[c]
        o_heads, new_states = [], []
        for h in range(GLA_HEADS):
            q, k, b = head(q4s[c], h, GLA_DK), head(chunks[c][1], h, GLA_DK), head(b4s[c], h, GLA_DK)
            v = head(chunks[c][2], h, GLA_DV)
            o_heads.append(intra[c][h] + _bdot(q * jnp.exp(b), cur[h]))
            b_last = b[chunk - 1:chunk, :]
            k_dec = k * jnp.exp(b_last - b)
            decay_col = jnp.exp(b[chunk - 8:chunk, :]).T[:, 7:8]
            new_states.append(decay_col * cur[h] + lax.dot_general(
                k_dec.astype(BF16), v.astype(BF16), tn, preferred_element_type=F32))
        states = new_states if chained else states
        end_states.append(new_states)
        outs.append(jnp.concatenate(o_heads, axis=1))
    return outs, (end_states[-1] if chained else end_states)


def _gla_prompt_kernel(q_ref, k_ref, v_ref, la_ref, wsel_ref, o_ref, s_ref, t_ref, kb_ref):
    @pl.when(pl.program_id(1) == 0)
    def _():
        s_ref[...] = jnp.zeros_like(s_ref)

    def body(trip, carry):
        states = [s_ref[0, h] for h in range(GLA_HEADS)]
        slices = [pl.ds(pl.multiple_of((GLA_TRIP * trip + c) * GLA_CHUNK, GLA_CHUNK), GLA_CHUNK)
                  for c in range(GLA_TRIP)]
        chunks = [(q_ref[sl, :], k_ref[sl, :], v_ref[sl, :], la_ref[sl, :]) for sl in slices]
        outs, states = _gla_chunks(chunks, states, GLA_CHUNK, GLA_SUB, wsel_ref[...],
                                   [t_ref.at[c] for c in range(GLA_TRIP)],
                                   [kb_ref.at[c] for c in range(GLA_TRIP)])
        for sl, o in zip(slices, outs):
            o_ref[sl, :] = o
        for h in range(GLA_HEADS):
            s_ref[0, h] = states[h]
        return carry

    lax.fori_loop(0, GLA_STEP // (GLA_TRIP * GLA_CHUNK), body, 0)


def _gla_prompt(batch, seq, qkvr, log_a):
    steps = seq // GLA_STEP
    row = lambda b, s: b * steps + s
    wsel = _gla_select_matrix(GLA_CHUNK, GLA_SUB)
    return pl.pallas_call(
        _gla_prompt_kernel,
        name="gla_prompt",
        grid=(batch, steps),
        in_specs=[pl.BlockSpec((GLA_STEP, GLA_KEY_WIDTH), lambda b, s: (row(b, s), 0)),
                  pl.BlockSpec((GLA_STEP, GLA_KEY_WIDTH), lambda b, s: (row(b, s), 1)),
                  pl.BlockSpec((GLA_STEP, GLA_WIDTH), lambda b, s: (row(b, s), 1)),
                  pl.BlockSpec((GLA_STEP, GLA_KEY_WIDTH), lambda b, s: (row(b, s), 0)),
                  pl.BlockSpec(wsel.shape, lambda b, s: (0, 0))],
        out_specs=[pl.BlockSpec((GLA_STEP, GLA_WIDTH), lambda b, s: (row(b, s), 0)),
                   pl.BlockSpec((1, GLA_HEADS, GLA_DK, GLA_DV), lambda b, s: (b, 0, 0, 0))],
        out_shape=[jax.ShapeDtypeStruct((batch * seq, GLA_WIDTH), F32),
                   jax.ShapeDtypeStruct((batch, GLA_HEADS, GLA_DK, GLA_DV), F32)],
        scratch_shapes=[pltpu.VMEM((GLA_TRIP, GLA_HEADS * GLA_CHUNK, GLA_SUB * GLA_DK), BF16),
                        pltpu.VMEM((GLA_TRIP, 2 * GLA_HEADS, GLA_CHUNK, GLA_DK), F32)],
        compiler_params=_cparams(2),
    )(qkvr, qkvr, qkvr, log_a, wsel)


GLA_DEC_BB = 8
GLA_DEC_TRIP = 4


def _gla_decode_kernel(seq, q_ref, k_ref, v_ref, la_ref, wsel_ref, s0_ref, o_ref, s_ref, t_ref, kb_ref):
    def body(trip, carry):
        elems = [GLA_DEC_TRIP * trip + c for c in range(GLA_DEC_TRIP)]
        slices = [pl.ds(pl.multiple_of(e * seq, seq), seq) for e in elems]
        chunks = [(q_ref[sl, :], k_ref[sl, :], v_ref[sl, :], la_ref[sl, :]) for sl in slices]
        states = [[s0_ref[e, h] for h in range(GLA_HEADS)] for e in elems]
        outs, new_states = _gla_chunks(chunks, states, seq, seq, wsel_ref[...],
                                       [t_ref.at[c] for c in range(GLA_DEC_TRIP)],
                                       [kb_ref.at[c] for c in range(GLA_DEC_TRIP)], chained=False)
        for e, sl, o, ns in zip(elems, slices, outs, new_states):
            o_ref[sl, :] = o
            for h in range(GLA_HEADS):
                s_ref[e, h] = ns[h]
        return carry

    lax.fori_loop(0, GLA_DEC_BB // GLA_DEC_TRIP, body, 0)


def _gla_decode(batch, seq, qkvr, log_a, state):
    rows = GLA_DEC_BB * seq
    wsel = _gla_select_matrix(seq, seq)
    state_spec = pl.BlockSpec((GLA_DEC_BB, GLA_HEADS, GLA_DK, GLA_DV), lambda i: (i, 0, 0, 0))
    return pl.pallas_call(
        functools.partial(_gla_decode_kernel, seq),
        name="gla_decode",
        grid=(batch // GLA_DEC_BB,),
        in_specs=[pl.BlockSpec((rows, GLA_KEY_WIDTH), lambda i: (i, 0)),
                  pl.BlockSpec((rows, GLA_KEY_WIDTH), lambda i: (i, 1)),
                  pl.BlockSpec((rows, GLA_WIDTH), lambda i: (i, 1)),
                  pl.BlockSpec((rows, GLA_KEY_WIDTH), lambda i: (i, 0)),
                  pl.BlockSpec(wsel.shape, lambda i: (0, 0)),
                  state_spec],
        out_specs=[pl.BlockSpec((rows, GLA_WIDTH), lambda i: (i, 0)), state_spec],
        out_shape=[jax.ShapeDtypeStruct((batch * seq, GLA_WIDTH), F32),
                   jax.ShapeDtypeStruct((batch, GLA_HEADS, GLA_DK, GLA_DV), F32)],
        scratch_shapes=[pltpu.VMEM((GLA_DEC_TRIP, GLA_HEADS * seq, seq * GLA_DK), F32),
                        pltpu.VMEM((GLA_DEC_TRIP, 2 * GLA_HEADS, seq, GLA_DK), F32)],
        compiler_params=_cparams(1),
    )(qkvr, qkvr, qkvr, log_a, wsel, state)


def _route(logits, cnt_ref):
    lt = logits.T
    n = lt.shape[1]
    big = jnp.int32(10 ** 6)
    row8 = lax.broadcasted_iota(jnp.int32, (EXPERTS_PER_GROUP, n), 0)
    lg = jnp.where(row8 < N_GROUPS, lt[0:EXPERTS_PER_GROUP], NEG)
    mg = jnp.max(lg, axis=0, keepdims=True)
    g_idx = jnp.min(jnp.where(lg == mg, row8, big), axis=0, keepdims=True)
    p_sel = 1.0 / jnp.sum(jnp.exp(lg - mg), axis=0, keepdims=True)
    le = jnp.zeros((EXPERTS_PER_GROUP, n), F32)
    for g in range(N_GROUPS):
        lo = EXPERTS_PER_GROUP * (g + 1)
        le = jnp.where(g_idx == g, lt[lo:lo + EXPERTS_PER_GROUP], le)
    m1 = jnp.max(le, axis=0, keepdims=True)
    i1 = jnp.min(jnp.where(le == m1, row8, big), axis=0, keepdims=True)
    rest = row8 != i1
    m2 = jnp.max(jnp.where(rest, le, NEG), axis=0, keepdims=True)
    i2 = jnp.min(jnp.where(rest & (le == m2), row8, big), axis=0, keepdims=True)
    e2 = jnp.exp(m2 - m1)
    w1 = p_sel / (1.0 + e2)
    w2 = p_sel * e2 / (1.0 + e2)
    ex1_i = g_idx * EXPERTS_PER_GROUP + i1
    ex2_i = g_idx * EXPERTS_PER_GROUP + i2
    ex1, ex2 = ex1_i.astype(F32), ex2_i.astype(F32)

    e_row = lax.broadcasted_iota(jnp.int32, (N_EXPERTS, n), 0)
    hit1, hit2 = e_row == ex1_i, e_row == ex2_i
    member = (hit1 | hit2).astype(BF16)
    before = (lax.broadcasted_iota(jnp.int32, (n, n), 0) < lax.broadcasted_iota(jnp.int32, (n, n), 1))
    prior = jnp.dot(member, before.astype(BF16), preferred_element_type=F32) + cnt_ref[:, 0:1]
    rank1 = jnp.sum(jnp.where(hit1, prior, 0.0), axis=0, keepdims=True)
    rank2 = jnp.sum(jnp.where(hit2, prior, 0.0), axis=0, keepdims=True)
    cnt_ref[...] = cnt_ref[...] + jnp.sum(member.astype(F32), axis=1, keepdims=True)

    packed = jnp.where(row8 == 0, ex1, jnp.where(row8 == 1, ex2,
                       jnp.where(row8 == 2, w1, jnp.where(row8 == 3, w2,
                                 jnp.where(row8 == 4, rank1, jnp.where(row8 == 5, rank2, 0.0))))))
    full = jnp.concatenate([packed, jnp.zeros((LANES - EXPERTS_PER_GROUP, n), F32)], axis=0)
    return full.T, packed


def _mix_out_kernel(grp, pos0, zero_first_halo, n_alias,
                    o_ref, r_ref, u_ref, halo_ref, x_ref, mod_ref, n2_ref, gn_ref, pw_ref, ps_ref,
                    wo_ref, wr_ref, br_ref, cnt0_ref, *rest):
    x1_ref, h2_ref, rt_ref, rtt_ref, cnt_ref, ext_ref, lvl_a, lvl_b, ymix_ref = rest[n_alias:]
    i = pl.program_id(0)

    @pl.when(i == 0)
    def _():
        cnt_ref[...] = cnt0_ref[...]

    @pl.when(i < grp.n_tiles)
    def _():
        _mix_out_tile(grp, pos0, zero_first_halo, i, o_ref, r_ref, u_ref, halo_ref, x_ref, mod_ref, n2_ref,
                      gn_ref, pw_ref, ps_ref, wo_ref, wr_ref, br_ref, x1_ref, h2_ref, rt_ref, rtt_ref, cnt_ref,
                      ext_ref, (lvl_a, lvl_b), ymix_ref)

    @pl.when(i >= grp.n_tiles)
    def _():
        h2_ref[...] = jnp.zeros_like(h2_ref)
        rt_ref[...] = jnp.zeros_like(rt_ref)
        rtt_ref[...] = jnp.zeros_like(rtt_ref)


def _mix_out_tile(grp, pos0, zero_first_halo, i, o_ref, r_ref, u_ref, halo_ref, x_ref, mod_ref, n2_ref,
                  gn_ref, pw_ref, ps_ref, wo_ref, wr_ref, br_ref, x1_ref, h2_ref, rt_ref, rtt_ref, cnt_ref,
                  ext_ref, lvl_refs, ymix_ref):
    nb, tt = grp.nb, grp.tt
    hist = halo_ref.shape[-2]

    for h in range(GLA_HEADS):
        cs = slice(h * GLA_DV, (h + 1) * GLA_DV)
        oh = o_ref[:, cs]
        ms = jnp.mean(oh * oh, axis=-1, keepdims=True)
        yh = oh * lax.rsqrt(ms + EPS) * gn_ref[:, cs] * _silu(r_ref[:, cs])
        ymix_ref[:, cs] = yh.astype(BF16)

    halo = halo_ref[...]
    if zero_first_halo:
        halo = jnp.where(i % grp.tiles_per_batch == 0, 0.0, halo)
    n_ext = EXT_TOK0 + tt
    ext_ref[:, 0:EXT_TOK0 - hist, :] = jnp.zeros((nb, EXT_TOK0 - hist, POOL_WIDTH), F32)
    ext_ref[:, EXT_TOK0 - hist:EXT_TOK0, :] = halo.reshape(nb, hist, POOL_WIDTH)
    u = u_ref[...].reshape(nb, tt, POOL_WIDTH)
    ext_ref[:, EXT_TOK0:n_ext, :] = u
    for lvl_ref in lvl_refs:
        lvl_ref[:, 0:EXT_PAD, :] = jnp.zeros((nb, EXT_PAD, POOL_GW), F32)
    t_idx = lax.broadcasted_iota(jnp.int32, (nb, tt, POOL_GW), 1)
    if grp.nb == 1:
        pos = (i % grp.tiles_per_batch) * TM + t_idx + pos0
    else:
        pos = t_idx + pos0
    y_gla = []
    part = D_MODEL // len(POOL_WINDOWS)
    for gi, w in enumerate(POOL_WINDOWS):
        y_gla.append(jnp.dot(ymix_ref[:, :GLA_WIDTH], wo_ref[:GLA_WIDTH, gi * part:(gi + 1) * part],
                             preferred_element_type=F32))
        cs = slice(gi * POOL_GW, (gi + 1) * POOL_GW)
        cur = lambda lo, hi: ext_ref[:, lo:hi, cs]
        d, level = 1, 0
        while d < w:
            nxt = lvl_refs[level % 2]
            nxt[:, EXT_PAD:n_ext, :] = cur(EXT_PAD, n_ext) + cur(EXT_PAD - d, n_ext - d)
            cur = lambda lo, hi, ref=nxt: ref[:, lo:hi, :]
            d, level = 2 * d, level + 1
        acc = cur(EXT_TOK0, n_ext)
        cnt = jnp.minimum(pos + 1, w).astype(F32)
        pooled = acc / cnt - u[:, :, cs]
        yp = _bdot(pooled.reshape(TM, POOL_GW), pw_ref[gi]) * ps_ref[:, cs]
        ymix_ref[:, GLA_WIDTH + gi * POOL_GW:GLA_WIDTH + (gi + 1) * POOL_GW] = yp.astype(BF16)

    y = jnp.concatenate(y_gla, axis=1) + jnp.dot(ymix_ref[:, GLA_WIDTH:], wo_ref[GLA_WIDTH:, :],
                                                 preferred_element_type=F32)
    x1 = x_ref[...] + _mod_rows(mod_ref, 2) * y.reshape(nb, tt, D_MODEL)
    x1_ref[...] = x1
    h2 = _rmsnorm_mod(x1, n2_ref[...], _mod_rows(mod_ref, 4), _mod_rows(mod_ref, 3)).reshape(TM, D_MODEL)
    h2_ref[...] = h2
    logits = _dot_3pass(h2, wr_ref[0], wr_ref[1]) + br_ref[...]
    rt_ref[...], rtt_ref[...] = _route(logits, cnt_ref)


def _mix_out(grp, n_tok, pos0, zero_first_halo, o, qkvr, u, halo_src, halo_block, halo_map, x, mod, norm2,
             gla_norm, pool_w, pool_scale, w_out, w_router, b_router, counts, shared=()):
    n_alias = len(shared)
    n = grp.n_tiles
    n_fill = 0 if shared else n_tok // TM - n
    assert n_fill == 0 or grp.tile_off == 0
    clamp = lambda f: (lambda i: f(jnp.minimum(i, n - 1)))
    const2 = lambda i: (0, 0)
    row = clamp(lambda i: (i, 0))
    off = grp.tile_off
    kern = functools.partial(_mix_out_kernel, grp, pos0, zero_first_halo, n_alias)
    return pl.pallas_call(
        kern,
        name="mix_out",
        grid=(n + n_fill,),
        in_specs=[pl.BlockSpec((TM, GLA_WIDTH), row),
                  pl.BlockSpec((TM, GLA_WIDTH), clamp(lambda i: (i, 2))),
                  pl.BlockSpec((TM, POOL_WIDTH), row),
                  pl.BlockSpec(halo_block, clamp(halo_map)),
                  pl.BlockSpec((grp.nb, grp.tt, D_MODEL), clamp(grp.x_map())),
                  pl.BlockSpec((grp.nb, 6, D_MODEL), clamp(grp.mod_map())),
                  pl.BlockSpec((1, 1, D_MODEL), lambda i: (0, 0, 0)),
                  pl.BlockSpec((1, GLA_WIDTH), const2),
                  pl.BlockSpec(pool_w.shape, lambda i: (0, 0, 0), pipeline_mode=pl.Buffered(1)),
                  pl.BlockSpec((1, POOL_WIDTH), const2),
                  pl.BlockSpec(w_out.shape, const2, pipeline_mode=pl.Buffered(1)),
                  pl.BlockSpec(w_router.shape, lambda i: (0, 0, 0), pipeline_mode=pl.Buffered(1)),
                  pl.BlockSpec((1, LANES), const2),
                  pl.BlockSpec((N_EXPERTS, LANES), const2)]
                 + [pl.BlockSpec(memory_space=pl.ANY)] * n_alias,
        out_specs=[pl.BlockSpec((grp.nb, grp.tt, D_MODEL), clamp(grp.x_map())),
                   pl.BlockSpec((TM, D_MODEL), lambda i: (i + off, 0)),
                   pl.BlockSpec((TM, LANES), lambda i: (i + off, 0)),
                   pl.BlockSpec((EXPERTS_PER_GROUP, TM), lambda i: (0, i + off)),
                   pl.BlockSpec((N_EXPERTS, LANES), const2)],
        out_shape=[jax.ShapeDtypeStruct(x.shape, F32),
                   jax.ShapeDtypeStruct((n_tok, D_MODEL), F32),
                   jax.ShapeDtypeStruct((n_tok, LANES), F32),
                   jax.ShapeDtypeStruct((EXPERTS_PER_GROUP, n_tok), F32),
                   jax.ShapeDtypeStruct((N_EXPERTS, LANES), F32)],
        scratch_shapes=[pltpu.VMEM((grp.nb, EXT_TOK0 + grp.tt, POOL_WIDTH), F32),
                        pltpu.VMEM((grp.nb, EXT_TOK0 + grp.tt, POOL_GW), F32),
                        pltpu.VMEM((grp.nb, EXT_TOK0 + grp.tt, POOL_GW), F32),
                        pltpu.VMEM((TM, D_MODEL), BF16)],
        input_output_aliases={14 + k: 1 + k for k in range(n_alias)},
        compiler_params=_cparams(1),
    )(o, qkvr, u, halo_src, x, mod, norm2.reshape(1, 1, D_MODEL), gla_norm.reshape(1, GLA_WIDTH),
      pool_w, pool_scale.reshape(1, POOL_WIDTH), w_out, w_router, b_router, counts, *shared)


def _row_copy(src_hbm, src_row, dst, dst_row, sem):
    return pltpu.make_async_copy(src_hbm.at[pl.ds(src_row, 1), :], dst.at[pl.ds(dst_row, 1), :], sem)


def _tile_wait(src_hbm, dst, sem):
    pltpu.make_async_copy(src_hbm.at[pl.ds(0, dst.shape[0]), :], dst, sem).wait()


MOE_AHEAD = 2


def _moe_gather_kernel(te_ref, src_ref, nu_ref, seg_ref, nxt_ref, h_hbm, w1_hbm, w3_hbm, w2_hbm, y_ref,
                       buf0, buf1, buf2, sem, wf1, wf3, wf2, wsem, w1b, w3b, w2b):
    i = pl.program_id(0)
    n_used = nu_ref[0]
    bufs = (buf0, buf1, buf2)
    n_slots = MOE_AHEAD + 1
    n_parts = 4

    def issue(tile, slot, part):
        base = tile * TME
        per = TME // n_parts
        for r in range(part * per, (part + 1) * per):
            _row_copy(h_hbm, src_ref[base + r], bufs[slot], r, sem.at[slot]).start(priority=r % 2)

    def weight_copies(expert, wslot):
        return (pltpu.make_async_copy(w1_hbm.at[expert], wf1.at[wslot], wsem.at[wslot]),
                pltpu.make_async_copy(w3_hbm.at[expert], wf3.at[wslot], wsem.at[wslot]),
                pltpu.make_async_copy(w2_hbm.at[expert], wf2.at[wslot], wsem.at[wslot]))

    def compute(slot, ahead):
        nxt = (lambda part: issue(i + MOE_AHEAD, (slot + MOE_AHEAD) % n_slots, part)) if ahead else (lambda part: None)
        _tile_wait(h_hbm, bufs[slot], sem.at[slot])
        nxt(0)
        x = bufs[slot][...].astype(BF16)
        nxt(1)
        a = jnp.dot(x, w1b[...], preferred_element_type=F32)
        nxt(2)
        b = jnp.dot(x, w3b[...], preferred_element_type=F32)
        nxt(3)
        hid = _silu(a) * b
        y_ref[...] = jnp.dot(hid.astype(BF16), w2b[...], preferred_element_type=F32)

    @pl.when(i == 0)
    def _():
        for c in weight_copies(te_ref[0], 0):
            c.start()
        for t in range(MOE_AHEAD):
            @pl.when(t < n_used)
            def _():
                for part in range(n_parts):
                    issue(t, t, part)

    prev = jnp.maximum(i - 1, 0)

    @pl.when((i < n_used) & ((i == 0) | (te_ref[i] != te_ref[prev])))
    def _():
        wslot = seg_ref[i] % 2
        for c in weight_copies(te_ref[i], wslot):
            c.wait()
        w1b[...] = wf1[wslot].astype(BF16)
        w3b[...] = wf3[wslot].astype(BF16)
        w2b[...] = wf2[wslot].astype(BF16)

        @pl.when(nxt_ref[i] >= 0)
        def _():
            for c in weight_copies(nxt_ref[i], 1 - wslot):
                c.start()

    for slot in range(n_slots):
        @pl.when((i + MOE_AHEAD < n_used) & (i % n_slots == slot))
        def _():
            compute(slot, True)

        @pl.when((i < n_used) & (i + MOE_AHEAD >= n_used) & (i % n_slots == slot))
        def _():
            compute(slot, False)

    @pl.when(i >= n_used)
    def _():
        y_ref[...] = jnp.zeros_like(y_ref)


def _moe_gather(plan, h2_all, w1, w3, w2):
    n_sorted = plan["src_row"].shape[0]
    grid_spec = pltpu.PrefetchScalarGridSpec(
        num_scalar_prefetch=5,
        grid=(n_sorted // TME,),
        in_specs=[pl.BlockSpec(memory_space=pl.ANY)] * 4,
        out_specs=pl.BlockSpec((TME, D_MODEL), lambda i, *_: (i, 0)),
        scratch_shapes=[pltpu.VMEM((TME, D_MODEL), F32)] * (MOE_AHEAD + 1)
                       + [pltpu.SemaphoreType.DMA((MOE_AHEAD + 1,)),
                          pltpu.VMEM((2, D_MODEL, EXPERT_FF), F32), pltpu.VMEM((2, D_MODEL, EXPERT_FF), F32),
                          pltpu.VMEM((2, EXPERT_FF, D_MODEL), F32),
                          pltpu.SemaphoreType.DMA((2,)),
                          pltpu.VMEM((D_MODEL, EXPERT_FF), BF16), pltpu.VMEM((D_MODEL, EXPERT_FF), BF16),
                          pltpu.VMEM((EXPERT_FF, D_MODEL), BF16)],
    )
    return pl.pallas_call(
        _moe_gather_kernel,
        name="moe",
        grid_spec=grid_spec,
        out_shape=jax.ShapeDtypeStruct((n_sorted, D_MODEL), F32),
        compiler_params=_cparams(1),
    )(plan["tile_expert"], plan["src_row"], plan["n_used"], plan["segment"], plan["next_expert"],
      h2_all, w1, w3, w2)


def _finish_kernel(grp, pos_ref, x1_ref, mod_ref, rt_ref, nf_ref, y_hbm, out_ref, *scratch):
    n_slots = FINISH_AHEAD + 1
    bufs_a, bufs_b, sem = scratch[:n_slots], scratch[n_slots:2 * n_slots], scratch[2 * n_slots]
    i = pl.program_id(0)
    n_steps = pl.num_programs(0)
    n_parts = 4
    n_tok = pos_ref.shape[0] // 2

    def issue(tile, slot, part):
        base = tile * TM + grp.row_off
        per = TM // n_parts
        for r in range(part * per, (part + 1) * per):
            _row_copy(y_hbm, pos_ref[base + r], bufs_a[slot], r, sem.at[slot]).start(priority=0)
            _row_copy(y_hbm, pos_ref[n_tok + base + r], bufs_b[slot], r, sem.at[slot]).start(priority=1)

    def compute(slot, ahead):
        nxt = ((lambda part: issue(i + FINISH_AHEAD, (slot + FINISH_AHEAD) % n_slots, part)) if ahead
               else (lambda part: None))
        _tile_wait(y_hbm, bufs_a[slot], sem.at[slot])
        _tile_wait(y_hbm, bufs_b[slot], sem.at[slot])
        nxt(0)
        rt = rt_ref[...]
        moe = rt[:, 2:3] * bufs_a[slot][...] + rt[:, 3:4] * bufs_b[slot][...]
        nxt(1)
        x2 = x1_ref[...] + _mod_rows(mod_ref, 5) * moe.reshape(grp.nb, grp.tt, D_MODEL)
        nxt(2)
        ms = jnp.mean(x2 * x2, axis=-1, keepdims=True)
        nxt(3)
        out_ref[...] = x2 * lax.rsqrt(ms + EPS) * nf_ref[...]

    @pl.when(i == 0)
    def _():
        for t in range(FINISH_AHEAD):
            @pl.when(t < n_steps)
            def _():
                for part in range(n_parts):
                    issue(t, t, part)

    for slot in range(n_slots):
        @pl.when((i + FINISH_AHEAD < n_steps) & (i % n_slots == slot))
        def _():
            compute(slot, True)

        @pl.when((i + FINISH_AHEAD >= n_steps) & (i % n_slots == slot))
        def _():
            compute(slot, False)


def _finish(grp, pos, x1, mod, route_all, norm_f, y_sorted):
    off = grp.tile_off
    grid_spec = pltpu.PrefetchScalarGridSpec(
        num_scalar_prefetch=1,
        grid=(grp.n_tiles,),
        in_specs=[pl.BlockSpec((grp.nb, grp.tt, D_MODEL), grp.x_map()),
                  pl.BlockSpec((grp.nb, 6, D_MODEL), grp.mod_map()),
                  pl.BlockSpec((TM, LANES), lambda i, p: (i + off, 0)),
                  pl.BlockSpec((1, 1, D_MODEL), lambda i, p: (0, 0, 0)),
                  pl.BlockSpec(memory_space=pl.ANY)],
        out_specs=pl.BlockSpec((grp.nb, grp.tt, D_MODEL), grp.x_map()),
        scratch_shapes=[pltpu.VMEM((TM, D_MODEL), F32)] * (2 * (FINISH_AHEAD + 1))
                       + [pltpu.SemaphoreType.DMA((FINISH_AHEAD + 1,))],
    )
    return pl.pallas_call(
        functools.partial(_finish_kernel, grp),
        name="finish",
        grid_spec=grid_spec,
        out_shape=jax.ShapeDtypeStruct(x1.shape, F32),
        compiler_params=_cparams(1),
    )(pos, x1, mod, route_all, norm_f.reshape(1, 1, D_MODEL), y_sorted)


def _slot_kernel(rt_ref, cnt_ref, pos_ref):
    expert = rt_ref[0:2, :].astype(jnp.int32)
    rank = rt_ref[4:6, :].astype(jnp.int32)
    padded = jnp.ceil(cnt_ref[...] / TME) * TME
    r = lax.broadcasted_iota(jnp.int32, (N_EXPERTS, N_EXPERTS), 0)
    c = lax.broadcasted_iota(jnp.int32, (N_EXPERTS, N_EXPERTS), 1)
    starts = _dot_exact_lhs((c < r).astype(BF16), padded)
    start_of = jnp.zeros(expert.shape, F32)
    for e in range(N_EXPERTS):
        start_of = jnp.where(expert == e, starts[e:e + 1, 0:1], start_of)
    pos_ref[...] = start_of.astype(jnp.int32) + rank


def _slots(route_t, counts):
    n_tok = route_t.shape[1]
    return pl.pallas_call(
        _slot_kernel,
        name="slots",
        out_shape=jax.ShapeDtypeStruct((2, n_tok), jnp.int32),
        compiler_params=pltpu.CompilerParams(vmem_limit_bytes=VMEM_LIMIT),
    )(route_t, counts)


def _sort_plan(route_t, counts):
    n_tok = route_t.shape[1]
    n_pairs = 2 * n_tok
    n_sorted = n_pairs + N_EXPERTS * TME
    pos = _slots(route_t, counts).reshape(n_pairs)
    counts = counts[:, 0].astype(jnp.int32)
    padded = ((counts + TME - 1) // TME) * TME
    ends = jnp.cumsum(padded)
    token = jnp.tile(jnp.arange(n_tok, dtype=jnp.int32), 2)
    src_row = (jnp.arange(n_sorted, dtype=jnp.int32) % n_tok).at[pos].set(
        token, unique_indices=True, mode="promise_in_bounds")
    tile_start = jnp.arange(n_sorted // TME, dtype=jnp.int32) * TME
    tile_expert = jnp.sum((tile_start[:, None] >= ends[None, :]).astype(jnp.int32), axis=1)
    tile_expert = jnp.minimum(tile_expert, N_EXPERTS - 1)
    n_used = ends[-1] // TME
    is_first = jnp.concatenate([jnp.ones((1,), jnp.int32),
                                (tile_expert[1:] != tile_expert[:-1]).astype(jnp.int32)])
    segment = jnp.cumsum(is_first) - 1
    next_tile = ends[tile_expert] // TME
    next_expert = jnp.where(next_tile < n_used, tile_expert[jnp.minimum(next_tile, n_sorted // TME - 1)], -1)
    return dict(pos=pos.astype(jnp.int32), src_row=src_row, tile_expert=tile_expert.astype(jnp.int32),
                n_used=n_used.astype(jnp.int32).reshape(1), segment=segment.astype(jnp.int32),
                next_expert=next_expert.astype(jnp.int32))


def kernel(x_prompt, x_sample, c_prompt, c_sample, state_gla, state_pool, w_ada, b_ada, norm1, norm2, w_in,
           gate_up, gate_bias, gla_norm, pool_w, pool_scale, w_out, w_group, b_group, w_expert, b_expert,
           w1, w3, w2, norm_f):
    assert w_ada.shape[0] == 1, "single-layer step"
    bp, tp, _ = x_prompt.shape
    bs, ts, _ = x_sample.shape
    grp_p = _Group(bp, tp, 0, bs)
    grp_s = _Group(bs, ts, bp * tp, 0)
    n_tok = bp * tp + bs * ts

    n_c = bp + bs
    n_c_pad = -(-n_c // 8) * 8
    c_all = jnp.concatenate([c_sample, c_prompt, jnp.zeros((n_c_pad - n_c, D_MODEL), F32)], axis=0)
    mod = _adaln(c_all, w_ada[0], b_ada[0]).reshape(n_c_pad, 6, D_MODEL)
    mod_p = mod_s = mod

    w_in_t = jnp.swapaxes(w_in.reshape(w_in.shape[1:]), 0, 1).astype(BF16)
    gup = gate_up[0].astype(BF16)
    gb = gate_bias[0].reshape(1, GLA_KEY_WIDTH)
    pw = pool_w[0].astype(BF16)
    wo = w_out[0].astype(BF16)
    gap = EXPERTS_PER_GROUP - N_GROUPS
    tail = LANES - EXPERTS_PER_GROUP - N_EXPERTS
    w_router = jnp.concatenate(
        [w_group[0], jnp.zeros((D_MODEL, gap), F32),
         jnp.transpose(w_expert[0], (1, 0, 2)).reshape(D_MODEL, N_EXPERTS),
         jnp.zeros((D_MODEL, tail), F32)], axis=1)
    w_router_hi = w_router.astype(BF16)
    w_router = jnp.stack([w_router_hi, (w_router - w_router_hi.astype(F32)).astype(BF16)])
    b_router = jnp.concatenate([b_group[0], jnp.zeros((gap,), F32), b_expert[0].reshape(N_EXPERTS),
                                jnp.zeros((tail,), F32)]).reshape(1, LANES)

    qkvr_p, la_p, u_p = _in_proj(grp_p, x_prompt, mod_p, norm1[0], w_in_t, gup, gb)
    qkvr_s, la_s, u_s = _in_proj(grp_s, x_sample, mod_s, norm1[0], w_in_t, gup, gb)

    o_p, gla_p = _gla_prompt(bp, tp, qkvr_p, la_p)
    o_s, gla_s = _gla_decode(bs, ts, qkvr_s, la_s, state_gla.reshape(state_gla.shape[1:]))

    halo_per_tile = TM // HALO
    halo_map_p = lambda i: (jnp.maximum(i * halo_per_tile - 1, 0), 0)
    counts = jnp.zeros((N_EXPERTS, LANES), F32)
    x1_p, h2_all, route_all, route_t, counts = _mix_out(
        grp_p, n_tok, 0, True, o_p, qkvr_p, u_p, u_p, (HALO, POOL_WIDTH), halo_map_p, x_prompt, mod_p,
        norm2[0], gla_norm[0], pw, pool_scale[0], wo, w_router, b_router, counts)
    x1_s, h2_all, route_all, route_t, counts = _mix_out(
        grp_s, n_tok, PAST_LEN, False, o_s, qkvr_s, u_s, state_pool.reshape(bs, POOL_BUF, POOL_WIDTH),
        (grp_s.nb, POOL_BUF, POOL_WIDTH), lambda i: (i, 0, 0), x_sample, mod_s, norm2[0], gla_norm[0], pw,
        pool_scale[0], wo, w_router, b_router, counts, shared=(h2_all, route_all, route_t))

    plan = _sort_plan(route_t, counts)
    y_sorted = _moe_gather(plan, h2_all, w1.reshape(w1.shape[1:]), w3.reshape(w3.shape[1:]),
                           w2.reshape(w2.shape[1:]))

    y_p = _finish(grp_p, plan["pos"], x1_p, mod_p, route_all, norm_f, y_sorted)
    y_s = _finish(grp_s, plan["pos"], x1_s, mod_s, route_all, norm_f, y_sorted)

    u_p3 = u_p.reshape(bp, tp, POOL_WIDTH)
    u_s3 = u_s.reshape(bs, ts, POOL_WIDTH)
    assert tp >= POOL_BUF > ts
    pool_p = u_p3[:, tp - POOL_BUF:]
    pool_s = jnp.concatenate([state_pool.reshape(bs, POOL_BUF, POOL_WIDTH)[:, ts:], u_s3], axis=1)
    lead = lambda a: a.reshape((1,) + a.shape)
    return (y_p, y_s, lead(gla_p), lead(pool_p), lead(gla_s), lead(pool_s))
```

```python
import functools

import jax
import jax.numpy as jnp
from jax import lax
from jax.experimental import pallas as pl
from jax.experimental.pallas import tpu as pltpu

D_MODEL = 2048
GLA_HEADS = 4
GLA_DK = 128
GLA_DV = 256
GLA_KEY_WIDTH = GLA_HEADS * GLA_DK
GLA_WIDTH = GLA_HEADS * GLA_DV
POOL_WIDTH = 1024
POOL_WINDOWS = (2, 4, 8, 16)
POOL_GW = 256
POOL_BUF = 15
HALO = 16
EXT_PAD = 8
EXT_TOK0 = EXT_PAD + HALO
GATE_RANK = 16
GATE_TEMP = 16.0
N_GROUPS = 4
EXPERTS_PER_GROUP = 8
N_EXPERTS = 32
EXPERT_FF = 512
EPS = 1e-6
PAST_LEN = 16384
QKVR_WIDTH = 2 * GLA_KEY_WIDTH + 2 * GLA_WIDTH

LANES = 128
ADALN_COLS = 1024
TM = 256
TME = 256
FINISH_AHEAD = 2
GLA_CHUNK = 64
GLA_SUB = 8
GLA_STEP = 512
GLA_TRIP = 4
VMEM_LIMIT = 56 * 1024 * 1024

BF16 = jnp.bfloat16
F32 = jnp.float32
NEG = -1e30
LOG2E = 1.4426950408889634


def _cparams(n_axes):
    return pltpu.CompilerParams(dimension_semantics=("arbitrary",) * n_axes,
                                vmem_limit_bytes=VMEM_LIMIT)


def _silu(x):
    return x / (1.0 + jnp.exp(-x))


def _bdot(a, b):
    return jnp.dot(a.astype(BF16), b.astype(BF16), preferred_element_type=F32)


def _split3(a):
    a1 = a.astype(BF16)
    r1 = a - a1.astype(F32)
    a2 = r1.astype(BF16)
    a3 = (r1 - a2.astype(F32)).astype(BF16)
    return a1, a2, a3


def _split2(a):
    hi = a.astype(BF16)
    lo = (a - hi.astype(F32)).astype(BF16)
    return hi, lo


def _dot_3pass(a, b_hi, b_lo):
    a_hi, a_lo = _split2(a)
    d = lambda x, y: jnp.dot(x, y, preferred_element_type=F32)
    return d(a_hi, b_hi) + (d(a_hi, b_lo) + d(a_lo, b_hi))


def _dot_exact_lhs(tri_bf16, g):
    g1, g2, g3 = _split3(g)
    d = lambda y: jnp.dot(tri_bf16, y, preferred_element_type=F32)
    return d(g1) + (d(g2) + d(g3))


def _adaln_kernel(c_ref, w_ref, b_ref, o_ref):
    c = c_ref[...]
    o_ref[...] = _bdot(_silu(c), w_ref[...]) + b_ref[...]


def _adaln(c_all, w_ada, b_ada):
    n, d = c_all.shape
    width = w_ada.shape[1]
    tn = ADALN_COLS
    return pl.pallas_call(
        _adaln_kernel,
        name="adaln",
        grid=(width // tn,),
        in_specs=[pl.BlockSpec((n, d), lambda j: (0, 0)),
                  pl.BlockSpec((d, tn), lambda j: (0, j)),
                  pl.BlockSpec((1, tn), lambda j: (0, j))],
        out_specs=pl.BlockSpec((n, tn), lambda j: (0, j)),
        out_shape=jax.ShapeDtypeStruct((n, width), F32),
        compiler_params=_cparams(1),
    )(c_all, w_ada, b_ada.reshape(1, width))


class _Group:
    def __init__(self, batch, seq, row_off, mod_off):
        self.batch, self.seq, self.row_off, self.mod_off = batch, seq, row_off, mod_off
        if seq >= TM:
            assert seq % TM == 0
            self.nb, self.tt = 1, TM
            self.tiles_per_batch = seq // TM
            self.n_tiles = batch * self.tiles_per_batch
        else:
            assert TM % seq == 0 and batch % (TM // seq) == 0
            self.nb, self.tt = TM // seq, seq
            self.tiles_per_batch = 1
            self.n_tiles = batch // self.nb
        self.rows = batch * seq
        self.tile_off = row_off // TM

    def x_map(self):
        if self.nb == 1:
            tpb = self.tiles_per_batch
            return lambda i, *_: (i // tpb, i % tpb, 0)
        return lambda i, *_: (i, 0, 0)

    def mod_map(self):
        assert self.mod_off % self.nb == 0
        off = self.mod_off // self.nb
        if self.nb == 1:
            tpb = self.tiles_per_batch
            return lambda i, *_: (i // tpb + off, 0, 0)
        return lambda i, *_: (i + off, 0, 0)


def _mod_rows(mod_ref, idx):
    return mod_ref[:, idx:idx + 1, :]


def _rmsnorm_mod(x, gain, scale, shift):
    ms = jnp.mean(x * x, axis=-1, keepdims=True)
    y = x * lax.rsqrt(ms + EPS) * gain
    return y * (1.0 + scale) + shift


def _in_proj_kernel(x_ref, mod_ref, n1_ref, wq_ref, wg_ref, wu_ref, gup_ref, gb_ref,
                    qkvr_ref, la_ref, u_ref):
    x = x_ref[...]
    h = _rmsnorm_mod(x, n1_ref[...], _mod_rows(mod_ref, 1), _mod_rows(mod_ref, 0))
    hb = h.reshape(TM, D_MODEL).astype(BF16)
    nt = (((1,), (1,)), ((), ()))
    qkvr_ref[...] = lax.dot_general(hb, wq_ref[...], nt, preferred_element_type=F32)
    u_ref[...] = lax.dot_general(hb, wu_ref[...], nt, preferred_element_type=F32)
    g_lr = lax.dot_general(hb, wg_ref[...], nt, preferred_element_type=F32)
    pre = jnp.dot(g_lr.astype(BF16), gup_ref[...], preferred_element_type=F32) + gb_ref[...]
    log_sig = jnp.minimum(pre, 0.0) - jnp.log1p(jnp.exp(-jnp.abs(pre)))
    la_ref[...] = log_sig / GATE_TEMP


def _in_proj(grp, x, mod, norm1, w_in_t, gup, gb):
    const = lambda i: (0, 0)
    u_row0 = QKVR_WIDTH + GATE_RANK
    row = lambda i: (i, 0)
    return pl.pallas_call(
        _in_proj_kernel,
        name="in_proj",
        grid=(grp.n_tiles,),
        in_specs=[pl.BlockSpec((grp.nb, grp.tt, D_MODEL), grp.x_map()),
                  pl.BlockSpec((grp.nb, 6, D_MODEL), grp.mod_map()),
                  pl.BlockSpec((1, 1, D_MODEL), lambda i: (0, 0, 0)),
                  pl.BlockSpec((QKVR_WIDTH, D_MODEL), const, pipeline_mode=pl.Buffered(1)),
                  pl.BlockSpec((GATE_RANK, D_MODEL), lambda i: (QKVR_WIDTH // GATE_RANK, 0),
                               pipeline_mode=pl.Buffered(1)),
                  pl.BlockSpec((pl.Element(POOL_WIDTH), pl.Element(D_MODEL)), lambda i: (u_row0, 0),
                               pipeline_mode=pl.Buffered(1)),
                  pl.BlockSpec(gup.shape, const, pipeline_mode=pl.Buffered(1)),
                  pl.BlockSpec(gb.shape, const, pipeline_mode=pl.Buffered(1))],
        out_specs=[pl.BlockSpec((TM, QKVR_WIDTH), row),
                   pl.BlockSpec((TM, GLA_KEY_WIDTH), row),
                   pl.BlockSpec((TM, POOL_WIDTH), row)],
        out_shape=[jax.ShapeDtypeStruct((grp.rows, QKVR_WIDTH), F32),
                   jax.ShapeDtypeStruct((grp.rows, GLA_KEY_WIDTH), F32),
                   jax.ShapeDtypeStruct((grp.rows, POOL_WIDTH), F32)],
        compiler_params=_cparams(1),
    )(x, mod, norm1.reshape(1, 1, D_MODEL), w_in_t, w_in_t, w_in_t, gup, gb)


def _gla_select_matrix(chunk, sub):
    r = jnp.arange(sub * GLA_DK, dtype=jnp.int32)[:, None] // GLA_DK
    l = jnp.arange(LANES, dtype=jnp.int32)[None, :]
    return ((l % sub == r) & (l < chunk)).astype(BF16)


def _gla_chunks(chunks, states, chunk, sub, wsel, t_refs, kb_refs, chained=True):
    n_sub = chunk // sub
    rows = lax.broadcasted_iota(jnp.int32, (chunk, chunk), 0)
    cols = lax.broadcasted_iota(jnp.int32, (chunk, chunk), 1)
    tri = (rows >= cols).astype(BF16)
    nt = (((1,), (1,)), ((), ()))
    tn = (((0,), (0,)), ((), ()))
    key_row = lax.broadcasted_iota(jnp.int32, (chunk, LANES), 0)
    lane = lax.broadcasted_iota(jnp.int32, (sub, chunk), 1)
    row = lax.broadcasted_iota(jnp.int32, (sub, chunk), 0)
    head = lambda a, h, w: a[:, h * w:(h + 1) * w]
    n = len(chunks)

    b4s = [_dot_exact_lhs(tri, g4) for (_, _, _, g4) in chunks]
    q4s = [q4 * (GLA_DK ** -0.5) for (q4, _, _, _) in chunks]

    for c in range(n):
        kb_ref, t_ref, k4 = kb_refs[c], t_refs[c], chunks[c][1]
        for h in range(GLA_HEADS):
            kb_ref[h] = head(k4, h, GLA_DK)
            kb_ref[GLA_HEADS + h] = head(b4s[c], h, GLA_DK) * LOG2E
        for h in range(GLA_HEADS):
            for s in range(n_sub):
                lo = s * sub
                r0 = (h * n_sub + s) * sub
                q_s, b_s = head(q4s[c], h, GLA_DK)[lo:lo + sub, :], kb_ref[GLA_HEADS + h, lo:lo + sub, :]
                for jl in range(sub):
                    k_j = jnp.broadcast_to(kb_ref[h, lo + jl:lo + jl + 1, :], (sub, GLA_DK))
                    b_j = jnp.broadcast_to(kb_ref[GLA_HEADS + h, lo + jl:lo + jl + 1, :], (sub, GLA_DK))
                    decay = jnp.exp2(jnp.minimum(b_s - b_j, 0.0))
                    t_ref[r0:r0 + sub, jl * GLA_DK:(jl + 1) * GLA_DK] = (q_s * k_j * decay).astype(t_ref.dtype)
    p_diags = [jnp.dot(t_refs[c][...].astype(BF16), wsel, preferred_element_type=F32) for c in range(n)]

    intra = []
    for c in range(n):
        per_head = []
        for h in range(GLA_HEADS):
            q, k, b = head(q4s[c], h, GLA_DK), head(chunks[c][1], h, GLA_DK), head(b4s[c], h, GLA_DK)
            p_blocks = []
            for s in range(n_sub):
                lo = s * sub
                r0 = (h * n_sub + s) * sub
                in_block = (lane >= lo) & (lane - lo <= row)
                p = jnp.where(in_block, p_diags[c][r0:r0 + sub, :chunk], 0.0)
                if s > 0:
                    ref_row = b[lo - 1:lo, :]
                    q_rel = q[lo:lo + sub, :] * jnp.exp(b[lo:lo + sub, :] - ref_row)
                    k_rel = k * jnp.exp(jnp.where(key_row < lo, ref_row - b, NEG))
                    p = p + lax.dot_general(q_rel.astype(BF16), k_rel.astype(BF16), nt,
                                            preferred_element_type=F32)
                p_blocks.append(p)
            p_full = p_blocks[0] if n_sub == 1 else jnp.concatenate(p_blocks, axis=0)
            per_head.append(_bdot(p_full, head(chunks[c][2], h, GLA_DV)))
        intra.append(per_head)

    outs, end_states = [], []
    for c in range(n):
        cur = states if chained else states[c]
        o_heads, new_states = [], []
        for h in range(GLA_HEADS):
            q, k, b = head(q4s[c], h, GLA_DK), head(chunks[c][1], h, GLA_DK), head(b4s[c], h, GLA_DK)
            v = head(chunks[c][2], h, GLA_DV)
            o_heads.append(intra[c][h] + _bdot(q * jnp.exp(b), cur[h]))
            b_last = b[chunk - 1:chunk, :]
            k_dec = k * jnp.exp(b_last - b)
            decay_col = jnp.exp(b[chunk - 8:chunk, :]).T[:, 7:8]
            new_states.append(decay_col * cur[h] + lax.dot_general(
                k_dec.astype(BF16), v.astype(BF16), tn, preferred_element_type=F32))
        states = new_states if chained else states
        end_states.append(new_states)
        outs.append(jnp.concatenate(o_heads, axis=1))
    return outs, (end_states[-1] if chained else end_states)


def _gla_prompt_kernel(q_ref, k_ref, v_ref, la_ref, wsel_ref, o_ref, s_ref, t_ref, kb_ref):
    @pl.when(pl.program_id(1) == 0)
    def _():
        s_ref[...] = jnp.zeros_like(s_ref)

    def body(trip, carry):
        states = [s_ref[0, h] for h in range(GLA_HEADS)]
        slices = [pl.ds(pl.multiple_of((GLA_TRIP * trip + c) * GLA_CHUNK, GLA_CHUNK), GLA_CHUNK)
                  for c in range(GLA_TRIP)]
        chunks = [(q_ref[sl, :], k_ref[sl, :], v_ref[sl, :], la_ref[sl, :]) for sl in slices]
        outs, states = _gla_chunks(chunks, states, GLA_CHUNK, GLA_SUB, wsel_ref[...],
                                   [t_ref.at[c] for c in range(GLA_TRIP)],
                                   [kb_ref.at[c] for c in range(GLA_TRIP)])
        for sl, o in zip(slices, outs):
            o_ref[sl, :] = o
        for h in range(GLA_HEADS):
            s_ref[0, h] = states[h]
        return carry

    lax.fori_loop(0, GLA_STEP // (GLA_TRIP * GLA_CHUNK), body, 0)


def _gla_prompt(batch, seq, qkvr, log_a):
    steps = seq // GLA_STEP
    row = lambda b, s: b * steps + s
    wsel = _gla_select_matrix(GLA_CHUNK, GLA_SUB)
    return pl.pallas_call(
        _gla_prompt_kernel,
        name="gla_prompt",
        grid=(batch, steps),
        in_specs=[pl.BlockSpec((GLA_STEP, GLA_KEY_WIDTH), lambda b, s: (row(b, s), 0)),
                  pl.BlockSpec((GLA_STEP, GLA_KEY_WIDTH), lambda b, s: (row(b, s), 1)),
                  pl.BlockSpec((GLA_STEP, GLA_WIDTH), lambda b, s: (row(b, s), 1)),
                  pl.BlockSpec((GLA_STEP, GLA_KEY_WIDTH), lambda b, s: (row(b, s), 0)),
                  pl.BlockSpec(wsel.shape, lambda b, s: (0, 0))],
        out_specs=[pl.BlockSpec((GLA_STEP, GLA_WIDTH), lambda b, s: (row(b, s), 0)),
                   pl.BlockSpec((1, GLA_HEADS, GLA_DK, GLA_DV), lambda b, s: (b, 0, 0, 0))],
        out_shape=[jax.ShapeDtypeStruct((batch * seq, GLA_WIDTH), F32),
                   jax.ShapeDtypeStruct((batch, GLA_HEADS, GLA_DK, GLA_DV), F32)],
        scratch_shapes=[pltpu.VMEM((GLA_TRIP, GLA_HEADS * GLA_CHUNK, GLA_SUB * GLA_DK), BF16),
                        pltpu.VMEM((GLA_TRIP, 2 * GLA_HEADS, GLA_CHUNK, GLA_DK), F32)],
        compiler_params=_cparams(2),
    )(qkvr, qkvr, qkvr, log_a, wsel)


GLA_DEC_BB = 8
GLA_DEC_TRIP = 4


def _gla_decode_kernel(seq, q_ref, k_ref, v_ref, la_ref, wsel_ref, s0_ref, o_ref, s_ref, t_ref, kb_ref):
    def body(trip, carry):
        elems = [GLA_DEC_TRIP * trip + c for c in range(GLA_DEC_TRIP)]
        slices = [pl.ds(pl.multiple_of(e * seq, seq), seq) for e in elems]
        chunks = [(q_ref[sl, :], k_ref[sl, :], v_ref[sl, :], la_ref[sl, :]) for sl in slices]
        states = [[s0_ref[e, h] for h in range(GLA_HEADS)] for e in elems]
        outs, new_states = _gla_chunks(chunks, states, seq, seq, wsel_ref[...],
                                       [t_ref.at[c] for c in range(GLA_DEC_TRIP)],
                                       [kb_ref.at[c] for c in range(GLA_DEC_TRIP)], chained=False)
        for e, sl, o, ns in zip(elems, slices, outs, new_states):
            o_ref[sl, :] = o
            for h in range(GLA_HEADS):
                s_ref[e, h] = ns[h]
        return carry

    lax.fori_loop(0, GLA_DEC_BB // GLA_DEC_TRIP, body, 0)


def _gla_decode(batch, seq, qkvr, log_a, state):
    rows = GLA_DEC_BB * seq
    wsel = _gla_select_matrix(seq, seq)
    state_spec = pl.BlockSpec((GLA_DEC_BB, GLA_HEADS, GLA_DK, GLA_DV), lambda i: (i, 0, 0, 0))
    return pl.pallas_call(
        functools.partial(_gla_decode_kernel, seq),
        name="gla_decode",
        grid=(batch // GLA_DEC_BB,),
        in_specs=[pl.BlockSpec((rows, GLA_KEY_WIDTH), lambda i: (i, 0)),
                  pl.BlockSpec((rows, GLA_KEY_WIDTH), lambda i: (i, 1)),
                  pl.BlockSpec((rows, GLA_WIDTH), lambda i: (i, 1)),
                  pl.BlockSpec((rows, GLA_KEY_WIDTH), lambda i: (i, 0)),
                  pl.BlockSpec(wsel.shape, lambda i: (0, 0)),
                  state_spec],
        out_specs=[pl.BlockSpec((rows, GLA_WIDTH), lambda i: (i, 0)), state_spec],
        out_shape=[jax.ShapeDtypeStruct((batch * seq, GLA_WIDTH), F32),
                   jax.ShapeDtypeStruct((batch, GLA_HEADS, GLA_DK, GLA_DV), F32)],
        scratch_shapes=[pltpu.VMEM((GLA_DEC_TRIP, GLA_HEADS * seq, seq * GLA_DK), F32),
                        pltpu.VMEM((GLA_DEC_TRIP, 2 * GLA_HEADS, seq, GLA_DK), F32)],
        compiler_params=_cparams(1),
    )(qkvr, qkvr, qkvr, log_a, wsel, state)


def _route(logits, cnt_ref):
    lt = logits.T
    n = lt.shape[1]
    big = jnp.int32(10 ** 6)
    row8 = lax.broadcasted_iota(jnp.int32, (EXPERTS_PER_GROUP, n), 0)
    lg = jnp.where(row8 < N_GROUPS, lt[0:EXPERTS_PER_GROUP], NEG)
    mg = jnp.max(lg, axis=0, keepdims=True)
    g_idx = jnp.min(jnp.where(lg == mg, row8, big), axis=0, keepdims=True)
    p_sel = 1.0 / jnp.sum(jnp.exp(lg - mg), axis=0, keepdims=True)
    le = jnp.zeros((EXPERTS_PER_GROUP, n), F32)
    for g in range(N_GROUPS):
        lo = EXPERTS_PER_GROUP * (g + 1)
        le = jnp.where(g_idx == g, lt[lo:lo + EXPERTS_PER_GROUP], le)
    m1 = jnp.max(le, axis=0, keepdims=True)
    i1 = jnp.min(jnp.where(le == m1, row8, big), axis=0, keepdims=True)
    rest = row8 != i1
    m2 = jnp.max(jnp.where(rest, le, NEG), axis=0, keepdims=True)
    i2 = jnp.min(jnp.where(rest & (le == m2), row8, big), axis=0, keepdims=True)
    e2 = jnp.exp(m2 - m1)
    w1 = p_sel / (1.0 + e2)
    w2 = p_sel * e2 / (1.0 + e2)
    ex1_i = g_idx * EXPERTS_PER_GROUP + i1
    ex2_i = g_idx * EXPERTS_PER_GROUP + i2
    ex1, ex2 = ex1_i.astype(F32), ex2_i.astype(F32)

    e_row = lax.broadcasted_iota(jnp.int32, (N_EXPERTS, n), 0)
    hit1, hit2 = e_row == ex1_i, e_row == ex2_i
    member = (hit1 | hit2).astype(BF16)
    before = (lax.broadcasted_iota(jnp.int32, (n, n), 0) < lax.broadcasted_iota(jnp.int32, (n, n), 1))
    prior = jnp.dot(member, before.astype(BF16), preferred_element_type=F32) + cnt_ref[:, 0:1]
    rank1 = jnp.sum(jnp.where(hit1, prior, 0.0), axis=0, keepdims=True)
    rank2 = jnp.sum(jnp.where(hit2, prior, 0.0), axis=0, keepdims=True)
    cnt_ref[...] = cnt_ref[...] + jnp.sum(member.astype(F32), axis=1, keepdims=True)

    packed = jnp.where(row8 == 0, ex1, jnp.where(row8 == 1, ex2,
                       jnp.where(row8 == 2, w1, jnp.where(row8 == 3, w2,
                                 jnp.where(row8 == 4, rank1, jnp.where(row8 == 5, rank2, 0.0))))))
    full = jnp.concatenate([packed, jnp.zeros((LANES - EXPERTS_PER_GROUP, n), F32)], axis=0)
    return full.T, packed


def _mix_out_kernel(grp, pos0, zero_first_halo, n_alias,
                    o_ref, r_ref, u_ref, halo_ref, x_ref, mod_ref, n2_ref, gn_ref, pw_ref, ps_ref,
                    wo_ref, wr_ref, br_ref, cnt0_ref, *rest):
    x1_ref, h2_ref, rt_ref, rtt_ref, cnt_ref, ext_ref, lvl_a, lvl_b, ymix_ref = rest[n_alias:]
    i = pl.program_id(0)

    @pl.when(i == 0)
    def _():
        cnt_ref[...] = cnt0_ref[...]

    @pl.when(i < grp.n_tiles)
    def _():
        _mix_out_tile(grp, pos0, zero_first_halo, i, o_ref, r_ref, u_ref, halo_ref, x_ref, mod_ref, n2_ref,
                      gn_ref, pw_ref, ps_ref, wo_ref, wr_ref, br_ref, x1_ref, h2_ref, rt_ref, rtt_ref, cnt_ref,
                      ext_ref, (lvl_a, lvl_b), ymix_ref)

    @pl.when(i >= grp.n_tiles)
    def _():
        h2_ref[...] = jnp.zeros_like(h2_ref)
        rt_ref[...] = jnp.zeros_like(rt_ref)
        rtt_ref[...] = jnp.zeros_like(rtt_ref)


def _mix_out_tile(grp, pos0, zero_first_halo, i, o_ref, r_ref, u_ref, halo_ref, x_ref, mod_ref, n2_ref,
                  gn_ref, pw_ref, ps_ref, wo_ref, wr_ref, br_ref, x1_ref, h2_ref, rt_ref, rtt_ref, cnt_ref,
                  ext_ref, lvl_refs, ymix_ref):
    nb, tt = grp.nb, grp.tt
    hist = halo_ref.shape[-2]

    for h in range(GLA_HEADS):
        cs = slice(h * GLA_DV, (h + 1) * GLA_DV)
        oh = o_ref[:, cs]
        ms = jnp.mean(oh * oh, axis=-1, keepdims=True)
        yh = oh * lax.rsqrt(ms + EPS) * gn_ref[:, cs] * _silu(r_ref[:, cs])
        ymix_ref[:, cs] = yh.astype(BF16)

    halo = halo_ref[...]
    if zero_first_halo:
        halo = jnp.where(i % grp.tiles_per_batch == 0, 0.0, halo)
    n_ext = EXT_TOK0 + tt
    ext_ref[:, 0:EXT_TOK0 - hist, :] = jnp.zeros((nb, EXT_TOK0 - hist, POOL_WIDTH), F32)
    ext_ref[:, EXT_TOK0 - hist:EXT_TOK0, :] = halo.reshape(nb, hist, POOL_WIDTH)
    u = u_ref[...].reshape(nb, tt, POOL_WIDTH)
    ext_ref[:, EXT_TOK0:n_ext, :] = u
    for lvl_ref in lvl_refs:
        lvl_ref[:, 0:EXT_PAD, :] = jnp.zeros((nb, EXT_PAD, POOL_GW), F32)
    t_idx = lax.broadcasted_iota(jnp.int32, (nb, tt, POOL_GW), 1)
    if grp.nb == 1:
        pos = (i % grp.tiles_per_batch) * TM + t_idx + pos0
    else:
        pos = t_idx + pos0
    y_gla = []
    part = D_MODEL // len(POOL_WINDOWS)
    for gi, w in enumerate(POOL_WINDOWS):
        y_gla.append(jnp.dot(ymix_ref[:, :GLA_WIDTH], wo_ref[:GLA_WIDTH, gi * part:(gi + 1) * part],
                             preferred_element_type=F32))
        cs = slice(gi * POOL_GW, (gi + 1) * POOL_GW)
        cur = lambda lo, hi: ext_ref[:, lo:hi, cs]
        d, level = 1, 0
        while d < w:
            nxt = lvl_refs[level % 2]
            nxt[:, EXT_PAD:n_ext, :] = cur(EXT_PAD, n_ext) + cur(EXT_PAD - d, n_ext - d)
            cur = lambda lo, hi, ref=nxt: ref[:, lo:hi, :]
            d, level = 2 * d, level + 1
        acc = cur(EXT_TOK0, n_ext)
        cnt = jnp.minimum(pos + 1, w).astype(F32)
        pooled = acc / cnt - u[:, :, cs]
        yp = _bdot(pooled.reshape(TM, POOL_GW), pw_ref[gi]) * ps_ref[:, cs]
        ymix_ref[:, GLA_WIDTH + gi * POOL_GW:GLA_WIDTH + (gi + 1) * POOL_GW] = yp.astype(BF16)

    y = jnp.concatenate(y_gla, axis=1) + jnp.dot(ymix_ref[:, GLA_WIDTH:], wo_ref[GLA_WIDTH:, :],
                                                 preferred_element_type=F32)
    x1 = x_ref[...] + _mod_rows(mod_ref, 2) * y.reshape(nb, tt, D_MODEL)
    x1_ref[...] = x1
    h2 = _rmsnorm_mod(x1, n2_ref[...], _mod_rows(mod_ref, 4), _mod_rows(mod_ref, 3)).reshape(TM, D_MODEL)
    h2_ref[...] = h2
    logits = _dot_3pass(h2, wr_ref[0], wr_ref[1]) + br_ref[...]
    rt_ref[...], rtt_ref[...] = _route(logits, cnt_ref)


def _mix_out(grp, n_tok, pos0, zero_first_halo, o, qkvr, u, halo_src, halo_block, halo_map, x, mod, norm2,
             gla_norm, pool_w, pool_scale, w_out, w_router, b_router, counts, shared=()):
    n_alias = len(shared)
    n = grp.n_tiles
    n_fill = 0 if shared else n_tok // TM - n
    assert n_fill == 0 or grp.tile_off == 0
    clamp = lambda f: (lambda i: f(jnp.minimum(i, n - 1)))
    const2 = lambda i: (0, 0)
    row = clamp(lambda i: (i, 0))
    off = grp.tile_off
    kern = functools.partial(_mix_out_kernel, grp, pos0, zero_first_halo, n_alias)
    return pl.pallas_call(
        kern,
        name="mix_out",
        grid=(n + n_fill,),
        in_specs=[pl.BlockSpec((TM, GLA_WIDTH), row),
                  pl.BlockSpec((TM, GLA_WIDTH), clamp(lambda i: (i, 2))),
                  pl.BlockSpec((TM, POOL_WIDTH), row),
                  pl.BlockSpec(halo_block, clamp(halo_map)),
                  pl.BlockSpec((grp.nb, grp.tt, D_MODEL), clamp(grp.x_map())),
                  pl.BlockSpec((grp.nb, 6, D_MODEL), clamp(grp.mod_map())),
                  pl.BlockSpec((1, 1, D_MODEL), lambda i: (0, 0, 0)),
                  pl.BlockSpec((1, GLA_WIDTH), const2),
                  pl.BlockSpec(pool_w.shape, lambda i: (0, 0, 0), pipeline_mode=pl.Buffered(1)),
                  pl.BlockSpec((1, POOL_WIDTH), const2),
                  pl.BlockSpec(w_out.shape, const2, pipeline_mode=pl.Buffered(1)),
                  pl.BlockSpec(w_router.shape, lambda i: (0, 0, 0), pipeline_mode=pl.Buffered(1)),
                  pl.BlockSpec((1, LANES), const2),
                  pl.BlockSpec((N_EXPERTS, LANES), const2)]
                 + [pl.BlockSpec(memory_space=pl.ANY)] * n_alias,
        out_specs=[pl.BlockSpec((grp.nb, grp.tt, D_MODEL), clamp(grp.x_map())),
                   pl.BlockSpec((TM, D_MODEL), lambda i: (i + off, 0)),
                   pl.BlockSpec((TM, LANES), lambda i: (i + off, 0)),
                   pl.BlockSpec((EXPERTS_PER_GROUP, TM), lambda i: (0, i + off)),
                   pl.BlockSpec((N_EXPERTS, LANES), const2)],
        out_shape=[jax.ShapeDtypeStruct(x.shape, F32),
                   jax.ShapeDtypeStruct((n_tok, D_MODEL), F32),
                   jax.ShapeDtypeStruct((n_tok, LANES), F32),
                   jax.ShapeDtypeStruct((EXPERTS_PER_GROUP, n_tok), F32),
                   jax.ShapeDtypeStruct((N_EXPERTS, LANES), F32)],
        scratch_shapes=[pltpu.VMEM((grp.nb, EXT_TOK0 + grp.tt, POOL_WIDTH), F32),
                        pltpu.VMEM((grp.nb, EXT_TOK0 + grp.tt, POOL_GW), F32),
                        pltpu.VMEM((grp.nb, EXT_TOK0 + grp.tt, POOL_GW), F32),
                        pltpu.VMEM((TM, D_MODEL), BF16)],
        input_output_aliases={14 + k: 1 + k for k in range(n_alias)},
        compiler_params=_cparams(1),
    )(o, qkvr, u, halo_src, x, mod, norm2.reshape(1, 1, D_MODEL), gla_norm.reshape(1, GLA_WIDTH),
      pool_w, pool_scale.reshape(1, POOL_WIDTH), w_out, w_router, b_router, counts, *shared)


def _row_copy(src_hbm, src_row, dst, dst_row, sem):
    return pltpu.make_async_copy(src_hbm.at[pl.ds(src_row, 1), :], dst.at[pl.ds(dst_row, 1), :], sem)


def _tile_wait(src_hbm, dst, sem):
    pltpu.make_async_copy(src_hbm.at[pl.ds(0, dst.shape[0]), :], dst, sem).wait()


MOE_AHEAD = 2


def _moe_gather_kernel(te_ref, src_ref, nu_ref, seg_ref, nxt_ref, h_hbm, w1_hbm, w3_hbm, w2_hbm, y_ref,
                       buf0, buf1, buf2, sem, wf1, wf3, wf2, wsem, w1b, w3b, w2b):
    i = pl.program_id(0)
    n_used = nu_ref[0]
    bufs = (buf0, buf1, buf2)
    n_slots = MOE_AHEAD + 1
    n_parts = 4

    def issue(tile, slot, part):
        base = tile * TME
        per = TME // n_parts
        for r in range(part * per, (part + 1) * per):
            _row_copy(h_hbm, src_ref[base + r], bufs[slot], r, sem.at[slot]).start(priority=0)

    def weight_copies(expert, wslot):
        return (pltpu.make_async_copy(w1_hbm.at[expert], wf1.at[wslot], wsem.at[wslot]),
                pltpu.make_async_copy(w3_hbm.at[expert], wf3.at[wslot], wsem.at[wslot]),
                pltpu.make_async_copy(w2_hbm.at[expert], wf2.at[wslot], wsem.at[wslot]))

    def compute(slot, ahead):
        nxt = (lambda part: issue(i + MOE_AHEAD, (slot + MOE_AHEAD) % n_slots, part)) if ahead else (lambda part: None)
        _tile_wait(h_hbm, bufs[slot], sem.at[slot])
        nxt(0)
        x = bufs[slot][...].astype(BF16)
        nxt(1)
        a = jnp.dot(x, w1b[...], preferred_element_type=F32)
        nxt(2)
        b = jnp.dot(x, w3b[...], preferred_element_type=F32)
        nxt(3)
        hid = _silu(a) * b
        y_ref[...] = jnp.dot(hid.astype(BF16), w2b[...], preferred_element_type=F32)

    @pl.when(i == 0)
    def _():
        for c in weight_copies(te_ref[0], 0):
            c.start(priority=1)
        for t in range(MOE_AHEAD):
            @pl.when(t < n_used)
            def _():
                for part in range(n_parts):
                    issue(t, t, part)

    prev = jnp.maximum(i - 1, 0)

    @pl.when((i < n_used) & ((i == 0) | (te_ref[i] != te_ref[prev])))
    def _():
        wslot = seg_ref[i] % 2
        for c in weight_copies(te_ref[i], wslot):
            c.wait()
        w1b[...] = wf1[wslot].astype(BF16)
        w3b[...] = wf3[wslot].astype(BF16)
        w2b[...] = wf2[wslot].astype(BF16)

        @pl.when(nxt_ref[i] >= 0)
        def _():
            for c in weight_copies(nxt_ref[i], 1 - wslot):
                c.start(priority=1)

    for slot in range(n_slots):
        @pl.when((i + MOE_AHEAD < n_used) & (i % n_slots == slot))
        def _():
            compute(slot, True)

        @pl.when((i < n_used) & (i + MOE_AHEAD >= n_used) & (i % n_slots == slot))
        def _():
            compute(slot, False)

    @pl.when(i >= n_used)
    def _():
        y_ref[...] = jnp.zeros_like(y_ref)


def _moe_gather(plan, h2_all, w1, w3, w2):
    n_sorted = plan["src_row"].shape[0]
    grid_spec = pltpu.PrefetchScalarGridSpec(
        num_scalar_prefetch=5,
        grid=(n_sorted // TME,),
        in_specs=[pl.BlockSpec(memory_space=pl.ANY)] * 4,
        out_specs=pl.BlockSpec((TME, D_MODEL), lambda i, *_: (i, 0)),
        scratch_shapes=[pltpu.VMEM((TME, D_MODEL), F32)] * (MOE_AHEAD + 1)
                       + [pltpu.SemaphoreType.DMA((MOE_AHEAD + 1,)),
                          pltpu.VMEM((2, D_MODEL, EXPERT_FF), F32), pltpu.VMEM((2, D_MODEL, EXPERT_FF), F32),
                          pltpu.VMEM((2, EXPERT_FF, D_MODEL), F32),
                          pltpu.SemaphoreType.DMA((2,)),
                          pltpu.VMEM((D_MODEL, EXPERT_FF), BF16), pltpu.VMEM((D_MODEL, EXPERT_FF), BF16),
                          pltpu.VMEM((EXPERT_FF, D_MODEL), BF16)],
    )
    return pl.pallas_call(
        _moe_gather_kernel,
        name="moe",
        grid_spec=grid_spec,
        out_shape=jax.ShapeDtypeStruct((n_sorted, D_MODEL), F32),
        compiler_params=_cparams(1),
    )(plan["tile_expert"], plan["src_row"], plan["n_used"], plan["segment"], plan["next_expert"],
      h2_all, w1, w3, w2)


def _finish_kernel(grp, pos_ref, x1_ref, mod_ref, rt_ref, nf_ref, y_hbm, out_ref, *scratch):
    n_slots = FINISH_AHEAD + 1
    bufs_a, bufs_b, sem = scratch[:n_slots], scratch[n_slots:2 * n_slots], scratch[2 * n_slots]
    i = pl.program_id(0)
    n_steps = pl.num_programs(0)
    n_parts = 4
    n_tok = pos_ref.shape[0] // 2

    def issue(tile, slot, part):
        base = tile * TM + grp.row_off
        per = TM // n_parts
        for r in range(part * per, (part + 1) * per):
            _row_copy(y_hbm, pos_ref[base + r], bufs_a[slot], r, sem.at[slot]).start(priority=0)
            _row_copy(y_hbm, pos_ref[n_tok + base + r], bufs_b[slot], r, sem.at[slot]).start(priority=1)

    def compute(slot, ahead):
        nxt = ((lambda part: issue(i + FINISH_AHEAD, (slot + FINISH_AHEAD) % n_slots, part)) if ahead
               else (lambda part: None))
        _tile_wait(y_hbm, bufs_a[slot], sem.at[slot])
        _tile_wait(y_hbm, bufs_b[slot], sem.at[slot])
        nxt(0)
        rt = rt_ref[...]
        moe = rt[:, 2:3] * bufs_a[slot][...] + rt[:, 3:4] * bufs_b[slot][...]
        nxt(1)
        x2 = x1_ref[...] + _mod_rows(mod_ref, 5) * moe.reshape(grp.nb, grp.tt, D_MODEL)
        nxt(2)
        ms = jnp.mean(x2 * x2, axis=-1, keepdims=True)
        nxt(3)
        out_ref[...] = x2 * lax.rsqrt(ms + EPS) * nf_ref[...]

    @pl.when(i == 0)
    def _():
        for t in range(FINISH_AHEAD):
            @pl.when(t < n_steps)
            def _():
                for part in range(n_parts):
                    issue(t, t, part)

    for slot in range(n_slots):
        @pl.when((i + FINISH_AHEAD < n_steps) & (i % n_slots == slot))
        def _():
            compute(slot, True)

        @pl.when((i + FINISH_AHEAD >= n_steps) & (i % n_slots == slot))
        def _():
            compute(slot, False)


def _finish(grp, pos, x1, mod, route_all, norm_f, y_sorted):
    off = grp.tile_off
    grid_spec = pltpu.PrefetchScalarGridSpec(
        num_scalar_prefetch=1,
        grid=(grp.n_tiles,),
        in_specs=[pl.BlockSpec((grp.nb, grp.tt, D_MODEL), grp.x_map()),
                  pl.BlockSpec((grp.nb, 6, D_MODEL), grp.mod_map()),
                  pl.BlockSpec((TM, LANES), lambda i, p: (i + off, 0)),
                  pl.BlockSpec((1, 1, D_MODEL), lambda i, p: (0, 0, 0)),
                  pl.BlockSpec(memory_space=pl.ANY)],
        out_specs=pl.BlockSpec((grp.nb, grp.tt, D_MODEL), grp.x_map()),
        scratch_shapes=[pltpu.VMEM((TM, D_MODEL), F32)] * (2 * (FINISH_AHEAD + 1))
                       + [pltpu.SemaphoreType.DMA((FINISH_AHEAD + 1,))],
    )
    return pl.pallas_call(
        functools.partial(_finish_kernel, grp),
        name="finish",
        grid_spec=grid_spec,
        out_shape=jax.ShapeDtypeStruct(x1.shape, F32),
        compiler_params=_cparams(1),
    )(pos, x1, mod, route_all, norm_f.reshape(1, 1, D_MODEL), y_sorted)


def _slot_kernel(rt_ref, cnt_ref, pos_ref):
    expert = rt_ref[0:2, :].astype(jnp.int32)
    rank = rt_ref[4:6, :].astype(jnp.int32)
    padded = jnp.ceil(cnt_ref[...] / TME) * TME
    r = lax.broadcasted_iota(jnp.int32, (N_EXPERTS, N_EXPERTS), 0)
    c = lax.broadcasted_iota(jnp.int32, (N_EXPERTS, N_EXPERTS), 1)
    starts = _dot_exact_lhs((c < r).astype(BF16), padded)
    start_of = jnp.zeros(expert.shape, F32)
    for e in range(N_EXPERTS):
        start_of = jnp.where(expert == e, starts[e:e + 1, 0:1], start_of)
    pos_ref[...] = start_of.astype(jnp.int32) + rank


def _slots(route_t, counts):
    n_tok = route_t.shape[1]
    return pl.pallas_call(
        _slot_kernel,
        name="slots",
        out_shape=jax.ShapeDtypeStruct((2, n_tok), jnp.int32),
        compiler_params=pltpu.CompilerParams(vmem_limit_bytes=VMEM_LIMIT),
    )(route_t, counts)


def _sort_plan(route_t, counts):
    n_tok = route_t.shape[1]
    n_pairs = 2 * n_tok
    n_sorted = n_pairs + N_EXPERTS * TME
    pos = _slots(route_t, counts).reshape(n_pairs)
    counts = counts[:, 0].astype(jnp.int32)
    padded = ((counts + TME - 1) // TME) * TME
    ends = jnp.cumsum(padded)
    token = jnp.tile(jnp.arange(n_tok, dtype=jnp.int32), 2)
    src_row = (jnp.arange(n_sorted, dtype=jnp.int32) % n_tok).at[pos].set(
        token, unique_indices=True, mode="promise_in_bounds")
    tile_start = jnp.arange(n_sorted // TME, dtype=jnp.int32) * TME
    tile_expert = jnp.sum((tile_start[:, None] >= ends[None, :]).astype(jnp.int32), axis=1)
    tile_expert = jnp.minimum(tile_expert, N_EXPERTS - 1)
    n_used = ends[-1] // TME
    is_first = jnp.concatenate([jnp.ones((1,), jnp.int32),
                                (tile_expert[1:] != tile_expert[:-1]).astype(jnp.int32)])
    segment = jnp.cumsum(is_first) - 1
    next_tile = ends[tile_expert] // TME
    next_expert = jnp.where(next_tile < n_used, tile_expert[jnp.minimum(next_tile, n_sorted // TME - 1)], -1)
    return dict(pos=pos.astype(jnp.int32), src_row=src_row, tile_expert=tile_expert.astype(jnp.int32),
                n_used=n_used.astype(jnp.int32).reshape(1), segment=segment.astype(jnp.int32),
                next_expert=next_expert.astype(jnp.int32))


def kernel(x_prompt, x_sample, c_prompt, c_sample, state_gla, state_pool, w_ada, b_ada, norm1, norm2, w_in,
           gate_up, gate_bias, gla_norm, pool_w, pool_scale, w_out, w_group, b_group, w_expert, b_expert,
           w1, w3, w2, norm_f):
    assert w_ada.shape[0] == 1, "single-layer step"
    bp, tp, _ = x_prompt.shape
    bs, ts, _ = x_sample.shape
    grp_p = _Group(bp, tp, 0, bs)
    grp_s = _Group(bs, ts, bp * tp, 0)
    n_tok = bp * tp + bs * ts

    n_c = bp + bs
    n_c_pad = -(-n_c // 8) * 8
    c_all = jnp.concatenate([c_sample, c_prompt, jnp.zeros((n_c_pad - n_c, D_MODEL), F32)], axis=0)
    mod = _adaln(c_all, w_ada[0], b_ada[0]).reshape(n_c_pad, 6, D_MODEL)
    mod_p = mod_s = mod

    w_in_t = jnp.swapaxes(w_in.reshape(w_in.shape[1:]), 0, 1).astype(BF16)
    gup = gate_up[0].astype(BF16)
    gb = gate_bias[0].reshape(1, GLA_KEY_WIDTH)
    pw = pool_w[0].astype(BF16)
    wo = w_out[0].astype(BF16)
    gap = EXPERTS_PER_GROUP - N_GROUPS
    tail = LANES - EXPERTS_PER_GROUP - N_EXPERTS
    w_router = jnp.concatenate(
        [w_group[0], jnp.zeros((D_MODEL, gap), F32),
         jnp.transpose(w_expert[0], (1, 0, 2)).reshape(D_MODEL, N_EXPERTS),
         jnp.zeros((D_MODEL, tail), F32)], axis=1)
    w_router_hi = w_router.astype(BF16)
    w_router = jnp.stack([w_router_hi, (w_router - w_router_hi.astype(F32)).astype(BF16)])
    b_router = jnp.concatenate([b_group[0], jnp.zeros((gap,), F32), b_expert[0].reshape(N_EXPERTS),
                                jnp.zeros((tail,), F32)]).reshape(1, LANES)

    qkvr_p, la_p, u_p = _in_proj(grp_p, x_prompt, mod_p, norm1[0], w_in_t, gup, gb)
    qkvr_s, la_s, u_s = _in_proj(grp_s, x_sample, mod_s, norm1[0], w_in_t, gup, gb)

    o_p, gla_p = _gla_prompt(bp, tp, qkvr_p, la_p)
    o_s, gla_s = _gla_decode(bs, ts, qkvr_s, la_s, state_gla.reshape(state_gla.shape[1:]))

    halo_per_tile = TM // HALO
    halo_map_p = lambda i: (jnp.maximum(i * halo_per_tile - 1, 0), 0)
    counts = jnp.zeros((N_EXPERTS, LANES), F32)
    x1_p, h2_all, route_all, route_t, counts = _mix_out(
        grp_p, n_tok, 0, True, o_p, qkvr_p, u_p, u_p, (HALO, POOL_WIDTH), halo_map_p, x_prompt, mod_p,
        norm2[0], gla_norm[0], pw, pool_scale[0], wo, w_router, b_router, counts)
    x1_s, h2_all, route_all, route_t, counts = _mix_out(
        grp_s, n_tok, PAST_LEN, False, o_s, qkvr_s, u_s, state_pool.reshape(bs, POOL_BUF, POOL_WIDTH),
        (grp_s.nb, POOL_BUF, POOL_WIDTH), lambda i: (i, 0, 0), x_sample, mod_s, norm2[0], gla_norm[0], pw,
        pool_scale[0], wo, w_router, b_router, counts, shared=(h2_all, route_all, route_t))

    plan = _sort_plan(route_t, counts)
    y_sorted = _moe_gather(plan, h2_all, w1.reshape(w1.shape[1:]), w3.reshape(w3.shape[1:]),
                           w2.reshape(w2.shape[1:]))

    y_p = _finish(grp_p, plan["pos"], x1_p, mod_p, route_all, norm_f, y_sorted)
    y_s = _finish(grp_s, plan["pos"], x1_s, mod_s, route_all, norm_f, y_sorted)

    u_p3 = u_p.reshape(bp, tp, POOL_WIDTH)
    u_s3 = u_s.reshape(bs, ts, POOL_WIDTH)
    assert tp >= POOL_BUF > ts
    pool_p = u_p3[:, tp - POOL_BUF:]
    pool_s = jnp.concatenate([state_pool.reshape(bs, POOL_BUF, POOL_WIDTH)[:, ts:], u_s3], axis=1)
    lead = lambda a: a.reshape((1,) + a.shape)
    return (y_p, y_s, lead(gla_p), lead(pool_p), lead(gla_s), lead(pool_s))
```

```python
import functools

import jax
import jax.numpy as jnp
from jax import lax
from jax.experimental import pallas as pl
from jax.experimental.pallas import tpu as pltpu

D_MODEL = 2048
GLA_HEADS = 4
GLA_DK = 128
GLA_DV = 256
GLA_KEY_WIDTH = GLA_HEADS * GLA_DK
GLA_WIDTH = GLA_HEADS * GLA_DV
POOL_WIDTH = 1024
POOL_WINDOWS = (2, 4, 8, 16)
POOL_GW = 256
POOL_BUF = 15
HALO = 16
EXT_PAD = 8
EXT_TOK0 = EXT_PAD + HALO
GATE_RANK = 16
GATE_TEMP = 16.0
N_GROUPS = 4
EXPERTS_PER_GROUP = 8
N_EXPERTS = 32
EXPERT_FF = 512
EPS = 1e-6
PAST_LEN = 16384
QKVR_WIDTH = 2 * GLA_KEY_WIDTH + 2 * GLA_WIDTH

LANES = 128
ADALN_COLS = 1024
TM = 256
TME = 256
FINISH_AHEAD = 2
GLA_CHUNK = 64
GLA_SUB = 8
GLA_STEP = 512
GLA_TRIP = 4
VMEM_LIMIT = 56 * 1024 * 1024

BF16 = jnp.bfloat16
F32 = jnp.float32
NEG = -1e30
LOG2E = 1.4426950408889634


def _cparams(n_axes):
    return pltpu.CompilerParams(dimension_semantics=("arbitrary",) * n_axes,
                                vmem_limit_bytes=VMEM_LIMIT)


def _silu(x):
    return x / (1.0 + jnp.exp(-x))


def _bdot(a, b):
    return jnp.dot(a.astype(BF16), b.astype(BF16), preferred_element_type=F32)


def _split3(a):
    a1 = a.astype(BF16)
    r1 = a - a1.astype(F32)
    a2 = r1.astype(BF16)
    a3 = (r1 - a2.astype(F32)).astype(BF16)
    return a1, a2, a3


def _split2(a):
    hi = a.astype(BF16)
    lo = (a - hi.astype(F32)).astype(BF16)
    return hi, lo


def _dot_3pass(a, b_hi, b_lo):
    a_hi, a_lo = _split2(a)
    d = lambda x, y: jnp.dot(x, y, preferred_element_type=F32)
    return d(a_hi, b_hi) + (d(a_hi, b_lo) + d(a_lo, b_hi))


def _dot_exact_lhs(tri_bf16, g):
    g1, g2, g3 = _split3(g)
    d = lambda y: jnp.dot(tri_bf16, y, preferred_element_type=F32)
    return d(g1) + (d(g2) + d(g3))


def _adaln_kernel(c_ref, w_ref, b_ref, o_ref):
    c = c_ref[...]
    o_ref[...] = _bdot(_silu(c), w_ref[...]) + b_ref[...]


def _adaln(c_all, w_ada, b_ada):
    n, d = c_all.shape
    width = w_ada.shape[1]
    tn = ADALN_COLS
    return pl.pallas_call(
        _adaln_kernel,
        name="adaln",
        grid=(width // tn,),
        in_specs=[pl.BlockSpec((n, d), lambda j: (0, 0)),
                  pl.BlockSpec((d, tn), lambda j: (0, j)),
                  pl.BlockSpec((1, tn), lambda j: (0, j))],
        out_specs=pl.BlockSpec((n, tn), lambda j: (0, j)),
        out_shape=jax.ShapeDtypeStruct((n, width), F32),
        compiler_params=_cparams(1),
    )(c_all, w_ada, b_ada.reshape(1, width))


class _Group:
    def __init__(self, batch, seq, row_off, mod_off):
        self.batch, self.seq, self.row_off, self.mod_off = batch, seq, row_off, mod_off
        if seq >= TM:
            assert seq % TM == 0
            self.nb, self.tt = 1, TM
            self.tiles_per_batch = seq // TM
            self.n_tiles = batch * self.tiles_per_batch
        else:
            assert TM % seq == 0 and batch % (TM // seq) == 0
            self.nb, self.tt = TM // seq, seq
            self.tiles_per_batch = 1
            self.n_tiles = batch // self.nb
        self.rows = batch * seq
        self.tile_off = row_off // TM

    def x_map(self):
        if self.nb == 1:
            tpb = self.tiles_per_batch
            return lambda i, *_: (i // tpb, i % tpb, 0)
        return lambda i, *_: (i, 0, 0)

    def mod_map(self):
        assert self.mod_off % self.nb == 0
        off = self.mod_off // self.nb
        if self.nb == 1:
            tpb = self.tiles_per_batch
            return lambda i, *_: (i // tpb + off, 0, 0)
        return lambda i, *_: (i + off, 0, 0)


def _mod_rows(mod_ref, idx):
    return mod_ref[:, idx:idx + 1, :]


def _rmsnorm_mod(x, gain, scale, shift):
    ms = jnp.mean(x * x, axis=-1, keepdims=True)
    y = x * lax.rsqrt(ms + EPS) * gain
    return y * (1.0 + scale) + shift


def _in_proj_kernel(x_ref, mod_ref, n1_ref, wq_ref, wg_ref, wu_ref, gup_ref, gb_ref,
                    qkvr_ref, la_ref, u_ref):
    x = x_ref[...]
    h = _rmsnorm_mod(x, n1_ref[...], _mod_rows(mod_ref, 1), _mod_rows(mod_ref, 0))
    hb = h.reshape(TM, D_MODEL).astype(BF16)
    nt = (((1,), (1,)), ((), ()))
    qkvr_ref[...] = lax.dot_general(hb, wq_ref[...], nt, preferred_element_type=F32)
    u_ref[...] = lax.dot_general(hb, wu_ref[...], nt, preferred_element_type=F32)
    g_lr = lax.dot_general(hb, wg_ref[...], nt, preferred_element_type=F32)
    pre = jnp.dot(g_lr.astype(BF16), gup_ref[...], preferred_element_type=F32) + gb_ref[...]
    log_sig = jnp.minimum(pre, 0.0) - jnp.log1p(jnp.exp(-jnp.abs(pre)))
    la_ref[...] = log_sig / GATE_TEMP


def _in_proj(grp, x, mod, norm1, w_in_t, gup, gb):
    const = lambda i: (0, 0)
    u_row0 = QKVR_WIDTH + GATE_RANK
    row = lambda i: (i, 0)
    return pl.pallas_call(
        _in_proj_kernel,
        name="in_proj",
        grid=(grp.n_tiles,),
        in_specs=[pl.BlockSpec((grp.nb, grp.tt, D_MODEL), grp.x_map()),
                  pl.BlockSpec((grp.nb, 6, D_MODEL), grp.mod_map()),
                  pl.BlockSpec((1, 1, D_MODEL), lambda i: (0, 0, 0)),
                  pl.BlockSpec((QKVR_WIDTH, D_MODEL), const, pipeline_mode=pl.Buffered(1)),
                  pl.BlockSpec((GATE_RANK, D_MODEL), lambda i: (QKVR_WIDTH // GATE_RANK, 0),
                               pipeline_mode=pl.Buffered(1)),
                  pl.BlockSpec((pl.Element(POOL_WIDTH), pl.Element(D_MODEL)), lambda i: (u_row0, 0),
                               pipeline_mode=pl.Buffered(1)),
                  pl.BlockSpec(gup.shape, const, pipeline_mode=pl.Buffered(1)),
                  pl.BlockSpec(gb.shape, const, pipeline_mode=pl.Buffered(1))],
        out_specs=[pl.BlockSpec((TM, QKVR_WIDTH), row),
                   pl.BlockSpec((TM, GLA_KEY_WIDTH), row),
                   pl.BlockSpec((TM, POOL_WIDTH), row)],
        out_shape=[jax.ShapeDtypeStruct((grp.rows, QKVR_WIDTH), F32),
                   jax.ShapeDtypeStruct((grp.rows, GLA_KEY_WIDTH), F32),
                   jax.ShapeDtypeStruct((grp.rows, POOL_WIDTH), F32)],
        compiler_params=_cparams(1),
    )(x, mod, norm1.reshape(1, 1, D_MODEL), w_in_t, w_in_t, w_in_t, gup, gb)


def _gla_select_matrix(chunk, sub):
    r = jnp.arange(sub * GLA_DK, dtype=jnp.int32)[:, None] // GLA_DK
    l = jnp.arange(LANES, dtype=jnp.int32)[None, :]
    return ((l % sub == r) & (l < chunk)).astype(BF16)


def _gla_chunks(chunks, states, chunk, sub, wsel, t_refs, kb_refs, chained=True):
    n_sub = chunk // sub
    rows = lax.broadcasted_iota(jnp.int32, (chunk, chunk), 0)
    cols = lax.broadcasted_iota(jnp.int32, (chunk, chunk), 1)
    tri = (rows >= cols).astype(BF16)
    nt = (((1,), (1,)), ((), ()))
    tn = (((0,), (0,)), ((), ()))
    key_row = lax.broadcasted_iota(jnp.int32, (chunk, LANES), 0)
    lane = lax.broadcasted_iota(jnp.int32, (sub, chunk), 1)
    row = lax.broadcasted_iota(jnp.int32, (sub, chunk), 0)
    head = lambda a, h, w: a[:, h * w:(h + 1) * w]
    n = len(chunks)

    b4s = [_dot_exact_lhs(tri, g4) for (_, _, _, g4) in chunks]
    q4s = [q4 * (GLA_DK ** -0.5) for (q4, _, _, _) in chunks]

    for c in range(n):
        kb_ref, t_ref, k4 = kb_refs[c], t_refs[c], chunks[c][1]
        for h in range(GLA_HEADS):
            kb_ref[h] = head(k4, h, GLA_DK)
            kb_ref[GLA_HEADS + h] = head(b4s[c], h, GLA_DK) * LOG2E
        for h in range(GLA_HEADS):
            for s in range(n_sub):
                lo = s * sub
                r0 = (h * n_sub + s) * sub
                q_s, b_s = head(q4s[c], h, GLA_DK)[lo:lo + sub, :], kb_ref[GLA_HEADS + h, lo:lo + sub, :]
                for jl in range(sub):
                    k_j = jnp.broadcast_to(kb_ref[h, lo + jl:lo + jl + 1, :], (sub, GLA_DK))
                    b_j = jnp.broadcast_to(kb_ref[GLA_HEADS + h, lo + jl:lo + jl + 1, :], (sub, GLA_DK))
                    decay = jnp.exp2(jnp.minimum(b_s - b_j, 0.0))
                    t_ref[r0:r0 + sub, jl * GLA_DK:(jl + 1) * GLA_DK] = (q_s * k_j * decay).astype(t_ref.dtype)
    p_diags = [jnp.dot(t_refs[c][...].astype(BF16), wsel, preferred_element_type=F32) for c in range(n)]

    intra = []
    for c in range(n):
        per_head = []
        for h in range(GLA_HEADS):
            q, k, b = head(q4s[c], h, GLA_DK), head(chunks[c][1], h, GLA_DK), head(b4s[c], h, GLA_DK)
            p_blocks = []
            for s in range(n_sub):
                lo = s * sub
                r0 = (h * n_sub + s) * sub
                in_block = (lane >= lo) & (lane - lo <= row)
                p = jnp.where(in_block, p_diags[c][r0:r0 + sub, :chunk], 0.0)
                if s > 0:
                    ref_row = b[lo - 1:lo, :]
                    q_rel = q[lo:lo + sub, :] * jnp.exp(b[lo:lo + sub, :] - ref_row)
                    k_rel = k * jnp.exp(jnp.where(key_row < lo, ref_row - b, NEG))
                    p = p + lax.dot_general(q_rel.astype(BF16), k_rel.astype(BF16), nt,
                                            preferred_element_type=F32)
                p_blocks.append(p)
            p_full = p_blocks[0] if n_sub == 1 else jnp.concatenate(p_blocks, axis=0)
            per_head.append(_bdot(p_full, head(chunks[c][2], h, GLA_DV)))
        intra.append(per_head)

    outs, end_states = [], []
    for c in range(n):
        cur = states if chained else states[c]
        o_heads, new_states = [], []
        for h in range(GLA_HEADS):
            q, k, b = head(q4s[c], h, GLA_DK), head(chunks[c][1], h, GLA_DK), head(b4s[c], h, GLA_DK)
            v = head(chunks[c][2], h, GLA_DV)
            o_heads.append(intra[c][h] + _bdot(q * jnp.exp(b), cur[h]))
            b_last = b[chunk - 1:chunk, :]
            k_dec = k * jnp.exp(b_last - b)
            decay_col = jnp.exp(b[chunk - 8:chunk, :]).T[:, 7:8]
            new_states.append(decay_col * cur[h] + lax.dot_general(
                k_dec.astype(BF16), v.astype(BF16), tn, preferred_element_type=F32))
        states = new_states if chained else states
        end_states.append(new_states)
        outs.append(jnp.concatenate(o_heads, axis=1))
    return outs, (end_states[-1] if chained else end_states)


def _gla_prompt_kernel(q_ref, k_ref, v_ref, la_ref, wsel_ref, o_ref, s_ref, t_ref, kb_ref):
    @pl.when(pl.program_id(1) == 0)
    def _():
        s_ref[...] = jnp.zeros_like(s_ref)

    def body(trip, carry):
        states = [s_ref[0, h] for h in range(GLA_HEADS)]
        slices = [pl.ds(pl.multiple_of((GLA_TRIP * trip + c) * GLA_CHUNK, GLA_CHUNK), GLA_CHUNK)
                  for c in range(GLA_TRIP)]
        chunks = [(q_ref[sl, :], k_ref[sl, :], v_ref[sl, :], la_ref[sl, :]) for sl in slices]
        outs, states = _gla_chunks(chunks, states, GLA_CHUNK, GLA_SUB, wsel_ref[...],
                                   [t_ref.at[c] for c in range(GLA_TRIP)],
                                   [kb_ref.at[c] for c in range(GLA_TRIP)])
        for sl, o in zip(slices, outs):
            o_ref[sl, :] = o
        for h in range(GLA_HEADS):
            s_ref[0, h] = states[h]
        return carry

    lax.fori_loop(0, GLA_STEP // (GLA_TRIP * GLA_CHUNK), body, 0)


def _gla_prompt(batch, seq, qkvr, log_a):
    steps = seq // GLA_STEP
    row = lambda b, s: b * steps + s
    wsel = _gla_select_matrix(GLA_CHUNK, GLA_SUB)
    return pl.pallas_call(
        _gla_prompt_kernel,
        name="gla_prompt",
        grid=(batch, steps),
        in_specs=[pl.BlockSpec((GLA_STEP, GLA_KEY_WIDTH), lambda b, s: (row(b, s), 0)),
                  pl.BlockSpec((GLA_STEP, GLA_KEY_WIDTH), lambda b, s: (row(b, s), 1)),
                  pl.BlockSpec((GLA_STEP, GLA_WIDTH), lambda b, s: (row(b, s), 1)),
                  pl.BlockSpec((GLA_STEP, GLA_KEY_WIDTH), lambda b, s: (row(b, s), 0)),
                  pl.BlockSpec(wsel.shape, lambda b, s: (0, 0))],
        out_specs=[pl.BlockSpec((GLA_STEP, GLA_WIDTH), lambda b, s: (row(b, s), 0)),
                   pl.BlockSpec((1, GLA_HEADS, GLA_DK, GLA_DV), lambda b, s: (b, 0, 0, 0))],
        out_shape=[jax.ShapeDtypeStruct((batch * seq, GLA_WIDTH), F32),
                   jax.ShapeDtypeStruct((batch, GLA_HEADS, GLA_DK, GLA_DV), F32)],
        scratch_shapes=[pltpu.VMEM((GLA_TRIP, GLA_HEADS * GLA_CHUNK, GLA_SUB * GLA_DK), BF16),
                        pltpu.VMEM((GLA_TRIP, 2 * GLA_HEADS, GLA_CHUNK, GLA_DK), F32)],
        compiler_params=_cparams(2),
    )(qkvr, qkvr, qkvr, log_a, wsel)


GLA_DEC_BB = 8
GLA_DEC_TRIP = 4


def _gla_decode_kernel(seq, q_ref, k_ref, v_ref, la_ref, wsel_ref, s0_ref, o_ref, s_ref, t_ref, kb_ref):
    def body(trip, carry):
        elems = [GLA_DEC_TRIP * trip + c for c in range(GLA_DEC_TRIP)]
        slices = [pl.ds(pl.multiple_of(e * seq, seq), seq) for e in elems]
        chunks = [(q_ref[sl, :], k_ref[sl, :], v_ref[sl, :], la_ref[sl, :]) for sl in slices]
        states = [[s0_ref[e, h] for h in range(GLA_HEADS)] for e in elems]
        outs, new_states = _gla_chunks(chunks, states, seq, seq, wsel_ref[...],
                                       [t_ref.at[c] for c in range(GLA_DEC_TRIP)],
                                       [kb_ref.at[c] for c in range(GLA_DEC_TRIP)], chained=False)
        for e, sl, o, ns in zip(elems, slices, outs, new_states):
            o_ref[sl, :] = o
            for h in range(GLA_HEADS):
                s_ref[e, h] = ns[h]
        return carry

    lax.fori_loop(0, GLA_DEC_BB // GLA_DEC_TRIP, body, 0)


def _gla_decode(batch, seq, qkvr, log_a, state):
    rows = GLA_DEC_BB * seq
    wsel = _gla_select_matrix(seq, seq)
    state_spec = pl.BlockSpec((GLA_DEC_BB, GLA_HEADS, GLA_DK, GLA_DV), lambda i: (i, 0, 0, 0))
    return pl.pallas_call(
        functools.partial(_gla_decode_kernel, seq),
        name="gla_decode",
        grid=(batch // GLA_DEC_BB,),
        in_specs=[pl.BlockSpec((rows, GLA_KEY_WIDTH), lambda i: (i, 0)),
                  pl.BlockSpec((rows, GLA_KEY_WIDTH), lambda i: (i, 1)),
                  pl.BlockSpec((rows, GLA_WIDTH), lambda i: (i, 1)),
                  pl.BlockSpec((rows, GLA_KEY_WIDTH), lambda i: (i, 0)),
                  pl.BlockSpec(wsel.shape, lambda i: (0, 0)),
                  state_spec],
        out_specs=[pl.BlockSpec((rows, GLA_WIDTH), lambda i: (i, 0)), state_spec],
        out_shape=[jax.ShapeDtypeStruct((batch * seq, GLA_WIDTH), F32),
                   jax.ShapeDtypeStruct((batch, GLA_HEADS, GLA_DK, GLA_DV), F32)],
        scratch_shapes=[pltpu.VMEM((GLA_DEC_TRIP, GLA_HEADS * seq, seq * GLA_DK), F32),
                        pltpu.VMEM((GLA_DEC_TRIP, 2 * GLA_HEADS, seq, GLA_DK), F32)],
        compiler_params=_cparams(1),
    )(qkvr, qkvr, qkvr, log_a, wsel, state)


def _route(logits, cnt_ref):
    lt = logits.T
    n = lt.shape[1]
    big = jnp.int32(10 ** 6)
    row8 = lax.broadcasted_iota(jnp.int32, (EXPERTS_PER_GROUP, n), 0)
    lg = jnp.where(row8 < N_GROUPS, lt[0:EXPERTS_PER_GROUP], NEG)
    mg = jnp.max(lg, axis=0, keepdims=True)
    g_idx = jnp.min(jnp.where(lg == mg, row8, big), axis=0, keepdims=True)
    p_sel = 1.0 / jnp.sum(jnp.exp(lg - mg), axis=0, keepdims=True)
    le = jnp.zeros((EXPERTS_PER_GROUP, n), F32)
    for g in range(N_GROUPS):
        lo = EXPERTS_PER_GROUP * (g + 1)
        le = jnp.where(g_idx == g, lt[lo:lo + EXPERTS_PER_GROUP], le)
    m1 = jnp.max(le, axis=0, keepdims=True)
    i1 = jnp.min(jnp.where(le == m1, row8, big), axis=0, keepdims=True)
    rest = row8 != i1
    m2 = jnp.max(jnp.where(rest, le, NEG), axis=0, keepdims=True)
    i2 = jnp.min(jnp.where(rest & (le == m2), row8, big), axis=0, keepdims=True)
    e2 = jnp.exp(m2 - m1)
    w1 = p_sel / (1.0 + e2)
    w2 = p_sel * e2 / (1.0 + e2)
    ex1_i = g_idx * EXPERTS_PER_GROUP + i1
    ex2_i = g_idx * EXPERTS_PER_GROUP + i2
    ex1, ex2 = ex1_i.astype(F32), ex2_i.astype(F32)

    e_row = lax.broadcasted_iota(jnp.int32, (N_EXPERTS, n), 0)
    hit1, hit2 = e_row == ex1_i, e_row == ex2_i
    member = (hit1 | hit2).astype(BF16)
    before = (lax.broadcasted_iota(jnp.int32, (n, n), 0) < lax.broadcasted_iota(jnp.int32, (n, n), 1))
    prior = jnp.dot(member, before.astype(BF16), preferred_element_type=F32) + cnt_ref[:, 0:1]
    rank1 = jnp.sum(jnp.where(hit1, prior, 0.0), axis=0, keepdims=True)
    rank2 = jnp.sum(jnp.where(hit2, prior, 0.0), axis=0, keepdims=True)
    cnt_ref[...] = cnt_ref[...] + jnp.sum(member.astype(F32), axis=1, keepdims=True)

    packed = jnp.where(row8 == 0, ex1, jnp.where(row8 == 1, ex2,
                       jnp.where(row8 == 2, w1, jnp.where(row8 == 3, w2,
                                 jnp.where(row8 == 4, rank1, jnp.where(row8 == 5, rank2, 0.0))))))
    full = jnp.concatenate([packed, jnp.zeros((LANES - EXPERTS_PER_GROUP, n), F32)], axis=0)
    return full.T, packed


def _mix_out_kernel(grp, pos0, zero_first_halo, n_alias,
                    o_ref, r_ref, u_ref, halo_ref, x_ref, mod_ref, n2_ref, gn_ref, pw_ref, ps_ref,
                    wo_ref, wr_ref, br_ref, cnt0_ref, *rest):
    x1_ref, h2_ref, rt_ref, rtt_ref, cnt_ref, ext_ref, lvl_a, lvl_b, ymix_ref = rest[n_alias:]
    i = pl.program_id(0)

    @pl.when(i == 0)
    def _():
        cnt_ref[...] = cnt0_ref[...]

    @pl.when(i < grp.n_tiles)
    def _():
        _mix_out_tile(grp, pos0, zero_first_halo, i, o_ref, r_ref, u_ref, halo_ref, x_ref, mod_ref, n2_ref,
                      gn_ref, pw_ref, ps_ref, wo_ref, wr_ref, br_ref, x1_ref, h2_ref, rt_ref, rtt_ref, cnt_ref,
                      ext_ref, (lvl_a, lvl_b), ymix_ref)

    @pl.when(i >= grp.n_tiles)
    def _():
        h2_ref[...] = jnp.zeros_like(h2_ref)
        rt_ref[...] = jnp.zeros_like(rt_ref)
        rtt_ref[...] = jnp.zeros_like(rtt_ref)


def _mix_out_tile(grp, pos0, zero_first_halo, i, o_ref, r_ref, u_ref, halo_ref, x_ref, mod_ref, n2_ref,
                  gn_ref, pw_ref, ps_ref, wo_ref, wr_ref, br_ref, x1_ref, h2_ref, rt_ref, rtt_ref, cnt_ref,
                  ext_ref, lvl_refs, ymix_ref):
    nb, tt = grp.nb, grp.tt
    hist = halo_ref.shape[-2]

    for h in range(GLA_HEADS):
        cs = slice(h * GLA_DV, (h + 1) * GLA_DV)
        oh = o_ref[:, cs]
        ms = jnp.mean(oh * oh, axis=-1, keepdims=True)
        yh = oh * lax.rsqrt(ms + EPS) * gn_ref[:, cs] * _silu(r_ref[:, cs])
        ymix_ref[:, cs] = yh.astype(BF16)

    halo = halo_ref[...]
    if zero_first_halo:
        halo = jnp.where(i % grp.tiles_per_batch == 0, 0.0, halo)
    n_ext = EXT_TOK0 + tt
    ext_ref[:, 0:EXT_TOK0 - hist, :] = jnp.zeros((nb, EXT_TOK0 - hist, POOL_WIDTH), F32)
    ext_ref[:, EXT_TOK0 - hist:EXT_TOK0, :] = halo.reshape(nb, hist, POOL_WIDTH)
    u = u_ref[...].reshape(nb, tt, POOL_WIDTH)
    ext_ref[:, EXT_TOK0:n_ext, :] = u
    for lvl_ref in lvl_refs:
        lvl_ref[:, 0:EXT_PAD, :] = jnp.zeros((nb, EXT_PAD, POOL_GW), F32)
    t_idx = lax.broadcasted_iota(jnp.int32, (nb, tt, POOL_GW), 1)
    if grp.nb == 1:
        pos = (i % grp.tiles_per_batch) * TM + t_idx + pos0
    else:
        pos = t_idx + pos0
    y_gla = []
    part = D_MODEL // len(POOL_WINDOWS)
    for gi, w in enumerate(POOL_WINDOWS):
        y_gla.append(jnp.dot(ymix_ref[:, :GLA_WIDTH], wo_ref[:GLA_WIDTH, gi * part:(gi + 1) * part],
                             preferred_element_type=F32))
        cs = slice(gi * POOL_GW, (gi + 1) * POOL_GW)
        cur = lambda lo, hi: ext_ref[:, lo:hi, cs]
        d, level = 1, 0
        while d < w:
            nxt = lvl_refs[level % 2]
            nxt[:, EXT_PAD:n_ext, :] = cur(EXT_PAD, n_ext) + cur(EXT_PAD - d, n_ext - d)
            cur = lambda lo, hi, ref=nxt: ref[:, lo:hi, :]
            d, level = 2 * d, level + 1
        acc = cur(EXT_TOK0, n_ext)
        cnt = jnp.minimum(pos + 1, w).astype(F32)
        pooled = acc / cnt - u[:, :, cs]
        yp = _bdot(pooled.reshape(TM, POOL_GW), pw_ref[gi]) * ps_ref[:, cs]
        ymix_ref[:, GLA_WIDTH + gi * POOL_GW:GLA_WIDTH + (gi + 1) * POOL_GW] = yp.astype(BF16)

    y = jnp.concatenate(y_gla, axis=1) + jnp.dot(ymix_ref[:, GLA_WIDTH:], wo_ref[GLA_WIDTH:, :],
                                                 preferred_element_type=F32)
    x1 = x_ref[...] + _mod_rows(mod_ref, 2) * y.reshape(nb, tt, D_MODEL)
    x1_ref[...] = x1
    h2 = _rmsnorm_mod(x1, n2_ref[...], _mod_rows(mod_ref, 4), _mod_rows(mod_ref, 3)).reshape(TM, D_MODEL)
    h2_ref[...] = h2
    logits = _dot_3pass(h2, wr_ref[0], wr_ref[1]) + br_ref[...]
    rt_ref[...], rtt_ref[...] = _route(logits, cnt_ref)


def _mix_out(grp, n_tok, pos0, zero_first_halo, o, qkvr, u, halo_src, halo_block, halo_map, x, mod, norm2,
             gla_norm, pool_w, pool_scale, w_out, w_router, b_router, counts, shared=()):
    n_alias = len(shared)
    n = grp.n_tiles
    n_fill = 0 if shared else n_tok // TM - n
    assert n_fill == 0 or grp.tile_off == 0
    clamp = lambda f: (lambda i: f(jnp.minimum(i, n - 1)))
    const2 = lambda i: (0, 0)
    row = clamp(lambda i: (i, 0))
    off = grp.tile_off
    kern = functools.partial(_mix_out_kernel, grp, pos0, zero_first_halo, n_alias)
    return pl.pallas_call(
        kern,
        name="mix_out",
        grid=(n + n_fill,),
        in_specs=[pl.BlockSpec((TM, GLA_WIDTH), row),
                  pl.BlockSpec((TM, GLA_WIDTH), clamp(lambda i: (i, 2))),
                  pl.BlockSpec((TM, POOL_WIDTH), row),
                  pl.BlockSpec(halo_block, clamp(halo_map)),
                  pl.BlockSpec((grp.nb, grp.tt, D_MODEL), clamp(grp.x_map())),
                  pl.BlockSpec((grp.nb, 6, D_MODEL), clamp(grp.mod_map())),
                  pl.BlockSpec((1, 1, D_MODEL), lambda i: (0, 0, 0)),
                  pl.BlockSpec((1, GLA_WIDTH), const2),
                  pl.BlockSpec(pool_w.shape, lambda i: (0, 0, 0), pipeline_mode=pl.Buffered(1)),
                  pl.BlockSpec((1, POOL_WIDTH), const2),
                  pl.BlockSpec(w_out.shape, const2, pipeline_mode=pl.Buffered(1)),
                  pl.BlockSpec(w_router.shape, lambda i: (0, 0, 0), pipeline_mode=pl.Buffered(1)),
                  pl.BlockSpec((1, LANES), const2),
                  pl.BlockSpec((N_EXPERTS, LANES), const2)]
                 + [pl.BlockSpec(memory_space=pl.ANY)] * n_alias,
        out_specs=[pl.BlockSpec((grp.nb, grp.tt, D_MODEL), clamp(grp.x_map())),
                   pl.BlockSpec((TM, D_MODEL), lambda i: (i + off, 0)),
                   pl.BlockSpec((TM, LANES), lambda i: (i + off, 0)),
                   pl.BlockSpec((EXPERTS_PER_GROUP, TM), lambda i: (0, i + off)),
                   pl.BlockSpec((N_EXPERTS, LANES), const2)],
        out_shape=[jax.ShapeDtypeStruct(x.shape, F32),
                   jax.ShapeDtypeStruct((n_tok, D_MODEL), F32),
                   jax.ShapeDtypeStruct((n_tok, LANES), F32),
                   jax.ShapeDtypeStruct((EXPERTS_PER_GROUP, n_tok), F32),
                   jax.ShapeDtypeStruct((N_EXPERTS, LANES), F32)],
        scratch_shapes=[pltpu.VMEM((grp.nb, EXT_TOK0 + grp.tt, POOL_WIDTH), F32),
                        pltpu.VMEM((grp.nb, EXT_TOK0 + grp.tt, POOL_GW), F32),
                        pltpu.VMEM((grp.nb, EXT_TOK0 + grp.tt, POOL_GW), F32),
                        pltpu.VMEM((TM, D_MODEL), BF16)],
        input_output_aliases={14 + k: 1 + k for k in range(n_alias)},
        compiler_params=_cparams(1),
    )(o, qkvr, u, halo_src, x, mod, norm2.reshape(1, 1, D_MODEL), gla_norm.reshape(1, GLA_WIDTH),
      pool_w, pool_scale.reshape(1, POOL_WIDTH), w_out, w_router, b_router, counts, *shared)


def _row_copy(src_hbm, src_row, dst, dst_row, sem):
    return pltpu.make_async_copy(src_hbm.at[pl.ds(src_row, 1), :], dst.at[pl.ds(dst_row, 1), :], sem)


def _tile_wait(src_hbm, dst, sem):
    pltpu.make_async_copy(src_hbm.at[pl.ds(0, dst.shape[0]), :], dst, sem).wait()


MOE_AHEAD = 2


def _moe_gather_kernel(te_ref, src_ref, nu_ref, seg_ref, nxt_ref, h_hbm, w1_hbm, w3_hbm, w2_hbm, y_ref,
                       buf0, buf1, buf2, sem, wf1, wf3, wf2, wsem, w1b, w3b, w2b):
    i = pl.program_id(0)
    n_used = nu_ref[0]
    bufs = (buf0, buf1, buf2)
    n_slots = MOE_AHEAD + 1
    n_parts = 4

    def issue(tile, slot, part):
        base = tile * TME
        per = TME // n_parts
        for r in range(part * per, (part + 1) * per):
            _row_copy(h_hbm, src_ref[base + r], bufs[slot], r, sem.at[slot]).start(priority=0)

    def weight_copies(expert, wslot):
        return (pltpu.make_async_copy(w1_hbm.at[expert], wf1.at[wslot], wsem.at[wslot]),
                pltpu.make_async_copy(w3_hbm.at[expert], wf3.at[wslot], wsem.at[wslot]),
                pltpu.make_async_copy(w2_hbm.at[expert], wf2.at[wslot], wsem.at[wslot]))

    def compute(slot, ahead):
        nxt = (lambda part: issue(i + MOE_AHEAD, (slot + MOE_AHEAD) % n_slots, part)) if ahead else (lambda part: None)
        _tile_wait(h_hbm, bufs[slot], sem.at[slot])
        nxt(0)
        x = bufs[slot][...].astype(BF16)
        nxt(1)
        a = jnp.dot(x, w1b[...], preferred_element_type=F32)
        nxt(2)
        b = jnp.dot(x, w3b[...], preferred_element_type=F32)
        nxt(3)
        hid = _silu(a) * b
        y_ref[...] = jnp.dot(hid.astype(BF16), w2b[...], preferred_element_type=F32)

    @pl.when(i == 0)
    def _():
        for c in weight_copies(te_ref[0], 0):
            c.start(priority=1)
        for t in range(MOE_AHEAD):
            @pl.when(t < n_used)
            def _():
                for part in range(n_parts):
                    issue(t, t, part)

    prev = jnp.maximum(i - 1, 0)

    @pl.when((i < n_used) & ((i == 0) | (te_ref[i] != te_ref[prev])))
    def _():
        wslot = seg_ref[i] % 2
        for c in weight_copies(te_ref[i], wslot):
            c.wait()
        w1b[...] = wf1[wslot].astype(BF16)
        w3b[...] = wf3[wslot].astype(BF16)
        w2b[...] = wf2[wslot].astype(BF16)

        @pl.when(nxt_ref[i] >= 0)
        def _():
            for c in weight_copies(nxt_ref[i], 1 - wslot):
                c.start(priority=1)

    for slot in range(n_slots):
        @pl.when((i + MOE_AHEAD < n_used) & (i % n_slots == slot))
        def _():
            compute(slot, True)

        @pl.when((i < n_used) & (i + MOE_AHEAD >= n_used) & (i % n_slots == slot))
        def _():
            compute(slot, False)

    @pl.when(i >= n_used)
    def _():
        y_ref[...] = jnp.zeros_like(y_ref)


def _moe_gather(plan, h2_all, w1, w3, w2):
    n_sorted = plan["src_row"].shape[0]
    grid_spec = pltpu.PrefetchScalarGridSpec(
        num_scalar_prefetch=5,
        grid=(n_sorted // TME,),
        in_specs=[pl.BlockSpec(memory_space=pl.ANY)] * 4,
        out_specs=pl.BlockSpec((TME, D_MODEL), lambda i, *_: (i, 0)),
        scratch_shapes=[pltpu.VMEM((TME, D_MODEL), F32)] * (MOE_AHEAD + 1)
                       + [pltpu.SemaphoreType.DMA((MOE_AHEAD + 1,)),
                          pltpu.VMEM((2, D_MODEL, EXPERT_FF), F32), pltpu.VMEM((2, D_MODEL, EXPERT_FF), F32),
                          pltpu.VMEM((2, EXPERT_FF, D_MODEL), F32),
                          pltpu.SemaphoreType.DMA((2,)),
                          pltpu.VMEM((D_MODEL, EXPERT_FF), BF16), pltpu.VMEM((D_MODEL, EXPERT_FF), BF16),
                          pltpu.VMEM((EXPERT_FF, D_MODEL), BF16)],
    )
    return pl.pallas_call(
        _moe_gather_kernel,
        name="moe",
        grid_spec=grid_spec,
        out_shape=jax.ShapeDtypeStruct((n_sorted, D_MODEL), F32),
        compiler_params=_cparams(1),
    )(plan["tile_expert"], plan["src_row"], plan["n_used"], plan["segment"], plan["next_expert"],
      h2_all, w1, w3, w2)


def _finish_kernel(grp, pos_ref, x1_ref, mod_ref, rt_ref, nf_ref, y_hbm, out_ref, *scratch):
    n_slots = FINISH_AHEAD + 1
    bufs_a, bufs_b, sem = scratch[:n_slots], scratch[n_slots:2 * n_slots], scratch[2 * n_slots]
    i = pl.program_id(0)
    n_steps = pl.num_programs(0)
    n_parts = 4
    n_tok = pos_ref.shape[0] // 2

    def issue(tile, slot, part):
        base = tile * TM + grp.row_off
        per = TM // n_parts
        for r in range(part * per, (part + 1) * per):
            _row_copy(y_hbm, pos_ref[base + r], bufs_a[slot], r, sem.at[slot]).start(priority=1)
            _row_copy(y_hbm, pos_ref[n_tok + base + r], bufs_b[slot], r, sem.at[slot]).start(priority=1)

    def compute(slot, ahead):
        nxt = ((lambda part: issue(i + FINISH_AHEAD, (slot + FINISH_AHEAD) % n_slots, part)) if ahead
               else (lambda part: None))
        _tile_wait(y_hbm, bufs_a[slot], sem.at[slot])
        _tile_wait(y_hbm, bufs_b[slot], sem.at[slot])
        nxt(0)
        rt = rt_ref[...]
        moe = rt[:, 2:3] * bufs_a[slot][...] + rt[:, 3:4] * bufs_b[slot][...]
        nxt(1)
        x2 = x1_ref[...] + _mod_rows(mod_ref, 5) * moe.reshape(grp.nb, grp.tt, D_MODEL)
        nxt(2)
        ms = jnp.mean(x2 * x2, axis=-1, keepdims=True)
        nxt(3)
        out_ref[...] = x2 * lax.rsqrt(ms + EPS) * nf_ref[...]

    @pl.when(i == 0)
    def _():
        for t in range(FINISH_AHEAD):
            @pl.when(t < n_steps)
            def _():
                for part in range(n_parts):
                    issue(t, t, part)

    for slot in range(n_slots):
        @pl.when((i + FINISH_AHEAD < n_steps) & (i % n_slots == slot))
        def _():
            compute(slot, True)

        @pl.when((i + FINISH_AHEAD >= n_steps) & (i % n_slots == slot))
        def _():
            compute(slot, False)


def _finish(grp, pos, x1, mod, route_all, norm_f, y_sorted):
    off = grp.tile_off
    grid_spec = pltpu.PrefetchScalarGridSpec(
        num_scalar_prefetch=1,
        grid=(grp.n_tiles,),
        in_specs=[pl.BlockSpec((grp.nb, grp.tt, D_MODEL), grp.x_map()),
                  pl.BlockSpec((grp.nb, 6, D_MODEL), grp.mod_map()),
                  pl.BlockSpec((TM, LANES), lambda i, p: (i + off, 0)),
                  pl.BlockSpec((1, 1, D_MODEL), lambda i, p: (0, 0, 0)),
                  pl.BlockSpec(memory_space=pl.ANY)],
        out_specs=pl.BlockSpec((grp.nb, grp.tt, D_MODEL), grp.x_map()),
        scratch_shapes=[pltpu.VMEM((TM, D_MODEL), F32)] * (2 * (FINISH_AHEAD + 1))
                       + [pltpu.SemaphoreType.DMA((FINISH_AHEAD + 1,))],
    )
    return pl.pallas_call(
        functools.partial(_finish_kernel, grp),
        name="finish",
        grid_spec=grid_spec,
        out_shape=jax.ShapeDtypeStruct(x1.shape, F32),
        compiler_params=_cparams(1),
    )(pos, x1, mod, route_all, norm_f.reshape(1, 1, D_MODEL), y_sorted)


def _slot_kernel(rt_ref, cnt_ref, pos_ref):
    expert = rt_ref[0:2, :].astype(jnp.int32)
    rank = rt_ref[4:6, :].astype(jnp.int32)
    padded = jnp.ceil(cnt_ref[...] / TME) * TME
    r = lax.broadcasted_iota(jnp.int32, (N_EXPERTS, N_EXPERTS), 0)
    c = lax.broadcasted_iota(jnp.int32, (N_EXPERTS, N_EXPERTS), 1)
    starts = _dot_exact_lhs((c < r).astype(BF16), padded)
    start_of = jnp.zeros(expert.shape, F32)
    for e in range(N_EXPERTS):
        start_of = jnp.where(expert == e, starts[e:e + 1, 0:1], start_of)
    pos_ref[...] = start_of.astype(jnp.int32) + rank


def _slots(route_t, counts):
    n_tok = route_t.shape[1]
    return pl.pallas_call(
        _slot_kernel,
        name="slots",
        out_shape=jax.ShapeDtypeStruct((2, n_tok), jnp.int32),
        compiler_params=pltpu.CompilerParams(vmem_limit_bytes=VMEM_LIMIT),
    )(route_t, counts)


def _sort_plan(route_t, counts):
    n_tok = route_t.shape[1]
    n_pairs = 2 * n_tok
    n_sorted = n_pairs + N_EXPERTS * TME
    pos = _slots(route_t, counts).reshape(n_pairs)
    counts = counts[:, 0].astype(jnp.int32)
    padded = ((counts + TME - 1) // TME) * TME
    ends = jnp.cumsum(padded)
    token = jnp.tile(jnp.arange(n_tok, dtype=jnp.int32), 2)
    src_row = (jnp.arange(n_sorted, dtype=jnp.int32) % n_tok).at[pos].set(
        token, unique_indices=True, mode="promise_in_bounds")
    tile_start = jnp.arange(n_sorted // TME, dtype=jnp.int32) * TME
    tile_expert = jnp.sum((tile_start[:, None] >= ends[None, :]).astype(jnp.int32), axis=1)
    tile_expert = jnp.minimum(tile_expert, N_EXPERTS - 1)
    n_used = ends[-1] // TME
    is_first = jnp.concatenate([jnp.ones((1,), jnp.int32),
                                (tile_expert[1:] != tile_expert[:-1]).astype(jnp.int32)])
    segment = jnp.cumsum(is_first) - 1
    next_tile = ends[tile_expert] // TME
    next_expert = jnp.where(next_tile < n_used, tile_expert[jnp.minimum(next_tile, n_sorted // TME - 1)], -1)
    return dict(pos=pos.astype(jnp.int32), src_row=src_row, tile_expert=tile_expert.astype(jnp.int32),
                n_used=n_used.astype(jnp.int32).reshape(1), segment=segment.astype(jnp.int32),
                next_expert=next_expert.astype(jnp.int32))


def kernel(x_prompt, x_sample, c_prompt, c_sample, state_gla, state_pool, w_ada, b_ada, norm1, norm2, w_in,
           gate_up, gate_bias, gla_norm, pool_w, pool_scale, w_out, w_group, b_group, w_expert, b_expert,
           w1, w3, w2, norm_f):
    assert w_ada.shape[0] == 1, "single-layer step"
    bp, tp, _ = x_prompt.shape
    bs, ts, _ = x_sample.shape
    grp_p = _Group(bp, tp, 0, bs)
    grp_s = _Group(bs, ts, bp * tp, 0)
    n_tok = bp * tp + bs * ts

    n_c = bp + bs
    n_c_pad = -(-n_c // 8) * 8
    c_all = jnp.concatenate([c_sample, c_prompt, jnp.zeros((n_c_pad - n_c, D_MODEL), F32)], axis=0)
    mod = _adaln(c_all, w_ada[0], b_ada[0]).reshape(n_c_pad, 6, D_MODEL)
    mod_p = mod_s = mod

    w_in_t = jnp.swapaxes(w_in.reshape(w_in.shape[1:]), 0, 1).astype(BF16)
    gup = gate_up[0].astype(BF16)
    gb = gate_bias[0].reshape(1, GLA_KEY_WIDTH)
    pw = pool_w[0].astype(BF16)
    wo = w_out[0].astype(BF16)
    gap = EXPERTS_PER_GROUP - N_GROUPS
    tail = LANES - EXPERTS_PER_GROUP - N_EXPERTS
    w_router = jnp.concatenate(
        [w_group[0], jnp.zeros((D_MODEL, gap), F32),
         jnp.transpose(w_expert[0], (1, 0, 2)).reshape(D_MODEL, N_EXPERTS),
         jnp.zeros((D_MODEL, tail), F32)], axis=1)
    w_router_hi = w_router.astype(BF16)
    w_router = jnp.stack([w_router_hi, (w_router - w_router_hi.astype(F32)).astype(BF16)])
    b_router = jnp.concatenate([b_group[0], jnp.zeros((gap,), F32), b_expert[0].reshape(N_EXPERTS),
                                jnp.zeros((tail,), F32)]).reshape(1, LANES)

    qkvr_p, la_p, u_p = _in_proj(grp_p, x_prompt, mod_p, norm1[0], w_in_t, gup, gb)
    qkvr_s, la_s, u_s = _in_proj(grp_s, x_sample, mod_s, norm1[0], w_in_t, gup, gb)

    o_p, gla_p = _gla_prompt(bp, tp, qkvr_p, la_p)
    o_s, gla_s = _gla_decode(bs, ts, qkvr_s, la_s, state_gla.reshape(state_gla.shape[1:]))

    halo_per_tile = TM // HALO
    halo_map_p = lambda i: (jnp.maximum(i * halo_per_tile - 1, 0), 0)
    counts = jnp.zeros((N_EXPERTS, LANES), F32)
    x1_p, h2_all, route_all, route_t, counts = _mix_out(
        grp_p, n_tok, 0, True, o_p, qkvr_p, u_p, u_p, (HALO, POOL_WIDTH), halo_map_p, x_prompt, mod_p,
        norm2[0], gla_norm[0], pw, pool_scale[0], wo, w_router, b_router, counts)
    x1_s, h2_all, route_all, route_t, counts = _mix_out(
        grp_s, n_tok, PAST_LEN, False, o_s, qkvr_s, u_s, state_pool.reshape(bs, POOL_BUF, POOL_WIDTH),
        (grp_s.nb, POOL_BUF, POOL_WIDTH), lambda i: (i, 0, 0), x_sample, mod_s, norm2[0], gla_norm[0], pw,
        pool_scale[0], wo, w_router, b_router, counts, shared=(h2_all, route_all, route_t))

    plan = _sort_plan(route_t, counts)
    y_sorted = _moe_gather(plan, h2_all, w1.reshape(w1.shape[1:]), w3.reshape(w3.shape[1:]),
                           w2.reshape(w2.shape[1:]))

    y_p = _finish(grp_p, plan["pos"], x1_p, mod_p, route_all, norm_f, y_sorted)
    y_s = _finish(grp_s, plan["pos"], x1_s, mod_s, route_all, norm_f, y_sorted)

    u_p3 = u_p.reshape(bp, tp, POOL_WIDTH)
    u_s3 = u_s.reshape(bs, ts, POOL_WIDTH)
    assert tp >= POOL_BUF > ts
    pool_p = u_p3[:, tp - POOL_BUF:]
    pool_s = jnp.concatenate([state_pool.reshape(bs, POOL_BUF, POOL_WIDTH)[:, ts:], u_s3], axis=1)
    lead = lambda a: a.reshape((1,) + a.shape)
    return (y_p, y_s, lead(gla_p), lead(pool_p), lead(gla_s), lead(pool_s))
```
